```python
import jax, jax.numpy as jnp
from jax import lax
import numpy as np

D_MODEL = 1024
BATCH = 1
SEQ = 16384
DEPTH = 1
DEC_BATCH = 32
DEC_SEQ = 64
PAST_LEN = 1024

CHUNK = 64
HEAD_DIM = 64
MIX_WIDTH = D_MODEL
EPS = 1e-6
SCALE = HEAD_DIM ** -0.5
NEG = -1e30
A_HEADS = 8
A_LEFT_CHUNKS = 8
A_BAND = A_LEFT_CHUNKS + 1
A_CTX = A_LEFT_CHUNKS * CHUNK
MAX_REL = 128
A_WIDTH = A_HEADS * HEAD_DIM
B_HEADS = 8
B_KV_HEADS = 2
B_GROUP = B_HEADS // B_KV_HEADS
WINDOW = 128
B_LEFT_CHUNKS = WINDOW // CHUNK
B_BAND = B_LEFT_CHUNKS + 1
B_CTX = WINDOW
B_WIDTH = B_HEADS * HEAD_DIM
B_KV_WIDTH = B_KV_HEADS * HEAD_DIM
ROPE_THETA = 10000.0
IN_COLS = 3 * A_WIDTH + B_WIDTH + 2 * B_KV_WIDTH
IN_SPLITS = [A_WIDTH, 2 * A_WIDTH, 3 * A_WIDTH, 3 * A_WIDTH + B_WIDTH, 3 * A_WIDTH + B_WIDTH + B_KV_WIDTH]
D_FF = 2816
CONV_W = 3

kernel_name = "hybrid_chunkband_swa_sink_convffn_step"


def rms_norm(x, g):
    xf = x.astype(jnp.float32)
    y = xf * lax.rsqrt(jnp.mean(xf * xf, axis=-1, keepdims=True) + EPS)
    return (y * g.astype(jnp.float32)).astype(x.dtype)


def rope(x, pos):
    half = HEAD_DIM // 2
    inv = jnp.power(ROPE_THETA, -jnp.arange(half, dtype=jnp.float32) / half)
    ang = pos.astype(jnp.float32)[:, None] * inv[None, :]
    cos = jnp.cos(ang)[:, None, :]
    sin = jnp.sin(ang)[:, None, :]
    x1 = x[..., :half].astype(jnp.float32)
    x2 = x[..., half:].astype(jnp.float32)
    return jnp.concatenate([x1 * cos - x2 * sin, x2 * cos + x1 * sin], axis=-1).astype(x.dtype)


def project_heads(h, pos, w_in, q_norm_a, k_norm_a, q_norm_b, k_norm_b):
    bsz, s, _ = h.shape
    p = jnp.einsum("bsd,dc->bsc", h, w_in)
    qa, ka, va, qb, kb, vb = jnp.split(p, IN_SPLITS, axis=-1)
    qa = rms_norm(qa.reshape(bsz, s, A_HEADS, HEAD_DIM), q_norm_a)
    ka = rms_norm(ka.reshape(bsz, s, A_HEADS, HEAD_DIM), k_norm_a)
    va = va.reshape(bsz, s, A_HEADS, HEAD_DIM)
    qb = rope(rms_norm(qb.reshape(bsz, s, B_HEADS, HEAD_DIM), q_norm_b), pos)
    kb = rope(rms_norm(kb.reshape(bsz, s, B_KV_HEADS, HEAD_DIM), k_norm_b), pos)
    vb = vb.reshape(bsz, s, B_KV_HEADS, HEAD_DIM)
    return qa, ka, va, qb, kb, vb


def rel_bias(table, qpos, kpos):
    d = jnp.clip(qpos[:, None] - kpos[None, :], -MAX_REL, MAX_REL) + MAX_REL
    return table[:, d].astype(jnp.float32)


def band_gather(x, n_left):
    bsz, s, nh, d = x.shape
    nc = s // CHUNK
    xc = x.reshape(bsz, nc, CHUNK, nh, d)
    xp = jnp.pad(xc, ((0, 0), (n_left, 0), (0, 0), (0, 0), (0, 0)))
    idx = jnp.arange(nc)[:, None] + jnp.arange(n_left + 1)[None, :]
    return xp[:, idx].reshape(bsz, nc, (n_left + 1) * CHUNK, nh, d)


def band_valid(nc, n_left):
    kchunk = jnp.arange((n_left + 1) * CHUNK) // CHUNK
    return (jnp.arange(nc)[:, None] - n_left + kchunk[None, :]) >= 0


def sink_softmax(scores, sinks):
    s = sinks.astype(jnp.float32).reshape(B_KV_HEADS, B_GROUP, 1, 1)
    m = jnp.maximum(jnp.max(scores, axis=-1, keepdims=True), s)
    e = jnp.exp(scores - m)
    return e / (jnp.sum(e, axis=-1, keepdims=True) + jnp.exp(s - m))


def band_attn_a_prompt(q, k, v, rel_table):
    bsz, s, nh, d = q.shape
    nc = s // CHUNK
    qc = q.reshape(bsz, nc, CHUNK, nh, d)
    kb = band_gather(k, A_LEFT_CHUNKS)
    vb = band_gather(v, A_LEFT_CHUNKS)
    scores = jnp.einsum("bnqhd,bnkhd->bnhqk", qc, kb, preferred_element_type=jnp.float32) * SCALE
    bias = rel_bias(rel_table, A_LEFT_CHUNKS * CHUNK + jnp.arange(CHUNK), jnp.arange(A_BAND * CHUNK))
    valid = band_valid(nc, A_LEFT_CHUNKS)
    scores = jnp.where(valid[None, :, None, None, :], scores + bias[None, None], NEG)
    p = jax.nn.softmax(scores, axis=-1)
    out = jnp.einsum("bnhqk,bnkhd->bnqhd", p.astype(v.dtype), vb)
    return out.reshape(bsz, s, A_WIDTH)


def band_attn_a_sample(q, k_new, v_new, k_cache, v_cache, rel_table):
    bsz, t = q.shape[:2]
    past = k_cache.shape[1]
    k = jnp.concatenate([k_cache.astype(k_new.dtype), k_new], axis=1)
    v = jnp.concatenate([v_cache.astype(v_new.dtype), v_new], axis=1)
    scores = jnp.einsum("bqhd,bkhd->bhqk", q, k, preferred_element_type=jnp.float32) * SCALE
    scores = scores + rel_bias(rel_table, past + jnp.arange(t), jnp.arange(past + t))[None]
    p = jax.nn.softmax(scores, axis=-1)
    out = jnp.einsum("bhqk,bkhd->bqhd", p.astype(v.dtype), v)
    return out.reshape(bsz, t, A_WIDTH)


def window_attn_b_prompt(q, k, v, sinks):
    bsz, s = q.shape[:2]
    nc = s // CHUNK
    qc = q.reshape(bsz, nc, CHUNK, B_KV_HEADS, B_GROUP, HEAD_DIM)
    kb = band_gather(k, B_LEFT_CHUNKS)
    vb = band_gather(v, B_LEFT_CHUNKS)
    scores = jnp.einsum("bnqgrd,bnkgd->bngrqk", qc, kb, preferred_element_type=jnp.float32) * SCALE
    valid = band_valid(nc, B_LEFT_CHUNKS)
    scores = jnp.where(valid[None, :, None, None, None, :], scores, NEG)
    p = sink_softmax(scores, sinks)
    out = jnp.einsum("bngrqk,bnkgd->bnqgrd", p.astype(v.dtype), vb)
    return out.reshape(bsz, s, B_WIDTH)


def window_attn_b_sample(q, k_new, v_new, k_cache, v_cache, sinks):
    bsz, t = q.shape[:2]
    k = jnp.concatenate([k_cache.astype(k_new.dtype), k_new], axis=1)
    v = jnp.concatenate([v_cache.astype(v_new.dtype), v_new], axis=1)
    qg = q.reshape(bsz, t, B_KV_HEADS, B_GROUP, HEAD_DIM)
    scores = jnp.einsum("bqgrd,bkgd->bgrqk", qg, k, preferred_element_type=jnp.float32) * SCALE
    p = sink_softmax(scores, sinks)
    out = jnp.einsum("bgrqk,bkgd->bqgrd", p.astype(v.dtype), v)
    return out.reshape(bsz, t, B_WIDTH)


def conv_ffn(h, conv_state, w_gate_up, conv_w, conv_b, w_down):
    t = h.shape[1]
    gate, up = jnp.split(jnp.einsum("bsd,df->bsf", h, w_gate_up), 2, axis=-1)
    ext = jnp.concatenate([conv_state.astype(gate.dtype), gate], axis=1)
    conv = conv_b
    for i in range(CONV_W):
        conv = conv + conv_w[i] * ext[:, i:i + t]
    y = jax.nn.silu(conv) * up
    return jnp.einsum("bsf,fd->bsd", y, w_down), ext[:, -(CONV_W - 1):]


def setup_inputs(seed: int = 0) -> dict:
    key = jax.random.key(seed)
    ks = jax.random.split(key, 21)

    def nrm(k, shape, scale=1.0):
        return scale * jax.random.normal(k, shape, jnp.float32)

    a_len = min(A_CTX, PAST_LEN)
    b_len = min(B_CTX, PAST_LEN)
    return {
        "x_prompt": nrm(ks[0], (BATCH, SEQ, D_MODEL)),
        "x_sample": nrm(ks[1], (DEC_BATCH, DEC_SEQ, D_MODEL)),
        "cache_a_k": nrm(ks[2], (DEPTH, DEC_BATCH, a_len, A_HEADS, HEAD_DIM)),
        "cache_a_v": nrm(ks[3], (DEPTH, DEC_BATCH, a_len, A_HEADS, HEAD_DIM)),
        "cache_b_k": nrm(ks[4], (DEPTH, DEC_BATCH, b_len, B_KV_HEADS, HEAD_DIM)),
        "cache_b_v": nrm(ks[5], (DEPTH, DEC_BATCH, b_len, B_KV_HEADS, HEAD_DIM)),
        "state_conv": nrm(ks[6], (DEPTH, DEC_BATCH, CONV_W - 1, D_FF)),
        "norm_attn": 1.0 + nrm(ks[7], (DEPTH, D_MODEL), 0.05),
        "w_in": nrm(ks[8], (DEPTH, D_MODEL, IN_COLS), D_MODEL ** -0.5),
        "q_norm_a": 1.0 + nrm(ks[9], (DEPTH, HEAD_DIM), 0.05),
        "k_norm_a": 1.0 + nrm(ks[10], (DEPTH, HEAD_DIM), 0.05),
        "rel_bias_a": nrm(ks[11], (DEPTH, A_HEADS, 2 * MAX_REL + 1), 0.1),
        "q_norm_b": 1.0 + nrm(ks[12], (DEPTH, HEAD_DIM), 0.05),
        "k_norm_b": 1.0 + nrm(ks[13], (DEPTH, HEAD_DIM), 0.05),
        "sinks_b": nrm(ks[14], (DEPTH, B_HEADS), 0.5),
        "w_out": nrm(ks[15], (DEPTH, MIX_WIDTH, D_MODEL), MIX_WIDTH ** -0.5),
        "norm_ffn": 1.0 + nrm(ks[16], (DEPTH, D_MODEL), 0.05),
        "w_gate_up": nrm(ks[17], (DEPTH, D_MODEL, 2 * D_FF), D_MODEL ** -0.5),
        "conv_w": nrm(ks[18], (DEPTH, CONV_W, D_FF), CONV_W ** -0.5),
        "conv_b": nrm(ks[19], (DEPTH, D_FF), 0.01),
        "w_down": nrm(ks[20], (DEPTH, D_FF, D_MODEL), D_FF ** -0.5),
    }


def reference(x_prompt, x_sample, cache_a_k, cache_a_v, cache_b_k, cache_b_v, state_conv,
              norm_attn, w_in, q_norm_a, k_norm_a, rel_bias_a, q_norm_b, k_norm_b, sinks_b,
              w_out, norm_ffn, w_gate_up, conv_w, conv_b, w_down):
    y_prompt = x_prompt
    y_sample = x_sample
    pos_p = jnp.arange(x_prompt.shape[1])
    pos_s = PAST_LEN + jnp.arange(x_sample.shape[1])
    zero_conv = jnp.zeros((x_prompt.shape[0], CONV_W - 1, D_FF), x_prompt.dtype)
    akp, avp, bkp, bvp, cvp = [], [], [], [], []
    aks, avs, bks, bvs, cvs = [], [], [], [], []
    for l in range(DEPTH):
        h = rms_norm(y_prompt, norm_attn[l])
        qa, ka, va, qb, kb, vb = project_heads(h, pos_p, w_in[l], q_norm_a[l], k_norm_a[l], q_norm_b[l], k_norm_b[l])
        oa = band_attn_a_prompt(qa, ka, va, rel_bias_a[l])
        ob = window_attn_b_prompt(qb, kb, vb, sinks_b[l])
        y_prompt = y_prompt + jnp.einsum("bsc,cd->bsd", jnp.concatenate([oa, ob], axis=-1), w_out[l])
        f, conv_new = conv_ffn(rms_norm(y_prompt, norm_ffn[l]), zero_conv, w_gate_up[l], conv_w[l], conv_b[l], w_down[l])
        y_prompt = y_prompt + f
        akp.append(ka[:, -A_CTX:])
        avp.append(va[:, -A_CTX:])
        bkp.append(kb[:, -B_CTX:])
        bvp.append(vb[:, -B_CTX:])
        cvp.append(conv_new)
        h = rms_norm(y_sample, norm_attn[l])
        qa, ka, va, qb, kb, vb = project_heads(h, pos_s, w_in[l], q_norm_a[l], k_norm_a[l], q_norm_b[l], k_norm_b[l])
        oa = band_attn_a_sample(qa, ka, va, cache_a_k[l], cache_a_v[l], rel_bias_a[l])
        ob = window_attn_b_sample(qb, kb, vb, cache_b_k[l], cache_b_v[l], sinks_b[l])
        y_sample = y_sample + jnp.einsum("bsc,cd->bsd", jnp.concatenate([oa, ob], axis=-1), w_out[l])
        f, conv_new = conv_ffn(rms_norm(y_sample, norm_ffn[l]), state_conv[l], w_gate_up[l], conv_w[l], conv_b[l], w_down[l])
        y_sample = y_sample + f
        aks.append(ka)
        avs.append(va)
        bks.append(kb)
        bvs.append(vb)
        cvs.append(conv_new)
    return (y_prompt, y_sample,
            jnp.stack(akp), jnp.stack(avp), jnp.stack(bkp), jnp.stack(bvp), jnp.stack(cvp),
            jnp.stack(aks), jnp.stack(avs), jnp.stack(bks), jnp.stack(bvs), jnp.stack(cvs))
```

```python
import functools

import jax
import jax.numpy as jnp
from jax import lax
from jax.experimental import pallas as pl
from jax.experimental.pallas import tpu as pltpu

D_MODEL = 1024
CHUNK = 64
HEAD_DIM = 64
EPS = 1e-6
SCALE = HEAD_DIM ** -0.5
NEG = -1e30
PAST_LEN = 1024
A_HEADS = 8
A_LEFT_CHUNKS = 8
A_CTX = A_LEFT_CHUNKS * CHUNK
A_WIN = A_CTX + CHUNK
MAX_REL = 128
A_WIDTH = A_HEADS * HEAD_DIM
B_HEADS = 8
B_KV_HEADS = 2
B_GROUP = B_HEADS // B_KV_HEADS
B_CTX = 128
B_WIN = B_CTX + CHUNK
B_WIDTH = B_HEADS * HEAD_DIM
B_KV_WIDTH = B_KV_HEADS * HEAD_DIM
ROPE_THETA = 10000.0
IN_COLS = 3 * A_WIDTH + B_WIDTH + 2 * B_KV_WIDTH
D_FF = 2816
CONV_W = 3

LANES = 128
TOKEN_TILE = 512
FF_CHUNK = 256
BIAS_VAR = 256
VMEM_LIMIT = 56 * 1024 * 1024

_BF16 = jnp.bfloat16
_F32 = jnp.float32


def _const_spec(shape):
    nd = len(shape)
    return pl.BlockSpec(shape, lambda i: (0,) * nd, pipeline_mode=pl.Buffered(1))


def _head_sumsq(t, bd):
    sq = (t * t).astype(_BF16)
    w = t.shape[1]
    if w <= 2 * LANES:
        return jnp.dot(sq, bd[:w, :w], preferred_element_type=_F32)
    parts = [jnp.dot(sq[:, c:c + 2 * LANES], bd, preferred_element_type=_F32) for c in range(0, w, 2 * LANES)]
    return jnp.concatenate(parts, axis=1)


def _head_norm(t, bd, gain):
    ss = _head_sumsq(t, bd)
    return t * lax.rsqrt(ss * (1.0 / HEAD_DIM) + EPS) * gain


def _rope(t, cos, sin_signed):
    outs = []
    for c in range(0, t.shape[1], LANES):
        tc = t[:, c:c + LANES]
        lane = lax.broadcasted_iota(jnp.int32, tc.shape, 1)
        first_half = (lane & (HEAD_DIM - 1)) < HEAD_DIM // 2
        rot = jnp.where(first_half, pltpu.roll(tc, LANES - HEAD_DIM // 2, 1), pltpu.roll(tc, HEAD_DIM // 2, 1))
        outs.append(tc * cos + rot * sin_signed)
    return outs[0] if len(outs) == 1 else jnp.concatenate(outs, axis=1)


def _rep_pair(t):
    rolled = pltpu.roll(t, HEAD_DIM, 1)
    lo = lax.broadcasted_iota(jnp.int32, t.shape, 1) < HEAD_DIM
    return jnp.concatenate([jnp.where(lo, t, rolled), jnp.where(lo, rolled, t)], axis=1)


def _proj_kernel(x_ref, gn_ref, w_ref, gqa_ref, gka_ref, gqb_ref, gkb_ref, cos_ref, sin_ref,
                 qa_ref, ka_ref, va_ref, qb_ref, kb_ref, vb_ref, kaf_ref, vaf_ref, kbf_ref, vbf_ref,
                 *, rows_a, rows_b):
    tm = x_ref.shape[0]
    x = x_ref[...]
    ms = jnp.mean(x * x, axis=-1, keepdims=True)
    h = (x * lax.rsqrt(ms + EPS) * gn_ref[...]).astype(_BF16)
    p = jnp.dot(h, w_ref[...], preferred_element_type=_F32)

    r = lax.broadcasted_iota(jnp.int32, (2 * LANES, 2 * LANES), 0) // HEAD_DIM
    c = lax.broadcasted_iota(jnp.int32, (2 * LANES, 2 * LANES), 1) // HEAD_DIM
    bd = jnp.where(r == c, 1.0, 0.0).astype(_BF16)

    cos = cos_ref[...]
    sin = sin_ref[...]
    o = 0
    qa = _head_norm(p[:, o:o + A_WIDTH], bd, gqa_ref[...]) * SCALE
    o += A_WIDTH
    ka = _head_norm(p[:, o:o + A_WIDTH], bd, gka_ref[...])
    o += A_WIDTH
    va = p[:, o:o + A_WIDTH]
    o += A_WIDTH
    qb = _rope(_head_norm(p[:, o:o + B_WIDTH], bd, gqb_ref[...]), cos, sin) * SCALE
    o += B_WIDTH
    kb = _rope(_head_norm(p[:, o:o + B_KV_WIDTH], bd, gkb_ref[...]), cos, sin)
    o += B_KV_WIDTH
    vb = p[:, o:o + B_KV_WIDTH]

    qa_ref[...] = qa.astype(_BF16)
    ka_ref[...] = ka.astype(_BF16)
    va_ref[...] = va.astype(_BF16)
    qb_ref[...] = qb.astype(_BF16)
    kb_ref[...] = _rep_pair(kb).astype(_BF16)
    vb_ref[...] = _rep_pair(vb).astype(_BF16)
    kaf_ref[...] = ka[tm - rows_a:, :]
    vaf_ref[...] = va[tm - rows_a:, :]
    kbf_ref[...] = kb[tm - rows_b:, :]
    vbf_ref[...] = vb[tm - rows_b:, :]


def _project(x, gn, w_in, gqa, gka, gqb, gkb, cos, sin, *, keep_all, name):
    t = x.shape[0]
    tm = TOKEN_TILE
    n = t // tm
    if keep_all:
        rows_a = rows_b = tm
        fa, fb = t, t
        fmap = lambda i: (i, 0)
    else:
        rows_a, rows_b = A_CTX, B_CTX
        fa, fb = A_CTX, B_CTX
        fmap = lambda i: (0, 0)
    tile = lambda w: pl.BlockSpec((tm, w), lambda i: (i, 0))
    trig = tile(LANES) if cos.shape[0] == t else _const_spec((tm, LANES))
    out_shape = [
        jax.ShapeDtypeStruct((t, A_WIDTH), _BF16), jax.ShapeDtypeStruct((t, A_WIDTH), _BF16),
        jax.ShapeDtypeStruct((t, A_WIDTH), _BF16), jax.ShapeDtypeStruct((t, B_WIDTH), _BF16),
        jax.ShapeDtypeStruct((t, 2 * B_KV_WIDTH), _BF16), jax.ShapeDtypeStruct((t, 2 * B_KV_WIDTH), _BF16),
        jax.ShapeDtypeStruct((fa, A_WIDTH), _F32), jax.ShapeDtypeStruct((fa, A_WIDTH), _F32),
        jax.ShapeDtypeStruct((fb, B_KV_WIDTH), _F32), jax.ShapeDtypeStruct((fb, B_KV_WIDTH), _F32),
    ]
    out_specs = [
        tile(A_WIDTH), tile(A_WIDTH), tile(A_WIDTH), tile(B_WIDTH), tile(2 * B_KV_WIDTH), tile(2 * B_KV_WIDTH),
        pl.BlockSpec((rows_a, A_WIDTH), fmap), pl.BlockSpec((rows_a, A_WIDTH), fmap),
        pl.BlockSpec((rows_b, B_KV_WIDTH), fmap), pl.BlockSpec((rows_b, B_KV_WIDTH), fmap),
    ]
    in_specs = [
        tile(D_MODEL), _const_spec((1, D_MODEL)), _const_spec((D_MODEL, IN_COLS)),
        _const_spec((1, A_WIDTH)), _const_spec((1, A_WIDTH)), _const_spec((1, B_WIDTH)),
        _const_spec((1, B_KV_WIDTH)), trig, trig,
    ]
    return pl.pallas_call(
        functools.partial(_proj_kernel, rows_a=rows_a, rows_b=rows_b),
        grid=(n,), in_specs=in_specs, out_specs=out_specs, out_shape=out_shape, name=name,
        compiler_params=pltpu.CompilerParams(dimension_semantics=("arbitrary",), vmem_limit_bytes=VMEM_LIMIT),
    )(x, gn, w_in, gqa, gka, gqb, gkb, cos, sin)


def _build_bias(u_ref, bias_ref):
    n_const = A_WIN - (B_WIN)
    qrow = lax.broadcasted_iota(jnp.int32, (CHUNK, BIAS_VAR), 0)
    for h in range(A_HEADS):
        prof = u_ref[h:h + 1, :]
        y = jnp.broadcast_to(prof, (CHUNK, BIAS_VAR))
        y = pltpu.roll(y, BIAS_VAR - (CHUNK - 1), 1)
        for b in range(6):
            y = jnp.where(((qrow >> b) & 1) == 1, pltpu.roll(y, 1 << b, 1), y)
        const = jnp.broadcast_to(prof[:, 0:1], (CHUNK, n_const))
        full = jnp.concatenate([const, y[:, :A_WIN - n_const]], axis=1)
        bias_ref[h // 4, (h % 4) * CHUNK:(h % 4 + 1) * CHUNK, :] = full


def _attend(qa, qb, kwa_ref, vwa_ref, kwb_ref, vwb_ref, r0, bias_ref, sink_ref, lim_a, lim_b):
    nt = (((1,), (1,)), ((), ()))
    outs = []
    head_of_lane = lax.broadcasted_iota(jnp.int32, (CHUNK, 2 * LANES), 1) // HEAD_DIM
    for g in range(2):
        qg = qa[:, 2 * LANES * g:2 * LANES * (g + 1)]
        qbd = jnp.concatenate([jnp.where(head_of_lane == r, qg, jnp.zeros_like(qg)) for r in range(4)], axis=0)
        kw = kwa_ref[pl.ds(r0, A_WIN), 2 * LANES * g:2 * LANES * (g + 1)]
        s = lax.dot_general(qbd, kw, nt, preferred_element_type=_F32) + bias_ref[g]
        if lim_a is not None:
            col = lax.broadcasted_iota(jnp.int32, s.shape, 1)
            s = jnp.where(col >= lim_a, s, NEG)
        m = jnp.max(s, axis=1, keepdims=True)
        e = jnp.exp(s - m)
        l = jnp.sum(e, axis=1, keepdims=True)
        vw = vwa_ref[pl.ds(r0, A_WIN), 2 * LANES * g:2 * LANES * (g + 1)]
        of = jnp.dot(e.astype(_BF16), vw, preferred_element_type=_F32) / l
        og = jnp.where(head_of_lane == 0, of[0:CHUNK], 0.0)
        for r in range(1, 4):
            og = og + jnp.where(head_of_lane == r, of[r * CHUNK:(r + 1) * CHUNK], 0.0)
        outs.append(og)
    lo = lax.broadcasted_iota(jnp.int32, (CHUNK, LANES), 1) < HEAD_DIM
    for g in range(B_KV_HEADS):
        rows = []
        for r in range(B_GROUP):
            hd = B_GROUP * g + r
            qp = qb[:, LANES * (hd // 2):LANES * (hd // 2 + 1)]
            keep = lo if hd % 2 == 0 else jnp.logical_not(lo)
            rows.append(jnp.where(keep, qp, jnp.zeros_like(qp)))
        qs = jnp.concatenate(rows, axis=0)
        kw = kwb_ref[pl.ds(r0, B_WIN), LANES * g:LANES * (g + 1)]
        s = lax.dot_general(qs, kw, nt, preferred_element_type=_F32)
        if lim_b is not None:
            col = lax.broadcasted_iota(jnp.int32, s.shape, 1)
            s = jnp.where(col >= lim_b, s, NEG)
        sk = sink_ref[g]
        m = jnp.maximum(jnp.max(s, axis=1, keepdims=True), sk)
        e = jnp.exp(s - m)
        l = jnp.sum(e, axis=1, keepdims=True) + jnp.exp(sk - m)
        vw = vwb_ref[pl.ds(r0, B_WIN), LANES * g:LANES * (g + 1)]
        of = jnp.dot(e.astype(_BF16), vw, preferred_element_type=_F32) / l
        outs.append(jnp.where(lo, of[0:CHUNK], of[CHUNK:2 * CHUNK]))
        outs.append(jnp.where(lo, of[2 * CHUNK:3 * CHUNK], of[3 * CHUNK:4 * CHUNK]))
    return jnp.concatenate(outs, axis=1)


def _attn_prompt_kernel(qa_ref, ka_ref, va_ref, qb_ref, kb_ref, vb_ref, u_ref, sink_ref, o_ref,
                        kwa, vwa, kwb, vwb, bias):
    i = pl.program_id(0)
    tm = qa_ref.shape[0]
    n_chunks = tm // CHUNK

    @pl.when(i == 0)
    def _():
        kwa[0:A_CTX, :] = jnp.zeros((A_CTX, A_WIDTH), _BF16)
        vwa[0:A_CTX, :] = jnp.zeros((A_CTX, A_WIDTH), _BF16)
        kwb[0:B_CTX, :] = jnp.zeros((B_CTX, 2 * B_KV_WIDTH), _BF16)
        vwb[0:B_CTX, :] = jnp.zeros((B_CTX, 2 * B_KV_WIDTH), _BF16)
        _build_bias(u_ref, bias)

    @pl.when(i > 0)
    def _():
        kwa[0:A_CTX, :] = kwa[tm:tm + A_CTX, :]
        vwa[0:A_CTX, :] = vwa[tm:tm + A_CTX, :]
        kwb[0:B_CTX, :] = kwb[tm:tm + B_CTX, :]
        vwb[0:B_CTX, :] = vwb[tm:tm + B_CTX, :]

    kwa[A_CTX:A_CTX + tm, :] = ka_ref[...]
    vwa[A_CTX:A_CTX + tm, :] = va_ref[...]
    kwb[B_CTX:B_CTX + tm, :] = kb_ref[...]
    vwb[B_CTX:B_CTX + tm, :] = vb_ref[...]

    def chunk(cc, masked):
        r0 = pl.multiple_of(cc * CHUNK, CHUNK)
        qa = qa_ref[pl.ds(r0, CHUNK), :]
        qb = qb_ref[pl.ds(r0, CHUNK), :]
        lim_a = (A_LEFT_CHUNKS - cc) * CHUNK if masked else None
        lim_b = jnp.maximum(B_CTX // CHUNK - cc, 0) * CHUNK if masked else None
        o = _attend(qa, qb, kwa, vwa, kwb, vwb, r0, bias, sink_ref, lim_a, lim_b)
        o_ref[pl.ds(r0, CHUNK), :] = o.astype(_BF16)

    @pl.when(i == 0)
    def _():
        def body(cc, carry):
            chunk(cc, True)
            return carry
        lax.fori_loop(0, n_chunks, body, 0)

    @pl.when(i > 0)
    def _():
        def body(cc, carry):
            chunk(cc, False)
            return carry
        lax.fori_loop(0, n_chunks, body, 0)


def _attn_sample_kernel(qa_ref, ka_ref, va_ref, qb_ref, kb_ref, vb_ref, cak_ref, cav_ref, cbk_ref, cbv_ref,
                        u_ref, sink_ref, o_ref, kwa, vwa, kwb, vwb, bias):
    @pl.when(pl.program_id(0) == 0)
    def _():
        _build_bias(u_ref, bias)

    kwa[0:A_CTX, :] = cak_ref[0].astype(_BF16)
    vwa[0:A_CTX, :] = cav_ref[0].astype(_BF16)
    kwb[0:B_CTX, :] = _rep_pair(cbk_ref[0]).astype(_BF16)
    vwb[0:B_CTX, :] = _rep_pair(cbv_ref[0]).astype(_BF16)
    kwa[A_CTX:A_WIN, :] = ka_ref[...]
    vwa[A_CTX:A_WIN, :] = va_ref[...]
    kwb[B_CTX:B_WIN, :] = kb_ref[...]
    vwb[B_CTX:B_WIN, :] = vb_ref[...]
    o = _attend(qa_ref[...], qb_ref[...], kwa, vwa, kwb, vwb, 0, bias, sink_ref, None, None)
    o_ref[...] = o.astype(_BF16)


def _attn_scratch(rows_a, rows_b):
    return [
        pltpu.VMEM((rows_a, A_WIDTH), _BF16), pltpu.VMEM((rows_a, A_WIDTH), _BF16),
        pltpu.VMEM((rows_b, 2 * B_KV_WIDTH), _BF16), pltpu.VMEM((rows_b, 2 * B_KV_WIDTH), _BF16),
        pltpu.VMEM((2, 4 * CHUNK, A_WIN), _F32),
    ]


def _attend_prompt(qa, ka, va, qb, kb, vb, u, sink):
    t = qa.shape[0]
    tm = TOKEN_TILE
    tile = lambda w: pl.BlockSpec((tm, w), lambda i: (i, 0))
    return pl.pallas_call(
        _attn_prompt_kernel,
        grid=(t // tm,),
        in_specs=[tile(A_WIDTH), tile(A_WIDTH), tile(A_WIDTH), tile(B_WIDTH), tile(2 * B_KV_WIDTH),
                  tile(2 * B_KV_WIDTH), _const_spec(u.shape), _const_spec(sink.shape)],
        out_specs=tile(A_WIDTH + B_WIDTH),
        out_shape=jax.ShapeDtypeStruct((t, A_WIDTH + B_WIDTH), _BF16),
        scratch_shapes=_attn_scratch(A_CTX + tm, B_CTX + tm),
        name="attn_prompt",
        compiler_params=pltpu.CompilerParams(dimension_semantics=("arbitrary",), vmem_limit_bytes=VMEM_LIMIT),
    )(qa, ka, va, qb, kb, vb, u, sink)


def _attend_sample(qa, ka, va, qb, kb, vb, cak, cav, cbk, cbv, u, sink):
    t = qa.shape[0]
    nb = t // CHUNK
    tile = lambda w: pl.BlockSpec((CHUNK, w), lambda i: (i, 0))
    cache = lambda r, w: pl.BlockSpec((1, r, w), lambda i: (i, 0, 0))
    return pl.pallas_call(
        _attn_sample_kernel,
        grid=(nb,),
        in_specs=[tile(A_WIDTH), tile(A_WIDTH), tile(A_WIDTH), tile(B_WIDTH), tile(2 * B_KV_WIDTH),
                  tile(2 * B_KV_WIDTH), cache(A_CTX, A_WIDTH), cache(A_CTX, A_WIDTH),
                  cache(B_CTX, B_KV_WIDTH), cache(B_CTX, B_KV_WIDTH),
                  _const_spec(u.shape), _const_spec(sink.shape)],
        out_specs=tile(A_WIDTH + B_WIDTH),
        out_shape=jax.ShapeDtypeStruct((t, A_WIDTH + B_WIDTH), _BF16),
        scratch_shapes=_attn_scratch(A_WIN, B_WIN),
        name="attn_sample",
        compiler_params=pltpu.CompilerParams(dimension_semantics=("arbitrary",), vmem_limit_bytes=VMEM_LIMIT),
    )(qa, ka, va, qb, kb, vb, cak, cav, cbk, cbv, u, sink)


def _ffn_kernel(o_ref, x_ref, wo_ref, gn_ref, wgu_ref, cw_ref, cb_ref, wd_ref, st_ref,
                y_ref, cn_ref, carry, *, seg_len, carry_rows):
    i = pl.program_id(0)
    tm = x_ref.shape[0]
    n_seg = tm // seg_len

    if carry_rows:
        @pl.when(i == 0)
        def _():
            carry[...] = st_ref[0]

    y1 = x_ref[...] + jnp.dot(o_ref[...], wo_ref[...], preferred_element_type=_F32)
    ms = jnp.mean(y1 * y1, axis=-1, keepdims=True)
    h = (y1 * lax.rsqrt(ms + EPS) * gn_ref[...]).astype(_BF16)

    row8 = lax.broadcasted_iota(jnp.int32, (8, FF_CHUNK), 0)
    acc = y1
    for c0 in range(0, D_FF, FF_CHUNK):
        cols = slice(c0, c0 + FF_CHUNK)
        g = jnp.dot(h, wgu_ref[:, c0:c0 + FF_CHUNK], preferred_element_type=_F32)
        up = jnp.dot(h, wgu_ref[:, D_FF + c0:D_FF + c0 + FF_CHUNK], preferred_element_type=_F32)
        g1 = pltpu.roll(g, 1, 0)
        g2 = pltpu.roll(g, 2, 0)
        p1, p2 = [], []
        for s in range(n_seg):
            a = s * seg_len
            prev = carry[:, cols] if carry_rows else st_ref[s, :, cols]
            prev0 = jnp.broadcast_to(prev[0:1, :], (8, FF_CHUNK))
            prev1 = jnp.broadcast_to(prev[1:2, :], (8, FF_CHUNK))
            p1.append(jnp.where(row8 == 0, prev1, g1[a:a + 8]))
            p1.append(g1[a + 8:a + seg_len])
            p2.append(jnp.where(row8 == 0, prev0, jnp.where(row8 == 1, prev1, g2[a:a + 8])))
            p2.append(g2[a + 8:a + seg_len])
            tail = g[a + seg_len - 8:a + seg_len][8 - (CONV_W - 1):, :]
            if carry_rows:
                carry[:, cols] = tail
            else:
                cn_ref[s, :, cols] = tail
        g1 = jnp.concatenate(p1, axis=0)
        g2 = jnp.concatenate(p2, axis=0)
        conv = cb_ref[:, cols] + cw_ref[0:1, cols] * g2 + cw_ref[1:2, cols] * g1 + cw_ref[2:3, cols] * g
        act = conv * jax.nn.sigmoid(conv) * up
        acc = acc + jnp.dot(act.astype(_BF16), wd_ref[c0:c0 + FF_CHUNK, :], preferred_element_type=_F32)
    y_ref[...] = acc
    if carry_rows:
        cn_ref[0] = carry[...]


def _out_ffn(o, x, w_out, gn, w_gu, conv_w, conv_b, w_down, state, *, carry_rows, name):
    t = x.shape[0]
    tm = TOKEN_TILE
    n = t // tm
    seg_len = tm if carry_rows else CHUNK
    n_seg = tm // seg_len
    tile = lambda w: pl.BlockSpec((tm, w), lambda i: (i, 0))
    if carry_rows:
        st_spec = _const_spec((1, CONV_W - 1, D_FF))
        cn_spec = pl.BlockSpec((1, CONV_W - 1, D_FF), lambda i: (0, 0, 0))
        cn_shape = (1, CONV_W - 1, D_FF)
    else:
        st_spec = pl.BlockSpec((n_seg, CONV_W - 1, D_FF), lambda i: (i, 0, 0))
        cn_spec = pl.BlockSpec((n_seg, CONV_W - 1, D_FF), lambda i: (i, 0, 0))
        cn_shape = (t // seg_len, CONV_W - 1, D_FF)
    return pl.pallas_call(
        functools.partial(_ffn_kernel, seg_len=seg_len, carry_rows=carry_rows),
        grid=(n,),
        in_specs=[tile(A_WIDTH + B_WIDTH), tile(D_MODEL), _const_spec((D_MODEL, D_MODEL)),
                  _const_spec((1, D_MODEL)), _const_spec((D_MODEL, 2 * D_FF)), _const_spec((CONV_W, D_FF)),
                  _const_spec((1, D_FF)), _const_spec((D_FF, D_MODEL)), st_spec],
        out_specs=[tile(D_MODEL), cn_spec],
        out_shape=[jax.ShapeDtypeStruct((t, D_MODEL), _F32), jax.ShapeDtypeStruct(cn_shape, _F32)],
        scratch_shapes=[pltpu.VMEM((CONV_W - 1, D_FF), _F32)],
        name=name,
        compiler_params=pltpu.CompilerParams(dimension_semantics=("arbitrary",), vmem_limit_bytes=VMEM_LIMIT),
    )(o, x, w_out, gn, w_gu, conv_w, conv_b, w_down, state)


def _rope_tables(pos):
    half = HEAD_DIM // 2
    inv = jnp.power(ROPE_THETA, -jnp.arange(half, dtype=_F32) / half)
    ang = pos.astype(_F32)[:, None] * inv[None, :]
    cos = jnp.cos(ang)
    sin = jnp.sin(ang)
    return jnp.tile(cos, (1, LANES // half)), jnp.tile(jnp.concatenate([-sin, sin], axis=1), (1, LANES // HEAD_DIM))


def _bias_profile(table):
    n_clip = B_WIN - 1 - MAX_REL
    head = jnp.broadcast_to(table[:, 2 * MAX_REL:], (A_HEADS, n_clip))
    return jnp.concatenate([head, table[:, ::-1][:, :BIAS_VAR - n_clip]], axis=1)


def kernel(x_prompt, x_sample, cache_a_k, cache_a_v, cache_b_k, cache_b_v, state_conv, norm_attn, w_in, q_norm_a,
           k_norm_a, rel_bias_a, q_norm_b, k_norm_b, sinks_b, w_out, norm_ffn, w_gate_up, conv_w, conv_b, w_down):
    assert norm_attn.shape[0] == 1
    bsz, seq, _ = x_prompt.shape
    dec_b, dec_s, _ = x_sample.shape
    assert bsz == 1 and dec_s == CHUNK and seq % TOKEN_TILE == 0 and (dec_b * dec_s) % TOKEN_TILE == 0
    assert cache_a_k.shape[2] == A_CTX and cache_b_k.shape[2] == B_CTX

    w_in_b = w_in[0].astype(_BF16)
    w_out_b = w_out[0].astype(_BF16)
    w_gu_b = w_gate_up[0].astype(_BF16)
    w_down_b = w_down[0].astype(_BF16)
    gn_a = norm_attn[0][None, :]
    gn_f = norm_ffn[0][None, :]
    gqa = jnp.tile(q_norm_a[0], A_HEADS)[None, :]
    gka = jnp.tile(k_norm_a[0], A_HEADS)[None, :]
    gqb = jnp.tile(q_norm_b[0], B_HEADS)[None, :]
    gkb = jnp.tile(k_norm_b[0], B_KV_HEADS)[None, :]
    u = _bias_profile(rel_bias_a[0])
    sink = jnp.repeat(sinks_b[0], CHUNK).reshape(B_KV_HEADS, B_GROUP * CHUNK, 1)
    cw = conv_w[0]
    cb = conv_b[0][None, :]

    cos_p, sin_p = _rope_tables(jnp.arange(seq))
    cos_s, sin_s = _rope_tables(PAST_LEN + jnp.arange(dec_s))
    reps = TOKEN_TILE // dec_s
    cos_s, sin_s = jnp.tile(cos_s, (reps, 1)), jnp.tile(sin_s, (reps, 1))

    xp = x_prompt.reshape(seq, D_MODEL)
    qa, ka, va, qb, kb, vb, kaf, vaf, kbf, vbf = _project(
        xp, gn_a, w_in_b, gqa, gka, gqb, gkb, cos_p, sin_p, keep_all=False, name="proj_prompt")
    o_p = _attend_prompt(qa, ka, va, qb, kb, vb, u, sink)
    zero_state = jnp.zeros((1, CONV_W - 1, D_FF), _F32)
    y_p, cn_p = _out_ffn(o_p, xp, w_out_b, gn_f, w_gu_b, cw, cb, w_down_b, zero_state,
                         carry_rows=True, name="ffn_prompt")

    xs = x_sample.reshape(dec_b * dec_s, D_MODEL)
    qa, ka, va, qb, kb, vb, kaf_s, vaf_s, kbf_s, vbf_s = _project(
        xs, gn_a, w_in_b, gqa, gka, gqb, gkb, cos_s, sin_s, keep_all=True, name="proj_sample")
    cak = cache_a_k[0].reshape(dec_b, A_CTX, A_WIDTH)
    cav = cache_a_v[0].reshape(dec_b, A_CTX, A_WIDTH)
    cbk = cache_b_k[0].reshape(dec_b, B_CTX, B_KV_WIDTH)
    cbv = cache_b_v[0].reshape(dec_b, B_CTX, B_KV_WIDTH)
    o_s = _attend_sample(qa, ka, va, qb, kb, vb, cak, cav, cbk, cbv, u, sink)
    y_s, cn_s = _out_ffn(o_s, xs, w_out_b, gn_f, w_gu_b, cw, cb, w_down_b, state_conv[0],
                         carry_rows=False, name="ffn_sample")

    return (
        y_p.reshape(1, seq, D_MODEL),
        y_s.reshape(dec_b, dec_s, D_MODEL),
        kaf.reshape(1, 1, A_CTX, A_HEADS, HEAD_DIM),
        vaf.reshape(1, 1, A_CTX, A_HEADS, HEAD_DIM),
        kbf.reshape(1, 1, B_CTX, B_KV_HEADS, HEAD_DIM),
        vbf.reshape(1, 1, B_CTX, B_KV_HEADS, HEAD_DIM),
        cn_p.reshape(1, 1, CONV_W - 1, D_FF),
        kaf_s.reshape(1, dec_b, dec_s, A_HEADS, HEAD_DIM),
        vaf_s.reshape(1, dec_b, dec_s, A_HEADS, HEAD_DIM),
        kbf_s.reshape(1, dec_b, dec_s, B_KV_HEADS, HEAD_DIM),
        vbf_s.reshape(1, dec_b, dec_s, B_KV_HEADS, HEAD_DIM),
        cn_s.reshape(1, dec_b, CONV_W - 1, D_FF),
    )
```

```python
import functools

import jax
import jax.numpy as jnp
from jax import lax
from jax.experimental import pallas as pl
from jax.experimental.pallas import tpu as pltpu

D_MODEL = 1024
CHUNK = 64
HEAD_DIM = 64
EPS = 1e-6
SCALE = HEAD_DIM ** -0.5
NEG = -1e30
PAST_LEN = 1024
A_HEADS = 8
A_LEFT_CHUNKS = 8
A_CTX = A_LEFT_CHUNKS * CHUNK
A_WIN = A_CTX + CHUNK
MAX_REL = 128
A_WIDTH = A_HEADS * HEAD_DIM
B_HEADS = 8
B_KV_HEADS = 2
B_GROUP = B_HEADS // B_KV_HEADS
B_CTX = 128
B_WIN = B_CTX + CHUNK
B_WIDTH = B_HEADS * HEAD_DIM
B_KV_WIDTH = B_KV_HEADS * HEAD_DIM
ROPE_THETA = 10000.0
IN_COLS = 3 * A_WIDTH + B_WIDTH + 2 * B_KV_WIDTH
D_FF = 2816
CONV_W = 3

LANES = 128
TOKEN_TILE = 512
SAMPLE_STREAMS = 4
FF_CHUNK = 256
BIAS_VAR = 256
VMEM_LIMIT = 56 * 1024 * 1024

_BF16 = jnp.bfloat16
_F32 = jnp.float32


def _const_spec(shape):
    nd = len(shape)
    return pl.BlockSpec(shape, lambda i: (0,) * nd, pipeline_mode=pl.Buffered(1))


def _head_sumsq(t, bd):
    sq = (t * t).astype(_BF16)
    w = t.shape[1]
    if w <= 2 * LANES:
        return jnp.dot(sq, bd[:w, :w], preferred_element_type=_F32)
    parts = [jnp.dot(sq[:, c:c + 2 * LANES], bd, preferred_element_type=_F32) for c in range(0, w, 2 * LANES)]
    return jnp.concatenate(parts, axis=1)


def _head_norm(t, bd, gain):
    ss = _head_sumsq(t, bd)
    return t * lax.rsqrt(ss * (1.0 / HEAD_DIM) + EPS) * gain


def _rope(t, cos, sin_signed):
    outs = []
    for c in range(0, t.shape[1], LANES):
        tc = t[:, c:c + LANES]
        lane = lax.broadcasted_iota(jnp.int32, tc.shape, 1)
        first_half = (lane & (HEAD_DIM - 1)) < HEAD_DIM // 2
        rot = jnp.where(first_half, pltpu.roll(tc, LANES - HEAD_DIM // 2, 1), pltpu.roll(tc, HEAD_DIM // 2, 1))
        outs.append(tc * cos + rot * sin_signed)
    return outs[0] if len(outs) == 1 else jnp.concatenate(outs, axis=1)


def _rep_pair(t):
    rolled = pltpu.roll(t, HEAD_DIM, 1)
    lo = lax.broadcasted_iota(jnp.int32, t.shape, 1) < HEAD_DIM
    return jnp.concatenate([jnp.where(lo, t, rolled), jnp.where(lo, rolled, t)], axis=1)


def _proj_kernel(x_ref, gn_ref, w_ref, gqa_ref, gka_ref, gqb_ref, gkb_ref, cos_ref, sin_ref,
                 qa_ref, ka_ref, va_ref, qb_ref, kb_ref, vb_ref, kaf_ref, vaf_ref, kbf_ref, vbf_ref,
                 *, rows_a, rows_b):
    tm = x_ref.shape[0]
    x = x_ref[...]
    ms = jnp.mean(x * x, axis=-1, keepdims=True)
    h = (x * lax.rsqrt(ms + EPS) * gn_ref[...]).astype(_BF16)

    r = lax.broadcasted_iota(jnp.int32, (2 * LANES, 2 * LANES), 0) // HEAD_DIM
    c = lax.broadcasted_iota(jnp.int32, (2 * LANES, 2 * LANES), 1) // HEAD_DIM
    bd = jnp.where(r == c, 1.0, 0.0).astype(_BF16)
    cos = cos_ref[...]
    sin = sin_ref[...]

    def proj(lo, width):
        return jnp.dot(h, w_ref[:, lo:lo + width], preferred_element_type=_F32)

    o_ka, o_va, o_qb, o_kvb = A_WIDTH, 2 * A_WIDTH, 3 * A_WIDTH, 3 * A_WIDTH + B_WIDTH
    p_qa = proj(0, A_WIDTH)
    p_ka = proj(o_ka, A_WIDTH)
    qa_ref[...] = (_head_norm(p_qa, bd, gqa_ref[...]) * SCALE).astype(_BF16)
    va = proj(o_va, A_WIDTH)
    ka = _head_norm(p_ka, bd, gka_ref[...])
    ka_ref[...] = ka.astype(_BF16)
    kaf_ref[...] = ka[tm - rows_a:, :]
    p_qb = proj(o_qb, B_WIDTH)
    va_ref[...] = va.astype(_BF16)
    vaf_ref[...] = va[tm - rows_a:, :]
    p_kvb = proj(o_kvb, 2 * B_KV_WIDTH)
    qb_ref[...] = (_rope(_head_norm(p_qb, bd, gqb_ref[...]), cos, sin) * SCALE).astype(_BF16)
    kb = _rope(_head_norm(p_kvb[:, :B_KV_WIDTH], bd, gkb_ref[...]), cos, sin)
    vb = p_kvb[:, B_KV_WIDTH:]
    kb_ref[...] = _rep_pair(kb).astype(_BF16)
    vb_ref[...] = _rep_pair(vb).astype(_BF16)
    kbf_ref[...] = kb[tm - rows_b:, :]
    vbf_ref[...] = vb[tm - rows_b:, :]


def _project(x, gn, w_in, gqa, gka, gqb, gkb, cos, sin, *, keep_all, name):
    t = x.shape[0]
    tm = TOKEN_TILE
    n = t // tm
    if keep_all:
        rows_a = rows_b = tm
        fa, fb = t, t
        fmap = lambda i: (i, 0)
    else:
        rows_a, rows_b = A_CTX, B_CTX
        fa, fb = A_CTX, B_CTX
        fmap = lambda i: (0, 0)
    tile = lambda w: pl.BlockSpec((tm, w), lambda i: (i, 0))
    trig = tile(LANES) if cos.shape[0] == t else _const_spec((tm, LANES))
    out_shape = [
        jax.ShapeDtypeStruct((t, A_WIDTH), _BF16), jax.ShapeDtypeStruct((t, A_WIDTH), _BF16),
        jax.ShapeDtypeStruct((t, A_WIDTH), _BF16), jax.ShapeDtypeStruct((t, B_WIDTH), _BF16),
        jax.ShapeDtypeStruct((t, 2 * B_KV_WIDTH), _BF16), jax.ShapeDtypeStruct((t, 2 * B_KV_WIDTH), _BF16),
        jax.ShapeDtypeStruct((fa, A_WIDTH), _F32), jax.ShapeDtypeStruct((fa, A_WIDTH), _F32),
        jax.ShapeDtypeStruct((fb, B_KV_WIDTH), _F32), jax.ShapeDtypeStruct((fb, B_KV_WIDTH), _F32),
    ]
    out_specs = [
        tile(A_WIDTH), tile(A_WIDTH), tile(A_WIDTH), tile(B_WIDTH), tile(2 * B_KV_WIDTH), tile(2 * B_KV_WIDTH),
        pl.BlockSpec((rows_a, A_WIDTH), fmap), pl.BlockSpec((rows_a, A_WIDTH), fmap),
        pl.BlockSpec((rows_b, B_KV_WIDTH), fmap), pl.BlockSpec((rows_b, B_KV_WIDTH), fmap),
    ]
    in_specs = [
        tile(D_MODEL), _const_spec((1, D_MODEL)), _const_spec((D_MODEL, IN_COLS)),
        _const_spec((1, A_WIDTH)), _const_spec((1, A_WIDTH)), _const_spec((1, B_WIDTH)),
        _const_spec((1, B_KV_WIDTH)), trig, trig,
    ]
    return pl.pallas_call(
        functools.partial(_proj_kernel, rows_a=rows_a, rows_b=rows_b),
        grid=(n,), in_specs=in_specs, out_specs=out_specs, out_shape=out_shape, name=name,
        compiler_params=pltpu.CompilerParams(dimension_semantics=("arbitrary",), vmem_limit_bytes=VMEM_LIMIT),
    )(x, gn, w_in, gqa, gka, gqb, gkb, cos, sin)


def _build_bias(u_ref, bias_ref):
    n_const = A_WIN - (B_WIN)
    qrow = lax.broadcasted_iota(jnp.int32, (CHUNK, BIAS_VAR), 0)
    for h in range(A_HEADS):
        prof = u_ref[h:h + 1, :]
        y = jnp.broadcast_to(prof, (CHUNK, BIAS_VAR))
        y = pltpu.roll(y, BIAS_VAR - (CHUNK - 1), 1)
        for b in range(6):
            y = jnp.where(((qrow >> b) & 1) == 1, pltpu.roll(y, 1 << b, 1), y)
        const = jnp.broadcast_to(prof[:, 0:1], (CHUNK, n_const))
        full = jnp.concatenate([const, y[:, :A_WIN - n_const]], axis=1)
        bias_ref[h // 4, (h % 4) * CHUNK:(h % 4 + 1) * CHUNK, :] = full


def _scores(qa, qb, kwa_ref, kwb_ref, r0, bias_ref, lim_a, lim_b):
    nt = (((1,), (1,)), ((), ()))
    out = []
    head_of_lane = lax.broadcasted_iota(jnp.int32, (CHUNK, 2 * LANES), 1) // HEAD_DIM
    for g in range(2):
        qg = qa[:, 2 * LANES * g:2 * LANES * (g + 1)]
        qbd = jnp.concatenate([jnp.where(head_of_lane == r, qg, jnp.zeros_like(qg)) for r in range(4)], axis=0)
        kw = kwa_ref[pl.ds(r0, A_WIN), 2 * LANES * g:2 * LANES * (g + 1)]
        s = lax.dot_general(qbd, kw, nt, preferred_element_type=_F32) + bias_ref[g]
        if lim_a:
            col = lax.broadcasted_iota(jnp.int32, s.shape, 1)
            s = jnp.where(col >= lim_a, s, NEG)
        out.append(s)
    lo = lax.broadcasted_iota(jnp.int32, (CHUNK, LANES), 1) < HEAD_DIM
    for g in range(B_KV_HEADS):
        rows = []
        for r in range(B_GROUP):
            hd = B_GROUP * g + r
            qp = qb[:, LANES * (hd // 2):LANES * (hd // 2 + 1)]
            keep = lo if hd % 2 == 0 else jnp.logical_not(lo)
            rows.append(jnp.where(keep, qp, jnp.zeros_like(qp)))
        qs = jnp.concatenate(rows, axis=0)
        kw = kwb_ref[pl.ds(r0, B_WIN), LANES * g:LANES * (g + 1)]
        s = lax.dot_general(qs, kw, nt, preferred_element_type=_F32)
        if lim_b:
            col = lax.broadcasted_iota(jnp.int32, s.shape, 1)
            s = jnp.where(col >= lim_b, s, NEG)
        out.append(s)
    return out


def _softmax_pv(scores, vwa_ref, vwb_ref, r0, sink_ref):
    outs = []
    head_of_lane = lax.broadcasted_iota(jnp.int32, (CHUNK, 2 * LANES), 1) // HEAD_DIM
    for g in range(2):
        s = scores[g]
        m = jnp.max(s, axis=1, keepdims=True)
        e = jnp.exp(s - m)
        l = jnp.sum(e, axis=1, keepdims=True)
        vw = vwa_ref[pl.ds(r0, A_WIN), 2 * LANES * g:2 * LANES * (g + 1)]
        of = jnp.dot(e.astype(_BF16), vw, preferred_element_type=_F32) / l
        og = jnp.where(head_of_lane == 0, of[0:CHUNK], 0.0)
        for r in range(1, 4):
            og = og + jnp.where(head_of_lane == r, of[r * CHUNK:(r + 1) * CHUNK], 0.0)
        outs.append(og)
    lo = lax.broadcasted_iota(jnp.int32, (CHUNK, LANES), 1) < HEAD_DIM
    for g in range(B_KV_HEADS):
        s = scores[2 + g]
        sk = sink_ref[g]
        m = jnp.maximum(jnp.max(s, axis=1, keepdims=True), sk)
        e = jnp.exp(s - m)
        l = jnp.sum(e, axis=1, keepdims=True) + jnp.exp(sk - m)
        vw = vwb_ref[pl.ds(r0, B_WIN), LANES * g:LANES * (g + 1)]
        of = jnp.dot(e.astype(_BF16), vw, preferred_element_type=_F32) / l
        outs.append(jnp.where(lo, of[0:CHUNK], of[CHUNK:2 * CHUNK]))
        outs.append(jnp.where(lo, of[2 * CHUNK:3 * CHUNK], of[3 * CHUNK:4 * CHUNK]))
    return jnp.concatenate(outs, axis=1)


def _attn_prompt_kernel(qa_ref, ka_ref, va_ref, qb_ref, kb_ref, vb_ref, u_ref, sink_ref, o_ref,
                        kwa, vwa, kwb, vwb, bias):
    i = pl.program_id(0)
    tm = qa_ref.shape[0]
    n_chunks = tm // CHUNK

    @pl.when(i == 0)
    def _():
        kwa[0:A_CTX, :] = jnp.zeros((A_CTX, A_WIDTH), _BF16)
        vwa[0:A_CTX, :] = jnp.zeros((A_CTX, A_WIDTH), _BF16)
        kwb[0:B_CTX, :] = jnp.zeros((B_CTX, 2 * B_KV_WIDTH), _BF16)
        vwb[0:B_CTX, :] = jnp.zeros((B_CTX, 2 * B_KV_WIDTH), _BF16)
        _build_bias(u_ref, bias)

    @pl.when(i > 0)
    def _():
        kwa[0:A_CTX, :] = kwa[tm:tm + A_CTX, :]
        vwa[0:A_CTX, :] = vwa[tm:tm + A_CTX, :]
        kwb[0:B_CTX, :] = kwb[tm:tm + B_CTX, :]
        vwb[0:B_CTX, :] = vwb[tm:tm + B_CTX, :]

    kwa[A_CTX:A_CTX + tm, :] = ka_ref[...]
    vwa[A_CTX:A_CTX + tm, :] = va_ref[...]
    kwb[B_CTX:B_CTX + tm, :] = kb_ref[...]
    vwb[B_CTX:B_CTX + tm, :] = vb_ref[...]

    def run(masked):
        def scores(c):
            r0 = c * CHUNK
            lim_a = (A_LEFT_CHUNKS - c) * CHUNK if masked else 0
            lim_b = max(B_CTX // CHUNK - c, 0) * CHUNK if masked else 0
            return _scores(qa_ref[r0:r0 + CHUNK, :], qb_ref[r0:r0 + CHUNK, :], kwa, kwb, r0, bias, lim_a, lim_b)

        nxt = scores(0)
        for c in range(n_chunks):
            cur = nxt
            if c + 1 < n_chunks:
                nxt = scores(c + 1)
            o = _softmax_pv(cur, vwa, vwb, c * CHUNK, sink_ref)
            o_ref[c * CHUNK:(c + 1) * CHUNK, :] = o.astype(_BF16)

    @pl.when(i == 0)
    def _():
        run(True)

    @pl.when(i > 0)
    def _():
        run(False)


def _attn_sample_kernel(qa_ref, ka_ref, va_ref, qb_ref, kb_ref, vb_ref, cak_ref, cav_ref, cbk_ref, cbv_ref,
                        u_ref, sink_ref, o_ref, kwa, vwa, kwb, vwb, bias):
    n_streams = cak_ref.shape[0]

    @pl.when(pl.program_id(0) == 0)
    def _():
        _build_bias(u_ref, bias)

    for b in range(n_streams):
        rows = slice(b * CHUNK, (b + 1) * CHUNK)
        kwa[b, 0:A_CTX, :] = cak_ref[b].astype(_BF16)
        vwa[b, 0:A_CTX, :] = cav_ref[b].astype(_BF16)
        kwb[b, 0:B_CTX, :] = _rep_pair(cbk_ref[b]).astype(_BF16)
        vwb[b, 0:B_CTX, :] = _rep_pair(cbv_ref[b]).astype(_BF16)
        kwa[b, A_CTX:A_WIN, :] = ka_ref[rows, :]
        vwa[b, A_CTX:A_WIN, :] = va_ref[rows, :]
        kwb[b, B_CTX:B_WIN, :] = kb_ref[rows, :]
        vwb[b, B_CTX:B_WIN, :] = vb_ref[rows, :]

    def scores(b):
        rows = slice(b * CHUNK, (b + 1) * CHUNK)
        return _scores(qa_ref[rows, :], qb_ref[rows, :], kwa.at[b], kwb.at[b], 0, bias, 0, 0)

    nxt = scores(0)
    for b in range(n_streams):
        cur = nxt
        if b + 1 < n_streams:
            nxt = scores(b + 1)
        o = _softmax_pv(cur, vwa.at[b], vwb.at[b], 0, sink_ref)
        o_ref[b * CHUNK:(b + 1) * CHUNK, :] = o.astype(_BF16)


def _attn_scratch(lead, rows_a, rows_b):
    return [
        pltpu.VMEM(lead + (rows_a, A_WIDTH), _BF16), pltpu.VMEM(lead + (rows_a, A_WIDTH), _BF16),
        pltpu.VMEM(lead + (rows_b, 2 * B_KV_WIDTH), _BF16), pltpu.VMEM(lead + (rows_b, 2 * B_KV_WIDTH), _BF16),
        pltpu.VMEM((2, 4 * CHUNK, A_WIN), _F32),
    ]


def _attend_prompt(qa, ka, va, qb, kb, vb, u, sink):
    t = qa.shape[0]
    tm = TOKEN_TILE
    tile = lambda w: pl.BlockSpec((tm, w), lambda i: (i, 0))
    return pl.pallas_call(
        _attn_prompt_kernel,
        grid=(t // tm,),
        in_specs=[tile(A_WIDTH), tile(A_WIDTH), tile(A_WIDTH), tile(B_WIDTH), tile(2 * B_KV_WIDTH),
                  tile(2 * B_KV_WIDTH), _const_spec(u.shape), _const_spec(sink.shape)],
        out_specs=tile(A_WIDTH + B_WIDTH),
        out_shape=jax.ShapeDtypeStruct((t, A_WIDTH + B_WIDTH), _BF16),
        scratch_shapes=_attn_scratch((), A_CTX + tm, B_CTX + tm),
        name="attn_prompt",
        compiler_params=pltpu.CompilerParams(dimension_semantics=("arbitrary",), vmem_limit_bytes=VMEM_LIMIT),
    )(qa, ka, va, qb, kb, vb, u, sink)


def _attend_sample(qa, ka, va, qb, kb, vb, cak, cav, cbk, cbv, u, sink):
    t = qa.shape[0]
    sb = SAMPLE_STREAMS
    tile = lambda w: pl.BlockSpec((sb * CHUNK, w), lambda i: (i, 0))
    cache = lambda r, w: pl.BlockSpec((sb, r, w), lambda i: (i, 0, 0))
    return pl.pallas_call(
        _attn_sample_kernel,
        grid=(t // (sb * CHUNK),),
        in_specs=[tile(A_WIDTH), tile(A_WIDTH), tile(A_WIDTH), tile(B_WIDTH), tile(2 * B_KV_WIDTH),
                  tile(2 * B_KV_WIDTH), cache(A_CTX, A_WIDTH), cache(A_CTX, A_WIDTH),
                  cache(B_CTX, B_KV_WIDTH), cache(B_CTX, B_KV_WIDTH),
                  _const_spec(u.shape), _const_spec(sink.shape)],
        out_specs=tile(A_WIDTH + B_WIDTH),
        out_shape=jax.ShapeDtypeStruct((t, A_WIDTH + B_WIDTH), _BF16),
        scratch_shapes=_attn_scratch((sb,), A_WIN, B_WIN),
        name="attn_sample",
        compiler_params=pltpu.CompilerParams(dimension_semantics=("arbitrary",), vmem_limit_bytes=VMEM_LIMIT),
    )(qa, ka, va, qb, kb, vb, cak, cav, cbk, cbv, u, sink)


def _ffn_kernel(o_ref, x_ref, wo_ref, gn_ref, wgu_ref, cw_ref, cb_ref, wd_ref, st_ref,
                y_ref, cn_ref, carry, act_buf, *, seg_len, carry_rows):
    i = pl.program_id(0)
    tm = x_ref.shape[0]
    n_seg = tm // seg_len

    if carry_rows:
        @pl.when(i == 0)
        def _():
            carry[...] = st_ref[0]

    y1 = x_ref[...] + jnp.dot(o_ref[...], wo_ref[...], preferred_element_type=_F32)
    ms = jnp.mean(y1 * y1, axis=-1, keepdims=True)
    h = (y1 * lax.rsqrt(ms + EPS) * gn_ref[...]).astype(_BF16)

    row8 = lax.broadcasted_iota(jnp.int32, (8, FF_CHUNK), 0)
    def gate_up(c0):
        g = jnp.dot(h, wgu_ref[:, c0:c0 + FF_CHUNK], preferred_element_type=_F32)
        up = jnp.dot(h, wgu_ref[:, D_FF + c0:D_FF + c0 + FF_CHUNK], preferred_element_type=_F32)
        return g, up

    nxt = gate_up(0)
    for c0 in range(0, D_FF, FF_CHUNK):
        cols = slice(c0, c0 + FF_CHUNK)
        g, up = nxt
        if c0 + FF_CHUNK < D_FF:
            nxt = gate_up(c0 + FF_CHUNK)
        g1 = pltpu.roll(g, 1, 0)
        g2 = pltpu.roll(g, 2, 0)
        p1, p2 = [], []
        for s in range(n_seg):
            a = s * seg_len
            prev = carry[:, cols] if carry_rows else st_ref[s, :, cols]
            prev0 = jnp.broadcast_to(prev[0:1, :], (8, FF_CHUNK))
            prev1 = jnp.broadcast_to(prev[1:2, :], (8, FF_CHUNK))
            p1.append(jnp.where(row8 == 0, prev1, g1[a:a + 8]))
            p1.append(g1[a + 8:a + seg_len])
            p2.append(jnp.where(row8 == 0, prev0, jnp.where(row8 == 1, prev1, g2[a:a + 8])))
            p2.append(g2[a + 8:a + seg_len])
            tail = g[a + seg_len - 8:a + seg_len][8 - (CONV_W - 1):, :]
            if carry_rows:
                carry[:, cols] = tail
            else:
                cn_ref[s, :, cols] = tail
        g1 = jnp.concatenate(p1, axis=0)
        g2 = jnp.concatenate(p2, axis=0)
        conv = cb_ref[:, cols] + cw_ref[0:1, cols] * g2 + cw_ref[1:2, cols] * g1 + cw_ref[2:3, cols] * g
        act_buf[:, cols] = (conv * jax.nn.sigmoid(conv) * up).astype(_BF16)
    y_ref[...] = y1 + jnp.dot(act_buf[...], wd_ref[...], preferred_element_type=_F32)
    if carry_rows:
        cn_ref[0] = carry[...]


def _out_ffn(o, x, w_out, gn, w_gu, conv_w, conv_b, w_down, state, *, carry_rows, name):
    t = x.shape[0]
    tm = TOKEN_TILE
    n = t // tm
    seg_len = tm if carry_rows else CHUNK
    n_seg = tm // seg_len
    tile = lambda w: pl.BlockSpec((tm, w), lambda i: (i, 0))
    if carry_rows:
        st_spec = _const_spec((1, CONV_W - 1, D_FF))
        cn_spec = pl.BlockSpec((1, CONV_W - 1, D_FF), lambda i: (0, 0, 0))
        cn_shape = (1, CONV_W - 1, D_FF)
    else:
        st_spec = pl.BlockSpec((n_seg, CONV_W - 1, D_FF), lambda i: (i, 0, 0))
        cn_spec = pl.BlockSpec((n_seg, CONV_W - 1, D_FF), lambda i: (i, 0, 0))
        cn_shape = (t // seg_len, CONV_W - 1, D_FF)
    return pl.pallas_call(
        functools.partial(_ffn_kernel, seg_len=seg_len, carry_rows=carry_rows),
        grid=(n,),
        in_specs=[tile(A_WIDTH + B_WIDTH), tile(D_MODEL), _const_spec((D_MODEL, D_MODEL)),
                  _const_spec((1, D_MODEL)), _const_spec((D_MODEL, 2 * D_FF)), _const_spec((CONV_W, D_FF)),
                  _const_spec((1, D_FF)), _const_spec((D_FF, D_MODEL)), st_spec],
        out_specs=[tile(D_MODEL), cn_spec],
        out_shape=[jax.ShapeDtypeStruct((t, D_MODEL), _F32), jax.ShapeDtypeStruct(cn_shape, _F32)],
        scratch_shapes=[pltpu.VMEM((CONV_W - 1, D_FF), _F32), pltpu.VMEM((tm, D_FF), _BF16)],
        name=name,
        compiler_params=pltpu.CompilerParams(dimension_semantics=("arbitrary",), vmem_limit_bytes=VMEM_LIMIT),
    )(o, x, w_out, gn, w_gu, conv_w, conv_b, w_down, state)


def _rope_tables(pos):
    half = HEAD_DIM // 2
    inv = jnp.power(ROPE_THETA, -jnp.arange(half, dtype=_F32) / half)
    ang = pos.astype(_F32)[:, None] * inv[None, :]
    cos = jnp.cos(ang)
    sin = jnp.sin(ang)
    return jnp.tile(cos, (1, LANES // half)), jnp.tile(jnp.concatenate([-sin, sin], axis=1), (1, LANES // HEAD_DIM))


def _bias_profile(table):
    n_clip = B_WIN - 1 - MAX_REL
    head = jnp.broadcast_to(table[:, 2 * MAX_REL:], (A_HEADS, n_clip))
    return jnp.concatenate([head, table[:, ::-1][:, :BIAS_VAR - n_clip]], axis=1)


def kernel(x_prompt, x_sample, cache_a_k, cache_a_v, cache_b_k, cache_b_v, state_conv, norm_attn, w_in, q_norm_a,
           k_norm_a, rel_bias_a, q_norm_b, k_norm_b, sinks_b, w_out, norm_ffn, w_gate_up, conv_w, conv_b, w_down):
    assert norm_attn.shape[0] == 1
    bsz, seq, _ = x_prompt.shape
    dec_b, dec_s, _ = x_sample.shape
    assert bsz == 1 and dec_s == CHUNK and seq % TOKEN_TILE == 0 and (dec_b * dec_s) % TOKEN_TILE == 0
    assert cache_a_k.shape[2] == A_CTX and cache_b_k.shape[2] == B_CTX

    w_in_b = w_in[0].astype(_BF16)
    w_out_b = w_out[0].astype(_BF16)
    w_gu_b = w_gate_up[0].astype(_BF16)
    w_down_b = w_down[0].astype(_BF16)
    gn_a = norm_attn[0][None, :]
    gn_f = norm_ffn[0][None, :]
    gqa = jnp.tile(q_norm_a[0], A_HEADS)[None, :]
    gka = jnp.tile(k_norm_a[0], A_HEADS)[None, :]
    gqb = jnp.tile(q_norm_b[0], B_HEADS)[None, :]
    gkb = jnp.tile(k_norm_b[0], B_KV_HEADS)[None, :]
    u = _bias_profile(rel_bias_a[0])
    sink = jnp.repeat(sinks_b[0], CHUNK).reshape(B_KV_HEADS, B_GROUP * CHUNK, 1)
    cw = conv_w[0]
    cb = conv_b[0][None, :]

    cos_p, sin_p = _rope_tables(jnp.arange(seq))
    cos_s, sin_s = _rope_tables(PAST_LEN + jnp.arange(dec_s))
    reps = TOKEN_TILE // dec_s
    cos_s, sin_s = jnp.tile(cos_s, (reps, 1)), jnp.tile(sin_s, (reps, 1))

    xp = x_prompt.reshape(seq, D_MODEL)
    qa, ka, va, qb, kb, vb, kaf, vaf, kbf, vbf = _project(
        xp, gn_a, w_in_b, gqa, gka, gqb, gkb, cos_p, sin_p, keep_all=False, name="proj_prompt")
    o_p = _attend_prompt(qa, ka, va, qb, kb, vb, u, sink)
    zero_state = jnp.zeros((1, CONV_W - 1, D_FF), _F32)
    y_p, cn_p = _out_ffn(o_p, xp, w_out_b, gn_f, w_gu_b, cw, cb, w_down_b, zero_state,
                         carry_rows=True, name="ffn_prompt")

    xs = x_sample.reshape(dec_b * dec_s, D_MODEL)
    qa, ka, va, qb, kb, vb, kaf_s, vaf_s, kbf_s, vbf_s = _project(
        xs, gn_a, w_in_b, gqa, gka, gqb, gkb, cos_s, sin_s, keep_all=True, name="proj_sample")
    cak = cache_a_k[0].reshape(dec_b, A_CTX, A_WIDTH)
    cav = cache_a_v[0].reshape(dec_b, A_CTX, A_WIDTH)
    cbk = cache_b_k[0].reshape(dec_b, B_CTX, B_KV_WIDTH)
    cbv = cache_b_v[0].reshape(dec_b, B_CTX, B_KV_WIDTH)
    o_s = _attend_sample(qa, ka, va, qb, kb, vb, cak, cav, cbk, cbv, u, sink)
    y_s, cn_s = _out_ffn(o_s, xs, w_out_b, gn_f, w_gu_b, cw, cb, w_down_b, state_conv[0],
                         carry_rows=False, name="ffn_sample")

    return (
        y_p.reshape(1, seq, D_MODEL),
        y_s.reshape(dec_b, dec_s, D_MODEL),
        kaf.reshape(1, 1, A_CTX, A_HEADS, HEAD_DIM),
        vaf.reshape(1, 1, A_CTX, A_HEADS, HEAD_DIM),
        kbf.reshape(1, 1, B_CTX, B_KV_HEADS, HEAD_DIM),
        vbf.reshape(1, 1, B_CTX, B_KV_HEADS, HEAD_DIM),
        cn_p.reshape(1, 1, CONV_W - 1, D_FF),
        kaf_s.reshape(1, dec_b, dec_s, A_HEADS, HEAD_DIM),
        vaf_s.reshape(1, dec_b, dec_s, A_HEADS, HEAD_DIM),
        kbf_s.reshape(1, dec_b, dec_s, B_KV_HEADS, HEAD_DIM),
        vbf_s.reshape(1, dec_b, dec_s, B_KV_HEADS, HEAD_DIM),
        cn_s.reshape(1, dec_b, CONV_W - 1, D_FF),
    )
```

```python
import functools

import jax
import jax.numpy as jnp
from jax import lax
from jax.experimental import pallas as pl
from jax.experimental.pallas import tpu as pltpu

D_MODEL = 1024
CHUNK = 64
HEAD_DIM = 64
EPS = 1e-6
SCALE = HEAD_DIM ** -0.5
NEG = -1e30
PAST_LEN = 1024
A_HEADS = 8
A_LEFT_CHUNKS = 8
A_CTX = A_LEFT_CHUNKS * CHUNK
A_WIN = A_CTX + CHUNK
MAX_REL = 128
A_WIDTH = A_HEADS * HEAD_DIM
B_HEADS = 8
B_KV_HEADS = 2
B_GROUP = B_HEADS // B_KV_HEADS
B_CTX = 128
B_WIN = B_CTX + CHUNK
B_WIDTH = B_HEADS * HEAD_DIM
B_KV_WIDTH = B_KV_HEADS * HEAD_DIM
ROPE_THETA = 10000.0
IN_COLS = 3 * A_WIDTH + B_WIDTH + 2 * B_KV_WIDTH
D_FF = 2816
CONV_W = 3

LANES = 128
TOKEN_TILE = 512
SAMPLE_STREAMS = 4
FF_CHUNK = 256
BIAS_VAR = 256
VMEM_LIMIT = 56 * 1024 * 1024

_BF16 = jnp.bfloat16
_F32 = jnp.float32


def _const_spec(shape):
    nd = len(shape)
    return pl.BlockSpec(shape, lambda i: (0,) * nd, pipeline_mode=pl.Buffered(1))


def _head_sumsq(t, bd):
    sq = (t * t).astype(_BF16)
    w = t.shape[1]
    if w <= 2 * LANES:
        return jnp.dot(sq, bd[:w, :w], preferred_element_type=_F32)
    parts = [jnp.dot(sq[:, c:c + 2 * LANES], bd, preferred_element_type=_F32) for c in range(0, w, 2 * LANES)]
    return jnp.concatenate(parts, axis=1)


def _head_norm(t, bd, gain):
    ss = _head_sumsq(t, bd)
    return t * lax.rsqrt(ss * (1.0 / HEAD_DIM) + EPS) * gain


def _rope(t, cos, sin_signed):
    outs = []
    for c in range(0, t.shape[1], LANES):
        tc = t[:, c:c + LANES]
        lane = lax.broadcasted_iota(jnp.int32, tc.shape, 1)
        first_half = (lane & (HEAD_DIM - 1)) < HEAD_DIM // 2
        rot = jnp.where(first_half, pltpu.roll(tc, LANES - HEAD_DIM // 2, 1), pltpu.roll(tc, HEAD_DIM // 2, 1))
        outs.append(tc * cos + rot * sin_signed)
    return outs[0] if len(outs) == 1 else jnp.concatenate(outs, axis=1)


def _rep_pair(t):
    rolled = pltpu.roll(t, HEAD_DIM, 1)
    lo = lax.broadcasted_iota(jnp.int32, t.shape, 1) < HEAD_DIM
    return jnp.concatenate([jnp.where(lo, t, rolled), jnp.where(lo, rolled, t)], axis=1)


def _store_heads(ref, t):
    n, w = t.shape
    heads = w // HEAD_DIM
    for hd in range(heads):
        ref[pl.ds(hd, n, stride=heads), :] = t[:, hd * HEAD_DIM:(hd + 1) * HEAD_DIM]


def _load_heads(ref, b, n, heads):
    parts = [ref[b, pl.ds(hd, n, stride=heads), :] for hd in range(heads)]
    return jnp.concatenate(parts, axis=1)


def _proj_kernel(x_ref, gn_ref, w_ref, gqa_ref, gka_ref, gqb_ref, gkb_ref, cos_ref, sin_ref,
                 qa_ref, ka_ref, va_ref, qb_ref, kb_ref, vb_ref, kaf_ref, vaf_ref, kbf_ref, vbf_ref,
                 *, rows_a, rows_b, last_only):
    tm = x_ref.shape[0]
    x = x_ref[...]
    ms = jnp.mean(x * x, axis=-1, keepdims=True)
    h = (x * lax.rsqrt(ms + EPS) * gn_ref[...]).astype(_BF16)

    r = lax.broadcasted_iota(jnp.int32, (2 * LANES, 2 * LANES), 0) // HEAD_DIM
    c = lax.broadcasted_iota(jnp.int32, (2 * LANES, 2 * LANES), 1) // HEAD_DIM
    bd = jnp.where(r == c, 1.0, 0.0).astype(_BF16)
    cos = cos_ref[...]
    sin = sin_ref[...]

    def proj(lo, width):
        return jnp.dot(h, w_ref[:, lo:lo + width], preferred_element_type=_F32)

    o_ka, o_va, o_qb, o_kvb = A_WIDTH, 2 * A_WIDTH, 3 * A_WIDTH, 3 * A_WIDTH + B_WIDTH
    p_qa = proj(0, A_WIDTH)
    p_ka = proj(o_ka, A_WIDTH)
    qa_ref[...] = (_head_norm(p_qa, bd, gqa_ref[...]) * SCALE).astype(_BF16)
    va = proj(o_va, A_WIDTH)
    ka = _head_norm(p_ka, bd, gka_ref[...])
    ka_ref[...] = ka.astype(_BF16)
    p_qb = proj(o_qb, B_WIDTH)
    va_ref[...] = va.astype(_BF16)
    p_kvb = proj(o_kvb, 2 * B_KV_WIDTH)
    qb_ref[...] = (_rope(_head_norm(p_qb, bd, gqb_ref[...]), cos, sin) * SCALE).astype(_BF16)
    kb = _rope(_head_norm(p_kvb[:, :B_KV_WIDTH], bd, gkb_ref[...]), cos, sin)
    vb = p_kvb[:, B_KV_WIDTH:]
    kb_ref[...] = _rep_pair(kb).astype(_BF16)
    vb_ref[...] = _rep_pair(vb).astype(_BF16)

    def cache_rows():
        _store_heads(kaf_ref, ka[tm - rows_a:, :])
        _store_heads(vaf_ref, va[tm - rows_a:, :])
        kbf_ref[...] = kb[tm - rows_b:, :]
        vbf_ref[...] = vb[tm - rows_b:, :]

    if last_only:
        pl.when(pl.program_id(0) == pl.num_programs(0) - 1)(cache_rows)
    else:
        cache_rows()


def _project(x, gn, w_in, gqa, gka, gqb, gkb, cos, sin, *, keep_all, name):
    t = x.shape[0]
    tm = TOKEN_TILE
    n = t // tm
    if keep_all:
        rows_a = rows_b = tm
        fa, fb = t, t
        fmap = lambda i: (i, 0)
    else:
        rows_a, rows_b = A_CTX, B_CTX
        fa, fb = A_CTX, B_CTX
        fmap = lambda i: (0, 0)
    tile = lambda w: pl.BlockSpec((tm, w), lambda i: (i, 0))
    trig = tile(LANES) if cos.shape[0] == t else _const_spec((tm, LANES))
    out_shape = [
        jax.ShapeDtypeStruct((t, A_WIDTH), _BF16), jax.ShapeDtypeStruct((t, A_WIDTH), _BF16),
        jax.ShapeDtypeStruct((t, A_WIDTH), _BF16), jax.ShapeDtypeStruct((t, B_WIDTH), _BF16),
        jax.ShapeDtypeStruct((t, 2 * B_KV_WIDTH), _BF16), jax.ShapeDtypeStruct((t, 2 * B_KV_WIDTH), _BF16),
        jax.ShapeDtypeStruct((fa * A_HEADS, HEAD_DIM), _F32), jax.ShapeDtypeStruct((fa * A_HEADS, HEAD_DIM), _F32),
        jax.ShapeDtypeStruct((fb, B_KV_WIDTH), _F32), jax.ShapeDtypeStruct((fb, B_KV_WIDTH), _F32),
    ]
    out_specs = [
        tile(A_WIDTH), tile(A_WIDTH), tile(A_WIDTH), tile(B_WIDTH), tile(2 * B_KV_WIDTH), tile(2 * B_KV_WIDTH),
        pl.BlockSpec((rows_a * A_HEADS, HEAD_DIM), fmap), pl.BlockSpec((rows_a * A_HEADS, HEAD_DIM), fmap),
        pl.BlockSpec((rows_b, B_KV_WIDTH), fmap), pl.BlockSpec((rows_b, B_KV_WIDTH), fmap),
    ]
    in_specs = [
        tile(D_MODEL), _const_spec((1, D_MODEL)), _const_spec((D_MODEL, IN_COLS)),
        _const_spec((1, A_WIDTH)), _const_spec((1, A_WIDTH)), _const_spec((1, B_WIDTH)),
        _const_spec((1, B_KV_WIDTH)), trig, trig,
    ]
    return pl.pallas_call(
        functools.partial(_proj_kernel, rows_a=rows_a, rows_b=rows_b, last_only=not keep_all),
        grid=(n,), in_specs=in_specs, out_specs=out_specs, out_shape=out_shape, name=name,
        compiler_params=pltpu.CompilerParams(dimension_semantics=("arbitrary",), vmem_limit_bytes=VMEM_LIMIT),
    )(x, gn, w_in, gqa, gka, gqb, gkb, cos, sin)


def _build_bias(u_ref, bias_ref):
    n_const = A_WIN - (B_WIN)
    qrow = lax.broadcasted_iota(jnp.int32, (CHUNK, BIAS_VAR), 0)
    for h in range(A_HEADS):
        prof = u_ref[h:h + 1, :]
        y = jnp.broadcast_to(prof, (CHUNK, BIAS_VAR))
        y = pltpu.roll(y, BIAS_VAR - (CHUNK - 1), 1)
        for b in range(6):
            y = jnp.where(((qrow >> b) & 1) == 1, pltpu.roll(y, 1 << b, 1), y)
        const = jnp.broadcast_to(prof[:, 0:1], (CHUNK, n_const))
        full = jnp.concatenate([const, y[:, :A_WIN - n_const]], axis=1)
        bias_ref[h // 4, (h % 4) * CHUNK:(h % 4 + 1) * CHUNK, :] = full


def _scores(qa, qb, kwa_ref, kwb_ref, r0, bias_ref, lim_a, lim_b):
    nt = (((1,), (1,)), ((), ()))
    out = []
    head_of_lane = lax.broadcasted_iota(jnp.int32, (CHUNK, 2 * LANES), 1) // HEAD_DIM
    for g in range(2):
        qg = qa[:, 2 * LANES * g:2 * LANES * (g + 1)]
        qbd = jnp.concatenate([jnp.where(head_of_lane == r, qg, jnp.zeros_like(qg)) for r in range(4)], axis=0)
        kw = kwa_ref[pl.ds(r0, A_WIN), 2 * LANES * g:2 * LANES * (g + 1)]
        s = lax.dot_general(qbd, kw, nt, preferred_element_type=_F32) + bias_ref[g]
        if lim_a:
            col = lax.broadcasted_iota(jnp.int32, s.shape, 1)
            s = jnp.where(col >= lim_a, s, NEG)
        out.append(s)
    lo = lax.broadcasted_iota(jnp.int32, (CHUNK, LANES), 1) < HEAD_DIM
    for g in range(B_KV_HEADS):
        rows = []
        for r in range(B_GROUP):
            hd = B_GROUP * g + r
            qp = qb[:, LANES * (hd // 2):LANES * (hd // 2 + 1)]
            keep = lo if hd % 2 == 0 else jnp.logical_not(lo)
            rows.append(jnp.where(keep, qp, jnp.zeros_like(qp)))
        qs = jnp.concatenate(rows, axis=0)
        kw = kwb_ref[pl.ds(r0, B_WIN), LANES * g:LANES * (g + 1)]
        s = lax.dot_general(qs, kw, nt, preferred_element_type=_F32)
        if lim_b:
            col = lax.broadcasted_iota(jnp.int32, s.shape, 1)
            s = jnp.where(col >= lim_b, s, NEG)
        out.append(s)
    return out


def _softmax_pv(scores, vwa_ref, vwb_ref, r0, sink_ref):
    outs = []
    head_of_lane = lax.broadcasted_iota(jnp.int32, (CHUNK, 2 * LANES), 1) // HEAD_DIM
    for g in range(2):
        s = scores[g]
        m = jnp.max(s, axis=1, keepdims=True)
        e = jnp.exp(s - m)
        l = jnp.sum(e, axis=1, keepdims=True)
        vw = vwa_ref[pl.ds(r0, A_WIN), 2 * LANES * g:2 * LANES * (g + 1)]
        of = jnp.dot(e.astype(_BF16), vw, preferred_element_type=_F32) / l
        og = jnp.where(head_of_lane == 0, of[0:CHUNK], 0.0)
        for r in range(1, 4):
            og = og + jnp.where(head_of_lane == r, of[r * CHUNK:(r + 1) * CHUNK], 0.0)
        outs.append(og)
    lo = lax.broadcasted_iota(jnp.int32, (CHUNK, LANES), 1) < HEAD_DIM
    for g in range(B_KV_HEADS):
        s = scores[2 + g]
        sk = sink_ref[g]
        m = jnp.maximum(jnp.max(s, axis=1, keepdims=True), sk)
        e = jnp.exp(s - m)
        l = jnp.sum(e, axis=1, keepdims=True) + jnp.exp(sk - m)
        vw = vwb_ref[pl.ds(r0, B_WIN), LANES * g:LANES * (g + 1)]
        of = jnp.dot(e.astype(_BF16), vw, preferred_element_type=_F32) / l
        outs.append(jnp.where(lo, of[0:CHUNK], of[CHUNK:2 * CHUNK]))
        outs.append(jnp.where(lo, of[2 * CHUNK:3 * CHUNK], of[3 * CHUNK:4 * CHUNK]))
    return jnp.concatenate(outs, axis=1)


def _attn_prompt_kernel(qa_ref, ka_ref, va_ref, qb_ref, kb_ref, vb_ref, u_ref, sink_ref, o_ref,
                        kwa, vwa, kwb, vwb, bias):
    i = pl.program_id(0)
    tm = qa_ref.shape[0]
    n_chunks = tm // CHUNK

    @pl.when(i == 0)
    def _():
        kwa[0:A_CTX, :] = jnp.zeros((A_CTX, A_WIDTH), _BF16)
        vwa[0:A_CTX, :] = jnp.zeros((A_CTX, A_WIDTH), _BF16)
        kwb[0:B_CTX, :] = jnp.zeros((B_CTX, 2 * B_KV_WIDTH), _BF16)
        vwb[0:B_CTX, :] = jnp.zeros((B_CTX, 2 * B_KV_WIDTH), _BF16)
        _build_bias(u_ref, bias)

    @pl.when(i > 0)
    def _():
        kwa[0:A_CTX, :] = kwa[tm:tm + A_CTX, :]
        vwa[0:A_CTX, :] = vwa[tm:tm + A_CTX, :]
        kwb[0:B_CTX, :] = kwb[tm:tm + B_CTX, :]
        vwb[0:B_CTX, :] = vwb[tm:tm + B_CTX, :]

    kwa[A_CTX:A_CTX + tm, :] = ka_ref[...]
    vwa[A_CTX:A_CTX + tm, :] = va_ref[...]
    kwb[B_CTX:B_CTX + tm, :] = kb_ref[...]
    vwb[B_CTX:B_CTX + tm, :] = vb_ref[...]

    def run(masked):
        def scores(c):
            r0 = c * CHUNK
            lim_a = (A_LEFT_CHUNKS - c) * CHUNK if masked else 0
            lim_b = max(B_CTX // CHUNK - c, 0) * CHUNK if masked else 0
            return _scores(qa_ref[r0:r0 + CHUNK, :], qb_ref[r0:r0 + CHUNK, :], kwa, kwb, r0, bias, lim_a, lim_b)

        nxt = scores(0)
        for c in range(n_chunks):
            cur = nxt
            if c + 1 < n_chunks:
                nxt = scores(c + 1)
            o = _softmax_pv(cur, vwa, vwb, c * CHUNK, sink_ref)
            o_ref[c * CHUNK:(c + 1) * CHUNK, :] = o.astype(_BF16)

    @pl.when(i == 0)
    def _():
        run(True)

    @pl.when(i > 0)
    def _():
        run(False)


def _attn_sample_kernel(qa_ref, ka_ref, va_ref, qb_ref, kb_ref, vb_ref, cak_ref, cav_ref, cbk_ref, cbv_ref,
                        u_ref, sink_ref, o_ref, kwa, vwa, kwb, vwb, bias):
    n_streams = cak_ref.shape[0]

    @pl.when(pl.program_id(0) == 0)
    def _():
        _build_bias(u_ref, bias)

    def fill(b):
        rows = slice(b * CHUNK, (b + 1) * CHUNK)
        kwa[b, 0:A_CTX, :] = _load_heads(cak_ref, b, A_CTX, A_HEADS).astype(_BF16)
        vwa[b, 0:A_CTX, :] = _load_heads(cav_ref, b, A_CTX, A_HEADS).astype(_BF16)
        kwb[b, 0:B_CTX, :] = _rep_pair(cbk_ref[b]).astype(_BF16)
        vwb[b, 0:B_CTX, :] = _rep_pair(cbv_ref[b]).astype(_BF16)
        kwa[b, A_CTX:A_WIN, :] = ka_ref[rows, :]
        vwa[b, A_CTX:A_WIN, :] = va_ref[rows, :]
        kwb[b, B_CTX:B_WIN, :] = kb_ref[rows, :]
        vwb[b, B_CTX:B_WIN, :] = vb_ref[rows, :]

    def scores(b):
        rows = slice(b * CHUNK, (b + 1) * CHUNK)
        return _scores(qa_ref[rows, :], qb_ref[rows, :], kwa.at[b], kwb.at[b], 0, bias, 0, 0)

    fill(0)
    nxt = scores(0)
    for b in range(n_streams):
        cur = nxt
        if b + 1 < n_streams:
            fill(b + 1)
            nxt = scores(b + 1)
        o = _softmax_pv(cur, vwa.at[b], vwb.at[b], 0, sink_ref)
        o_ref[b * CHUNK:(b + 1) * CHUNK, :] = o.astype(_BF16)


def _attn_scratch(lead, rows_a, rows_b):
    return [
        pltpu.VMEM(lead + (rows_a, A_WIDTH), _BF16), pltpu.VMEM(lead + (rows_a, A_WIDTH), _BF16),
        pltpu.VMEM(lead + (rows_b, 2 * B_KV_WIDTH), _BF16), pltpu.VMEM(lead + (rows_b, 2 * B_KV_WIDTH), _BF16),
        pltpu.VMEM((2, 4 * CHUNK, A_WIN), _F32),
    ]


def _attend_prompt(qa, ka, va, qb, kb, vb, u, sink):
    t = qa.shape[0]
    tm = TOKEN_TILE
    tile = lambda w: pl.BlockSpec((tm, w), lambda i: (i, 0))
    return pl.pallas_call(
        _attn_prompt_kernel,
        grid=(t // tm,),
        in_specs=[tile(A_WIDTH), tile(A_WIDTH), tile(A_WIDTH), tile(B_WIDTH), tile(2 * B_KV_WIDTH),
                  tile(2 * B_KV_WIDTH), _const_spec(u.shape), _const_spec(sink.shape)],
        out_specs=tile(A_WIDTH + B_WIDTH),
        out_shape=jax.ShapeDtypeStruct((t, A_WIDTH + B_WIDTH), _BF16),
        scratch_shapes=_attn_scratch((), A_CTX + tm, B_CTX + tm),
        name="attn_prompt",
        compiler_params=pltpu.CompilerParams(dimension_semantics=("arbitrary",), vmem_limit_bytes=VMEM_LIMIT),
    )(qa, ka, va, qb, kb, vb, u, sink)


def _attend_sample(qa, ka, va, qb, kb, vb, cak, cav, cbk, cbv, u, sink):
    t = qa.shape[0]
    sb = SAMPLE_STREAMS
    tile = lambda w: pl.BlockSpec((sb * CHUNK, w), lambda i: (i, 0))
    cache = lambda r, w: pl.BlockSpec((sb, r, w), lambda i: (i, 0, 0))
    return pl.pallas_call(
        _attn_sample_kernel,
        grid=(t // (sb * CHUNK),),
        in_specs=[tile(A_WIDTH), tile(A_WIDTH), tile(A_WIDTH), tile(B_WIDTH), tile(2 * B_KV_WIDTH),
                  tile(2 * B_KV_WIDTH), cache(A_CTX * A_HEADS, HEAD_DIM), cache(A_CTX * A_HEADS, HEAD_DIM),
                  cache(B_CTX, B_KV_WIDTH), cache(B_CTX, B_KV_WIDTH),
                  _const_spec(u.shape), _const_spec(sink.shape)],
        out_specs=tile(A_WIDTH + B_WIDTH),
        out_shape=jax.ShapeDtypeStruct((t, A_WIDTH + B_WIDTH), _BF16),
        scratch_shapes=_attn_scratch((sb,), A_WIN, B_WIN),
        name="attn_sample",
        compiler_params=pltpu.CompilerParams(dimension_semantics=("arbitrary",), vmem_limit_bytes=VMEM_LIMIT),
    )(qa, ka, va, qb, kb, vb, cak, cav, cbk, cbv, u, sink)


def _ffn_kernel(o_ref, x_ref, wo_ref, gn_ref, wgu_ref, cw_ref, cb_ref, wd_ref, st_ref,
                y_ref, cn_ref, carry, act_buf, *, seg_len, carry_rows):
    i = pl.program_id(0)
    tm = x_ref.shape[0]
    n_seg = tm // seg_len

    if carry_rows:
        @pl.when(i == 0)
        def _():
            carry[...] = st_ref[0]

    y1 = x_ref[...] + jnp.dot(o_ref[...], wo_ref[...], preferred_element_type=_F32)
    ms = jnp.mean(y1 * y1, axis=-1, keepdims=True)
    h = (y1 * lax.rsqrt(ms + EPS) * gn_ref[...]).astype(_BF16)

    row8 = lax.broadcasted_iota(jnp.int32, (8, FF_CHUNK), 0)
    def gate_up(c0):
        g = jnp.dot(h, wgu_ref[:, c0:c0 + FF_CHUNK], preferred_element_type=_F32)
        up = jnp.dot(h, wgu_ref[:, D_FF + c0:D_FF + c0 + FF_CHUNK], preferred_element_type=_F32)
        return g, up

    nxt = gate_up(0)
    for c0 in range(0, D_FF, FF_CHUNK):
        cols = slice(c0, c0 + FF_CHUNK)
        g, up = nxt
        if c0 + FF_CHUNK < D_FF:
            nxt = gate_up(c0 + FF_CHUNK)
        g1 = pltpu.roll(g, 1, 0)
        g2 = pltpu.roll(g, 2, 0)
        p1, p2 = [], []
        for s in range(n_seg):
            a = s * seg_len
            prev = carry[:, cols] if carry_rows else st_ref[s, :, cols]
            prev0 = jnp.broadcast_to(prev[0:1, :], (8, FF_CHUNK))
            prev1 = jnp.broadcast_to(prev[1:2, :], (8, FF_CHUNK))
            p1.append(jnp.where(row8 == 0, prev1, g1[a:a + 8]))
            p1.append(g1[a + 8:a + seg_len])
            p2.append(jnp.where(row8 == 0, prev0, jnp.where(row8 == 1, prev1, g2[a:a + 8])))
            p2.append(g2[a + 8:a + seg_len])
            tail = g[a + seg_len - 8:a + seg_len][8 - (CONV_W - 1):, :]
            if carry_rows:
                carry[:, cols] = tail
            else:
                cn_ref[s, :, cols] = tail
        g1 = jnp.concatenate(p1, axis=0)
        g2 = jnp.concatenate(p2, axis=0)
        conv = cb_ref[:, cols] + cw_ref[0:1, cols] * g2 + cw_ref[1:2, cols] * g1 + cw_ref[2:3, cols] * g
        act_buf[:, cols] = (conv * jax.nn.sigmoid(conv) * up).astype(_BF16)
    y_ref[...] = y1 + jnp.dot(act_buf[...], wd_ref[...], preferred_element_type=_F32)
    if carry_rows:
        cn_ref[0] = carry[...]


def _out_ffn(o, x, w_out, gn, w_gu, conv_w, conv_b, w_down, state, *, carry_rows, name):
    t = x.shape[0]
    tm = TOKEN_TILE
    n = t // tm
    seg_len = tm if carry_rows else CHUNK
    n_seg = tm // seg_len
    tile = lambda w: pl.BlockSpec((tm, w), lambda i: (i, 0))
    if carry_rows:
        st_spec = _const_spec((1, CONV_W - 1, D_FF))
        cn_spec = pl.BlockSpec((1, CONV_W - 1, D_FF), lambda i: (0, 0, 0))
        cn_shape = (1, CONV_W - 1, D_FF)
    else:
        st_spec = pl.BlockSpec((n_seg, CONV_W - 1, D_FF), lambda i: (i, 0, 0))
        cn_spec = pl.BlockSpec((n_seg, CONV_W - 1, D_FF), lambda i: (i, 0, 0))
        cn_shape = (t // seg_len, CONV_W - 1, D_FF)
    return pl.pallas_call(
        functools.partial(_ffn_kernel, seg_len=seg_len, carry_rows=carry_rows),
        grid=(n,),
        in_specs=[tile(A_WIDTH + B_WIDTH), tile(D_MODEL), _const_spec((D_MODEL, D_MODEL)),
                  _const_spec((1, D_MODEL)), _const_spec((D_MODEL, 2 * D_FF)), _const_spec((CONV_W, D_FF)),
                  _const_spec((1, D_FF)), _const_spec((D_FF, D_MODEL)), st_spec],
        out_specs=[tile(D_MODEL), cn_spec],
        out_shape=[jax.ShapeDtypeStruct((t, D_MODEL), _F32), jax.ShapeDtypeStruct(cn_shape, _F32)],
        scratch_shapes=[pltpu.VMEM((CONV_W - 1, D_FF), _F32), pltpu.VMEM((tm, D_FF), _BF16)],
        name=name,
        compiler_params=pltpu.CompilerParams(dimension_semantics=("arbitrary",), vmem_limit_bytes=VMEM_LIMIT),
    )(o, x, w_out, gn, w_gu, conv_w, conv_b, w_down, state)


def _rope_tables(pos):
    half = HEAD_DIM // 2
    lane = jnp.arange(LANES)
    inv = jnp.power(ROPE_THETA, -(lane % half).astype(_F32) / half)
    sign = jnp.where(lane % HEAD_DIM < half, -1.0, 1.0).astype(_F32)
    ang = pos.astype(_F32)[:, None] * inv[None, :]
    return jnp.cos(ang), jnp.sin(ang) * sign[None, :]


def _bias_profile(table):
    n_clip = B_WIN - 1 - MAX_REL
    head = jnp.broadcast_to(table[:, 2 * MAX_REL:], (A_HEADS, n_clip))
    return jnp.concatenate([head, table[:, ::-1][:, :BIAS_VAR - n_clip]], axis=1)


def kernel(x_prompt, x_sample, cache_a_k, cache_a_v, cache_b_k, cache_b_v, state_conv, norm_attn, w_in, q_norm_a,
           k_norm_a, rel_bias_a, q_norm_b, k_norm_b, sinks_b, w_out, norm_ffn, w_gate_up, conv_w, conv_b, w_down):
    assert norm_attn.shape[0] == 1
    bsz, seq, _ = x_prompt.shape
    dec_b, dec_s, _ = x_sample.shape
    assert bsz == 1 and dec_s == CHUNK and seq % TOKEN_TILE == 0 and (dec_b * dec_s) % TOKEN_TILE == 0
    assert cache_a_k.shape[2] == A_CTX and cache_b_k.shape[2] == B_CTX

    w_in_b = w_in[0].astype(_BF16)
    w_out_b = w_out[0].astype(_BF16)
    w_gu_b = w_gate_up[0].astype(_BF16)
    w_down_b = w_down[0].astype(_BF16)
    gn_a = norm_attn[0][None, :]
    gn_f = norm_ffn[0][None, :]
    gqa = jnp.tile(q_norm_a[0], A_HEADS)[None, :]
    gka = jnp.tile(k_norm_a[0], A_HEADS)[None, :]
    gqb = jnp.tile(q_norm_b[0], B_HEADS)[None, :]
    gkb = jnp.tile(k_norm_b[0], B_KV_HEADS)[None, :]
    u = _bias_profile(rel_bias_a[0])
    sink = jnp.repeat(sinks_b[0], CHUNK).reshape(B_KV_HEADS, B_GROUP * CHUNK, 1)
    cw = conv_w[0]
    cb = conv_b[0][None, :]

    cos_p, sin_p = _rope_tables(jnp.arange(seq))
    cos_s, sin_s = _rope_tables(PAST_LEN + jnp.arange(dec_s))
    reps = TOKEN_TILE // dec_s
    cos_s, sin_s = jnp.tile(cos_s, (reps, 1)), jnp.tile(sin_s, (reps, 1))

    xp = x_prompt.reshape(seq, D_MODEL)
    qa, ka, va, qb, kb, vb, kaf, vaf, kbf, vbf = _project(
        xp, gn_a, w_in_b, gqa, gka, gqb, gkb, cos_p, sin_p, keep_all=False, name="proj_prompt")
    o_p = _attend_prompt(qa, ka, va, qb, kb, vb, u, sink)
    zero_state = jnp.zeros((1, CONV_W - 1, D_FF), _F32)
    y_p, cn_p = _out_ffn(o_p, xp, w_out_b, gn_f, w_gu_b, cw, cb, w_down_b, zero_state,
                         carry_rows=True, name="ffn_prompt")

    xs = x_sample.reshape(dec_b * dec_s, D_MODEL)
    qa, ka, va, qb, kb, vb, kaf_s, vaf_s, kbf_s, vbf_s = _project(
        xs, gn_a, w_in_b, gqa, gka, gqb, gkb, cos_s, sin_s, keep_all=True, name="proj_sample")
    cak = cache_a_k[0].reshape(dec_b, A_CTX * A_HEADS, HEAD_DIM)
    cav = cache_a_v[0].reshape(dec_b, A_CTX * A_HEADS, HEAD_DIM)
    cbk = cache_b_k[0].reshape(dec_b, B_CTX, B_KV_WIDTH)
    cbv = cache_b_v[0].reshape(dec_b, B_CTX, B_KV_WIDTH)
    o_s = _attend_sample(qa, ka, va, qb, kb, vb, cak, cav, cbk, cbv, u, sink)
    y_s, cn_s = _out_ffn(o_s, xs, w_out_b, gn_f, w_gu_b, cw, cb, w_down_b, state_conv[0],
                         carry_rows=False, name="ffn_sample")

    return (
        y_p.reshape(1, seq, D_MODEL),
        y_s.reshape(dec_b, dec_s, D_MODEL),
        kaf.reshape(1, 1, A_CTX, A_HEADS, HEAD_DIM),
        vaf.reshape(1, 1, A_CTX, A_HEADS, HEAD_DIM),
        kbf.reshape(1, 1, B_CTX, B_KV_HEADS, HEAD_DIM),
        vbf.reshape(1, 1, B_CTX, B_KV_HEADS, HEAD_DIM),
        cn_p.reshape(1, 1, CONV_W - 1, D_FF),
        kaf_s.reshape(1, dec_b, dec_s, A_HEADS, HEAD_DIM),
        vaf_s.reshape(1, dec_b, dec_s, A_HEADS, HEAD_DIM),
        kbf_s.reshape(1, dec_b, dec_s, B_KV_HEADS, HEAD_DIM),
        vbf_s.reshape(1, dec_b, dec_s, B_KV_HEADS, HEAD_DIM),
        cn_s.reshape(1, dec_b, CONV_W - 1, D_FF),
    )
```

```python
import functools

import jax
import jax.numpy as jnp
from jax import lax
from jax.experimental import pallas as pl
from jax.experimental.pallas import tpu as pltpu

D_MODEL = 1024
CHUNK = 64
HEAD_DIM = 64
EPS = 1e-6
SCALE = HEAD_DIM ** -0.5
NEG = -1e30
PAST_LEN = 1024
A_HEADS = 8
A_LEFT_CHUNKS = 8
A_CTX = A_LEFT_CHUNKS * CHUNK
A_WIN = A_CTX + CHUNK
MAX_REL = 128
A_WIDTH = A_HEADS * HEAD_DIM
B_HEADS = 8
B_KV_HEADS = 2
B_GROUP = B_HEADS // B_KV_HEADS
B_CTX = 128
B_WIN = B_CTX + CHUNK
B_WIDTH = B_HEADS * HEAD_DIM
B_KV_WIDTH = B_KV_HEADS * HEAD_DIM
ROPE_THETA = 10000.0
IN_COLS = 3 * A_WIDTH + B_WIDTH + 2 * B_KV_WIDTH
D_FF = 2816
CONV_W = 3

LANES = 128
TOKEN_TILE = 512
SAMPLE_STREAMS = 4
FF_CHUNK = 256
BIAS_COLS = MAX_REL + CHUNK
BIAS_VAR = 256
VMEM_LIMIT = 56 * 1024 * 1024

_BF16 = jnp.bfloat16
_F32 = jnp.float32


def _const_spec(shape):
    nd = len(shape)
    return pl.BlockSpec(shape, lambda i: (0,) * nd, pipeline_mode=pl.Buffered(1))


def _head_sumsq(t, bd):
    sq = (t * t).astype(_BF16)
    w = t.shape[1]
    if w <= 2 * LANES:
        return jnp.dot(sq, bd[:w, :w], preferred_element_type=_F32)
    parts = [jnp.dot(sq[:, c:c + 2 * LANES], bd, preferred_element_type=_F32) for c in range(0, w, 2 * LANES)]
    return jnp.concatenate(parts, axis=1)


def _head_norm(t, bd, gain):
    ss = _head_sumsq(t, bd)
    return t * lax.rsqrt(ss * (1.0 / HEAD_DIM) + EPS) * gain


def _rope(t, cos, sin_signed):
    outs = []
    for c in range(0, t.shape[1], LANES):
        tc = t[:, c:c + LANES]
        lane = lax.broadcasted_iota(jnp.int32, tc.shape, 1)
        first_half = (lane & (HEAD_DIM - 1)) < HEAD_DIM // 2
        rot = jnp.where(first_half, pltpu.roll(tc, LANES - HEAD_DIM // 2, 1), pltpu.roll(tc, HEAD_DIM // 2, 1))
        outs.append(tc * cos + rot * sin_signed)
    return outs[0] if len(outs) == 1 else jnp.concatenate(outs, axis=1)


def _rep_pair(t):
    rolled = pltpu.roll(t, HEAD_DIM, 1)
    lo = lax.broadcasted_iota(jnp.int32, t.shape, 1) < HEAD_DIM
    return jnp.concatenate([jnp.where(lo, t, rolled), jnp.where(lo, rolled, t)], axis=1)


def _store_heads(ref, t):
    n, w = t.shape
    heads = w // HEAD_DIM
    for hd in range(heads):
        ref[pl.ds(hd, n, stride=heads), :] = t[:, hd * HEAD_DIM:(hd + 1) * HEAD_DIM]


def _proj_kernel(x_ref, gn_ref, w_ref, gqa_ref, gka_ref, gqb_ref, gkb_ref, ct_ref, st_ref, cr_ref, sr_ref,
                 qa_ref, ka_ref, va_ref, qb_ref, kb_ref, vb_ref, kaf_ref, vaf_ref, kbf_ref, vbf_ref,
                 *, last_only):
    tm = x_ref.shape[0]
    x = x_ref[...]
    ms = jnp.mean(x * x, axis=-1, keepdims=True)
    h = (x * lax.rsqrt(ms + EPS) * gn_ref[...]).astype(_BF16)

    r = lax.broadcasted_iota(jnp.int32, (2 * LANES, 2 * LANES), 0) // HEAD_DIM
    c = lax.broadcasted_iota(jnp.int32, (2 * LANES, 2 * LANES), 1) // HEAD_DIM
    bd = jnp.where(r == c, 1.0, 0.0).astype(_BF16)
    ct, st, cr, sr = ct_ref[0], st_ref[0], cr_ref[...], sr_ref[...]
    cos = ct * cr - st * sr
    sin = st * cr + ct * sr

    def proj(lo, width):
        return jnp.dot(h, w_ref[:, lo:lo + width], preferred_element_type=_F32)

    o_ka, o_va, o_qb, o_kvb = A_WIDTH, 2 * A_WIDTH, 3 * A_WIDTH, 3 * A_WIDTH + B_WIDTH
    p_qa = proj(0, A_WIDTH)
    p_ka = proj(o_ka, A_WIDTH)
    qa_ref[...] = (_head_norm(p_qa, bd, gqa_ref[...]) * SCALE).astype(_BF16)
    va = proj(o_va, A_WIDTH)
    ka = _head_norm(p_ka, bd, gka_ref[...])
    ka_ref[...] = ka.astype(_BF16)
    p_qb = proj(o_qb, B_WIDTH)
    va_ref[...] = va.astype(_BF16)
    p_kvb = proj(o_kvb, 2 * B_KV_WIDTH)
    qb_ref[...] = (_rope(_head_norm(p_qb, bd, gqb_ref[...]), cos, sin) * SCALE).astype(_BF16)
    kb = _rope(_head_norm(p_kvb[:, :B_KV_WIDTH], bd, gkb_ref[...]), cos, sin)
    vb = p_kvb[:, B_KV_WIDTH:]
    kb_ref[...] = _rep_pair(kb).astype(_BF16)
    vb_ref[...] = _rep_pair(vb).astype(_BF16)

    if last_only:
        @pl.when(pl.program_id(0) == pl.num_programs(0) - 1)
        def _():
            kaf_ref[...] = ka[tm - A_CTX:, :].T
            vaf_ref[...] = va[tm - A_CTX:, :].T
            kbf_ref[...] = kb[tm - B_CTX:, :].T
            vbf_ref[...] = vb[tm - B_CTX:, :].T
    else:
        _store_heads(kaf_ref, ka)
        _store_heads(vaf_ref, va)
        kbf_ref[...] = kb
        vbf_ref[...] = vb


def _project(x, gn, w_in, gqa, gka, gqb, gkb, rope, *, keep_all, name):
    t = x.shape[0]
    tm = TOKEN_TILE
    n = t // tm
    tile = lambda w: pl.BlockSpec((tm, w), lambda i: (i, 0))
    if keep_all:
        f32_shapes = [(t * A_HEADS, HEAD_DIM), (t * A_HEADS, HEAD_DIM), (t, B_KV_WIDTH), (t, B_KV_WIDTH)]
        f32_specs = [pl.BlockSpec((tm * A_HEADS, HEAD_DIM), lambda i: (i, 0))] * 2 + [tile(B_KV_WIDTH)] * 2
    else:
        f32_shapes = [(A_WIDTH, A_CTX), (A_WIDTH, A_CTX), (B_KV_WIDTH, B_CTX), (B_KV_WIDTH, B_CTX)]
        f32_specs = [pl.BlockSpec(sh, lambda i: (0, 0)) for sh in f32_shapes]
    ct, st, cr, sr = rope
    if ct.shape[0] == n:
        tile_trig = pl.BlockSpec((1, 1, LANES), lambda i: (i, 0, 0))
    else:
        tile_trig = _const_spec((1, 1, LANES))
    out_shape = [
        jax.ShapeDtypeStruct((t, A_WIDTH), _BF16), jax.ShapeDtypeStruct((t, A_WIDTH), _BF16),
        jax.ShapeDtypeStruct((t, A_WIDTH), _BF16), jax.ShapeDtypeStruct((t, B_WIDTH), _BF16),
        jax.ShapeDtypeStruct((t, 2 * B_KV_WIDTH), _BF16), jax.ShapeDtypeStruct((t, 2 * B_KV_WIDTH), _BF16),
    ] + [jax.ShapeDtypeStruct(sh, _F32) for sh in f32_shapes]
    out_specs = [
        tile(A_WIDTH), tile(A_WIDTH), tile(A_WIDTH), tile(B_WIDTH), tile(2 * B_KV_WIDTH), tile(2 * B_KV_WIDTH),
    ] + f32_specs
    in_specs = [
        tile(D_MODEL), _const_spec((1, D_MODEL)), _const_spec((D_MODEL, IN_COLS)),
        _const_spec((1, A_WIDTH)), _const_spec((1, A_WIDTH)), _const_spec((1, B_WIDTH)),
        _const_spec((1, B_KV_WIDTH)), tile_trig, tile_trig, _const_spec((tm, LANES)), _const_spec((tm, LANES)),
    ]
    return pl.pallas_call(
        functools.partial(_proj_kernel, last_only=not keep_all),
        grid=(n,), in_specs=in_specs, out_specs=out_specs, out_shape=out_shape, name=name,
        compiler_params=pltpu.CompilerParams(dimension_semantics=("arbitrary",), vmem_limit_bytes=VMEM_LIMIT),
    )(x, gn, w_in, gqa, gka, gqb, gkb, ct, st, cr, sr)


def _build_bias(u_ref, bias_ref):
    n_const = A_WIN - BIAS_COLS
    qrow = lax.broadcasted_iota(jnp.int32, (CHUNK, BIAS_VAR), 0)
    for h in range(A_HEADS):
        prof = u_ref[h:h + 1, :]
        y = jnp.broadcast_to(prof, (CHUNK, BIAS_VAR))
        y = pltpu.roll(y, BIAS_VAR - (CHUNK - 1), 1)
        for b in range(6):
            y = jnp.where(((qrow >> b) & 1) == 1, pltpu.roll(y, 1 << b, 1), y)
        const = jnp.broadcast_to(prof[:, 0:1], (CHUNK, n_const))
        full = jnp.concatenate([const, y[:, :BIAS_COLS]], axis=1)
        bias_ref[h // 4, (h % 4) * CHUNK:(h % 4 + 1) * CHUNK, :] = full


_NT = (((1,), (1,)), ((), ()))
_NN = (((1,), (0,)), ((), ()))


def _scores(qa, qb, keys_a, keys_b, bias_ref, lim_a, lim_b):
    out = []
    head_of_lane = lax.broadcasted_iota(jnp.int32, (CHUNK, 2 * LANES), 1) // HEAD_DIM
    for g in range(2):
        qg = qa[:, 2 * LANES * g:2 * LANES * (g + 1)]
        qbd = jnp.concatenate([jnp.where(head_of_lane == r, qg, jnp.zeros_like(qg)) for r in range(4)], axis=0)
        pieces, off = [], 0
        for keys, on_rows in keys_a(g):
            s = lax.dot_general(qbd, keys, _NT if on_rows else _NN, preferred_element_type=_F32)
            s = s + bias_ref[g, :, off:off + s.shape[1]]
            if lim_a:
                s = jnp.where(lax.broadcasted_iota(jnp.int32, s.shape, 1) >= lim_a, s, NEG)
            pieces.append(s)
            off += s.shape[1]
        out.append(pieces)
    lo = lax.broadcasted_iota(jnp.int32, (CHUNK, LANES), 1) < HEAD_DIM
    for g in range(B_KV_HEADS):
        rows = []
        for r in range(B_GROUP):
            hd = B_GROUP * g + r
            qp = qb[:, LANES * (hd // 2):LANES * (hd // 2 + 1)]
            keep = lo if hd % 2 == 0 else jnp.logical_not(lo)
            rows.append(jnp.where(keep, qp, jnp.zeros_like(qp)))
        qs = jnp.concatenate(rows, axis=0)
        pieces = []
        for keys, on_rows in keys_b(g):
            s = lax.dot_general(qs, keys, _NT if on_rows else _NN, preferred_element_type=_F32)
            if lim_b:
                s = jnp.where(lax.broadcasted_iota(jnp.int32, s.shape, 1) >= lim_b, s, NEG)
            pieces.append(s)
        out.append(pieces)
    return out


def _softmax_values(pieces, vals, sink):
    m = functools.reduce(jnp.maximum, [jnp.max(s, axis=1, keepdims=True) for s in pieces])
    if sink is not None:
        m = jnp.maximum(m, sink)
    es = [jnp.exp(s - m) for s in pieces]
    l = functools.reduce(jnp.add, [jnp.sum(e, axis=1, keepdims=True) for e in es])
    if sink is not None:
        l = l + jnp.exp(sink - m)
    acc = None
    for e, (v, on_rows) in zip(es, vals):
        part = lax.dot_general(e.astype(_BF16), v, _NN if on_rows else _NT, preferred_element_type=_F32)
        acc = part if acc is None else acc + part
    return acc / l


def _softmax_pv(scores, vals_a, vals_b, sink_ref):
    outs = []
    head_of_lane = lax.broadcasted_iota(jnp.int32, (CHUNK, 2 * LANES), 1) // HEAD_DIM
    for g in range(2):
        of = _softmax_values(scores[g], vals_a(g), None)
        og = jnp.where(head_of_lane == 0, of[0:CHUNK], 0.0)
        for r in range(1, 4):
            og = og + jnp.where(head_of_lane == r, of[r * CHUNK:(r + 1) * CHUNK], 0.0)
        outs.append(og)
    lo = lax.broadcasted_iota(jnp.int32, (CHUNK, LANES), 1) < HEAD_DIM
    for g in range(B_KV_HEADS):
        of = _softmax_values(scores[2 + g], vals_b(g), sink_ref[g])
        outs.append(jnp.where(lo, of[0:CHUNK], of[CHUNK:2 * CHUNK]))
        outs.append(jnp.where(lo, of[2 * CHUNK:3 * CHUNK], of[3 * CHUNK:4 * CHUNK]))
    return jnp.concatenate(outs, axis=1)


def _attn_prompt_kernel(qa_ref, ka_ref, va_ref, qb_ref, kb_ref, vb_ref, u_ref, sink_ref, o_ref,
                        kwa, vwa, kwb, vwb, bias):
    i = pl.program_id(0)
    tm = qa_ref.shape[0]
    n_chunks = tm // CHUNK

    @pl.when(i == 0)
    def _():
        kwa[0:A_CTX, :] = jnp.zeros((A_CTX, A_WIDTH), _BF16)
        vwa[0:A_CTX, :] = jnp.zeros((A_CTX, A_WIDTH), _BF16)
        kwb[0:B_CTX, :] = jnp.zeros((B_CTX, 2 * B_KV_WIDTH), _BF16)
        vwb[0:B_CTX, :] = jnp.zeros((B_CTX, 2 * B_KV_WIDTH), _BF16)
        _build_bias(u_ref, bias)

    @pl.when(i > 0)
    def _():
        kwa[0:A_CTX, :] = kwa[tm:tm + A_CTX, :]
        vwa[0:A_CTX, :] = vwa[tm:tm + A_CTX, :]
        kwb[0:B_CTX, :] = kwb[tm:tm + B_CTX, :]
        vwb[0:B_CTX, :] = vwb[tm:tm + B_CTX, :]

    kwa[A_CTX:A_CTX + tm, :] = ka_ref[...]
    vwa[A_CTX:A_CTX + tm, :] = va_ref[...]
    kwb[B_CTX:B_CTX + tm, :] = kb_ref[...]
    vwb[B_CTX:B_CTX + tm, :] = vb_ref[...]

    def window(ref, r0, rows, width):
        return lambda g: [(ref[r0:r0 + rows, width * g:width * (g + 1)], True)]

    def run(masked):
        def scores(c):
            r0 = c * CHUNK
            lim_a = (A_LEFT_CHUNKS - c) * CHUNK if masked else 0
            lim_b = max(B_CTX // CHUNK - c, 0) * CHUNK if masked else 0
            return _scores(qa_ref[r0:r0 + CHUNK, :], qb_ref[r0:r0 + CHUNK, :],
                           window(kwa, r0, A_WIN, 2 * LANES), window(kwb, r0, B_WIN, LANES), bias, lim_a, lim_b)

        nxt = scores(0)
        for c in range(n_chunks):
            cur = nxt
            if c + 1 < n_chunks:
                nxt = scores(c + 1)
            r0 = c * CHUNK
            o = _softmax_pv(cur, window(vwa, r0, A_WIN, 2 * LANES), window(vwb, r0, B_WIN, LANES), sink_ref)
            o_ref[r0:r0 + CHUNK, :] = o.astype(_BF16)

    @pl.when(i == 0)
    def _():
        run(True)

    @pl.when(i > 0)
    def _():
        run(False)


def _attn_sample_kernel(qa_ref, ka_ref, va_ref, qb_ref, kb_ref, vb_ref, cak_ref, cav_ref, cbk_ref, cbv_ref,
                        u_ref, sink_ref, o_ref, bias):
    n_streams = cak_ref.shape[0]

    @pl.when(pl.program_id(0) == 0)
    def _():
        _build_bias(u_ref, bias)

    def pieces_a(cache_ref, new_ref, b):
        rows = slice(b * CHUNK, (b + 1) * CHUNK)

        def get(g):
            cols = slice(2 * LANES * g, 2 * LANES * (g + 1))
            return [(cache_ref[b, cols, :].astype(_BF16), False), (new_ref[rows, cols], True)]
        return get

    def pieces_b(cache_ref, new_ref, b):
        rows = slice(b * CHUNK, (b + 1) * CHUNK)

        def get(g):
            past = cache_ref[b, HEAD_DIM * g:HEAD_DIM * (g + 1), :].astype(_BF16)
            return [(jnp.concatenate([past, past], axis=0), False), (new_ref[rows, LANES * g:LANES * (g + 1)], True)]
        return get

    def scores(b):
        rows = slice(b * CHUNK, (b + 1) * CHUNK)
        return _scores(qa_ref[rows, :], qb_ref[rows, :], pieces_a(cak_ref, ka_ref, b), pieces_b(cbk_ref, kb_ref, b),
                       bias, 0, 0)

    nxt = scores(0)
    for b in range(n_streams):
        cur = nxt
        if b + 1 < n_streams:
            nxt = scores(b + 1)
        o = _softmax_pv(cur, pieces_a(cav_ref, va_ref, b), pieces_b(cbv_ref, vb_ref, b), sink_ref)
        o_ref[b * CHUNK:(b + 1) * CHUNK, :] = o.astype(_BF16)


_BIAS_SCRATCH = pltpu.VMEM((2, 4 * CHUNK, A_WIN), _F32)


def _attend_prompt(qa, ka, va, qb, kb, vb, u, sink):
    t = qa.shape[0]
    tm = TOKEN_TILE
    tile = lambda w: pl.BlockSpec((tm, w), lambda i: (i, 0))
    return pl.pallas_call(
        _attn_prompt_kernel,
        grid=(t // tm,),
        in_specs=[tile(A_WIDTH), tile(A_WIDTH), tile(A_WIDTH), tile(B_WIDTH), tile(2 * B_KV_WIDTH),
                  tile(2 * B_KV_WIDTH), _const_spec(u.shape), _const_spec(sink.shape)],
        out_specs=tile(A_WIDTH + B_WIDTH),
        out_shape=jax.ShapeDtypeStruct((t, A_WIDTH + B_WIDTH), _BF16),
        scratch_shapes=[pltpu.VMEM((A_CTX + tm, A_WIDTH), _BF16), pltpu.VMEM((A_CTX + tm, A_WIDTH), _BF16),
                        pltpu.VMEM((B_CTX + tm, 2 * B_KV_WIDTH), _BF16),
                        pltpu.VMEM((B_CTX + tm, 2 * B_KV_WIDTH), _BF16), _BIAS_SCRATCH],
        name="attn_prompt",
        compiler_params=pltpu.CompilerParams(dimension_semantics=("arbitrary",), vmem_limit_bytes=VMEM_LIMIT),
    )(qa, ka, va, qb, kb, vb, u, sink)


def _attend_sample(qa, ka, va, qb, kb, vb, cak, cav, cbk, cbv, u, sink):
    t = qa.shape[0]
    sb = SAMPLE_STREAMS
    tile = lambda w: pl.BlockSpec((sb * CHUNK, w), lambda i: (i, 0))
    cache = lambda arr: pl.BlockSpec((sb,) + arr.shape[1:], lambda i: (i, 0, 0))
    return pl.pallas_call(
        _attn_sample_kernel,
        grid=(t // (sb * CHUNK),),
        in_specs=[tile(A_WIDTH), tile(A_WIDTH), tile(A_WIDTH), tile(B_WIDTH), tile(2 * B_KV_WIDTH),
                  tile(2 * B_KV_WIDTH), cache(cak), cache(cav), cache(cbk), cache(cbv),
                  _const_spec(u.shape), _const_spec(sink.shape)],
        out_specs=tile(A_WIDTH + B_WIDTH),
        out_shape=jax.ShapeDtypeStruct((t, A_WIDTH + B_WIDTH), _BF16),
        scratch_shapes=[_BIAS_SCRATCH],
        name="attn_sample",
        compiler_params=pltpu.CompilerParams(dimension_semantics=("arbitrary",), vmem_limit_bytes=VMEM_LIMIT),
    )(qa, ka, va, qb, kb, vb, cak, cav, cbk, cbv, u, sink)


def _ffn_kernel(o_ref, x_ref, wo_ref, gn_ref, wgu_ref, cw_ref, cb_ref, wd_ref, st_ref,
                y_ref, cn_ref, carry, act_buf, *, seg_len, carry_rows):
    i = pl.program_id(0)
    tm = x_ref.shape[0]
    n_seg = tm // seg_len

    if carry_rows:
        @pl.when(i == 0)
        def _():
            carry[...] = st_ref[0]

    y1 = x_ref[...] + jnp.dot(o_ref[...], wo_ref[...], preferred_element_type=_F32)
    ms = jnp.mean(y1 * y1, axis=-1, keepdims=True)
    h = (y1 * lax.rsqrt(ms + EPS) * gn_ref[...]).astype(_BF16)

    row8 = lax.broadcasted_iota(jnp.int32, (8, FF_CHUNK), 0)

    def gate_up(c0):
        g = jnp.dot(h, wgu_ref[:, c0:c0 + FF_CHUNK], preferred_element_type=_F32)
        up = jnp.dot(h, wgu_ref[:, D_FF + c0:D_FF + c0 + FF_CHUNK], preferred_element_type=_F32)
        return g, up

    nxt = gate_up(0)
    for c0 in range(0, D_FF, FF_CHUNK):
        cols = slice(c0, c0 + FF_CHUNK)
        g, up = nxt
        if c0 + FF_CHUNK < D_FF:
            nxt = gate_up(c0 + FF_CHUNK)
        g1 = pltpu.roll(g, 1, 0)
        g2 = pltpu.roll(g, 2, 0)
        p1, p2 = [], []
        for s in range(n_seg):
            a = s * seg_len
            prev = carry[:, cols] if carry_rows else st_ref[s, :, cols]
            prev0 = jnp.broadcast_to(prev[0:1, :], (8, FF_CHUNK))
            prev1 = jnp.broadcast_to(prev[1:2, :], (8, FF_CHUNK))
            p1.append(jnp.where(row8 == 0, prev1, g1[a:a + 8]))
            p1.append(g1[a + 8:a + seg_len])
            p2.append(jnp.where(row8 == 0, prev0, jnp.where(row8 == 1, prev1, g2[a:a + 8])))
            p2.append(g2[a + 8:a + seg_len])
            tail = g[a + seg_len - 8:a + seg_len][8 - (CONV_W - 1):, :]
            if carry_rows:
                carry[:, cols] = tail
            else:
                cn_ref[s, :, cols] = tail
        g1 = jnp.concatenate(p1, axis=0)
        g2 = jnp.concatenate(p2, axis=0)
        conv = cb_ref[:, cols] + cw_ref[0:1, cols] * g2 + cw_ref[1:2, cols] * g1 + cw_ref[2:3, cols] * g
        act_buf[:, cols] = (conv * jax.nn.sigmoid(conv) * up).astype(_BF16)
    y_ref[...] = y1 + jnp.dot(act_buf[...], wd_ref[...], preferred_element_type=_F32)
    if carry_rows:
        cn_ref[0] = carry[...]


def _out_ffn(o, x, w_out, gn, w_gu, conv_w, conv_b, w_down, state, *, carry_rows, name):
    t = x.shape[0]
    tm = TOKEN_TILE
    n = t // tm
    seg_len = tm if carry_rows else CHUNK
    n_seg = tm // seg_len
    tile = lambda w: pl.BlockSpec((tm, w), lambda i: (i, 0))
    if carry_rows:
        st_spec = _const_spec((1, CONV_W - 1, D_FF))
        cn_spec = pl.BlockSpec((1, CONV_W - 1, D_FF), lambda i: (0, 0, 0))
        cn_shape = (1, CONV_W - 1, D_FF)
    else:
        st_spec = pl.BlockSpec((n_seg, CONV_W - 1, D_FF), lambda i: (i, 0, 0))
        cn_spec = pl.BlockSpec((n_seg, CONV_W - 1, D_FF), lambda i: (i, 0, 0))
        cn_shape = (t // seg_len, CONV_W - 1, D_FF)
    return pl.pallas_call(
        functools.partial(_ffn_kernel, seg_len=seg_len, carry_rows=carry_rows),
        grid=(n,),
        in_specs=[tile(A_WIDTH + B_WIDTH), tile(D_MODEL), _const_spec((D_MODEL, D_MODEL)),
                  _const_spec((1, D_MODEL)), _const_spec((D_MODEL, 2 * D_FF)), _const_spec((CONV_W, D_FF)),
                  _const_spec((1, D_FF)), _const_spec((D_FF, D_MODEL)), st_spec],
        out_specs=[tile(D_MODEL), cn_spec],
        out_shape=[jax.ShapeDtypeStruct((t, D_MODEL), _F32), jax.ShapeDtypeStruct(cn_shape, _F32)],
        scratch_shapes=[pltpu.VMEM((CONV_W - 1, D_FF), _F32), pltpu.VMEM((tm, D_FF), _BF16)],
        name=name,
        compiler_params=pltpu.CompilerParams(dimension_semantics=("arbitrary",), vmem_limit_bytes=VMEM_LIMIT),
    )(o, x, w_out, gn, w_gu, conv_w, conv_b, w_down, state)


def _rope_tables(tile_pos, row_pos):
    half = HEAD_DIM // 2
    lane = jnp.arange(LANES)
    inv = jnp.power(ROPE_THETA, -(lane % half).astype(_F32) / half)
    sign = jnp.where(lane % HEAD_DIM < half, -1.0, 1.0).astype(_F32)
    a = tile_pos.astype(_F32)[:, None] * inv[None, :]
    b = row_pos.astype(_F32)[:, None] * inv[None, :]
    return (jnp.cos(a)[:, None, :], (jnp.sin(a) * sign)[:, None, :], jnp.cos(b), jnp.sin(b) * sign)


def _feature_major(cache):
    n, t, heads, d = cache.shape
    return jnp.transpose(cache, (0, 2, 3, 1)).reshape(n, heads * d, t)


def _token_major(rows, heads):
    t = rows.shape[1]
    return jnp.transpose(rows.reshape(heads, HEAD_DIM, t), (2, 0, 1))[None, None]


def _bias_profile(table):
    n_clip = BIAS_COLS - 1 - MAX_REL
    head = jnp.broadcast_to(table[:, 2 * MAX_REL:], (A_HEADS, n_clip))
    return jnp.concatenate([head, table[:, ::-1][:, :BIAS_VAR - n_clip]], axis=1)


def kernel(x_prompt, x_sample, cache_a_k, cache_a_v, cache_b_k, cache_b_v, state_conv, norm_attn, w_in, q_norm_a,
           k_norm_a, rel_bias_a, q_norm_b, k_norm_b, sinks_b, w_out, norm_ffn, w_gate_up, conv_w, conv_b, w_down):
    assert norm_attn.shape[0] == 1
    bsz, seq, _ = x_prompt.shape
    dec_b, dec_s, _ = x_sample.shape
    assert bsz == 1 and dec_s == CHUNK and seq % TOKEN_TILE == 0 and (dec_b * dec_s) % TOKEN_TILE == 0
    assert cache_a_k.shape[2] == A_CTX and cache_b_k.shape[2] == B_CTX

    w_in_b = w_in[0].astype(_BF16)
    w_out_b = w_out[0].astype(_BF16)
    w_gu_b = w_gate_up[0].astype(_BF16)
    w_down_b = w_down[0].astype(_BF16)
    gn_a = norm_attn[0][None, :]
    gn_f = norm_ffn[0][None, :]
    gqa = jnp.tile(q_norm_a[0], A_HEADS)[None, :]
    gka = jnp.tile(k_norm_a[0], A_HEADS)[None, :]
    gqb = jnp.tile(q_norm_b[0], B_HEADS)[None, :]
    gkb = jnp.tile(k_norm_b[0], B_KV_HEADS)[None, :]
    u = _bias_profile(rel_bias_a[0])
    sink = jnp.repeat(sinks_b[0], CHUNK).reshape(B_KV_HEADS, B_GROUP * CHUNK, 1)
    cw = conv_w[0]
    cb = conv_b[0][None, :]

    rope_p = _rope_tables(jnp.arange(0, seq, TOKEN_TILE), jnp.arange(TOKEN_TILE))
    rope_s = _rope_tables(jnp.full((1,), PAST_LEN), jnp.arange(TOKEN_TILE) % dec_s)

    xp = x_prompt.reshape(seq, D_MODEL)
    qa, ka, va, qb, kb, vb, kaf, vaf, kbf, vbf = _project(
        xp, gn_a, w_in_b, gqa, gka, gqb, gkb, rope_p, keep_all=False, name="proj_prompt")
    o_p = _attend_prompt(qa, ka, va, qb, kb, vb, u, sink)
    zero_state = jnp.zeros((1, CONV_W - 1, D_FF), _F32)
    y_p, cn_p = _out_ffn(o_p, xp, w_out_b, gn_f, w_gu_b, cw, cb, w_down_b, zero_state,
                         carry_rows=True, name="ffn_prompt")

    xs = x_sample.reshape(dec_b * dec_s, D_MODEL)
    qa, ka, va, qb, kb, vb, kaf_s, vaf_s, kbf_s, vbf_s = _project(
        xs, gn_a, w_in_b, gqa, gka, gqb, gkb, rope_s, keep_all=True, name="proj_sample")
    o_s = _attend_sample(qa, ka, va, qb, kb, vb, _feature_major(cache_a_k[0]), _feature_major(cache_a_v[0]),
                         _feature_major(cache_b_k[0]), _feature_major(cache_b_v[0]), u, sink)
    y_s, cn_s = _out_ffn(o_s, xs, w_out_b, gn_f, w_gu_b, cw, cb, w_down_b, state_conv[0],
                         carry_rows=False, name="ffn_sample")

    return (
        y_p.reshape(1, seq, D_MODEL),
        y_s.reshape(dec_b, dec_s, D_MODEL),
        _token_major(kaf, A_HEADS),
        _token_major(vaf, A_HEADS),
        _token_major(kbf, B_KV_HEADS),
        _token_major(vbf, B_KV_HEADS),
        cn_p.reshape(1, 1, CONV_W - 1, D_FF),
        kaf_s.reshape(1, dec_b, dec_s, A_HEADS, HEAD_DIM),
        vaf_s.reshape(1, dec_b, dec_s, A_HEADS, HEAD_DIM),
        kbf_s.reshape(1, dec_b, dec_s, B_KV_HEADS, HEAD_DIM),
        vbf_s.reshape(1, dec_b, dec_s, B_KV_HEADS, HEAD_DIM),
        cn_s.reshape(1, dec_b, CONV_W - 1, D_FF),
    )
```

```python
import functools

import jax
import jax.numpy as jnp
from jax import lax
from jax.experimental import pallas as pl
from jax.experimental.pallas import tpu as pltpu

D_MODEL = 1024
CHUNK = 64
HEAD_DIM = 64
EPS = 1e-6
SCALE = HEAD_DIM ** -0.5
NEG = -1e30
PAST_LEN = 1024
A_HEADS = 8
A_LEFT_CHUNKS = 8
A_CTX = A_LEFT_CHUNKS * CHUNK
A_WIN = A_CTX + CHUNK
MAX_REL = 128
A_WIDTH = A_HEADS * HEAD_DIM
B_HEADS = 8
B_KV_HEADS = 2
B_GROUP = B_HEADS // B_KV_HEADS
B_CTX = 128
B_WIN = B_CTX + CHUNK
B_WIDTH = B_HEADS * HEAD_DIM
B_KV_WIDTH = B_KV_HEADS * HEAD_DIM
ROPE_THETA = 10000.0
IN_COLS = 3 * A_WIDTH + B_WIDTH + 2 * B_KV_WIDTH
D_FF = 2816
CONV_W = 3

LANES = 128
TOKEN_TILE = 512
ATTN_TILE = 1024
SAMPLE_STREAMS = 4
FF_CHUNK = 256
BIAS_COLS = MAX_REL + CHUNK
BIAS_VAR = 256
VMEM_LIMIT = 56 * 1024 * 1024

_BF16 = jnp.bfloat16
_F32 = jnp.float32


def _const_spec(shape):
    nd = len(shape)
    return pl.BlockSpec(shape, lambda i: (0,) * nd, pipeline_mode=pl.Buffered(1))


def _head_sumsq(t, bd):
    sq = (t * t).astype(_BF16)
    w = t.shape[1]
    if w <= 2 * LANES:
        return jnp.dot(sq, bd[:w, :w], preferred_element_type=_F32)
    parts = [jnp.dot(sq[:, c:c + 2 * LANES], bd, preferred_element_type=_F32) for c in range(0, w, 2 * LANES)]
    return jnp.concatenate(parts, axis=1)


def _head_norm(t, bd, gain):
    ss = _head_sumsq(t, bd)
    return t * lax.rsqrt(ss * (1.0 / HEAD_DIM) + EPS) * gain


def _rope(t, cos, sin_signed):
    outs = []
    for c in range(0, t.shape[1], LANES):
        tc = t[:, c:c + LANES]
        lane = lax.broadcasted_iota(jnp.int32, tc.shape, 1)
        first_half = (lane & (HEAD_DIM - 1)) < HEAD_DIM // 2
        rot = jnp.where(first_half, pltpu.roll(tc, LANES - HEAD_DIM // 2, 1), pltpu.roll(tc, HEAD_DIM // 2, 1))
        outs.append(tc * cos + rot * sin_signed)
    return outs[0] if len(outs) == 1 else jnp.concatenate(outs, axis=1)


def _rep_pair(t):
    rolled = pltpu.roll(t, HEAD_DIM, 1)
    lo = lax.broadcasted_iota(jnp.int32, t.shape, 1) < HEAD_DIM
    return jnp.concatenate([jnp.where(lo, t, rolled), jnp.where(lo, rolled, t)], axis=1)


def _store_heads(ref, t):
    n, w = t.shape
    heads = w // HEAD_DIM
    for hd in range(heads):
        ref[pl.ds(hd, n, stride=heads), :] = t[:, hd * HEAD_DIM:(hd + 1) * HEAD_DIM]


def _proj_kernel(x_ref, gn_ref, w_ref, gqa_ref, gka_ref, gqb_ref, gkb_ref, ct_ref, st_ref, cr_ref, sr_ref, *rest,
                 last_only, n_cast):
    cast_in, rest = rest[:n_cast], rest[n_cast:]
    qa_ref, ka_ref, va_ref, qb_ref, kb_ref, vb_ref, kaf_ref, vaf_ref, kbf_ref, vbf_ref = rest[:10]
    for src, dst in zip(cast_in, rest[10:]):
        dst[...] = src[...].astype(_BF16)
    tm = x_ref.shape[0]
    x = x_ref[...]
    ms = jnp.mean(x * x, axis=-1, keepdims=True)
    h = (x * lax.rsqrt(ms + EPS) * gn_ref[...]).astype(_BF16)

    r = lax.broadcasted_iota(jnp.int32, (2 * LANES, 2 * LANES), 0) // HEAD_DIM
    c = lax.broadcasted_iota(jnp.int32, (2 * LANES, 2 * LANES), 1) // HEAD_DIM
    bd = jnp.where(r == c, 1.0, 0.0).astype(_BF16)
    ct, st, cr, sr = ct_ref[0], st_ref[0], cr_ref[...], sr_ref[...]
    cos = ct * cr - st * sr
    sin = st * cr + ct * sr

    def proj(lo, width):
        return jnp.dot(h, w_ref[:, lo:lo + width], preferred_element_type=_F32)

    o_ka, o_va, o_qb, o_kvb = A_WIDTH, 2 * A_WIDTH, 3 * A_WIDTH, 3 * A_WIDTH + B_WIDTH
    p_qa = proj(0, A_WIDTH)
    p_ka = proj(o_ka, A_WIDTH)
    qa_ref[...] = (_head_norm(p_qa, bd, gqa_ref[...]) * SCALE).astype(_BF16)
    va = proj(o_va, A_WIDTH)
    ka = _head_norm(p_ka, bd, gka_ref[...])
    ka_ref[...] = ka.astype(_BF16)
    p_qb = proj(o_qb, B_WIDTH)
    va_ref[...] = va.astype(_BF16)
    p_kvb = proj(o_kvb, 2 * B_KV_WIDTH)
    qb_ref[...] = (_rope(_head_norm(p_qb, bd, gqb_ref[...]), cos, sin) * SCALE).astype(_BF16)
    kb = _rope(_head_norm(p_kvb[:, :B_KV_WIDTH], bd, gkb_ref[...]), cos, sin)
    vb = p_kvb[:, B_KV_WIDTH:]
    kb_ref[...] = _rep_pair(kb).astype(_BF16)
    vb_ref[...] = _rep_pair(vb).astype(_BF16)

    if last_only:
        @pl.when(pl.program_id(0) == pl.num_programs(0) - 1)
        def _():
            kaf_ref[...] = ka[tm - A_CTX:, :].T
            vaf_ref[...] = va[tm - A_CTX:, :].T
            kbf_ref[...] = kb[tm - B_CTX:, :].T
            vbf_ref[...] = vb[tm - B_CTX:, :].T
    else:
        _store_heads(kaf_ref, ka)
        _store_heads(vaf_ref, va)
        kbf_ref[...] = kb
        vbf_ref[...] = vb


def _cast_blocks(w, n_steps):
    rows = w.shape[0]
    n_blocks = max(d for d in range(1, n_steps + 1) if rows % d == 0 and (rows // d) % 16 == 0)
    return pl.BlockSpec((rows // n_blocks, w.shape[1]), lambda i: (jnp.minimum(i, n_blocks - 1), 0))


def _project(x, gn, w_in, gqa, gka, gqb, gkb, rope, *, keep_all, name, cast=()):
    t = x.shape[0]
    tm = TOKEN_TILE
    n = t // tm
    cast_specs = [_cast_blocks(w, n) for w in cast]
    tile = lambda w: pl.BlockSpec((tm, w), lambda i: (i, 0))
    if keep_all:
        f32_shapes = [(t * A_HEADS, HEAD_DIM), (t * A_HEADS, HEAD_DIM), (t, B_KV_WIDTH), (t, B_KV_WIDTH)]
        f32_specs = [pl.BlockSpec((tm * A_HEADS, HEAD_DIM), lambda i: (i, 0))] * 2 + [tile(B_KV_WIDTH)] * 2
    else:
        f32_shapes = [(A_WIDTH, A_CTX), (A_WIDTH, A_CTX), (B_KV_WIDTH, B_CTX), (B_KV_WIDTH, B_CTX)]
        f32_specs = [pl.BlockSpec(sh, lambda i: (0, 0)) for sh in f32_shapes]
    ct, st, cr, sr = rope
    if ct.shape[0] == n:
        tile_trig = pl.BlockSpec((1, 1, LANES), lambda i: (i, 0, 0))
    else:
        tile_trig = _const_spec((1, 1, LANES))
    out_shape = [
        jax.ShapeDtypeStruct((t, A_WIDTH), _BF16), jax.ShapeDtypeStruct((t, A_WIDTH), _BF16),
        jax.ShapeDtypeStruct((t, A_WIDTH), _BF16), jax.ShapeDtypeStruct((t, B_WIDTH), _BF16),
        jax.ShapeDtypeStruct((t, 2 * B_KV_WIDTH), _BF16), jax.ShapeDtypeStruct((t, 2 * B_KV_WIDTH), _BF16),
    ] + [jax.ShapeDtypeStruct(sh, _F32) for sh in f32_shapes] + [jax.ShapeDtypeStruct(w.shape, _BF16) for w in cast]
    out_specs = [
        tile(A_WIDTH), tile(A_WIDTH), tile(A_WIDTH), tile(B_WIDTH), tile(2 * B_KV_WIDTH), tile(2 * B_KV_WIDTH),
    ] + f32_specs + cast_specs
    in_specs = [
        tile(D_MODEL), _const_spec((1, D_MODEL)), _const_spec((D_MODEL, IN_COLS)),
        _const_spec((1, A_WIDTH)), _const_spec((1, A_WIDTH)), _const_spec((1, B_WIDTH)),
        _const_spec((1, B_KV_WIDTH)), tile_trig, tile_trig, _const_spec((tm, LANES)), _const_spec((tm, LANES)),
    ] + cast_specs
    return pl.pallas_call(
        functools.partial(_proj_kernel, last_only=not keep_all, n_cast=len(cast)),
        grid=(n,), in_specs=in_specs, out_specs=out_specs, out_shape=out_shape, name=name,
        compiler_params=pltpu.CompilerParams(dimension_semantics=("arbitrary",), vmem_limit_bytes=VMEM_LIMIT),
    )(x, gn, w_in, gqa, gka, gqb, gkb, ct, st, cr, sr, *cast)


def _build_bias(u_ref, bias_ref):
    n_const = A_WIN - BIAS_COLS
    qrow = lax.broadcasted_iota(jnp.int32, (CHUNK, BIAS_VAR), 0)
    for h in range(A_HEADS):
        prof = u_ref[h:h + 1, :]
        y = jnp.broadcast_to(prof, (CHUNK, BIAS_VAR))
        y = pltpu.roll(y, BIAS_VAR - (CHUNK - 1), 1)
        for b in range(6):
            y = jnp.where(((qrow >> b) & 1) == 1, pltpu.roll(y, 1 << b, 1), y)
        const = jnp.broadcast_to(prof[:, 0:1], (CHUNK, n_const))
        full = jnp.concatenate([const, y[:, :BIAS_COLS]], axis=1)
        bias_ref[h // 4, (h % 4) * CHUNK:(h % 4 + 1) * CHUNK, :] = full


_NT = (((1,), (1,)), ((), ()))
_NN = (((1,), (0,)), ((), ()))


def _scores(qa, qb, keys_a, keys_b, bias_ref, lim_a, lim_b):
    out = []
    head_of_lane = lax.broadcasted_iota(jnp.int32, (CHUNK, 2 * LANES), 1) // HEAD_DIM
    for g in range(2):
        qg = qa[:, 2 * LANES * g:2 * LANES * (g + 1)]
        qbd = jnp.concatenate([jnp.where(head_of_lane == r, qg, jnp.zeros_like(qg)) for r in range(4)], axis=0)
        pieces, off = [], 0
        for keys, on_rows in keys_a(g):
            s = lax.dot_general(qbd, keys, _NT if on_rows else _NN, preferred_element_type=_F32)
            s = s + bias_ref[g, :, off:off + s.shape[1]]
            if lim_a:
                s = jnp.where(lax.broadcasted_iota(jnp.int32, s.shape, 1) >= lim_a, s, NEG)
            pieces.append(s)
            off += s.shape[1]
        out.append(pieces)
    lo = lax.broadcasted_iota(jnp.int32, (CHUNK, LANES), 1) < HEAD_DIM
    for g in range(B_KV_HEADS):
        rows = []
        for r in range(B_GROUP):
            hd = B_GROUP * g + r
            qp = qb[:, LANES * (hd // 2):LANES * (hd // 2 + 1)]
            keep = lo if hd % 2 == 0 else jnp.logical_not(lo)
            rows.append(jnp.where(keep, qp, jnp.zeros_like(qp)))
        qs = jnp.concatenate(rows, axis=0)
        pieces = []
        for keys, on_rows in keys_b(g):
            s = lax.dot_general(qs, keys, _NT if on_rows else _NN, preferred_element_type=_F32)
            if lim_b:
                s = jnp.where(lax.broadcasted_iota(jnp.int32, s.shape, 1) >= lim_b, s, NEG)
            pieces.append(s)
        out.append(pieces)
    return out


def _softmax_values(pieces, vals, sink):
    m = functools.reduce(jnp.maximum, [jnp.max(s, axis=1, keepdims=True) for s in pieces])
    if sink is not None:
        m = jnp.maximum(m, sink)
    es = [jnp.exp(s - m) for s in pieces]
    l = functools.reduce(jnp.add, [jnp.sum(e, axis=1, keepdims=True) for e in es])
    if sink is not None:
        l = l + jnp.exp(sink - m)
    acc = None
    for e, (v, on_rows) in zip(es, vals):
        part = lax.dot_general(e.astype(_BF16), v, _NN if on_rows else _NT, preferred_element_type=_F32)
        acc = part if acc is None else acc + part
    return acc / l


def _softmax_pv(scores, vals_a, vals_b, sink_ref):
    outs = []
    head_of_lane = lax.broadcasted_iota(jnp.int32, (CHUNK, 2 * LANES), 1) // HEAD_DIM
    for g in range(2):
        of = _softmax_values(scores[g], vals_a(g), None)
        og = jnp.where(head_of_lane == 0, of[0:CHUNK], 0.0)
        for r in range(1, 4):
            og = og + jnp.where(head_of_lane == r, of[r * CHUNK:(r + 1) * CHUNK], 0.0)
        outs.append(og)
    lo = lax.broadcasted_iota(jnp.int32, (CHUNK, LANES), 1) < HEAD_DIM
    for g in range(B_KV_HEADS):
        of = _softmax_values(scores[2 + g], vals_b(g), sink_ref[g])
        outs.append(jnp.where(lo, of[0:CHUNK], of[CHUNK:2 * CHUNK]))
        outs.append(jnp.where(lo, of[2 * CHUNK:3 * CHUNK], of[3 * CHUNK:4 * CHUNK]))
    return jnp.concatenate(outs, axis=1)


def _attn_prompt_kernel(qa_ref, ka_ref, va_ref, qb_ref, kb_ref, vb_ref, u_ref, sink_ref, o_ref,
                        kwa, vwa, kwb, vwb, bias):
    i = pl.program_id(0)
    tm = qa_ref.shape[0]
    n_chunks = tm // CHUNK

    @pl.when(i == 0)
    def _():
        kwa[0:A_CTX, :] = jnp.zeros((A_CTX, A_WIDTH), _BF16)
        vwa[0:A_CTX, :] = jnp.zeros((A_CTX, A_WIDTH), _BF16)
        kwb[0:B_CTX, :] = jnp.zeros((B_CTX, 2 * B_KV_WIDTH), _BF16)
        vwb[0:B_CTX, :] = jnp.zeros((B_CTX, 2 * B_KV_WIDTH), _BF16)
        _build_bias(u_ref, bias)

    @pl.when(i > 0)
    def _():
        kwa[0:A_CTX, :] = kwa[tm:tm + A_CTX, :]
        vwa[0:A_CTX, :] = vwa[tm:tm + A_CTX, :]
        kwb[0:B_CTX, :] = kwb[tm:tm + B_CTX, :]
        vwb[0:B_CTX, :] = vwb[tm:tm + B_CTX, :]

    kwa[A_CTX:A_CTX + tm, :] = ka_ref[...]
    vwa[A_CTX:A_CTX + tm, :] = va_ref[...]
    kwb[B_CTX:B_CTX + tm, :] = kb_ref[...]
    vwb[B_CTX:B_CTX + tm, :] = vb_ref[...]

    def window(ref, r0, rows, width):
        return lambda g: [(ref[r0:r0 + rows, width * g:width * (g + 1)], True)]

    def run(masked):
        def scores(c):
            r0 = c * CHUNK
            lim_a = max(A_LEFT_CHUNKS - c, 0) * CHUNK if masked else 0
            lim_b = max(B_CTX // CHUNK - c, 0) * CHUNK if masked else 0
            return _scores(qa_ref[r0:r0 + CHUNK, :], qb_ref[r0:r0 + CHUNK, :],
                           window(kwa, r0, A_WIN, 2 * LANES), window(kwb, r0, B_WIN, LANES), bias, lim_a, lim_b)

        nxt = scores(0)
        for c in range(n_chunks):
            cur = nxt
            if c + 1 < n_chunks:
                nxt = scores(c + 1)
            r0 = c * CHUNK
            o = _softmax_pv(cur, window(vwa, r0, A_WIN, 2 * LANES), window(vwb, r0, B_WIN, LANES), sink_ref)
            o_ref[r0:r0 + CHUNK, :] = o.astype(_BF16)

    @pl.when(i == 0)
    def _():
        run(True)

    @pl.when(i > 0)
    def _():
        run(False)


def _attn_sample_kernel(qa_ref, ka_ref, va_ref, qb_ref, kb_ref, vb_ref, cak_ref, cav_ref, cbk_ref, cbv_ref,
                        u_ref, sink_ref, o_ref, bias):
    n_streams = cak_ref.shape[0]

    @pl.when(pl.program_id(0) == 0)
    def _():
        _build_bias(u_ref, bias)

    def pieces_a(cache_ref, new_ref, b):
        rows = slice(b * CHUNK, (b + 1) * CHUNK)

        def get(g):
            cols = slice(2 * LANES * g, 2 * LANES * (g + 1))
            return [(cache_ref[b, cols, :].astype(_BF16), False), (new_ref[rows, cols], True)]
        return get

    def pieces_b(cache_ref, new_ref, b):
        rows = slice(b * CHUNK, (b + 1) * CHUNK)

        def get(g):
            past = cache_ref[b, HEAD_DIM * g:HEAD_DIM * (g + 1), :].astype(_BF16)
            return [(jnp.concatenate([past, past], axis=0), False), (new_ref[rows, LANES * g:LANES * (g + 1)], True)]
        return get

    def scores(b):
        rows = slice(b * CHUNK, (b + 1) * CHUNK)
        return _scores(qa_ref[rows, :], qb_ref[rows, :], pieces_a(cak_ref, ka_ref, b), pieces_b(cbk_ref, kb_ref, b),
                       bias, 0, 0)

    nxt = scores(0)
    for b in range(n_streams):
        cur = nxt
        if b + 1 < n_streams:
            nxt = scores(b + 1)
        o = _softmax_pv(cur, pieces_a(cav_ref, va_ref, b), pieces_b(cbv_ref, vb_ref, b), sink_ref)
        o_ref[b * CHUNK:(b + 1) * CHUNK, :] = o.astype(_BF16)


_BIAS_SCRATCH = pltpu.VMEM((2, 4 * CHUNK, A_WIN), _F32)


def _attend_prompt(qa, ka, va, qb, kb, vb, u, sink):
    t = qa.shape[0]
    tm = ATTN_TILE
    tile = lambda w: pl.BlockSpec((tm, w), lambda i: (i, 0))
    return pl.pallas_call(
        _attn_prompt_kernel,
        grid=(t // tm,),
        in_specs=[tile(A_WIDTH), tile(A_WIDTH), tile(A_WIDTH), tile(B_WIDTH), tile(2 * B_KV_WIDTH),
                  tile(2 * B_KV_WIDTH), _const_spec(u.shape), _const_spec(sink.shape)],
        out_specs=tile(A_WIDTH + B_WIDTH),
        out_shape=jax.ShapeDtypeStruct((t, A_WIDTH + B_WIDTH), _BF16),
        scratch_shapes=[pltpu.VMEM((A_CTX + tm, A_WIDTH), _BF16), pltpu.VMEM((A_CTX + tm, A_WIDTH), _BF16),
                        pltpu.VMEM((B_CTX + tm, 2 * B_KV_WIDTH), _BF16),
                        pltpu.VMEM((B_CTX + tm, 2 * B_KV_WIDTH), _BF16), _BIAS_SCRATCH],
        name="attn_prompt",
        compiler_params=pltpu.CompilerParams(dimension_semantics=("arbitrary",), vmem_limit_bytes=VMEM_LIMIT),
    )(qa, ka, va, qb, kb, vb, u, sink)


def _attend_sample(qa, ka, va, qb, kb, vb, cak, cav, cbk, cbv, u, sink):
    t = qa.shape[0]
    sb = SAMPLE_STREAMS
    tile = lambda w: pl.BlockSpec((sb * CHUNK, w), lambda i: (i, 0))
    cache = lambda arr: pl.BlockSpec((sb,) + arr.shape[1:], lambda i: (i, 0, 0))
    return pl.pallas_call(
        _attn_sample_kernel,
        grid=(t // (sb * CHUNK),),
        in_specs=[tile(A_WIDTH), tile(A_WIDTH), tile(A_WIDTH), tile(B_WIDTH), tile(2 * B_KV_WIDTH),
                  tile(2 * B_KV_WIDTH), cache(cak), cache(cav), cache(cbk), cache(cbv),
                  _const_spec(u.shape), _const_spec(sink.shape)],
        out_specs=tile(A_WIDTH + B_WIDTH),
        out_shape=jax.ShapeDtypeStruct((t, A_WIDTH + B_WIDTH), _BF16),
        scratch_shapes=[_BIAS_SCRATCH],
        name="attn_sample",
        compiler_params=pltpu.CompilerParams(dimension_semantics=("arbitrary",), vmem_limit_bytes=VMEM_LIMIT),
    )(qa, ka, va, qb, kb, vb, cak, cav, cbk, cbv, u, sink)


def _ffn_kernel(o_ref, x_ref, wo_ref, gn_ref, wgu_ref, cw_ref, cb_ref, wd_ref, st_ref,
                y_ref, cn_ref, carry, act_buf, *, seg_len, carry_rows):
    i = pl.program_id(0)
    tm = x_ref.shape[0]
    n_seg = tm // seg_len

    if carry_rows:
        @pl.when(i == 0)
        def _():
            carry[...] = st_ref[0]

    y1 = x_ref[...] + jnp.dot(o_ref[...], wo_ref[...], preferred_element_type=_F32)
    ms = jnp.mean(y1 * y1, axis=-1, keepdims=True)
    h = (y1 * lax.rsqrt(ms + EPS) * gn_ref[...]).astype(_BF16)

    row8 = lax.broadcasted_iota(jnp.int32, (8, FF_CHUNK), 0)

    def gate_up(c0):
        g = jnp.dot(h, wgu_ref[:, c0:c0 + FF_CHUNK], preferred_element_type=_F32)
        up = jnp.dot(h, wgu_ref[:, D_FF + c0:D_FF + c0 + FF_CHUNK], preferred_element_type=_F32)
        return g, up

    nxt = gate_up(0)
    for c0 in range(0, D_FF, FF_CHUNK):
        cols = slice(c0, c0 + FF_CHUNK)
        g, up = nxt
        if c0 + FF_CHUNK < D_FF:
            nxt = gate_up(c0 + FF_CHUNK)
        g1 = pltpu.roll(g, 1, 0)
        g2 = pltpu.roll(g, 2, 0)
        p1, p2 = [], []
        for s in range(n_seg):
            a = s * seg_len
            prev = carry[:, cols] if carry_rows else st_ref[s, :, cols]
            prev0 = jnp.broadcast_to(prev[0:1, :], (8, FF_CHUNK))
            prev1 = jnp.broadcast_to(prev[1:2, :], (8, FF_CHUNK))
            p1.append(jnp.where(row8 == 0, prev1, g1[a:a + 8]))
            p1.append(g1[a + 8:a + seg_len])
            p2.append(jnp.where(row8 == 0, prev0, jnp.where(row8 == 1, prev1, g2[a:a + 8])))
            p2.append(g2[a + 8:a + seg_len])
            tail = g[a + seg_len - 8:a + seg_len][8 - (CONV_W - 1):, :]
            if carry_rows:
                carry[:, cols] = tail
            else:
                cn_ref[s, :, cols] = tail
        g1 = jnp.concatenate(p1, axis=0)
        g2 = jnp.concatenate(p2, axis=0)
        conv = cb_ref[:, cols] + cw_ref[0:1, cols] * g2 + cw_ref[1:2, cols] * g1 + cw_ref[2:3, cols] * g
        act_buf[:, cols] = (conv * jax.nn.sigmoid(conv) * up).astype(_BF16)
    y_ref[...] = y1 + jnp.dot(act_buf[...], wd_ref[...], preferred_element_type=_F32)
    if carry_rows:
        cn_ref[0] = carry[...]


def _out_ffn(o, x, w_out, gn, w_gu, conv_w, conv_b, w_down, state, *, carry_rows, name):
    t = x.shape[0]
    tm = TOKEN_TILE
    n = t // tm
    seg_len = tm if carry_rows else CHUNK
    n_seg = tm // seg_len
    tile = lambda w: pl.BlockSpec((tm, w), lambda i: (i, 0))
    if carry_rows:
        st_spec = _const_spec((1, CONV_W - 1, D_FF))
        cn_spec = pl.BlockSpec((1, CONV_W - 1, D_FF), lambda i: (0, 0, 0))
        cn_shape = (1, CONV_W - 1, D_FF)
    else:
        st_spec = pl.BlockSpec((n_seg, CONV_W - 1, D_FF), lambda i: (i, 0, 0))
        cn_spec = pl.BlockSpec((n_seg, CONV_W - 1, D_FF), lambda i: (i, 0, 0))
        cn_shape = (t // seg_len, CONV_W - 1, D_FF)
    return pl.pallas_call(
        functools.partial(_ffn_kernel, seg_len=seg_len, carry_rows=carry_rows),
        grid=(n,),
        in_specs=[tile(A_WIDTH + B_WIDTH), tile(D_MODEL), _const_spec((D_MODEL, D_MODEL)),
                  _const_spec((1, D_MODEL)), _const_spec((D_MODEL, 2 * D_FF)), _const_spec((CONV_W, D_FF)),
                  _const_spec((1, D_FF)), _const_spec((D_FF, D_MODEL)), st_spec],
        out_specs=[tile(D_MODEL), cn_spec],
        out_shape=[jax.ShapeDtypeStruct((t, D_MODEL), _F32), jax.ShapeDtypeStruct(cn_shape, _F32)],
        scratch_shapes=[pltpu.VMEM((CONV_W - 1, D_FF), _F32), pltpu.VMEM((tm, D_FF), _BF16)],
        name=name,
        compiler_params=pltpu.CompilerParams(dimension_semantics=("arbitrary",), vmem_limit_bytes=VMEM_LIMIT),
    )(o, x, w_out, gn, w_gu, conv_w, conv_b, w_down, state)


def _rope_tables(tile_pos, row_pos):
    half = HEAD_DIM // 2
    lane = jnp.arange(LANES)
    inv = jnp.power(ROPE_THETA, -(lane % half).astype(_F32) / half)
    sign = jnp.where(lane % HEAD_DIM < half, -1.0, 1.0).astype(_F32)
    a = tile_pos.astype(_F32)[:, None] * inv[None, :]
    b = row_pos.astype(_F32)[:, None] * inv[None, :]
    return (jnp.cos(a)[:, None, :], (jnp.sin(a) * sign)[:, None, :], jnp.cos(b), jnp.sin(b) * sign)


def _feature_major(cache):
    n, t, heads, d = cache.shape
    return jnp.transpose(cache, (0, 2, 3, 1)).reshape(n, heads * d, t)


def _token_major(rows, heads):
    t = rows.shape[1]
    return jnp.transpose(rows.reshape(heads, HEAD_DIM, t), (2, 0, 1))[None, None]


def _bias_profile(table):
    n_clip = BIAS_COLS - 1 - MAX_REL
    head = jnp.broadcast_to(table[:, 2 * MAX_REL:], (A_HEADS, n_clip))
    return jnp.concatenate([head, table[:, ::-1][:, :BIAS_VAR - n_clip]], axis=1)


def kernel(x_prompt, x_sample, cache_a_k, cache_a_v, cache_b_k, cache_b_v, state_conv, norm_attn, w_in, q_norm_a,
           k_norm_a, rel_bias_a, q_norm_b, k_norm_b, sinks_b, w_out, norm_ffn, w_gate_up, conv_w, conv_b, w_down):
    assert norm_attn.shape[0] == 1
    bsz, seq, _ = x_prompt.shape
    dec_b, dec_s, _ = x_sample.shape
    assert bsz == 1 and dec_s == CHUNK and seq % TOKEN_TILE == 0 and (dec_b * dec_s) % TOKEN_TILE == 0
    assert cache_a_k.shape[2] == A_CTX and cache_b_k.shape[2] == B_CTX

    w_in_b = w_in[0].astype(_BF16)
    gn_a = norm_attn[0][None, :]
    gn_f = norm_ffn[0][None, :]
    gqa = jnp.tile(q_norm_a[0], A_HEADS)[None, :]
    gka = jnp.tile(k_norm_a[0], A_HEADS)[None, :]
    gqb = jnp.tile(q_norm_b[0], B_HEADS)[None, :]
    gkb = jnp.tile(k_norm_b[0], B_KV_HEADS)[None, :]
    u = _bias_profile(rel_bias_a[0])
    sink = jnp.repeat(sinks_b[0], CHUNK).reshape(B_KV_HEADS, B_GROUP * CHUNK, 1)
    cw = conv_w[0]
    cb = conv_b[0][None, :]

    rope_p = _rope_tables(jnp.arange(0, seq, TOKEN_TILE), jnp.arange(TOKEN_TILE))
    rope_s = _rope_tables(jnp.full((1,), PAST_LEN), jnp.arange(TOKEN_TILE) % dec_s)

    xp = x_prompt.reshape(seq, D_MODEL)
    qa, ka, va, qb, kb, vb, kaf, vaf, kbf, vbf, w_out_b, w_gu_b, w_down_b = _project(
        xp, gn_a, w_in_b, gqa, gka, gqb, gkb, rope_p, keep_all=False, name="proj_prompt",
        cast=(w_out[0], w_gate_up[0], w_down[0]))
    o_p = _attend_prompt(qa, ka, va, qb, kb, vb, u, sink)
    zero_state = jnp.zeros((1, CONV_W - 1, D_FF), _F32)
    y_p, cn_p = _out_ffn(o_p, xp, w_out_b, gn_f, w_gu_b, cw, cb, w_down_b, zero_state,
                         carry_rows=True, name="ffn_prompt")

    xs = x_sample.reshape(dec_b * dec_s, D_MODEL)
    qa, ka, va, qb, kb, vb, kaf_s, vaf_s, kbf_s, vbf_s = _project(
        xs, gn_a, w_in_b, gqa, gka, gqb, gkb, rope_s, keep_all=True, name="proj_sample")
    o_s = _attend_sample(qa, ka, va, qb, kb, vb, _feature_major(cache_a_k[0]), _feature_major(cache_a_v[0]),
                         _feature_major(cache_b_k[0]), _feature_major(cache_b_v[0]), u, sink)
    y_s, cn_s = _out_ffn(o_s, xs, w_out_b, gn_f, w_gu_b, cw, cb, w_down_b, state_conv[0],
                         carry_rows=False, name="ffn_sample")

    return (
        y_p.reshape(1, seq, D_MODEL),
        y_s.reshape(dec_b, dec_s, D_MODEL),
        _token_major(kaf, A_HEADS),
        _token_major(vaf, A_HEADS),
        _token_major(kbf, B_KV_HEADS),
        _token_major(vbf, B_KV_HEADS),
        cn_p.reshape(1, 1, CONV_W - 1, D_FF),
        kaf_s.reshape(1, dec_b, dec_s, A_HEADS, HEAD_DIM),
        vaf_s.reshape(1, dec_b, dec_s, A_HEADS, HEAD_DIM),
        kbf_s.reshape(1, dec_b, dec_s, B_KV_HEADS, HEAD_DIM),
        vbf_s.reshape(1, dec_b, dec_s, B_KV_HEADS, HEAD_DIM),
        cn_s.reshape(1, dec_b, CONV_W - 1, D_FF),
    )
```

```python
import functools

import jax
import jax.numpy as jnp
from jax import lax
from jax.experimental import pallas as pl
from jax.experimental.pallas import tpu as pltpu

D_MODEL = 1024
CHUNK = 64
HEAD_DIM = 64
EPS = 1e-6
SCALE = HEAD_DIM ** -0.5
LOG2E = 1.4426950408889634
Q_SCALE = SCALE * LOG2E
NEG = -1e30
PAST_LEN = 1024
A_HEADS = 8
A_LEFT_CHUNKS = 8
A_CTX = A_LEFT_CHUNKS * CHUNK
A_WIN = A_CTX + CHUNK
MAX_REL = 128
A_WIDTH = A_HEADS * HEAD_DIM
B_HEADS = 8
B_KV_HEADS = 2
B_GROUP = B_HEADS // B_KV_HEADS
B_CTX = 128
B_WIN = B_CTX + CHUNK
B_WIDTH = B_HEADS * HEAD_DIM
B_KV_WIDTH = B_KV_HEADS * HEAD_DIM
ROPE_THETA = 10000.0
IN_COLS = 3 * A_WIDTH + B_WIDTH + 2 * B_KV_WIDTH
D_FF = 2816
CONV_W = 3

LANES = 128
TOKEN_TILE = 512
ATTN_TILE = 1024
SAMPLE_STREAMS = 8
FF_CHUNK = 256
BIAS_COLS = MAX_REL + CHUNK
BIAS_VAR = 256
VMEM_LIMIT = 56 * 1024 * 1024

_BF16 = jnp.bfloat16
_F32 = jnp.float32


def _const_spec(shape):
    nd = len(shape)
    return pl.BlockSpec(shape, lambda i: (0,) * nd, pipeline_mode=pl.Buffered(1))


def _head_sumsq(t, bd):
    sq = (t * t).astype(_BF16)
    w = t.shape[1]
    if w <= 2 * LANES:
        return jnp.dot(sq, bd[:w, :w], preferred_element_type=_F32)
    parts = [jnp.dot(sq[:, c:c + 2 * LANES], bd, preferred_element_type=_F32) for c in range(0, w, 2 * LANES)]
    return jnp.concatenate(parts, axis=1)


def _head_norm(t, bd, gain):
    ss = _head_sumsq(t, bd)
    return t * lax.rsqrt(ss * (1.0 / HEAD_DIM) + EPS) * gain


def _rope(t, cos, sin_signed):
    outs = []
    for c in range(0, t.shape[1], LANES):
        tc = t[:, c:c + LANES]
        lane = lax.broadcasted_iota(jnp.int32, tc.shape, 1)
        first_half = (lane & (HEAD_DIM - 1)) < HEAD_DIM // 2
        rot = jnp.where(first_half, pltpu.roll(tc, LANES - HEAD_DIM // 2, 1), pltpu.roll(tc, HEAD_DIM // 2, 1))
        outs.append(tc * cos + rot * sin_signed)
    return outs[0] if len(outs) == 1 else jnp.concatenate(outs, axis=1)


def _rep_pair(t):
    rolled = pltpu.roll(t, HEAD_DIM, 1)
    lo = lax.broadcasted_iota(jnp.int32, t.shape, 1) < HEAD_DIM
    return jnp.concatenate([jnp.where(lo, t, rolled), jnp.where(lo, rolled, t)], axis=1)


def _store_heads(ref, t):
    n, w = t.shape
    heads = w // HEAD_DIM
    for hd in range(heads):
        ref[pl.ds(hd, n, stride=heads), :] = t[:, hd * HEAD_DIM:(hd + 1) * HEAD_DIM]


def _proj_kernel(x_ref, gn_ref, w_ref, gqa_ref, gka_ref, gqb_ref, gkb_ref, ct_ref, st_ref, cr_ref, sr_ref, *rest,
                 last_only, n_cast):
    cast_in, rest = rest[:n_cast], rest[n_cast:]
    qa_ref, ka_ref, va_ref, qb_ref, kb_ref, vb_ref, kaf_ref, vaf_ref, kbf_ref, vbf_ref = rest[:10]
    for src, dst in zip(cast_in, rest[10:]):
        dst[...] = src[...].astype(_BF16)
    tm = x_ref.shape[0]
    x = x_ref[...]
    ms = jnp.mean(x * x, axis=-1, keepdims=True)
    h = (x * lax.rsqrt(ms + EPS) * gn_ref[...]).astype(_BF16)

    r = lax.broadcasted_iota(jnp.int32, (2 * LANES, 2 * LANES), 0) // HEAD_DIM
    c = lax.broadcasted_iota(jnp.int32, (2 * LANES, 2 * LANES), 1) // HEAD_DIM
    bd = jnp.where(r == c, 1.0, 0.0).astype(_BF16)
    ct, st, cr, sr = ct_ref[0], st_ref[0], cr_ref[...], sr_ref[...]
    cos = ct * cr - st * sr
    sin = st * cr + ct * sr

    def proj(lo, width):
        return jnp.dot(h, w_ref[:, lo:lo + width], preferred_element_type=_F32)

    o_ka, o_va, o_qb, o_kvb = A_WIDTH, 2 * A_WIDTH, 3 * A_WIDTH, 3 * A_WIDTH + B_WIDTH
    p_qa = proj(0, A_WIDTH)
    p_ka = proj(o_ka, A_WIDTH)
    qa_ref[...] = (_head_norm(p_qa, bd, gqa_ref[...]) * Q_SCALE).astype(_BF16)
    va = proj(o_va, A_WIDTH)
    ka = _head_norm(p_ka, bd, gka_ref[...])
    ka_ref[...] = ka.astype(_BF16)
    p_qb = proj(o_qb, B_WIDTH)
    va_ref[...] = va.astype(_BF16)
    p_kvb = proj(o_kvb, 2 * B_KV_WIDTH)
    qb_ref[...] = (_rope(_head_norm(p_qb, bd, gqb_ref[...]), cos, sin) * Q_SCALE).astype(_BF16)
    kb = _rope(_head_norm(p_kvb[:, :B_KV_WIDTH], bd, gkb_ref[...]), cos, sin)
    vb = p_kvb[:, B_KV_WIDTH:]
    kb_ref[...] = _rep_pair(kb).astype(_BF16)
    vb_ref[...] = _rep_pair(vb).astype(_BF16)

    if last_only:
        @pl.when(pl.program_id(0) == pl.num_programs(0) - 1)
        def _():
            kaf_ref[...] = ka[tm - A_CTX:, :].T
            vaf_ref[...] = va[tm - A_CTX:, :].T
            kbf_ref[...] = kb[tm - B_CTX:, :].T
            vbf_ref[...] = vb[tm - B_CTX:, :].T
    else:
        _store_heads(kaf_ref, ka)
        _store_heads(vaf_ref, va)
        kbf_ref[...] = kb
        vbf_ref[...] = vb


def _cast_blocks(w, n_steps):
    rows = w.shape[0]
    n_blocks = max(d for d in range(1, n_steps + 1) if rows % d == 0 and (rows // d) % 16 == 0)
    return pl.BlockSpec((rows // n_blocks, w.shape[1]), lambda i: (jnp.minimum(i, n_blocks - 1), 0))


def _project(x, gn, w_in, gqa, gka, gqb, gkb, rope, *, keep_all, name, cast=()):
    t = x.shape[0]
    tm = TOKEN_TILE
    n = t // tm
    cast_specs = [_cast_blocks(w, n) for w in cast]
    tile = lambda w: pl.BlockSpec((tm, w), lambda i: (i, 0))
    if keep_all:
        f32_shapes = [(t * A_HEADS, HEAD_DIM), (t * A_HEADS, HEAD_DIM), (t, B_KV_WIDTH), (t, B_KV_WIDTH)]
        f32_specs = [pl.BlockSpec((tm * A_HEADS, HEAD_DIM), lambda i: (i, 0))] * 2 + [tile(B_KV_WIDTH)] * 2
    else:
        f32_shapes = [(A_WIDTH, A_CTX), (A_WIDTH, A_CTX), (B_KV_WIDTH, B_CTX), (B_KV_WIDTH, B_CTX)]
        f32_specs = [pl.BlockSpec(sh, lambda i: (0, 0)) for sh in f32_shapes]
    ct, st, cr, sr = rope
    if ct.shape[0] == n:
        tile_trig = pl.BlockSpec((1, 1, LANES), lambda i: (i, 0, 0))
    else:
        tile_trig = _const_spec((1, 1, LANES))
    out_shape = [
        jax.ShapeDtypeStruct((t, A_WIDTH), _BF16), jax.ShapeDtypeStruct((t, A_WIDTH), _BF16),
        jax.ShapeDtypeStruct((t, A_WIDTH), _BF16), jax.ShapeDtypeStruct((t, B_WIDTH), _BF16),
        jax.ShapeDtypeStruct((t, 2 * B_KV_WIDTH), _BF16), jax.ShapeDtypeStruct((t, 2 * B_KV_WIDTH), _BF16),
    ] + [jax.ShapeDtypeStruct(sh, _F32) for sh in f32_shapes] + [jax.ShapeDtypeStruct(w.shape, _BF16) for w in cast]
    out_specs = [
        tile(A_WIDTH), tile(A_WIDTH), tile(A_WIDTH), tile(B_WIDTH), tile(2 * B_KV_WIDTH), tile(2 * B_KV_WIDTH),
    ] + f32_specs + cast_specs
    in_specs = [
        tile(D_MODEL), _const_spec((1, D_MODEL)), _const_spec((D_MODEL, IN_COLS)),
        _const_spec((1, A_WIDTH)), _const_spec((1, A_WIDTH)), _const_spec((1, B_WIDTH)),
        _const_spec((1, B_KV_WIDTH)), tile_trig, tile_trig, _const_spec((tm, LANES)), _const_spec((tm, LANES)),
    ] + cast_specs
    return pl.pallas_call(
        functools.partial(_proj_kernel, last_only=not keep_all, n_cast=len(cast)),
        grid=(n,), in_specs=in_specs, out_specs=out_specs, out_shape=out_shape, name=name,
        compiler_params=pltpu.CompilerParams(dimension_semantics=("arbitrary",), vmem_limit_bytes=VMEM_LIMIT),
    )(x, gn, w_in, gqa, gka, gqb, gkb, ct, st, cr, sr, *cast)


def _build_bias(u_ref, bias_ref):
    n_const = A_WIN - BIAS_COLS
    qrow = lax.broadcasted_iota(jnp.int32, (CHUNK, BIAS_VAR), 0)
    for h in range(A_HEADS):
        prof = u_ref[h:h + 1, :]
        y = jnp.broadcast_to(prof, (CHUNK, BIAS_VAR))
        y = pltpu.roll(y, BIAS_VAR - (CHUNK - 1), 1)
        for b in range(6):
            y = jnp.where(((qrow >> b) & 1) == 1, pltpu.roll(y, 1 << b, 1), y)
        const = jnp.broadcast_to(prof[:, 0:1], (CHUNK, n_const))
        full = jnp.concatenate([const, y[:, :BIAS_COLS]], axis=1)
        bias_ref[h // 4, (h % 4) * CHUNK:(h % 4 + 1) * CHUNK, :] = full * LOG2E


_NT = (((1,), (1,)), ((), ()))
_NN = (((1,), (0,)), ((), ()))


def _scores(qa, qb, keys_a, keys_b, bias_ref, lim_a, lim_b):
    out = []
    head_of_lane = lax.broadcasted_iota(jnp.int32, (CHUNK, 2 * LANES), 1) // HEAD_DIM
    for g in range(2):
        qg = qa[:, 2 * LANES * g:2 * LANES * (g + 1)]
        qbd = jnp.concatenate([jnp.where(head_of_lane == r, qg, jnp.zeros_like(qg)) for r in range(4)], axis=0)
        pieces, off = [], 0
        for keys, on_rows in keys_a(g):
            s = lax.dot_general(qbd, keys, _NT if on_rows else _NN, preferred_element_type=_F32)
            s = s + bias_ref[g, :, off:off + s.shape[1]]
            if lim_a:
                s = jnp.where(lax.broadcasted_iota(jnp.int32, s.shape, 1) >= lim_a, s, NEG)
            pieces.append(s)
            off += s.shape[1]
        out.append(pieces)
    lo = lax.broadcasted_iota(jnp.int32, (CHUNK, LANES), 1) < HEAD_DIM
    for g in range(B_KV_HEADS):
        rows = []
        for r in range(B_GROUP):
            hd = B_GROUP * g + r
            qp = qb[:, LANES * (hd // 2):LANES * (hd // 2 + 1)]
            keep = lo if hd % 2 == 0 else jnp.logical_not(lo)
            rows.append(jnp.where(keep, qp, jnp.zeros_like(qp)))
        qs = jnp.concatenate(rows, axis=0)
        pieces = []
        for keys, on_rows in keys_b(g):
            s = lax.dot_general(qs, keys, _NT if on_rows else _NN, preferred_element_type=_F32)
            if lim_b:
                s = jnp.where(lax.broadcasted_iota(jnp.int32, s.shape, 1) >= lim_b, s, NEG)
            pieces.append(s)
        out.append(pieces)
    return out


def _softmax_values(pieces, vals, sink):
    m = functools.reduce(jnp.maximum, [jnp.max(s, axis=1, keepdims=True) for s in pieces])
    if sink is not None:
        m = jnp.maximum(m, sink)
    es = [jnp.exp2(s - m) for s in pieces]
    l = functools.reduce(jnp.add, [jnp.sum(e, axis=1, keepdims=True) for e in es])
    if sink is not None:
        l = l + jnp.exp2(sink - m)
    acc = None
    for e, (v, on_rows) in zip(es, vals):
        part = lax.dot_general(e.astype(_BF16), v, _NN if on_rows else _NT, preferred_element_type=_F32)
        acc = part if acc is None else acc + part
    return acc / l


def _softmax_pv(scores, vals_a, vals_b, sink_ref):
    outs = []
    head_of_lane = lax.broadcasted_iota(jnp.int32, (CHUNK, 2 * LANES), 1) // HEAD_DIM
    for g in range(2):
        of = _softmax_values(scores[g], vals_a(g), None)
        og = jnp.where(head_of_lane == 0, of[0:CHUNK], 0.0)
        for r in range(1, 4):
            og = og + jnp.where(head_of_lane == r, of[r * CHUNK:(r + 1) * CHUNK], 0.0)
        outs.append(og)
    lo = lax.broadcasted_iota(jnp.int32, (CHUNK, LANES), 1) < HEAD_DIM
    for g in range(B_KV_HEADS):
        of = _softmax_values(scores[2 + g], vals_b(g), sink_ref[g] * LOG2E)
        outs.append(jnp.where(lo, of[0:CHUNK], of[CHUNK:2 * CHUNK]))
        outs.append(jnp.where(lo, of[2 * CHUNK:3 * CHUNK], of[3 * CHUNK:4 * CHUNK]))
    return jnp.concatenate(outs, axis=1)


def _attn_prompt_kernel(qa_ref, ka_ref, va_ref, qb_ref, kb_ref, vb_ref, u_ref, sink_ref, o_ref,
                        kwa, vwa, kwb, vwb, bias):
    i = pl.program_id(0)
    tm = qa_ref.shape[0]
    n_chunks = tm // CHUNK

    @pl.when(i == 0)
    def _():
        kwa[0:A_CTX, :] = jnp.zeros((A_CTX, A_WIDTH), _BF16)
        vwa[0:A_CTX, :] = jnp.zeros((A_CTX, A_WIDTH), _BF16)
        kwb[0:B_CTX, :] = jnp.zeros((B_CTX, 2 * B_KV_WIDTH), _BF16)
        vwb[0:B_CTX, :] = jnp.zeros((B_CTX, 2 * B_KV_WIDTH), _BF16)
        _build_bias(u_ref, bias)

    @pl.when(i > 0)
    def _():
        kwa[0:A_CTX, :] = kwa[tm:tm + A_CTX, :]
        vwa[0:A_CTX, :] = vwa[tm:tm + A_CTX, :]
        kwb[0:B_CTX, :] = kwb[tm:tm + B_CTX, :]
        vwb[0:B_CTX, :] = vwb[tm:tm + B_CTX, :]

    kwa[A_CTX:A_CTX + tm, :] = ka_ref[...]
    vwa[A_CTX:A_CTX + tm, :] = va_ref[...]
    kwb[B_CTX:B_CTX + tm, :] = kb_ref[...]
    vwb[B_CTX:B_CTX + tm, :] = vb_ref[...]

    def window(ref, r0, rows, width):
        return lambda g: [(ref[r0:r0 + rows, width * g:width * (g + 1)], True)]

    def run(masked):
        def scores(c):
            r0 = c * CHUNK
            lim_a = max(A_LEFT_CHUNKS - c, 0) * CHUNK if masked else 0
            lim_b = max(B_CTX // CHUNK - c, 0) * CHUNK if masked else 0
            return _scores(qa_ref[r0:r0 + CHUNK, :], qb_ref[r0:r0 + CHUNK, :],
                           window(kwa, r0, A_WIN, 2 * LANES), window(kwb, r0, B_WIN, LANES), bias, lim_a, lim_b)

        nxt = scores(0)
        for c in range(n_chunks):
            cur = nxt
            if c + 1 < n_chunks:
                nxt = scores(c + 1)
            r0 = c * CHUNK
            o = _softmax_pv(cur, window(vwa, r0, A_WIN, 2 * LANES), window(vwb, r0, B_WIN, LANES), sink_ref)
            o_ref[r0:r0 + CHUNK, :] = o.astype(_BF16)

    @pl.when(i == 0)
    def _():
        run(True)

    @pl.when(i > 0)
    def _():
        run(False)


def _attn_sample_kernel(qa_ref, ka_ref, va_ref, qb_ref, kb_ref, vb_ref, cak_ref, cav_ref, cbk_ref, cbv_ref,
                        u_ref, sink_ref, o_ref, bias):
    n_streams = cak_ref.shape[0]

    @pl.when(pl.program_id(0) == 0)
    def _():
        _build_bias(u_ref, bias)

    def pieces_a(cache_ref, new_ref, b):
        rows = slice(b * CHUNK, (b + 1) * CHUNK)

        def get(g):
            cols = slice(2 * LANES * g, 2 * LANES * (g + 1))
            return [(cache_ref[b, cols, :].astype(_BF16), False), (new_ref[rows, cols], True)]
        return get

    def pieces_b(cache_ref, new_ref, b):
        rows = slice(b * CHUNK, (b + 1) * CHUNK)

        def get(g):
            past = cache_ref[b, HEAD_DIM * g:HEAD_DIM * (g + 1), :].astype(_BF16)
            return [(jnp.concatenate([past, past], axis=0), False), (new_ref[rows, LANES * g:LANES * (g + 1)], True)]
        return get

    def scores(b):
        rows = slice(b * CHUNK, (b + 1) * CHUNK)
        return _scores(qa_ref[rows, :], qb_ref[rows, :], pieces_a(cak_ref, ka_ref, b), pieces_b(cbk_ref, kb_ref, b),
                       bias, 0, 0)

    nxt = scores(0)
    for b in range(n_streams):
        cur = nxt
        if b + 1 < n_streams:
            nxt = scores(b + 1)
        o = _softmax_pv(cur, pieces_a(cav_ref, va_ref, b), pieces_b(cbv_ref, vb_ref, b), sink_ref)
        o_ref[b * CHUNK:(b + 1) * CHUNK, :] = o.astype(_BF16)


_BIAS_SCRATCH = pltpu.VMEM((2, 4 * CHUNK, A_WIN), _F32)


def _attend_prompt(qa, ka, va, qb, kb, vb, u, sink):
    t = qa.shape[0]
    tm = ATTN_TILE
    tile = lambda w: pl.BlockSpec((tm, w), lambda i: (i, 0))
    return pl.pallas_call(
        _attn_prompt_kernel,
        grid=(t // tm,),
        in_specs=[tile(A_WIDTH), tile(A_WIDTH), tile(A_WIDTH), tile(B_WIDTH), tile(2 * B_KV_WIDTH),
                  tile(2 * B_KV_WIDTH), _const_spec(u.shape), _const_spec(sink.shape)],
        out_specs=tile(A_WIDTH + B_WIDTH),
        out_shape=jax.ShapeDtypeStruct((t, A_WIDTH + B_WIDTH), _BF16),
        scratch_shapes=[pltpu.VMEM((A_CTX + tm, A_WIDTH), _BF16), pltpu.VMEM((A_CTX + tm, A_WIDTH), _BF16),
                        pltpu.VMEM((B_CTX + tm, 2 * B_KV_WIDTH), _BF16),
                        pltpu.VMEM((B_CTX + tm, 2 * B_KV_WIDTH), _BF16), _BIAS_SCRATCH],
        name="attn_prompt",
        compiler_params=pltpu.CompilerParams(dimension_semantics=("arbitrary",), vmem_limit_bytes=VMEM_LIMIT),
    )(qa, ka, va, qb, kb, vb, u, sink)


def _attend_sample(qa, ka, va, qb, kb, vb, cak, cav, cbk, cbv, u, sink):
    t = qa.shape[0]
    sb = SAMPLE_STREAMS
    tile = lambda w: pl.BlockSpec((sb * CHUNK, w), lambda i: (i, 0))
    cache = lambda arr: pl.BlockSpec((sb,) + arr.shape[1:], lambda i: (i, 0, 0))
    return pl.pallas_call(
        _attn_sample_kernel,
        grid=(t // (sb * CHUNK),),
        in_specs=[tile(A_WIDTH), tile(A_WIDTH), tile(A_WIDTH), tile(B_WIDTH), tile(2 * B_KV_WIDTH),
                  tile(2 * B_KV_WIDTH), cache(cak), cache(cav), cache(cbk), cache(cbv),
                  _const_spec(u.shape), _const_spec(sink.shape)],
        out_specs=tile(A_WIDTH + B_WIDTH),
        out_shape=jax.ShapeDtypeStruct((t, A_WIDTH + B_WIDTH), _BF16),
        scratch_shapes=[_BIAS_SCRATCH],
        name="attn_sample",
        compiler_params=pltpu.CompilerParams(dimension_semantics=("arbitrary",), vmem_limit_bytes=VMEM_LIMIT),
    )(qa, ka, va, qb, kb, vb, cak, cav, cbk, cbv, u, sink)


def _ffn_kernel(o_ref, x_ref, wo_ref, gn_ref, wgu_ref, cw_ref, cb_ref, wd_ref, st_ref,
                y_ref, cn_ref, carry, act_buf, *, seg_len, carry_rows):
    i = pl.program_id(0)
    tm = x_ref.shape[0]
    n_seg = tm // seg_len

    if carry_rows:
        @pl.when(i == 0)
        def _():
            carry[...] = st_ref[0]

    y1 = x_ref[...] + jnp.dot(o_ref[...], wo_ref[...], preferred_element_type=_F32)
    ms = jnp.mean(y1 * y1, axis=-1, keepdims=True)
    h = (y1 * lax.rsqrt(ms + EPS) * gn_ref[...]).astype(_BF16)

    row8 = lax.broadcasted_iota(jnp.int32, (8, FF_CHUNK), 0)

    def gate_up(c0):
        g = jnp.dot(h, wgu_ref[:, c0:c0 + FF_CHUNK], preferred_element_type=_F32)
        up = jnp.dot(h, wgu_ref[:, D_FF + c0:D_FF + c0 + FF_CHUNK], preferred_element_type=_F32)
        return g, up

    nxt = gate_up(0)
    for c0 in range(0, D_FF, FF_CHUNK):
        cols = slice(c0, c0 + FF_CHUNK)
        g, up = nxt
        if c0 + FF_CHUNK < D_FF:
            nxt = gate_up(c0 + FF_CHUNK)
        g1 = pltpu.roll(g, 1, 0)
        g2 = pltpu.roll(g, 2, 0)
        p1, p2 = [], []
        for s in range(n_seg):
            a = s * seg_len
            prev = carry[:, cols] if carry_rows else st_ref[s, :, cols]
            prev0 = jnp.broadcast_to(prev[0:1, :], (8, FF_CHUNK))
            prev1 = jnp.broadcast_to(prev[1:2, :], (8, FF_CHUNK))
            p1.append(jnp.where(row8 == 0, prev1, g1[a:a + 8]))
            p1.append(g1[a + 8:a + seg_len])
            p2.append(jnp.where(row8 == 0, prev0, jnp.where(row8 == 1, prev1, g2[a:a + 8])))
            p2.append(g2[a + 8:a + seg_len])
            tail = g[a + seg_len - 8:a + seg_len][8 - (CONV_W - 1):, :]
            if carry_rows:
                carry[:, cols] = tail
            else:
                cn_ref[s, :, cols] = tail
        g1 = jnp.concatenate(p1, axis=0)
        g2 = jnp.concatenate(p2, axis=0)
        conv = cb_ref[:, cols] + cw_ref[0:1, cols] * g2 + cw_ref[1:2, cols] * g1 + cw_ref[2:3, cols] * g
        act_buf[:, cols] = (conv * jax.nn.sigmoid(conv) * up).astype(_BF16)
    y_ref[...] = y1 + jnp.dot(act_buf[...], wd_ref[...], preferred_element_type=_F32)
    if carry_rows:
        cn_ref[0] = carry[...]


def _out_ffn(o, x, w_out, gn, w_gu, conv_w, conv_b, w_down, state, *, carry_rows, name):
    t = x.shape[0]
    tm = TOKEN_TILE
    n = t // tm
    seg_len = tm if carry_rows else CHUNK
    n_seg = tm // seg_len
    tile = lambda w: pl.BlockSpec((tm, w), lambda i: (i, 0))
    if carry_rows:
        st_spec = _const_spec((1, CONV_W - 1, D_FF))
        cn_spec = pl.BlockSpec((1, CONV_W - 1, D_FF), lambda i: (0, 0, 0))
        cn_shape = (1, CONV_W - 1, D_FF)
    else:
        st_spec = pl.BlockSpec((n_seg, CONV_W - 1, D_FF), lambda i: (i, 0, 0))
        cn_spec = pl.BlockSpec((n_seg, CONV_W - 1, D_FF), lambda i: (i, 0, 0))
        cn_shape = (t // seg_len, CONV_W - 1, D_FF)
    return pl.pallas_call(
        functools.partial(_ffn_kernel, seg_len=seg_len, carry_rows=carry_rows),
        grid=(n,),
        in_specs=[tile(A_WIDTH + B_WIDTH), tile(D_MODEL), _const_spec((D_MODEL, D_MODEL)),
                  _const_spec((1, D_MODEL)), _const_spec((D_MODEL, 2 * D_FF)), _const_spec((CONV_W, D_FF)),
                  _const_spec((1, D_FF)), _const_spec((D_FF, D_MODEL)), st_spec],
        out_specs=[tile(D_MODEL), cn_spec],
        out_shape=[jax.ShapeDtypeStruct((t, D_MODEL), _F32), jax.ShapeDtypeStruct(cn_shape, _F32)],
        scratch_shapes=[pltpu.VMEM((CONV_W - 1, D_FF), _F32), pltpu.VMEM((tm, D_FF), _BF16)],
        name=name,
        compiler_params=pltpu.CompilerParams(dimension_semantics=("arbitrary",), vmem_limit_bytes=VMEM_LIMIT),
    )(o, x, w_out, gn, w_gu, conv_w, conv_b, w_down, state)


def _rope_tables(tile_pos, row_pos):
    half = HEAD_DIM // 2
    lane = jnp.arange(LANES)
    inv = jnp.power(ROPE_THETA, -(lane % half).astype(_F32) / half)
    sign = jnp.where(lane % HEAD_DIM < half, -1.0, 1.0).astype(_F32)
    a = tile_pos.astype(_F32)[:, None] * inv[None, :]
    b = row_pos.astype(_F32)[:, None] * inv[None, :]
    return (jnp.cos(a)[:, None, :], (jnp.sin(a) * sign)[:, None, :], jnp.cos(b), jnp.sin(b) * sign)


def _feature_major(cache):
    n, t, heads, d = cache.shape
    return jnp.transpose(cache, (0, 2, 3, 1)).reshape(n, heads * d, t)


def _token_major(rows, heads):
    t = rows.shape[1]
    return jnp.transpose(rows.reshape(heads, HEAD_DIM, t), (2, 0, 1))[None, None]


def _bias_profile(table):
    n_clip = BIAS_COLS - 1 - MAX_REL
    head = jnp.broadcast_to(table[:, 2 * MAX_REL:], (A_HEADS, n_clip))
    return jnp.concatenate([head, table[:, ::-1][:, :BIAS_VAR - n_clip]], axis=1)


def kernel(x_prompt, x_sample, cache_a_k, cache_a_v, cache_b_k, cache_b_v, state_conv, norm_attn, w_in, q_norm_a,
           k_norm_a, rel_bias_a, q_norm_b, k_norm_b, sinks_b, w_out, norm_ffn, w_gate_up, conv_w, conv_b, w_down):
    assert norm_attn.shape[0] == 1
    bsz, seq, _ = x_prompt.shape
    dec_b, dec_s, _ = x_sample.shape
    assert bsz == 1 and dec_s == CHUNK and seq % TOKEN_TILE == 0 and (dec_b * dec_s) % TOKEN_TILE == 0
    assert cache_a_k.shape[2] == A_CTX and cache_b_k.shape[2] == B_CTX

    w_in_b = w_in[0].astype(_BF16)
    gn_a = norm_attn[0][None, :]
    gn_f = norm_ffn[0][None, :]
    gqa = jnp.tile(q_norm_a[0], A_HEADS)[None, :]
    gka = jnp.tile(k_norm_a[0], A_HEADS)[None, :]
    gqb = jnp.tile(q_norm_b[0], B_HEADS)[None, :]
    gkb = jnp.tile(k_norm_b[0], B_KV_HEADS)[None, :]
    u = _bias_profile(rel_bias_a[0])
    sink = jnp.repeat(sinks_b[0], CHUNK).reshape(B_KV_HEADS, B_GROUP * CHUNK, 1)
    cw = conv_w[0]
    cb = conv_b[0][None, :]

    rope_p = _rope_tables(jnp.arange(0, seq, TOKEN_TILE), jnp.arange(TOKEN_TILE))
    rope_s = _rope_tables(jnp.full((1,), PAST_LEN), jnp.arange(TOKEN_TILE) % dec_s)

    xp = x_prompt.reshape(seq, D_MODEL)
    qa, ka, va, qb, kb, vb, kaf, vaf, kbf, vbf, w_out_b, w_gu_b, w_down_b = _project(
        xp, gn_a, w_in_b, gqa, gka, gqb, gkb, rope_p, keep_all=False, name="proj_prompt",
        cast=(w_out[0], w_gate_up[0], w_down[0]))
    o_p = _attend_prompt(qa, ka, va, qb, kb, vb, u, sink)
    zero_state = jnp.zeros((1, CONV_W - 1, D_FF), _F32)
    y_p, cn_p = _out_ffn(o_p, xp, w_out_b, gn_f, w_gu_b, cw, cb, w_down_b, zero_state,
                         carry_rows=True, name="ffn_prompt")

    xs = x_sample.reshape(dec_b * dec_s, D_MODEL)
    qa, ka, va, qb, kb, vb, kaf_s, vaf_s, kbf_s, vbf_s = _project(
        xs, gn_a, w_in_b, gqa, gka, gqb, gkb, rope_s, keep_all=True, name="proj_sample")
    o_s = _attend_sample(qa, ka, va, qb, kb, vb, _feature_major(cache_a_k[0]), _feature_major(cache_a_v[0]),
                         _feature_major(cache_b_k[0]), _feature_major(cache_b_v[0]), u, sink)
    y_s, cn_s = _out_ffn(o_s, xs, w_out_b, gn_f, w_gu_b, cw, cb, w_down_b, state_conv[0],
                         carry_rows=False, name="ffn_sample")

    return (
        y_p.reshape(1, seq, D_MODEL),
        y_s.reshape(dec_b, dec_s, D_MODEL),
        _token_major(kaf, A_HEADS),
        _token_major(vaf, A_HEADS),
        _token_major(kbf, B_KV_HEADS),
        _token_major(vbf, B_KV_HEADS),
        cn_p.reshape(1, 1, CONV_W - 1, D_FF),
        kaf_s.reshape(1, dec_b, dec_s, A_HEADS, HEAD_DIM),
        vaf_s.reshape(1, dec_b, dec_s, A_HEADS, HEAD_DIM),
        kbf_s.reshape(1, dec_b, dec_s, B_KV_HEADS, HEAD_DIM),
        vbf_s.reshape(1, dec_b, dec_s, B_KV_HEADS, HEAD_DIM),
        cn_s.reshape(1, dec_b, CONV_W - 1, D_FF),
    )
```

```python
import functools

import jax
import jax.numpy as jnp
from jax import lax
from jax.experimental import pallas as pl
from jax.experimental.pallas import tpu as pltpu

D_MODEL = 1024
CHUNK = 64
HEAD_DIM = 64
EPS = 1e-6
SCALE = HEAD_DIM ** -0.5
LOG2E = 1.4426950408889634
Q_SCALE = SCALE * LOG2E
NEG = -1e30
PAST_LEN = 1024
A_HEADS = 8
A_LEFT_CHUNKS = 8
A_CTX = A_LEFT_CHUNKS * CHUNK
A_WIN = A_CTX + CHUNK
MAX_REL = 128
A_WIDTH = A_HEADS * HEAD_DIM
B_HEADS = 8
B_KV_HEADS = 2
B_GROUP = B_HEADS // B_KV_HEADS
B_CTX = 128
B_WIN = B_CTX + CHUNK
B_WIDTH = B_HEADS * HEAD_DIM
B_KV_WIDTH = B_KV_HEADS * HEAD_DIM
ROPE_THETA = 10000.0
IN_COLS = 3 * A_WIDTH + B_WIDTH + 2 * B_KV_WIDTH
D_FF = 2816
CONV_W = 3

LANES = 128
TOKEN_TILE = 512
PROMPT_PROJ_TILE = 1024
ATTN_TILE = 1024
SAMPLE_STREAMS = 4
FF_CHUNK = 256
BIAS_COLS = MAX_REL + CHUNK
BIAS_VAR = 256
VMEM_LIMIT = 56 * 1024 * 1024

_BF16 = jnp.bfloat16
_F32 = jnp.float32


def _const_spec(shape):
    nd = len(shape)
    return pl.BlockSpec(shape, lambda i: (0,) * nd, pipeline_mode=pl.Buffered(1))


def _head_sumsq(t, bd):
    sq = (t * t).astype(_BF16)
    w = t.shape[1]
    if w <= 2 * LANES:
        return jnp.dot(sq, bd[:w, :w], preferred_element_type=_F32)
    parts = [jnp.dot(sq[:, c:c + 2 * LANES], bd, preferred_element_type=_F32) for c in range(0, w, 2 * LANES)]
    return jnp.concatenate(parts, axis=1)


def _head_norm(t, bd, gain):
    ss = _head_sumsq(t, bd)
    return t * lax.rsqrt(ss * (1.0 / HEAD_DIM) + EPS) * gain


def _rope(t, cos, sin_signed):
    outs = []
    for c in range(0, t.shape[1], LANES):
        tc = t[:, c:c + LANES]
        lane = lax.broadcasted_iota(jnp.int32, tc.shape, 1)
        first_half = (lane & (HEAD_DIM - 1)) < HEAD_DIM // 2
        rot = jnp.where(first_half, pltpu.roll(tc, LANES - HEAD_DIM // 2, 1), pltpu.roll(tc, HEAD_DIM // 2, 1))
        outs.append(tc * cos + rot * sin_signed)
    return outs[0] if len(outs) == 1 else jnp.concatenate(outs, axis=1)


def _rep_pair(t):
    rolled = pltpu.roll(t, HEAD_DIM, 1)
    lo = lax.broadcasted_iota(jnp.int32, t.shape, 1) < HEAD_DIM
    return jnp.concatenate([jnp.where(lo, t, rolled), jnp.where(lo, rolled, t)], axis=1)


def _store_heads(ref, t):
    n, w = t.shape
    heads = w // HEAD_DIM
    for hd in range(heads):
        ref[pl.ds(hd, n, stride=heads), :] = t[:, hd * HEAD_DIM:(hd + 1) * HEAD_DIM]


def _proj_kernel(x_ref, gn_ref, w_ref, gqa_ref, gka_ref, gqb_ref, gkb_ref, ct_ref, st_ref, cr_ref, sr_ref, *rest,
                 last_only, n_cast):
    cast_in, rest = rest[:n_cast], rest[n_cast:]
    qa_ref, ka_ref, va_ref, qb_ref, kb_ref, vb_ref, kaf_ref, vaf_ref, kbf_ref, vbf_ref = rest[:10]
    for src, dst in zip(cast_in, rest[10:]):
        dst[...] = src[...].astype(_BF16)
    tm = x_ref.shape[0]
    x = x_ref[...]
    ms = jnp.mean(x * x, axis=-1, keepdims=True)
    h = (x * lax.rsqrt(ms + EPS) * gn_ref[...]).astype(_BF16)

    r = lax.broadcasted_iota(jnp.int32, (2 * LANES, 2 * LANES), 0) // HEAD_DIM
    c = lax.broadcasted_iota(jnp.int32, (2 * LANES, 2 * LANES), 1) // HEAD_DIM
    bd = jnp.where(r == c, 1.0, 0.0).astype(_BF16)
    ct, st, cr, sr = ct_ref[0], st_ref[0], cr_ref[...], sr_ref[...]
    cos = ct * cr - st * sr
    sin = st * cr + ct * sr

    def proj(lo, width):
        return jnp.dot(h, w_ref[:, lo:lo + width], preferred_element_type=_F32)

    o_ka, o_va, o_qb, o_kvb = A_WIDTH, 2 * A_WIDTH, 3 * A_WIDTH, 3 * A_WIDTH + B_WIDTH
    p_qa = proj(0, A_WIDTH)
    p_ka = proj(o_ka, A_WIDTH)
    qa_ref[...] = (_head_norm(p_qa, bd, gqa_ref[...]) * Q_SCALE).astype(_BF16)
    va = proj(o_va, A_WIDTH)
    ka = _head_norm(p_ka, bd, gka_ref[...])
    ka_ref[...] = ka.astype(_BF16)
    p_qb = proj(o_qb, B_WIDTH)
    va_ref[...] = va.astype(_BF16)
    p_kvb = proj(o_kvb, 2 * B_KV_WIDTH)
    qb_ref[...] = (_rope(_head_norm(p_qb, bd, gqb_ref[...]), cos, sin) * Q_SCALE).astype(_BF16)
    kb = _rope(_head_norm(p_kvb[:, :B_KV_WIDTH], bd, gkb_ref[...]), cos, sin)
    vb = p_kvb[:, B_KV_WIDTH:]
    kb_ref[...] = _rep_pair(kb).astype(_BF16)
    vb_ref[...] = _rep_pair(vb).astype(_BF16)

    if last_only:
        @pl.when(pl.program_id(0) == pl.num_programs(0) - 1)
        def _():
            kaf_ref[...] = ka[tm - A_CTX:, :].T
            vaf_ref[...] = va[tm - A_CTX:, :].T
            kbf_ref[...] = kb[tm - B_CTX:, :].T
            vbf_ref[...] = vb[tm - B_CTX:, :].T
    else:
        _store_heads(kaf_ref, ka)
        _store_heads(vaf_ref, va)
        kbf_ref[...] = kb
        vbf_ref[...] = vb


def _cast_blocks(w, n_steps):
    rows = w.shape[0]
    n_blocks = max(d for d in range(1, n_steps + 1) if rows % d == 0 and (rows // d) % 16 == 0)
    return pl.BlockSpec((rows // n_blocks, w.shape[1]), lambda i: (jnp.minimum(i, n_blocks - 1), 0))


def _project(x, gn, w_in, gqa, gka, gqb, gkb, rope, *, keep_all, name, cast=()):
    t = x.shape[0]
    tm = rope[2].shape[0]
    n = t // tm
    cast_specs = [_cast_blocks(w, n) for w in cast]
    tile = lambda w: pl.BlockSpec((tm, w), lambda i: (i, 0))
    if keep_all:
        f32_shapes = [(t * A_HEADS, HEAD_DIM), (t * A_HEADS, HEAD_DIM), (t, B_KV_WIDTH), (t, B_KV_WIDTH)]
        f32_specs = [pl.BlockSpec((tm * A_HEADS, HEAD_DIM), lambda i: (i, 0))] * 2 + [tile(B_KV_WIDTH)] * 2
    else:
        f32_shapes = [(A_WIDTH, A_CTX), (A_WIDTH, A_CTX), (B_KV_WIDTH, B_CTX), (B_KV_WIDTH, B_CTX)]
        f32_specs = [pl.BlockSpec(sh, lambda i: (0, 0)) for sh in f32_shapes]
    ct, st, cr, sr = rope
    if ct.shape[0] == n:
        tile_trig = pl.BlockSpec((1, 1, LANES), lambda i: (i, 0, 0))
    else:
        tile_trig = _const_spec((1, 1, LANES))
    out_shape = [
        jax.ShapeDtypeStruct((t, A_WIDTH), _BF16), jax.ShapeDtypeStruct((t, A_WIDTH), _BF16),
        jax.ShapeDtypeStruct((t, A_WIDTH), _BF16), jax.ShapeDtypeStruct((t, B_WIDTH), _BF16),
        jax.ShapeDtypeStruct((t, 2 * B_KV_WIDTH), _BF16), jax.ShapeDtypeStruct((t, 2 * B_KV_WIDTH), _BF16),
    ] + [jax.ShapeDtypeStruct(sh, _F32) for sh in f32_shapes] + [jax.ShapeDtypeStruct(w.shape, _BF16) for w in cast]
    out_specs = [
        tile(A_WIDTH), tile(A_WIDTH), tile(A_WIDTH), tile(B_WIDTH), tile(2 * B_KV_WIDTH), tile(2 * B_KV_WIDTH),
    ] + f32_specs + cast_specs
    in_specs = [
        tile(D_MODEL), _const_spec((1, D_MODEL)), _const_spec((D_MODEL, IN_COLS)),
        _const_spec((1, A_WIDTH)), _const_spec((1, A_WIDTH)), _const_spec((1, B_WIDTH)),
        _const_spec((1, B_KV_WIDTH)), tile_trig, tile_trig, _const_spec((tm, LANES)), _const_spec((tm, LANES)),
    ] + cast_specs
    return pl.pallas_call(
        functools.partial(_proj_kernel, last_only=not keep_all, n_cast=len(cast)),
        grid=(n,), in_specs=in_specs, out_specs=out_specs, out_shape=out_shape, name=name,
        compiler_params=pltpu.CompilerParams(dimension_semantics=("arbitrary",), vmem_limit_bytes=VMEM_LIMIT),
    )(x, gn, w_in, gqa, gka, gqb, gkb, ct, st, cr, sr, *cast)


def _build_bias(u_ref, bias_ref):
    n_const = A_WIN - BIAS_COLS
    qrow = lax.broadcasted_iota(jnp.int32, (CHUNK, BIAS_VAR), 0)
    for h in range(A_HEADS):
        prof = u_ref[h:h + 1, :]
        y = jnp.broadcast_to(prof, (CHUNK, BIAS_VAR))
        y = pltpu.roll(y, BIAS_VAR - (CHUNK - 1), 1)
        for b in range(6):
            y = jnp.where(((qrow >> b) & 1) == 1, pltpu.roll(y, 1 << b, 1), y)
        const = jnp.broadcast_to(prof[:, 0:1], (CHUNK, n_const))
        full = jnp.concatenate([const, y[:, :BIAS_COLS]], axis=1)
        bias_ref[h // 4, (h % 4) * CHUNK:(h % 4 + 1) * CHUNK, :] = full * LOG2E


_NT = (((1,), (1,)), ((), ()))
_NN = (((1,), (0,)), ((), ()))


def _scores(qa, qb, keys_a, keys_b, bias_ref, lim_a, lim_b):
    out = []
    head_of_lane = lax.broadcasted_iota(jnp.int32, (CHUNK, 2 * LANES), 1) // HEAD_DIM
    for g in range(2):
        qg = qa[:, 2 * LANES * g:2 * LANES * (g + 1)]
        qbd = jnp.concatenate([jnp.where(head_of_lane == r, qg, jnp.zeros_like(qg)) for r in range(4)], axis=0)
        pieces, off = [], 0
        for keys, on_rows in keys_a(g):
            s = lax.dot_general(qbd, keys, _NT if on_rows else _NN, preferred_element_type=_F32)
            s = s + bias_ref[g, :, off:off + s.shape[1]]
            if lim_a:
                s = jnp.where(lax.broadcasted_iota(jnp.int32, s.shape, 1) >= lim_a, s, NEG)
            pieces.append(s)
            off += s.shape[1]
        out.append(pieces)
    lo = lax.broadcasted_iota(jnp.int32, (CHUNK, LANES), 1) < HEAD_DIM
    for g in range(B_KV_HEADS):
        rows = []
        for r in range(B_GROUP):
            hd = B_GROUP * g + r
            qp = qb[:, LANES * (hd // 2):LANES * (hd // 2 + 1)]
            keep = lo if hd % 2 == 0 else jnp.logical_not(lo)
            rows.append(jnp.where(keep, qp, jnp.zeros_like(qp)))
        qs = jnp.concatenate(rows, axis=0)
        pieces = []
        for keys, on_rows in keys_b(g):
            s = lax.dot_general(qs, keys, _NT if on_rows else _NN, preferred_element_type=_F32)
            if lim_b:
                s = jnp.where(lax.broadcasted_iota(jnp.int32, s.shape, 1) >= lim_b, s, NEG)
            pieces.append(s)
        out.append(pieces)
    return out


def _row_reduce(pieces, combine, reduce, fill):
    acc = None
    for s in pieces:
        for c in range(0, s.shape[1], LANES):
            tile = s[:, c:c + LANES]
            if tile.shape[1] < LANES:
                pad = jnp.full((tile.shape[0], LANES - tile.shape[1]), fill, tile.dtype)
                tile = jnp.concatenate([tile, pad], axis=1)
            acc = tile if acc is None else combine(acc, tile)
    return reduce(acc, axis=1, keepdims=True)


def _softmax_values(pieces, vals, sink):
    m = _row_reduce(pieces, jnp.maximum, jnp.max, -jnp.inf)
    if sink is not None:
        m = jnp.maximum(m, sink)
    es = [jnp.exp2(s - m) for s in pieces]
    l = _row_reduce(es, jnp.add, jnp.sum, 0.0)
    if sink is not None:
        l = l + jnp.exp2(sink - m)
    acc = None
    for e, (v, on_rows) in zip(es, vals):
        part = lax.dot_general(e.astype(_BF16), v, _NN if on_rows else _NT, preferred_element_type=_F32)
        acc = part if acc is None else acc + part
    return acc / l


def _softmax_pv(scores, vals_a, vals_b, sink_ref):
    outs = []
    head_of_lane = lax.broadcasted_iota(jnp.int32, (CHUNK, 2 * LANES), 1) // HEAD_DIM
    for g in range(2):
        of = _softmax_values(scores[g], vals_a(g), None)
        og = of[3 * CHUNK:4 * CHUNK]
        for r in (2, 1, 0):
            og = jnp.where(head_of_lane == r, of[r * CHUNK:(r + 1) * CHUNK], og)
        outs.append(og)
    lo = lax.broadcasted_iota(jnp.int32, (CHUNK, LANES), 1) < HEAD_DIM
    for g in range(B_KV_HEADS):
        of = _softmax_values(scores[2 + g], vals_b(g), sink_ref[g] * LOG2E)
        outs.append(jnp.where(lo, of[0:CHUNK], of[CHUNK:2 * CHUNK]))
        outs.append(jnp.where(lo, of[2 * CHUNK:3 * CHUNK], of[3 * CHUNK:4 * CHUNK]))
    return jnp.concatenate(outs, axis=1)


def _attn_prompt_kernel(qa_ref, ka_ref, va_ref, qb_ref, kb_ref, vb_ref, u_ref, sink_ref, o_ref,
                        kwa, vwa, kwb, vwb, bias):
    i = pl.program_id(0)
    tm = qa_ref.shape[0]
    n_chunks = tm // CHUNK

    @pl.when(i == 0)
    def _():
        kwa[0:A_CTX, :] = jnp.zeros((A_CTX, A_WIDTH), _BF16)
        vwa[0:A_CTX, :] = jnp.zeros((A_CTX, A_WIDTH), _BF16)
        kwb[0:B_CTX, :] = jnp.zeros((B_CTX, 2 * B_KV_WIDTH), _BF16)
        vwb[0:B_CTX, :] = jnp.zeros((B_CTX, 2 * B_KV_WIDTH), _BF16)
        _build_bias(u_ref, bias)

    @pl.when(i > 0)
    def _():
        kwa[0:A_CTX, :] = kwa[tm:tm + A_CTX, :]
        vwa[0:A_CTX, :] = vwa[tm:tm + A_CTX, :]
        kwb[0:B_CTX, :] = kwb[tm:tm + B_CTX, :]
        vwb[0:B_CTX, :] = vwb[tm:tm + B_CTX, :]

    kwa[A_CTX:A_CTX + tm, :] = ka_ref[...]
    vwa[A_CTX:A_CTX + tm, :] = va_ref[...]
    kwb[B_CTX:B_CTX + tm, :] = kb_ref[...]
    vwb[B_CTX:B_CTX + tm, :] = vb_ref[...]

    def window(ref, r0, rows, width):
        return lambda g: [(ref[r0:r0 + rows, width * g:width * (g + 1)], True)]

    def run(masked):
        def scores(c):
            r0 = c * CHUNK
            lim_a = max(A_LEFT_CHUNKS - c, 0) * CHUNK if masked else 0
            lim_b = max(B_CTX // CHUNK - c, 0) * CHUNK if masked else 0
            return _scores(qa_ref[r0:r0 + CHUNK, :], qb_ref[r0:r0 + CHUNK, :],
                           window(kwa, r0, A_WIN, 2 * LANES), window(kwb, r0, B_WIN, LANES), bias, lim_a, lim_b)

        nxt = scores(0)
        for c in range(n_chunks):
            cur = nxt
            if c + 1 < n_chunks:
                nxt = scores(c + 1)
            r0 = c * CHUNK
            o = _softmax_pv(cur, window(vwa, r0, A_WIN, 2 * LANES), window(vwb, r0, B_WIN, LANES), sink_ref)
            o_ref[r0:r0 + CHUNK, :] = o.astype(_BF16)

    @pl.when(i == 0)
    def _():
        run(True)

    @pl.when(i > 0)
    def _():
        run(False)


def _attn_sample_kernel(qa_ref, ka_ref, va_ref, qb_ref, kb_ref, vb_ref, cak_ref, cav_ref, cbk_ref, cbv_ref,
                        u_ref, sink_ref, o_ref, bias):
    n_streams = cak_ref.shape[0]

    @pl.when(pl.program_id(0) == 0)
    def _():
        _build_bias(u_ref, bias)

    def pieces_a(cache_ref, new_ref, b):
        rows = slice(b * CHUNK, (b + 1) * CHUNK)

        def get(g):
            cols = slice(2 * LANES * g, 2 * LANES * (g + 1))
            return [(cache_ref[b, cols, :].astype(_BF16), False), (new_ref[rows, cols], True)]
        return get

    def pieces_b(cache_ref, new_ref, b):
        rows = slice(b * CHUNK, (b + 1) * CHUNK)

        def get(g):
            past = cache_ref[b, HEAD_DIM * g:HEAD_DIM * (g + 1), :].astype(_BF16)
            return [(jnp.concatenate([past, past], axis=0), False), (new_ref[rows, LANES * g:LANES * (g + 1)], True)]
        return get

    def scores(b):
        rows = slice(b * CHUNK, (b + 1) * CHUNK)
        return _scores(qa_ref[rows, :], qb_ref[rows, :], pieces_a(cak_ref, ka_ref, b), pieces_b(cbk_ref, kb_ref, b),
                       bias, 0, 0)

    nxt = scores(0)
    for b in range(n_streams):
        cur = nxt
        if b + 1 < n_streams:
            nxt = scores(b + 1)
        o = _softmax_pv(cur, pieces_a(cav_ref, va_ref, b), pieces_b(cbv_ref, vb_ref, b), sink_ref)
        o_ref[b * CHUNK:(b + 1) * CHUNK, :] = o.astype(_BF16)


_BIAS_SCRATCH = pltpu.VMEM((2, 4 * CHUNK, A_WIN), _F32)


def _attend_prompt(qa, ka, va, qb, kb, vb, u, sink):
    t = qa.shape[0]
    tm = ATTN_TILE
    tile = lambda w: pl.BlockSpec((tm, w), lambda i: (i, 0))
    return pl.pallas_call(
        _attn_prompt_kernel,
        grid=(t // tm,),
        in_specs=[tile(A_WIDTH), tile(A_WIDTH), tile(A_WIDTH), tile(B_WIDTH), tile(2 * B_KV_WIDTH),
                  tile(2 * B_KV_WIDTH), _const_spec(u.shape), _const_spec(sink.shape)],
        out_specs=tile(A_WIDTH + B_WIDTH),
        out_shape=jax.ShapeDtypeStruct((t, A_WIDTH + B_WIDTH), _BF16),
        scratch_shapes=[pltpu.VMEM((A_CTX + tm, A_WIDTH), _BF16), pltpu.VMEM((A_CTX + tm, A_WIDTH), _BF16),
                        pltpu.VMEM((B_CTX + tm, 2 * B_KV_WIDTH), _BF16),
                        pltpu.VMEM((B_CTX + tm, 2 * B_KV_WIDTH), _BF16), _BIAS_SCRATCH],
        name="attn_prompt",
        compiler_params=pltpu.CompilerParams(dimension_semantics=("arbitrary",), vmem_limit_bytes=VMEM_LIMIT),
    )(qa, ka, va, qb, kb, vb, u, sink)


def _attend_sample(qa, ka, va, qb, kb, vb, cak, cav, cbk, cbv, u, sink):
    t = qa.shape[0]
    sb = SAMPLE_STREAMS
    tile = lambda w: pl.BlockSpec((sb * CHUNK, w), lambda i: (i, 0))
    cache = lambda arr: pl.BlockSpec((sb,) + arr.shape[1:], lambda i: (i, 0, 0))
    return pl.pallas_call(
        _attn_sample_kernel,
        grid=(t // (sb * CHUNK),),
        in_specs=[tile(A_WIDTH), tile(A_WIDTH), tile(A_WIDTH), tile(B_WIDTH), tile(2 * B_KV_WIDTH),
                  tile(2 * B_KV_WIDTH), cache(cak), cache(cav), cache(cbk), cache(cbv),
                  _const_spec(u.shape), _const_spec(sink.shape)],
        out_specs=tile(A_WIDTH + B_WIDTH),
        out_shape=jax.ShapeDtypeStruct((t, A_WIDTH + B_WIDTH), _BF16),
        scratch_shapes=[_BIAS_SCRATCH],
        name="attn_sample",
        compiler_params=pltpu.CompilerParams(dimension_semantics=("arbitrary",), vmem_limit_bytes=VMEM_LIMIT),
    )(qa, ka, va, qb, kb, vb, cak, cav, cbk, cbv, u, sink)


def _ffn_kernel(o_ref, x_ref, wo_ref, gn_ref, wgu_ref, cw_ref, cb_ref, wd_ref, st_ref,
                y_ref, cn_ref, carry, act_buf, *, seg_len, carry_rows):
    i = pl.program_id(0)
    tm = x_ref.shape[0]
    n_seg = tm // seg_len

    if carry_rows:
        @pl.when(i == 0)
        def _():
            carry[...] = st_ref[0]

    y1 = x_ref[...] + jnp.dot(o_ref[...], wo_ref[...], preferred_element_type=_F32)
    ms = jnp.mean(y1 * y1, axis=-1, keepdims=True)
    h = (y1 * lax.rsqrt(ms + EPS) * gn_ref[...]).astype(_BF16)

    row8 = lax.broadcasted_iota(jnp.int32, (8, FF_CHUNK), 0)

    def gate_up(c0):
        g = jnp.dot(h, wgu_ref[:, c0:c0 + FF_CHUNK], preferred_element_type=_F32)
        up = jnp.dot(h, wgu_ref[:, D_FF + c0:D_FF + c0 + FF_CHUNK], preferred_element_type=_F32)
        return g, up

    nxt = gate_up(0)
    for c0 in range(0, D_FF, FF_CHUNK):
        cols = slice(c0, c0 + FF_CHUNK)
        g, up = nxt
        if c0 + FF_CHUNK < D_FF:
            nxt = gate_up(c0 + FF_CHUNK)
        g1 = pltpu.roll(g, 1, 0)
        g2 = pltpu.roll(g, 2, 0)
        p1, p2 = [], []
        for s in range(n_seg):
            a = s * seg_len
            prev = carry[:, cols] if carry_rows else st_ref[s, :, cols]
            prev0 = jnp.broadcast_to(prev[0:1, :], (8, FF_CHUNK))
            prev1 = jnp.broadcast_to(prev[1:2, :], (8, FF_CHUNK))
            p1.append(jnp.where(row8 == 0, prev1, g1[a:a + 8]))
            p1.append(g1[a + 8:a + seg_len])
            p2.append(jnp.where(row8 == 0, prev0, jnp.where(row8 == 1, prev1, g2[a:a + 8])))
            p2.append(g2[a + 8:a + seg_len])
            tail = g[a + seg_len - 8:a + seg_len][8 - (CONV_W - 1):, :]
            if carry_rows:
                carry[:, cols] = tail
            else:
                cn_ref[s, :, cols] = tail
        g1 = jnp.concatenate(p1, axis=0)
        g2 = jnp.concatenate(p2, axis=0)
        conv = cb_ref[:, cols] + cw_ref[0:1, cols] * g2 + cw_ref[1:2, cols] * g1 + cw_ref[2:3, cols] * g
        act_buf[:, cols] = (conv * jax.nn.sigmoid(conv) * up).astype(_BF16)
    y_ref[...] = y1 + jnp.dot(act_buf[...], wd_ref[...], preferred_element_type=_F32)
    if carry_rows:
        cn_ref[0] = carry[...]


def _out_ffn(o, x, w_out, gn, w_gu, conv_w, conv_b, w_down, state, *, carry_rows, name):
    t = x.shape[0]
    tm = TOKEN_TILE
    n = t // tm
    seg_len = tm if carry_rows else CHUNK
    n_seg = tm // seg_len
    tile = lambda w: pl.BlockSpec((tm, w), lambda i: (i, 0))
    if carry_rows:
        st_spec = _const_spec((1, CONV_W - 1, D_FF))
        cn_spec = pl.BlockSpec((1, CONV_W - 1, D_FF), lambda i: (0, 0, 0))
        cn_shape = (1, CONV_W - 1, D_FF)
    else:
        st_spec = pl.BlockSpec((n_seg, CONV_W - 1, D_FF), lambda i: (i, 0, 0))
        cn_spec = pl.BlockSpec((n_seg, CONV_W - 1, D_FF), lambda i: (i, 0, 0))
        cn_shape = (t // seg_len, CONV_W - 1, D_FF)
    return pl.pallas_call(
        functools.partial(_ffn_kernel, seg_len=seg_len, carry_rows=carry_rows),
        grid=(n,),
        in_specs=[tile(A_WIDTH + B_WIDTH), tile(D_MODEL), _const_spec((D_MODEL, D_MODEL)),
                  _const_spec((1, D_MODEL)), _const_spec((D_MODEL, 2 * D_FF)), _const_spec((CONV_W, D_FF)),
                  _const_spec((1, D_FF)), _const_spec((D_FF, D_MODEL)), st_spec],
        out_specs=[tile(D_MODEL), cn_spec],
        out_shape=[jax.ShapeDtypeStruct((t, D_MODEL), _F32), jax.ShapeDtypeStruct(cn_shape, _F32)],
        scratch_shapes=[pltpu.VMEM((CONV_W - 1, D_FF), _F32), pltpu.VMEM((tm, D_FF), _BF16)],
        name=name,
        compiler_params=pltpu.CompilerParams(dimension_semantics=("arbitrary",), vmem_limit_bytes=VMEM_LIMIT),
    )(o, x, w_out, gn, w_gu, conv_w, conv_b, w_down, state)


def _rope_tables(tile_pos, row_pos):
    half = HEAD_DIM // 2
    lane = jnp.arange(LANES)
    inv = jnp.power(ROPE_THETA, -(lane % half).astype(_F32) / half)
    sign = jnp.where(lane % HEAD_DIM < half, -1.0, 1.0).astype(_F32)
    a = tile_pos.astype(_F32)[:, None] * inv[None, :]
    b = row_pos.astype(_F32)[:, None] * inv[None, :]
    return (jnp.cos(a)[:, None, :], (jnp.sin(a) * sign)[:, None, :], jnp.cos(b), jnp.sin(b) * sign)


def _feature_major(cache):
    n, t, heads, d = cache.shape
    return jnp.transpose(cache, (0, 2, 3, 1)).reshape(n, heads * d, t)


def _token_major(rows, heads):
    t = rows.shape[1]
    return jnp.transpose(rows.reshape(heads, HEAD_DIM, t), (2, 0, 1))[None, None]


def _bias_profile(table):
    n_clip = BIAS_COLS - 1 - MAX_REL
    head = jnp.broadcast_to(table[:, 2 * MAX_REL:], (A_HEADS, n_clip))
    return jnp.concatenate([head, table[:, ::-1][:, :BIAS_VAR - n_clip]], axis=1)


def kernel(x_prompt, x_sample, cache_a_k, cache_a_v, cache_b_k, cache_b_v, state_conv, norm_attn, w_in, q_norm_a,
           k_norm_a, rel_bias_a, q_norm_b, k_norm_b, sinks_b, w_out, norm_ffn, w_gate_up, conv_w, conv_b, w_down):
    assert norm_attn.shape[0] == 1
    bsz, seq, _ = x_prompt.shape
    dec_b, dec_s, _ = x_sample.shape
    assert bsz == 1 and dec_s == CHUNK and (dec_b * dec_s) % TOKEN_TILE == 0
    assert seq % TOKEN_TILE == 0 and seq % PROMPT_PROJ_TILE == 0 and seq % ATTN_TILE == 0
    assert cache_a_k.shape[2] == A_CTX and cache_b_k.shape[2] == B_CTX

    w_in_b = w_in[0].astype(_BF16)
    gn_a = norm_attn[0][None, :]
    gn_f = norm_ffn[0][None, :]
    gqa = jnp.tile(q_norm_a[0], A_HEADS)[None, :]
    gka = jnp.tile(k_norm_a[0], A_HEADS)[None, :]
    gqb = jnp.tile(q_norm_b[0], B_HEADS)[None, :]
    gkb = jnp.tile(k_norm_b[0], B_KV_HEADS)[None, :]
    u = _bias_profile(rel_bias_a[0])
    sink = jnp.repeat(sinks_b[0], CHUNK).reshape(B_KV_HEADS, B_GROUP * CHUNK, 1)
    cw = conv_w[0]
    cb = conv_b[0][None, :]

    rope_p = _rope_tables(jnp.arange(0, seq, PROMPT_PROJ_TILE), jnp.arange(PROMPT_PROJ_TILE))
    rope_s = _rope_tables(jnp.full((1,), PAST_LEN), jnp.arange(TOKEN_TILE) % dec_s)

    xp = x_prompt.reshape(seq, D_MODEL)
    qa, ka, va, qb, kb, vb, kaf, vaf, kbf, vbf, w_out_b, w_gu_b, w_down_b = _project(
        xp, gn_a, w_in_b, gqa, gka, gqb, gkb, rope_p, keep_all=False, name="proj_prompt",
        cast=(w_out[0], w_gate_up[0], w_down[0]))
    o_p = _attend_prompt(qa, ka, va, qb, kb, vb, u, sink)
    zero_state = jnp.zeros((1, CONV_W - 1, D_FF), _F32)
    y_p, cn_p = _out_ffn(o_p, xp, w_out_b, gn_f, w_gu_b, cw, cb, w_down_b, zero_state,
                         carry_rows=True, name="ffn_prompt")

    xs = x_sample.reshape(dec_b * dec_s, D_MODEL)
    qa, ka, va, qb, kb, vb, kaf_s, vaf_s, kbf_s, vbf_s = _project(
        xs, gn_a, w_in_b, gqa, gka, gqb, gkb, rope_s, keep_all=True, name="proj_sample")
    o_s = _attend_sample(qa, ka, va, qb, kb, vb, _feature_major(cache_a_k[0]), _feature_major(cache_a_v[0]),
                         _feature_major(cache_b_k[0]), _feature_major(cache_b_v[0]), u, sink)
    y_s, cn_s = _out_ffn(o_s, xs, w_out_b, gn_f, w_gu_b, cw, cb, w_down_b, state_conv[0],
                         carry_rows=False, name="ffn_sample")

    return (
        y_p.reshape(1, seq, D_MODEL),
        y_s.reshape(dec_b, dec_s, D_MODEL),
        _token_major(kaf, A_HEADS),
        _token_major(vaf, A_HEADS),
        _token_major(kbf, B_KV_HEADS),
        _token_major(vbf, B_KV_HEADS),
        cn_p.reshape(1, 1, CONV_W - 1, D_FF),
        kaf_s.reshape(1, dec_b, dec_s, A_HEADS, HEAD_DIM),
        vaf_s.reshape(1, dec_b, dec_s, A_HEADS, HEAD_DIM),
        kbf_s.reshape(1, dec_b, dec_s, B_KV_HEADS, HEAD_DIM),
        vbf_s.reshape(1, dec_b, dec_s, B_KV_HEADS, HEAD_DIM),
        cn_s.reshape(1, dec_b, CONV_W - 1, D_FF),
    )
```

```python
import functools

import jax
import jax.numpy as jnp
from jax import lax
from jax.experimental import pallas as pl
from jax.experimental.pallas import tpu as pltpu

D_MODEL = 1024
CHUNK = 64
HEAD_DIM = 64
EPS = 1e-6
SCALE = HEAD_DIM ** -0.5
LOG2E = 1.4426950408889634
Q_SCALE = SCALE * LOG2E
NEG = -1e30
PAST_LEN = 1024
A_HEADS = 8
A_LEFT_CHUNKS = 8
A_CTX = A_LEFT_CHUNKS * CHUNK
A_WIN = A_CTX + CHUNK
MAX_REL = 128
A_WIDTH = A_HEADS * HEAD_DIM
B_HEADS = 8
B_KV_HEADS = 2
B_GROUP = B_HEADS // B_KV_HEADS
B_CTX = 128
B_WIN = B_CTX + CHUNK
B_WIDTH = B_HEADS * HEAD_DIM
B_KV_WIDTH = B_KV_HEADS * HEAD_DIM
ROPE_THETA = 10000.0
IN_COLS = 3 * A_WIDTH + B_WIDTH + 2 * B_KV_WIDTH
D_FF = 2816
CONV_W = 3

LANES = 128
TOKEN_TILE = 512
PROMPT_PROJ_TILE = 1024
ATTN_TILE = 1024
SAMPLE_STREAMS = 4
FF_CHUNK = 256
BIAS_COLS = MAX_REL + CHUNK
BIAS_VAR = 256
VMEM_LIMIT = 56 * 1024 * 1024

_BF16 = jnp.bfloat16
_F32 = jnp.float32


def _const_spec(shape):
    nd = len(shape)
    return pl.BlockSpec(shape, lambda i: (0,) * nd, pipeline_mode=pl.Buffered(1))


def _head_sumsq(t, bd):
    sq = (t * t).astype(_BF16)
    w = t.shape[1]
    if w <= 2 * LANES:
        return jnp.dot(sq, bd[:w, :w], preferred_element_type=_F32)
    parts = [jnp.dot(sq[:, c:c + 2 * LANES], bd, preferred_element_type=_F32) for c in range(0, w, 2 * LANES)]
    return jnp.concatenate(parts, axis=1)


def _head_norm(t, bd, gain):
    ss = _head_sumsq(t, bd)
    return t * lax.rsqrt(ss * (1.0 / HEAD_DIM) + EPS) * gain


def _rope(t, cos, sin_signed):
    outs = []
    for c in range(0, t.shape[1], LANES):
        tc = t[:, c:c + LANES]
        lane = lax.broadcasted_iota(jnp.int32, tc.shape, 1)
        first_half = (lane & (HEAD_DIM - 1)) < HEAD_DIM // 2
        rot = jnp.where(first_half, pltpu.roll(tc, LANES - HEAD_DIM // 2, 1), pltpu.roll(tc, HEAD_DIM // 2, 1))
        outs.append(tc * cos + rot * sin_signed)
    return outs[0] if len(outs) == 1 else jnp.concatenate(outs, axis=1)


def _rep_pair(t):
    rolled = pltpu.roll(t, HEAD_DIM, 1)
    lo = lax.broadcasted_iota(jnp.int32, t.shape, 1) < HEAD_DIM
    return jnp.concatenate([jnp.where(lo, t, rolled), jnp.where(lo, rolled, t)], axis=1)


def _store_heads(ref, t):
    n, w = t.shape
    heads = w // HEAD_DIM
    for hd in range(heads):
        ref[pl.ds(hd, n, stride=heads), :] = t[:, hd * HEAD_DIM:(hd + 1) * HEAD_DIM]


def _proj_kernel(x_ref, gn_ref, w_ref, gqa_ref, gka_ref, gqb_ref, gkb_ref, ct_ref, st_ref, cr_ref, sr_ref, *rest,
                 last_only, n_cast):
    cast_in, rest = rest[:n_cast], rest[n_cast:]
    qa_ref, ka_ref, va_ref, qb_ref, kb_ref, vb_ref, kaf_ref, vaf_ref, kbf_ref, vbf_ref = rest[:10]
    for src, dst in zip(cast_in, rest[10:]):
        dst[...] = src[...].astype(_BF16)
    tm = x_ref.shape[0]
    x = x_ref[...]
    ms = jnp.mean(x * x, axis=-1, keepdims=True)
    h = (x * lax.rsqrt(ms + EPS) * gn_ref[...]).astype(_BF16)

    r = lax.broadcasted_iota(jnp.int32, (2 * LANES, 2 * LANES), 0) // HEAD_DIM
    c = lax.broadcasted_iota(jnp.int32, (2 * LANES, 2 * LANES), 1) // HEAD_DIM
    bd = jnp.where(r == c, 1.0, 0.0).astype(_BF16)
    ct, st, cr, sr = ct_ref[0], st_ref[0], cr_ref[...], sr_ref[...]
    cos = ct * cr - st * sr
    sin = st * cr + ct * sr

    def proj(lo, width):
        return jnp.dot(h, w_ref[:, lo:lo + width], preferred_element_type=_F32)

    o_ka, o_va, o_qb, o_kvb = A_WIDTH, 2 * A_WIDTH, 3 * A_WIDTH, 3 * A_WIDTH + B_WIDTH
    p_qa = proj(0, A_WIDTH)
    p_ka = proj(o_ka, A_WIDTH)
    qa_ref[...] = (_head_norm(p_qa, bd, gqa_ref[...]) * Q_SCALE).astype(_BF16)
    va = proj(o_va, A_WIDTH)
    ka = _head_norm(p_ka, bd, gka_ref[...])
    ka_ref[...] = ka.astype(_BF16)
    p_qb = proj(o_qb, B_WIDTH)
    va_ref[...] = va.astype(_BF16)
    p_kvb = proj(o_kvb, 2 * B_KV_WIDTH)
    qb_ref[...] = (_rope(_head_norm(p_qb, bd, gqb_ref[...]), cos, sin) * Q_SCALE).astype(_BF16)
    kb = _rope(_head_norm(p_kvb[:, :B_KV_WIDTH], bd, gkb_ref[...]), cos, sin)
    vb = p_kvb[:, B_KV_WIDTH:]
    kb_ref[...] = _rep_pair(kb).astype(_BF16)
    vb_ref[...] = _rep_pair(vb).astype(_BF16)

    if last_only:
        @pl.when(pl.program_id(0) == pl.num_programs(0) - 1)
        def _():
            kaf_ref[...] = ka[tm - A_CTX:, :].T
            vaf_ref[...] = va[tm - A_CTX:, :].T
            kbf_ref[...] = kb[tm - B_CTX:, :].T
            vbf_ref[...] = vb[tm - B_CTX:, :].T
    else:
        _store_heads(kaf_ref, ka)
        _store_heads(vaf_ref, va)
        kbf_ref[...] = kb
        vbf_ref[...] = vb


def _cast_blocks(w, n_steps):
    rows = w.shape[0]
    n_blocks = max(d for d in range(1, n_steps + 1) if rows % d == 0 and (rows // d) % 16 == 0)
    return pl.BlockSpec((rows // n_blocks, w.shape[1]), lambda i: (jnp.minimum(i, n_blocks - 1), 0))


def _project(x, gn, w_in, gqa, gka, gqb, gkb, rope, *, keep_all, name, cast=()):
    t = x.shape[0]
    tm = rope[2].shape[0]
    n = t // tm
    cast_specs = [_cast_blocks(w, n) for w in cast]
    tile = lambda w: pl.BlockSpec((tm, w), lambda i: (i, 0))
    if keep_all:
        f32_shapes = [(t * A_HEADS, HEAD_DIM), (t * A_HEADS, HEAD_DIM), (t, B_KV_WIDTH), (t, B_KV_WIDTH)]
        f32_specs = [pl.BlockSpec((tm * A_HEADS, HEAD_DIM), lambda i: (i, 0))] * 2 + [tile(B_KV_WIDTH)] * 2
    else:
        f32_shapes = [(A_WIDTH, A_CTX), (A_WIDTH, A_CTX), (B_KV_WIDTH, B_CTX), (B_KV_WIDTH, B_CTX)]
        f32_specs = [pl.BlockSpec(sh, lambda i: (0, 0)) for sh in f32_shapes]
    ct, st, cr, sr = rope
    if ct.shape[0] == n:
        tile_trig = pl.BlockSpec((1, 1, LANES), lambda i: (i, 0, 0))
    else:
        tile_trig = _const_spec((1, 1, LANES))
    out_shape = [
        jax.ShapeDtypeStruct((t, A_WIDTH), _BF16), jax.ShapeDtypeStruct((t, A_WIDTH), _BF16),
        jax.ShapeDtypeStruct((t, A_WIDTH), _BF16), jax.ShapeDtypeStruct((t, B_WIDTH), _BF16),
        jax.ShapeDtypeStruct((t, 2 * B_KV_WIDTH), _BF16), jax.ShapeDtypeStruct((t, 2 * B_KV_WIDTH), _BF16),
    ] + [jax.ShapeDtypeStruct(sh, _F32) for sh in f32_shapes] + [jax.ShapeDtypeStruct(w.shape, _BF16) for w in cast]
    out_specs = [
        tile(A_WIDTH), tile(A_WIDTH), tile(A_WIDTH), tile(B_WIDTH), tile(2 * B_KV_WIDTH), tile(2 * B_KV_WIDTH),
    ] + f32_specs + cast_specs
    in_specs = [
        tile(D_MODEL), _const_spec((1, D_MODEL)), _const_spec((D_MODEL, IN_COLS)),
        _const_spec((1, A_WIDTH)), _const_spec((1, A_WIDTH)), _const_spec((1, B_WIDTH)),
        _const_spec((1, B_KV_WIDTH)), tile_trig, tile_trig, _const_spec((tm, LANES)), _const_spec((tm, LANES)),
    ] + cast_specs
    return pl.pallas_call(
        functools.partial(_proj_kernel, last_only=not keep_all, n_cast=len(cast)),
        grid=(n,), in_specs=in_specs, out_specs=out_specs, out_shape=out_shape, name=name,
        compiler_params=pltpu.CompilerParams(dimension_semantics=("arbitrary",), vmem_limit_bytes=VMEM_LIMIT),
    )(x, gn, w_in, gqa, gka, gqb, gkb, ct, st, cr, sr, *cast)


def _build_bias(u_ref, bias_ref):
    qrow = lax.broadcasted_iota(jnp.int32, (CHUNK, BIAS_VAR), 0)
    for h in range(A_HEADS):
        prof = u_ref[h:h + 1, :]
        y = jnp.broadcast_to(prof - prof[:, 0:1], (CHUNK, BIAS_VAR))
        y = pltpu.roll(y, BIAS_VAR - (CHUNK - 1), 1)
        for b in range(6):
            y = jnp.where(((qrow >> b) & 1) == 1, pltpu.roll(y, 1 << b, 1), y)
        bias_ref[h // 4, (h % 4) * CHUNK:(h % 4 + 1) * CHUNK, :] = y[:, :BIAS_COLS] * LOG2E


_NT = (((1,), (1,)), ((), ()))
_NN = (((1,), (0,)), ((), ()))


def _scores(qa, qb, keys_a, keys_b, bias_ref, lim_a, lim_b):
    out = []
    head_of_lane = lax.broadcasted_iota(jnp.int32, (CHUNK, 2 * LANES), 1) // HEAD_DIM
    for g in range(2):
        qg = qa[:, 2 * LANES * g:2 * LANES * (g + 1)]
        qbd = jnp.concatenate([jnp.where(head_of_lane == r, qg, jnp.zeros_like(qg)) for r in range(4)], axis=0)
        pieces, off = [], 0
        for keys, on_rows in keys_a(g):
            s = lax.dot_general(qbd, keys, _NT if on_rows else _NN, preferred_element_type=_F32)
            n = s.shape[1]
            if off + n > A_WIN - BIAS_COLS:
                lo_col = max(A_WIN - BIAS_COLS - off, 0)
                biased = s[:, lo_col:] + bias_ref[g, :, off + lo_col - (A_WIN - BIAS_COLS):off + n - (A_WIN - BIAS_COLS)]
                s = biased if lo_col == 0 else jnp.concatenate([s[:, :lo_col], biased], axis=1)
            if lim_a:
                s = jnp.where(lax.broadcasted_iota(jnp.int32, s.shape, 1) >= lim_a, s, NEG)
            pieces.append(s)
            off += n
        out.append(pieces)
    lo = lax.broadcasted_iota(jnp.int32, (CHUNK, LANES), 1) < HEAD_DIM
    for g in range(B_KV_HEADS):
        rows = []
        for r in range(B_GROUP):
            hd = B_GROUP * g + r
            qp = qb[:, LANES * (hd // 2):LANES * (hd // 2 + 1)]
            keep = lo if hd % 2 == 0 else jnp.logical_not(lo)
            rows.append(jnp.where(keep, qp, jnp.zeros_like(qp)))
        qs = jnp.concatenate(rows, axis=0)
        pieces = []
        for keys, on_rows in keys_b(g):
            s = lax.dot_general(qs, keys, _NT if on_rows else _NN, preferred_element_type=_F32)
            if lim_b:
                s = jnp.where(lax.broadcasted_iota(jnp.int32, s.shape, 1) >= lim_b, s, NEG)
            pieces.append(s)
        out.append(pieces)
    return out


def _row_reduce(pieces, combine, reduce, fill):
    acc = None
    for s in pieces:
        for c in range(0, s.shape[1], LANES):
            tile = s[:, c:c + LANES]
            if tile.shape[1] < LANES:
                pad = jnp.full((tile.shape[0], LANES - tile.shape[1]), fill, tile.dtype)
                tile = jnp.concatenate([tile, pad], axis=1)
            acc = tile if acc is None else combine(acc, tile)
    return reduce(acc, axis=1, keepdims=True)


def _softmax_values(pieces, vals, sink):
    m = _row_reduce(pieces, jnp.maximum, jnp.max, -jnp.inf)
    if sink is not None:
        m = jnp.maximum(m, sink)
    es = [jnp.exp2(s - m) for s in pieces]
    l = _row_reduce(es, jnp.add, jnp.sum, 0.0)
    if sink is not None:
        l = l + jnp.exp2(sink - m)
    acc = None
    for e, (v, on_rows) in zip(es, vals):
        part = lax.dot_general(e.astype(_BF16), v, _NN if on_rows else _NT, preferred_element_type=_F32)
        acc = part if acc is None else acc + part
    return acc / l


def _softmax_pv(scores, vals_a, vals_b, sink_ref):
    outs = []
    head_of_lane = lax.broadcasted_iota(jnp.int32, (CHUNK, 2 * LANES), 1) // HEAD_DIM
    for g in range(2):
        of = _softmax_values(scores[g], vals_a(g), None)
        og = of[3 * CHUNK:4 * CHUNK]
        for r in (2, 1, 0):
            og = jnp.where(head_of_lane == r, of[r * CHUNK:(r + 1) * CHUNK], og)
        outs.append(og)
    lo = lax.broadcasted_iota(jnp.int32, (CHUNK, LANES), 1) < HEAD_DIM
    for g in range(B_KV_HEADS):
        of = _softmax_values(scores[2 + g], vals_b(g), sink_ref[g] * LOG2E)
        outs.append(jnp.where(lo, of[0:CHUNK], of[CHUNK:2 * CHUNK]))
        outs.append(jnp.where(lo, of[2 * CHUNK:3 * CHUNK], of[3 * CHUNK:4 * CHUNK]))
    return jnp.concatenate(outs, axis=1)


def _attn_prompt_kernel(qa_ref, ka_ref, va_ref, qb_ref, kb_ref, vb_ref, u_ref, sink_ref, o_ref,
                        kwa, vwa, kwb, vwb, bias):
    i = pl.program_id(0)
    tm = qa_ref.shape[0]
    n_chunks = tm // CHUNK

    @pl.when(i == 0)
    def _():
        kwa[0:A_CTX, :] = jnp.zeros((A_CTX, A_WIDTH), _BF16)
        vwa[0:A_CTX, :] = jnp.zeros((A_CTX, A_WIDTH), _BF16)
        kwb[0:B_CTX, :] = jnp.zeros((B_CTX, 2 * B_KV_WIDTH), _BF16)
        vwb[0:B_CTX, :] = jnp.zeros((B_CTX, 2 * B_KV_WIDTH), _BF16)
        _build_bias(u_ref, bias)

    @pl.when(i > 0)
    def _():
        kwa[0:A_CTX, :] = kwa[tm:tm + A_CTX, :]
        vwa[0:A_CTX, :] = vwa[tm:tm + A_CTX, :]
        kwb[0:B_CTX, :] = kwb[tm:tm + B_CTX, :]
        vwb[0:B_CTX, :] = vwb[tm:tm + B_CTX, :]

    kwa[A_CTX:A_CTX + tm, :] = ka_ref[...]
    vwa[A_CTX:A_CTX + tm, :] = va_ref[...]
    kwb[B_CTX:B_CTX + tm, :] = kb_ref[...]
    vwb[B_CTX:B_CTX + tm, :] = vb_ref[...]

    def window(ref, r0, rows, width):
        return lambda g: [(ref[r0:r0 + rows, width * g:width * (g + 1)], True)]

    def run(masked):
        def scores(c):
            r0 = c * CHUNK
            lim_a = max(A_LEFT_CHUNKS - c, 0) * CHUNK if masked else 0
            lim_b = max(B_CTX // CHUNK - c, 0) * CHUNK if masked else 0
            return _scores(qa_ref[r0:r0 + CHUNK, :], qb_ref[r0:r0 + CHUNK, :],
                           window(kwa, r0, A_WIN, 2 * LANES), window(kwb, r0, B_WIN, LANES), bias, lim_a, lim_b)

        nxt = scores(0)
        for c in range(n_chunks):
            cur = nxt
            if c + 1 < n_chunks:
                nxt = scores(c + 1)
            r0 = c * CHUNK
            o = _softmax_pv(cur, window(vwa, r0, A_WIN, 2 * LANES), window(vwb, r0, B_WIN, LANES), sink_ref)
            o_ref[r0:r0 + CHUNK, :] = o.astype(_BF16)

    @pl.when(i == 0)
    def _():
        run(True)

    @pl.when(i > 0)
    def _():
        run(False)


def _attn_sample_kernel(qa_ref, ka_ref, va_ref, qb_ref, kb_ref, vb_ref, cak_ref, cav_ref, cbk_ref, cbv_ref,
                        u_ref, sink_ref, o_ref, bias):
    n_streams = cak_ref.shape[0]

    @pl.when(pl.program_id(0) == 0)
    def _():
        _build_bias(u_ref, bias)

    def pieces_a(cache_ref, new_ref, b):
        rows = slice(b * CHUNK, (b + 1) * CHUNK)

        def get(g):
            cols = slice(2 * LANES * g, 2 * LANES * (g + 1))
            return [(cache_ref[b, cols, :].astype(_BF16), False), (new_ref[rows, cols], True)]
        return get

    def pieces_b(cache_ref, new_ref, b):
        rows = slice(b * CHUNK, (b + 1) * CHUNK)

        def get(g):
            past = cache_ref[b, HEAD_DIM * g:HEAD_DIM * (g + 1), :].astype(_BF16)
            return [(jnp.concatenate([past, past], axis=0), False), (new_ref[rows, LANES * g:LANES * (g + 1)], True)]
        return get

    def scores(b):
        rows = slice(b * CHUNK, (b + 1) * CHUNK)
        return _scores(qa_ref[rows, :], qb_ref[rows, :], pieces_a(cak_ref, ka_ref, b), pieces_b(cbk_ref, kb_ref, b),
                       bias, 0, 0)

    nxt = scores(0)
    for b in range(n_streams):
        cur = nxt
        if b + 1 < n_streams:
            nxt = scores(b + 1)
        o = _softmax_pv(cur, pieces_a(cav_ref, va_ref, b), pieces_b(cbv_ref, vb_ref, b), sink_ref)
        o_ref[b * CHUNK:(b + 1) * CHUNK, :] = o.astype(_BF16)


_BIAS_SCRATCH = pltpu.VMEM((2, 4 * CHUNK, BIAS_COLS), _F32)


def _attend_prompt(qa, ka, va, qb, kb, vb, u, sink):
    t = qa.shape[0]
    tm = ATTN_TILE
    tile = lambda w: pl.BlockSpec((tm, w), lambda i: (i, 0))
    return pl.pallas_call(
        _attn_prompt_kernel,
        grid=(t // tm,),
        in_specs=[tile(A_WIDTH), tile(A_WIDTH), tile(A_WIDTH), tile(B_WIDTH), tile(2 * B_KV_WIDTH),
                  tile(2 * B_KV_WIDTH), _const_spec(u.shape), _const_spec(sink.shape)],
        out_specs=tile(A_WIDTH + B_WIDTH),
        out_shape=jax.ShapeDtypeStruct((t, A_WIDTH + B_WIDTH), _BF16),
        scratch_shapes=[pltpu.VMEM((A_CTX + tm, A_WIDTH), _BF16), pltpu.VMEM((A_CTX + tm, A_WIDTH), _BF16),
                        pltpu.VMEM((B_CTX + tm, 2 * B_KV_WIDTH), _BF16),
                        pltpu.VMEM((B_CTX + tm, 2 * B_KV_WIDTH), _BF16), _BIAS_SCRATCH],
        name="attn_prompt",
        compiler_params=pltpu.CompilerParams(dimension_semantics=("arbitrary",), vmem_limit_bytes=VMEM_LIMIT),
    )(qa, ka, va, qb, kb, vb, u, sink)


def _attend_sample(qa, ka, va, qb, kb, vb, cak, cav, cbk, cbv, u, sink):
    t = qa.shape[0]
    sb = SAMPLE_STREAMS
    tile = lambda w: pl.BlockSpec((sb * CHUNK, w), lambda i: (i, 0))
    cache = lambda arr: pl.BlockSpec((sb,) + arr.shape[1:], lambda i: (i, 0, 0))
    return pl.pallas_call(
        _attn_sample_kernel,
        grid=(t // (sb * CHUNK),),
        in_specs=[tile(A_WIDTH), tile(A_WIDTH), tile(A_WIDTH), tile(B_WIDTH), tile(2 * B_KV_WIDTH),
                  tile(2 * B_KV_WIDTH), cache(cak), cache(cav), cache(cbk), cache(cbv),
                  _const_spec(u.shape), _const_spec(sink.shape)],
        out_specs=tile(A_WIDTH + B_WIDTH),
        out_shape=jax.ShapeDtypeStruct((t, A_WIDTH + B_WIDTH), _BF16),
        scratch_shapes=[_BIAS_SCRATCH],
        name="attn_sample",
        compiler_params=pltpu.CompilerParams(dimension_semantics=("arbitrary",), vmem_limit_bytes=VMEM_LIMIT),
    )(qa, ka, va, qb, kb, vb, cak, cav, cbk, cbv, u, sink)


def _ffn_tile(o_ref, x_ref, wo_ref, gn_ref, wgu_ref, cw_ref, cb_ref, wd_ref, st_ref,
              y_ref, cn_ref, carry, act_buf, *, seg_len, carry_rows):
    tm = x_ref.shape[0]
    n_seg = tm // seg_len

    if carry_rows:
        @pl.when(pl.program_id(0) == 0)
        def _():
            carry[...] = jnp.zeros(carry.shape, carry.dtype)

    y1 = x_ref[...] + jnp.dot(o_ref[...], wo_ref[...], preferred_element_type=_F32)
    ms = jnp.mean(y1 * y1, axis=-1, keepdims=True)
    h = (y1 * lax.rsqrt(ms + EPS) * gn_ref[...]).astype(_BF16)

    row8 = lax.broadcasted_iota(jnp.int32, (8, FF_CHUNK), 0)

    def gate_up(c0):
        g = jnp.dot(h, wgu_ref[:, c0:c0 + FF_CHUNK], preferred_element_type=_F32)
        up = jnp.dot(h, wgu_ref[:, D_FF + c0:D_FF + c0 + FF_CHUNK], preferred_element_type=_F32)
        return g, up

    nxt = gate_up(0)
    for c0 in range(0, D_FF, FF_CHUNK):
        cols = slice(c0, c0 + FF_CHUNK)
        g, up = nxt
        if c0 + FF_CHUNK < D_FF:
            nxt = gate_up(c0 + FF_CHUNK)
        g1 = pltpu.roll(g, 1, 0)
        g2 = pltpu.roll(g, 2, 0)
        p1, p2 = [], []
        for s in range(n_seg):
            a = s * seg_len
            prev = carry[:, cols] if carry_rows else st_ref[s, :, cols]
            prev0 = jnp.broadcast_to(prev[0:1, :], (8, FF_CHUNK))
            prev1 = jnp.broadcast_to(prev[1:2, :], (8, FF_CHUNK))
            p1.append(jnp.where(row8 == 0, prev1, g1[a:a + 8]))
            p1.append(g1[a + 8:a + seg_len])
            p2.append(jnp.where(row8 == 0, prev0, jnp.where(row8 == 1, prev1, g2[a:a + 8])))
            p2.append(g2[a + 8:a + seg_len])
            tail = g[a + seg_len - 8:a + seg_len][8 - (CONV_W - 1):, :]
            if carry_rows:
                carry[:, cols] = tail
            else:
                cn_ref[s, :, cols] = tail
        g1 = jnp.concatenate(p1, axis=0)
        g2 = jnp.concatenate(p2, axis=0)
        conv = cb_ref[:, cols] + cw_ref[0:1, cols] * g2 + cw_ref[1:2, cols] * g1 + cw_ref[2:3, cols] * g
        act_buf[:, cols] = (conv * jax.nn.sigmoid(conv) * up).astype(_BF16)
    y_ref[...] = y1 + jnp.dot(act_buf[...], wd_ref[...], preferred_element_type=_F32)
    if carry_rows:
        cn_ref[0] = carry[...]


def _ffn_kernel(op_ref, xp_ref, os_ref, xs_ref, wo_ref, gn_ref, wgu_ref, cw_ref, cb_ref, wd_ref, st_ref,
                yp_ref, cnp_ref, ys_ref, cns_ref, carry, act_buf, *, n_prompt):
    i = pl.program_id(0)
    weights = (wo_ref, gn_ref, wgu_ref, cw_ref, cb_ref, wd_ref, st_ref)

    @pl.when(i < n_prompt)
    def _():
        _ffn_tile(op_ref, xp_ref, *weights, yp_ref, cnp_ref, carry, act_buf,
                  seg_len=xp_ref.shape[0], carry_rows=True)

    @pl.when(i >= n_prompt)
    def _():
        _ffn_tile(os_ref, xs_ref, *weights, ys_ref, cns_ref, carry, act_buf, seg_len=CHUNK, carry_rows=False)


def _out_ffn(o_p, x_p, o_s, x_s, w_out, gn, w_gu, conv_w, conv_b, w_down, state):
    tm = TOKEN_TILE
    n_p, n_s = x_p.shape[0] // tm, x_s.shape[0] // tm
    n_seg = tm // CHUNK
    p_map = lambda i: (jnp.minimum(i, n_p - 1), 0)
    s_map = lambda i: (jnp.maximum(i - n_p, 0), 0)
    s_map3 = lambda i: (jnp.maximum(i - n_p, 0), 0, 0)
    st_block = (n_seg, CONV_W - 1, D_FF)
    return pl.pallas_call(
        functools.partial(_ffn_kernel, n_prompt=n_p),
        grid=(n_p + n_s,),
        in_specs=[pl.BlockSpec((tm, A_WIDTH + B_WIDTH), p_map), pl.BlockSpec((tm, D_MODEL), p_map),
                  pl.BlockSpec((tm, A_WIDTH + B_WIDTH), s_map), pl.BlockSpec((tm, D_MODEL), s_map),
                  _const_spec((D_MODEL, D_MODEL)), _const_spec((1, D_MODEL)), _const_spec((D_MODEL, 2 * D_FF)),
                  _const_spec((CONV_W, D_FF)), _const_spec((1, D_FF)), _const_spec((D_FF, D_MODEL)),
                  pl.BlockSpec(st_block, s_map3)],
        out_specs=[pl.BlockSpec((tm, D_MODEL), p_map), pl.BlockSpec((1, CONV_W - 1, D_FF), lambda i: (0, 0, 0)),
                   pl.BlockSpec((tm, D_MODEL), s_map), pl.BlockSpec(st_block, s_map3)],
        out_shape=[jax.ShapeDtypeStruct(x_p.shape, _F32), jax.ShapeDtypeStruct((1, CONV_W - 1, D_FF), _F32),
                   jax.ShapeDtypeStruct(x_s.shape, _F32), jax.ShapeDtypeStruct(state.shape, _F32)],
        scratch_shapes=[pltpu.VMEM((CONV_W - 1, D_FF), _F32), pltpu.VMEM((tm, D_FF), _BF16)],
        name="out_ffn",
        compiler_params=pltpu.CompilerParams(dimension_semantics=("arbitrary",), vmem_limit_bytes=VMEM_LIMIT),
    )(o_p, x_p, o_s, x_s, w_out, gn, w_gu, conv_w, conv_b, w_down, state)


def _rope_tables(tile_pos, row_pos):
    half = HEAD_DIM // 2
    lane = jnp.arange(LANES)
    inv = jnp.power(ROPE_THETA, -(lane % half).astype(_F32) / half)
    sign = jnp.where(lane % HEAD_DIM < half, -1.0, 1.0).astype(_F32)
    a = tile_pos.astype(_F32)[:, None] * inv[None, :]
    b = row_pos.astype(_F32)[:, None] * inv[None, :]
    return (jnp.cos(a)[:, None, :], (jnp.sin(a) * sign)[:, None, :], jnp.cos(b), jnp.sin(b) * sign)


def _feature_major(cache):
    n, t, heads, d = cache.shape
    return jnp.transpose(cache, (0, 2, 3, 1)).reshape(n, heads * d, t)


def _token_major(rows, heads):
    t = rows.shape[1]
    return jnp.transpose(rows.reshape(heads, HEAD_DIM, t), (2, 0, 1))[None, None]


def _bias_profile(table):
    n_clip = BIAS_COLS - 1 - MAX_REL
    head = jnp.broadcast_to(table[:, 2 * MAX_REL:], (A_HEADS, n_clip))
    return jnp.concatenate([head, table[:, ::-1][:, :BIAS_VAR - n_clip]], axis=1)


def kernel(x_prompt, x_sample, cache_a_k, cache_a_v, cache_b_k, cache_b_v, state_conv, norm_attn, w_in, q_norm_a,
           k_norm_a, rel_bias_a, q_norm_b, k_norm_b, sinks_b, w_out, norm_ffn, w_gate_up, conv_w, conv_b, w_down):
    assert norm_attn.shape[0] == 1
    bsz, seq, _ = x_prompt.shape
    dec_b, dec_s, _ = x_sample.shape
    assert bsz == 1 and dec_s == CHUNK and (dec_b * dec_s) % TOKEN_TILE == 0
    assert seq % TOKEN_TILE == 0 and seq % PROMPT_PROJ_TILE == 0 and seq % ATTN_TILE == 0
    assert cache_a_k.shape[2] == A_CTX and cache_b_k.shape[2] == B_CTX

    w_in_b = w_in[0].astype(_BF16)
    gn_a = norm_attn[0][None, :]
    gn_f = norm_ffn[0][None, :]
    gqa = jnp.tile(q_norm_a[0], A_HEADS)[None, :]
    gka = jnp.tile(k_norm_a[0], A_HEADS)[None, :]
    gqb = jnp.tile(q_norm_b[0], B_HEADS)[None, :]
    gkb = jnp.tile(k_norm_b[0], B_KV_HEADS)[None, :]
    u = _bias_profile(rel_bias_a[0])
    sink = jnp.repeat(sinks_b[0], CHUNK).reshape(B_KV_HEADS, B_GROUP * CHUNK, 1)
    cw = conv_w[0]
    cb = conv_b[0][None, :]

    rope_p = _rope_tables(jnp.arange(0, seq, PROMPT_PROJ_TILE), jnp.arange(PROMPT_PROJ_TILE))
    rope_s = _rope_tables(jnp.full((1,), PAST_LEN), jnp.arange(TOKEN_TILE) % dec_s)

    xp = x_prompt.reshape(seq, D_MODEL)
    qa, ka, va, qb, kb, vb, kaf, vaf, kbf, vbf, w_out_b, w_gu_b, w_down_b = _project(
        xp, gn_a, w_in_b, gqa, gka, gqb, gkb, rope_p, keep_all=False, name="proj_prompt",
        cast=(w_out[0], w_gate_up[0], w_down[0]))
    o_p = _attend_prompt(qa, ka, va, qb, kb, vb, u, sink)

    xs = x_sample.reshape(dec_b * dec_s, D_MODEL)
    qa, ka, va, qb, kb, vb, kaf_s, vaf_s, kbf_s, vbf_s = _project(
        xs, gn_a, w_in_b, gqa, gka, gqb, gkb, rope_s, keep_all=True, name="proj_sample")
    o_s = _attend_sample(qa, ka, va, qb, kb, vb, _feature_major(cache_a_k[0]), _feature_major(cache_a_v[0]),
                         _feature_major(cache_b_k[0]), _feature_major(cache_b_v[0]), u, sink)

    y_p, cn_p, y_s, cn_s = _out_ffn(o_p, xp, o_s, xs, w_out_b, gn_f, w_gu_b, cw, cb, w_down_b, state_conv[0])

    return (
        y_p.reshape(1, seq, D_MODEL),
        y_s.reshape(dec_b, dec_s, D_MODEL),
        _token_major(kaf, A_HEADS),
        _token_major(vaf, A_HEADS),
        _token_major(kbf, B_KV_HEADS),
        _token_major(vbf, B_KV_HEADS),
        cn_p.reshape(1, 1, CONV_W - 1, D_FF),
        kaf_s.reshape(1, dec_b, dec_s, A_HEADS, HEAD_DIM),
        vaf_s.reshape(1, dec_b, dec_s, A_HEADS, HEAD_DIM),
        kbf_s.reshape(1, dec_b, dec_s, B_KV_HEADS, HEAD_DIM),
        vbf_s.reshape(1, dec_b, dec_s, B_KV_HEADS, HEAD_DIM),
        cn_s.reshape(1, dec_b, CONV_W - 1, D_FF),
    )
```

```python
import functools

import jax
import jax.numpy as jnp
from jax import lax
from jax.experimental import pallas as pl
from jax.experimental.pallas import tpu as pltpu

D_MODEL = 1024
CHUNK = 64
HEAD_DIM = 64
EPS = 1e-6
SCALE = HEAD_DIM ** -0.5
LOG2E = 1.4426950408889634
Q_SCALE = SCALE * LOG2E
NEG = -1e30
PAST_LEN = 1024
A_HEADS = 8
A_LEFT_CHUNKS = 8
A_CTX = A_LEFT_CHUNKS * CHUNK
A_WIN = A_CTX + CHUNK
MAX_REL = 128
A_WIDTH = A_HEADS * HEAD_DIM
B_HEADS = 8
B_KV_HEADS = 2
B_GROUP = B_HEADS // B_KV_HEADS
B_CTX = 128
B_WIN = B_CTX + CHUNK
B_WIDTH = B_HEADS * HEAD_DIM
B_KV_WIDTH = B_KV_HEADS * HEAD_DIM
ROPE_THETA = 10000.0
IN_COLS = 3 * A_WIDTH + B_WIDTH + 2 * B_KV_WIDTH
D_FF = 2816
CONV_W = 3

LANES = 128
TOKEN_TILE = 512
PROMPT_PROJ_TILE = 1024
ATTN_TILE = 1024
SAMPLE_STREAMS = 4
FF_CHUNK = 256
BIAS_COLS = MAX_REL + CHUNK
BIAS_VAR = 256
VMEM_LIMIT = 56 * 1024 * 1024

_BF16 = jnp.bfloat16
_F32 = jnp.float32


def _const_spec(shape):
    nd = len(shape)
    return pl.BlockSpec(shape, lambda i: (0,) * nd, pipeline_mode=pl.Buffered(1))


def _head_sumsq(t, bd):
    sq = (t * t).astype(_BF16)
    w = t.shape[1]
    if w <= 2 * LANES:
        return jnp.dot(sq, bd[:w, :w], preferred_element_type=_F32)
    parts = [jnp.dot(sq[:, c:c + 2 * LANES], bd, preferred_element_type=_F32) for c in range(0, w, 2 * LANES)]
    return jnp.concatenate(parts, axis=1)


def _head_norm(t, bd, gain):
    ss = _head_sumsq(t, bd)
    return t * lax.rsqrt(ss * (1.0 / HEAD_DIM) + EPS) * gain


def _rope(t, cos, sin_signed):
    outs = []
    for c in range(0, t.shape[1], LANES):
        tc = t[:, c:c + LANES]
        lane = lax.broadcasted_iota(jnp.int32, tc.shape, 1)
        first_half = (lane & (HEAD_DIM - 1)) < HEAD_DIM // 2
        rot = jnp.where(first_half, pltpu.roll(tc, LANES - HEAD_DIM // 2, 1), pltpu.roll(tc, HEAD_DIM // 2, 1))
        outs.append(tc * cos + rot * sin_signed)
    return outs[0] if len(outs) == 1 else jnp.concatenate(outs, axis=1)


def _rep_pair(t):
    rolled = pltpu.roll(t, HEAD_DIM, 1)
    lo = lax.broadcasted_iota(jnp.int32, t.shape, 1) < HEAD_DIM
    return jnp.concatenate([jnp.where(lo, t, rolled), jnp.where(lo, rolled, t)], axis=1)


def _store_heads(ref, t):
    n, w = t.shape
    heads = w // HEAD_DIM
    for hd in range(heads):
        ref[pl.ds(hd, n, stride=heads), :] = t[:, hd * HEAD_DIM:(hd + 1) * HEAD_DIM]


def _proj_kernel(x_ref, gn_ref, w_ref, gqa_ref, gka_ref, gqb_ref, gkb_ref, ct_ref, st_ref, cr_ref, sr_ref, *rest,
                 last_only, n_cast):
    cast_in, rest = rest[:n_cast], rest[n_cast:]
    qa_ref, ka_ref, va_ref, qb_ref, kb_ref, vb_ref, kaf_ref, vaf_ref, kbf_ref, vbf_ref = rest[:10]
    for src, dst in zip(cast_in, rest[10:]):
        dst[...] = src[...].astype(_BF16)
    tm = x_ref.shape[0]
    x = x_ref[...]
    ms = jnp.mean(x * x, axis=-1, keepdims=True)
    h = (x * lax.rsqrt(ms + EPS) * gn_ref[...]).astype(_BF16)

    r = lax.broadcasted_iota(jnp.int32, (2 * LANES, 2 * LANES), 0) // HEAD_DIM
    c = lax.broadcasted_iota(jnp.int32, (2 * LANES, 2 * LANES), 1) // HEAD_DIM
    bd = jnp.where(r == c, 1.0, 0.0).astype(_BF16)
    ct, st, cr, sr = ct_ref[0], st_ref[0], cr_ref[...], sr_ref[...]
    cos = ct * cr - st * sr
    sin = st * cr + ct * sr

    def proj(lo, width):
        return jnp.dot(h, w_ref[:, lo:lo + width], preferred_element_type=_F32)

    o_ka, o_va, o_qb, o_kvb = A_WIDTH, 2 * A_WIDTH, 3 * A_WIDTH, 3 * A_WIDTH + B_WIDTH
    p_qa = proj(0, A_WIDTH)
    p_ka = proj(o_ka, A_WIDTH)
    qa_ref[...] = (_head_norm(p_qa, bd, gqa_ref[...]) * Q_SCALE).astype(_BF16)
    va = proj(o_va, A_WIDTH)
    ka = _head_norm(p_ka, bd, gka_ref[...])
    ka_ref[...] = ka.astype(_BF16)
    p_qb = proj(o_qb, B_WIDTH)
    va_ref[...] = va.astype(_BF16)
    p_kvb = proj(o_kvb, 2 * B_KV_WIDTH)
    qb_ref[...] = (_rope(_head_norm(p_qb, bd, gqb_ref[...]), cos, sin) * Q_SCALE).astype(_BF16)
    kb = _rope(_head_norm(p_kvb[:, :B_KV_WIDTH], bd, gkb_ref[...]), cos, sin)
    vb = p_kvb[:, B_KV_WIDTH:]
    kb_ref[...] = _rep_pair(kb).astype(_BF16)
    vb_ref[...] = _rep_pair(vb).astype(_BF16)

    if last_only:
        @pl.when(pl.program_id(0) == pl.num_programs(0) - 1)
        def _():
            kaf_ref[...] = ka[tm - A_CTX:, :].T
            vaf_ref[...] = va[tm - A_CTX:, :].T
            kbf_ref[...] = kb[tm - B_CTX:, :].T
            vbf_ref[...] = vb[tm - B_CTX:, :].T
    else:
        _store_heads(kaf_ref, ka)
        _store_heads(vaf_ref, va)
        kbf_ref[...] = kb
        vbf_ref[...] = vb


def _cast_blocks(w, n_steps):
    rows = w.shape[0]
    n_blocks = max(d for d in range(1, n_steps + 1) if rows % d == 0 and (rows // d) % 16 == 0)
    return pl.BlockSpec((rows // n_blocks, w.shape[1]), lambda i: (jnp.minimum(i, n_blocks - 1), 0))


def _project(x, gn, w_in, gqa, gka, gqb, gkb, rope, *, keep_all, name, cast=()):
    t = x.shape[0]
    tm = rope[2].shape[0]
    n = t // tm
    cast_specs = [_cast_blocks(w, n) for w in cast]
    tile = lambda w: pl.BlockSpec((tm, w), lambda i: (i, 0))
    if keep_all:
        f32_shapes = [(t * A_HEADS, HEAD_DIM), (t * A_HEADS, HEAD_DIM), (t, B_KV_WIDTH), (t, B_KV_WIDTH)]
        f32_specs = [pl.BlockSpec((tm * A_HEADS, HEAD_DIM), lambda i: (i, 0))] * 2 + [tile(B_KV_WIDTH)] * 2
    else:
        f32_shapes = [(A_WIDTH, A_CTX), (A_WIDTH, A_CTX), (B_KV_WIDTH, B_CTX), (B_KV_WIDTH, B_CTX)]
        f32_specs = [pl.BlockSpec(sh, lambda i: (0, 0)) for sh in f32_shapes]
    ct, st, cr, sr = rope
    if ct.shape[0] == n:
        tile_trig = pl.BlockSpec((1, 1, LANES), lambda i: (i, 0, 0))
    else:
        tile_trig = _const_spec((1, 1, LANES))
    out_shape = [
        jax.ShapeDtypeStruct((t, A_WIDTH), _BF16), jax.ShapeDtypeStruct((t, A_WIDTH), _BF16),
        jax.ShapeDtypeStruct((t, A_WIDTH), _BF16), jax.ShapeDtypeStruct((t, B_WIDTH), _BF16),
        jax.ShapeDtypeStruct((t, 2 * B_KV_WIDTH), _BF16), jax.ShapeDtypeStruct((t, 2 * B_KV_WIDTH), _BF16),
    ] + [jax.ShapeDtypeStruct(sh, _F32) for sh in f32_shapes] + [jax.ShapeDtypeStruct(w.shape, _BF16) for w in cast]
    out_specs = [
        tile(A_WIDTH), tile(A_WIDTH), tile(A_WIDTH), tile(B_WIDTH), tile(2 * B_KV_WIDTH), tile(2 * B_KV_WIDTH),
    ] + f32_specs + cast_specs
    in_specs = [
        tile(D_MODEL), _const_spec((1, D_MODEL)), _const_spec((D_MODEL, IN_COLS)),
        _const_spec((1, A_WIDTH)), _const_spec((1, A_WIDTH)), _const_spec((1, B_WIDTH)),
        _const_spec((1, B_KV_WIDTH)), tile_trig, tile_trig, _const_spec((tm, LANES)), _const_spec((tm, LANES)),
    ] + cast_specs
    return pl.pallas_call(
        functools.partial(_proj_kernel, last_only=not keep_all, n_cast=len(cast)),
        grid=(n,), in_specs=in_specs, out_specs=out_specs, out_shape=out_shape, name=name,
        compiler_params=pltpu.CompilerParams(dimension_semantics=("arbitrary",), vmem_limit_bytes=VMEM_LIMIT),
    )(x, gn, w_in, gqa, gka, gqb, gkb, ct, st, cr, sr, *cast)


def _build_bias(u_ref, bias_ref):
    qrow = lax.broadcasted_iota(jnp.int32, (CHUNK, BIAS_VAR), 0)
    for h in range(A_HEADS):
        prof = u_ref[h:h + 1, :]
        y = jnp.broadcast_to(prof - prof[:, 0:1], (CHUNK, BIAS_VAR))
        y = pltpu.roll(y, BIAS_VAR - (CHUNK - 1), 1)
        for b in range(6):
            y = jnp.where(((qrow >> b) & 1) == 1, pltpu.roll(y, 1 << b, 1), y)
        bias_ref[h // 4, (h % 4) * CHUNK:(h % 4 + 1) * CHUNK, :] = y[:, :BIAS_COLS] * LOG2E


_NT = (((1,), (1,)), ((), ()))
_NN = (((1,), (0,)), ((), ()))


def _scores(qa, qb, keys_a, keys_b, bias_ref, lim_a, lim_b):
    out = []
    head_of_lane = lax.broadcasted_iota(jnp.int32, (CHUNK, 2 * LANES), 1) // HEAD_DIM
    for g in range(2):
        qg = qa[:, 2 * LANES * g:2 * LANES * (g + 1)]
        qbd = jnp.concatenate([jnp.where(head_of_lane == r, qg, jnp.zeros_like(qg)) for r in range(4)], axis=0)
        pieces, off = [], 0
        for keys, on_rows in keys_a(g):
            s = lax.dot_general(qbd, keys, _NT if on_rows else _NN, preferred_element_type=_F32)
            n = s.shape[1]
            if off + n > A_WIN - BIAS_COLS:
                lo_col = max(A_WIN - BIAS_COLS - off, 0)
                biased = s[:, lo_col:] + bias_ref[g, :, off + lo_col - (A_WIN - BIAS_COLS):off + n - (A_WIN - BIAS_COLS)]
                s = biased if lo_col == 0 else jnp.concatenate([s[:, :lo_col], biased], axis=1)
            if lim_a:
                s = jnp.where(lax.broadcasted_iota(jnp.int32, s.shape, 1) >= lim_a, s, NEG)
            pieces.append(s)
            off += n
        out.append(pieces)
    lo = lax.broadcasted_iota(jnp.int32, (CHUNK, LANES), 1) < HEAD_DIM
    for g in range(B_KV_HEADS):
        rows = []
        for r in range(B_GROUP):
            hd = B_GROUP * g + r
            qp = qb[:, LANES * (hd // 2):LANES * (hd // 2 + 1)]
            keep = lo if hd % 2 == 0 else jnp.logical_not(lo)
            rows.append(jnp.where(keep, qp, jnp.zeros_like(qp)))
        qs = jnp.concatenate(rows, axis=0)
        pieces = []
        for keys, on_rows in keys_b(g):
            s = lax.dot_general(qs, keys, _NT if on_rows else _NN, preferred_element_type=_F32)
            if lim_b:
                s = jnp.where(lax.broadcasted_iota(jnp.int32, s.shape, 1) >= lim_b, s, NEG)
            pieces.append(s)
        out.append(pieces)
    return out


def _row_reduce(pieces, combine, reduce, fill):
    acc = None
    for s in pieces:
        for c in range(0, s.shape[1], LANES):
            tile = s[:, c:c + LANES]
            if tile.shape[1] < LANES:
                pad = jnp.full((tile.shape[0], LANES - tile.shape[1]), fill, tile.dtype)
                tile = jnp.concatenate([tile, pad], axis=1)
            acc = tile if acc is None else combine(acc, tile)
    return reduce(acc, axis=1, keepdims=True)


def _softmax_values(pieces, vals, sink):
    m = _row_reduce(pieces, jnp.maximum, jnp.max, -jnp.inf)
    if sink is not None:
        m = jnp.maximum(m, sink)
    es = [jnp.exp2(s - m) for s in pieces]
    l = _row_reduce(es, jnp.add, jnp.sum, 0.0)
    if sink is not None:
        l = l + jnp.exp2(sink - m)
    acc = None
    for e, (v, on_rows) in zip(es, vals):
        part = lax.dot_general(e.astype(_BF16), v, _NN if on_rows else _NT, preferred_element_type=_F32)
        acc = part if acc is None else acc + part
    return acc / l


def _softmax_pv(scores, vals_a, vals_b, sink_ref):
    outs = []
    head_of_lane = lax.broadcasted_iota(jnp.int32, (CHUNK, 2 * LANES), 1) // HEAD_DIM
    for g in range(2):
        of = _softmax_values(scores[g], vals_a(g), None)
        og = of[3 * CHUNK:4 * CHUNK]
        for r in (2, 1, 0):
            og = jnp.where(head_of_lane == r, of[r * CHUNK:(r + 1) * CHUNK], og)
        outs.append(og)
    lo = lax.broadcasted_iota(jnp.int32, (CHUNK, LANES), 1) < HEAD_DIM
    for g in range(B_KV_HEADS):
        of = _softmax_values(scores[2 + g], vals_b(g), sink_ref[g] * LOG2E)
        outs.append(jnp.where(lo, of[0:CHUNK], of[CHUNK:2 * CHUNK]))
        outs.append(jnp.where(lo, of[2 * CHUNK:3 * CHUNK], of[3 * CHUNK:4 * CHUNK]))
    return jnp.concatenate(outs, axis=1)


def _attn_prompt_kernel(qa_ref, ka_ref, va_ref, qb_ref, kb_ref, vb_ref, u_ref, sink_ref, o_ref,
                        kwa, vwa, kwb, vwb, bias):
    i = pl.program_id(0)
    tm = qa_ref.shape[0]
    n_chunks = tm // CHUNK

    @pl.when(i == 0)
    def _():
        kwa[0:A_CTX, :] = jnp.zeros((A_CTX, A_WIDTH), _BF16)
        vwa[0:A_CTX, :] = jnp.zeros((A_CTX, A_WIDTH), _BF16)
        kwb[0:B_CTX, :] = jnp.zeros((B_CTX, 2 * B_KV_WIDTH), _BF16)
        vwb[0:B_CTX, :] = jnp.zeros((B_CTX, 2 * B_KV_WIDTH), _BF16)
        _build_bias(u_ref, bias)

    @pl.when(i > 0)
    def _():
        kwa[0:A_CTX, :] = kwa[tm:tm + A_CTX, :]
        vwa[0:A_CTX, :] = vwa[tm:tm + A_CTX, :]
        kwb[0:B_CTX, :] = kwb[tm:tm + B_CTX, :]
        vwb[0:B_CTX, :] = vwb[tm:tm + B_CTX, :]

    kwa[A_CTX:A_CTX + tm, :] = ka_ref[...]
    vwa[A_CTX:A_CTX + tm, :] = va_ref[...]
    kwb[B_CTX:B_CTX + tm, :] = kb_ref[...]
    vwb[B_CTX:B_CTX + tm, :] = vb_ref[...]

    def window(ref, r0, rows, width):
        return lambda g: [(ref[r0:r0 + rows, width * g:width * (g + 1)], True)]

    def run(masked):
        def scores(c):
            r0 = c * CHUNK
            lim_a = max(A_LEFT_CHUNKS - c, 0) * CHUNK if masked else 0
            lim_b = max(B_CTX // CHUNK - c, 0) * CHUNK if masked else 0
            return _scores(qa_ref[r0:r0 + CHUNK, :], qb_ref[r0:r0 + CHUNK, :],
                           window(kwa, r0, A_WIN, 2 * LANES), window(kwb, r0, B_WIN, LANES), bias, lim_a, lim_b)

        nxt = scores(0)
        for c in range(n_chunks):
            cur = nxt
            if c + 1 < n_chunks:
                nxt = scores(c + 1)
            r0 = c * CHUNK
            o = _softmax_pv(cur, window(vwa, r0, A_WIN, 2 * LANES), window(vwb, r0, B_WIN, LANES), sink_ref)
            o_ref[r0:r0 + CHUNK, :] = o.astype(_BF16)

    @pl.when(i == 0)
    def _():
        run(True)

    @pl.when(i > 0)
    def _():
        run(False)


def _attn_sample_kernel(qa_ref, ka_ref, va_ref, qb_ref, kb_ref, vb_ref, cak_ref, cav_ref, cbk_ref, cbv_ref,
                        u_ref, sink_ref, o_ref, bias):
    n_streams = cak_ref.shape[0]

    @pl.when(pl.program_id(0) == 0)
    def _():
        _build_bias(u_ref, bias)

    def pieces_a(cache_ref, new_ref, b):
        rows = slice(b * CHUNK, (b + 1) * CHUNK)

        def get(g):
            cols = slice(2 * LANES * g, 2 * LANES * (g + 1))
            return [(cache_ref[b, cols, :].astype(_BF16), False), (new_ref[rows, cols], True)]
        return get

    def pieces_b(cache_ref, new_ref, b):
        rows = slice(b * CHUNK, (b + 1) * CHUNK)

        def get(g):
            past = cache_ref[b, HEAD_DIM * g:HEAD_DIM * (g + 1), :].astype(_BF16)
            return [(jnp.concatenate([past, past], axis=0), False), (new_ref[rows, LANES * g:LANES * (g + 1)], True)]
        return get

    def scores(b):
        rows = slice(b * CHUNK, (b + 1) * CHUNK)
        return _scores(qa_ref[rows, :], qb_ref[rows, :], pieces_a(cak_ref, ka_ref, b), pieces_b(cbk_ref, kb_ref, b),
                       bias, 0, 0)

    nxt = scores(0)
    for b in range(n_streams):
        cur = nxt
        if b + 1 < n_streams:
            nxt = scores(b + 1)
        o = _softmax_pv(cur, pieces_a(cav_ref, va_ref, b), pieces_b(cbv_ref, vb_ref, b), sink_ref)
        o_ref[b * CHUNK:(b + 1) * CHUNK, :] = o.astype(_BF16)


_BIAS_SCRATCH = pltpu.VMEM((2, 4 * CHUNK, BIAS_COLS), _F32)


def _attend_prompt(qa, ka, va, qb, kb, vb, u, sink):
    t = qa.shape[0]
    tm = ATTN_TILE
    tile = lambda w: pl.BlockSpec((tm, w), lambda i: (i, 0))
    return pl.pallas_call(
        _attn_prompt_kernel,
        grid=(t // tm,),
        in_specs=[tile(A_WIDTH), tile(A_WIDTH), tile(A_WIDTH), tile(B_WIDTH), tile(2 * B_KV_WIDTH),
                  tile(2 * B_KV_WIDTH), _const_spec(u.shape), _const_spec(sink.shape)],
        out_specs=tile(A_WIDTH + B_WIDTH),
        out_shape=jax.ShapeDtypeStruct((t, A_WIDTH + B_WIDTH), _BF16),
        scratch_shapes=[pltpu.VMEM((A_CTX + tm, A_WIDTH), _BF16), pltpu.VMEM((A_CTX + tm, A_WIDTH), _BF16),
                        pltpu.VMEM((B_CTX + tm, 2 * B_KV_WIDTH), _BF16),
                        pltpu.VMEM((B_CTX + tm, 2 * B_KV_WIDTH), _BF16), _BIAS_SCRATCH],
        name="attn_prompt",
        compiler_params=pltpu.CompilerParams(dimension_semantics=("arbitrary",), vmem_limit_bytes=VMEM_LIMIT),
    )(qa, ka, va, qb, kb, vb, u, sink)


def _attend_sample(qa, ka, va, qb, kb, vb, cak, cav, cbk, cbv, u, sink):
    t = qa.shape[0]
    sb = SAMPLE_STREAMS
    tile = lambda w: pl.BlockSpec((sb * CHUNK, w), lambda i: (i, 0))
    cache = lambda arr: pl.BlockSpec((sb,) + arr.shape[1:], lambda i: (i, 0, 0))
    return pl.pallas_call(
        _attn_sample_kernel,
        grid=(t // (sb * CHUNK),),
        in_specs=[tile(A_WIDTH), tile(A_WIDTH), tile(A_WIDTH), tile(B_WIDTH), tile(2 * B_KV_WIDTH),
                  tile(2 * B_KV_WIDTH), cache(cak), cache(cav), cache(cbk), cache(cbv),
                  _const_spec(u.shape), _const_spec(sink.shape)],
        out_specs=tile(A_WIDTH + B_WIDTH),
        out_shape=jax.ShapeDtypeStruct((t, A_WIDTH + B_WIDTH), _BF16),
        scratch_shapes=[_BIAS_SCRATCH],
        name="attn_sample",
        compiler_params=pltpu.CompilerParams(dimension_semantics=("arbitrary",), vmem_limit_bytes=VMEM_LIMIT),
    )(qa, ka, va, qb, kb, vb, cak, cav, cbk, cbv, u, sink)


def _ffn_kernel(o_ref, x_ref, wo_ref, gn_ref, wgu_ref, cw_ref, cb_ref, wd_ref, st_ref,
                y_ref, cn_ref, carry, act_buf, *, seg_len, carry_rows):
    i = pl.program_id(0)
    tm = x_ref.shape[0]
    n_seg = tm // seg_len

    if carry_rows:
        @pl.when(i == 0)
        def _():
            carry[...] = st_ref[0]

    y1 = x_ref[...] + jnp.dot(o_ref[...], wo_ref[...], preferred_element_type=_F32)
    ms = jnp.mean(y1 * y1, axis=-1, keepdims=True)
    h = (y1 * lax.rsqrt(ms + EPS) * gn_ref[...]).astype(_BF16)

    row8 = lax.broadcasted_iota(jnp.int32, (8, FF_CHUNK), 0)

    def gate_up(c0):
        g = jnp.dot(h, wgu_ref[:, c0:c0 + FF_CHUNK], preferred_element_type=_F32)
        up = jnp.dot(h, wgu_ref[:, D_FF + c0:D_FF + c0 + FF_CHUNK], preferred_element_type=_F32)
        return g, up

    nxt = gate_up(0)
    for c0 in range(0, D_FF, FF_CHUNK):
        cols = slice(c0, c0 + FF_CHUNK)
        g, up = nxt
        if c0 + FF_CHUNK < D_FF:
            nxt = gate_up(c0 + FF_CHUNK)
        g1 = pltpu.roll(g, 1, 0)
        g2 = pltpu.roll(g, 2, 0)
        p1, p2 = [], []
        for s in range(n_seg):
            a = s * seg_len
            prev = carry[:, cols] if carry_rows else st_ref[s, :, cols]
            prev0 = jnp.broadcast_to(prev[0:1, :], (8, FF_CHUNK))
            prev1 = jnp.broadcast_to(prev[1:2, :], (8, FF_CHUNK))
            p1.append(jnp.where(row8 == 0, prev1, g1[a:a + 8]))
            p1.append(g1[a + 8:a + seg_len])
            p2.append(jnp.where(row8 == 0, prev0, jnp.where(row8 == 1, prev1, g2[a:a + 8])))
            p2.append(g2[a + 8:a + seg_len])
            tail = g[a + seg_len - 8:a + seg_len][8 - (CONV_W - 1):, :]
            if carry_rows:
                carry[:, cols] = tail
            else:
                cn_ref[s, :, cols] = tail
        g1 = jnp.concatenate(p1, axis=0)
        g2 = jnp.concatenate(p2, axis=0)
        conv = cb_ref[:, cols] + cw_ref[0:1, cols] * g2 + cw_ref[1:2, cols] * g1 + cw_ref[2:3, cols] * g
        act_buf[:, cols] = (conv * jax.nn.sigmoid(conv) * up).astype(_BF16)
    y_ref[...] = y1 + jnp.dot(act_buf[...], wd_ref[...], preferred_element_type=_F32)
    if carry_rows:
        cn_ref[0] = carry[...]


def _out_ffn(o, x, w_out, gn, w_gu, conv_w, conv_b, w_down, state, *, carry_rows, name):
    t = x.shape[0]
    tm = TOKEN_TILE
    n = t // tm
    seg_len = tm if carry_rows else CHUNK
    n_seg = tm // seg_len
    tile = lambda w: pl.BlockSpec((tm, w), lambda i: (i, 0))
    if carry_rows:
        st_spec = _const_spec((1, CONV_W - 1, D_FF))
        cn_spec = pl.BlockSpec((1, CONV_W - 1, D_FF), lambda i: (0, 0, 0))
        cn_shape = (1, CONV_W - 1, D_FF)
    else:
        st_spec = pl.BlockSpec((n_seg, CONV_W - 1, D_FF), lambda i: (i, 0, 0))
        cn_spec = pl.BlockSpec((n_seg, CONV_W - 1, D_FF), lambda i: (i, 0, 0))
        cn_shape = (t // seg_len, CONV_W - 1, D_FF)
    return pl.pallas_call(
        functools.partial(_ffn_kernel, seg_len=seg_len, carry_rows=carry_rows),
        grid=(n,),
        in_specs=[tile(A_WIDTH + B_WIDTH), tile(D_MODEL), _const_spec((D_MODEL, D_MODEL)),
                  _const_spec((1, D_MODEL)), _const_spec((D_MODEL, 2 * D_FF)), _const_spec((CONV_W, D_FF)),
                  _const_spec((1, D_FF)), _const_spec((D_FF, D_MODEL)), st_spec],
        out_specs=[tile(D_MODEL), cn_spec],
        out_shape=[jax.ShapeDtypeStruct((t, D_MODEL), _F32), jax.ShapeDtypeStruct(cn_shape, _F32)],
        scratch_shapes=[pltpu.VMEM((CONV_W - 1, D_FF), _F32), pltpu.VMEM((tm, D_FF), _BF16)],
        name=name,
        compiler_params=pltpu.CompilerParams(dimension_semantics=("arbitrary",), vmem_limit_bytes=VMEM_LIMIT),
    )(o, x, w_out, gn, w_gu, conv_w, conv_b, w_down, state)


def _rope_tables(tile_pos, row_pos):
    half = HEAD_DIM // 2
    lane = jnp.arange(LANES)
    inv = jnp.power(ROPE_THETA, -(lane % half).astype(_F32) / half)
    sign = jnp.where(lane % HEAD_DIM < half, -1.0, 1.0).astype(_F32)
    a = tile_pos.astype(_F32)[:, None] * inv[None, :]
    b = row_pos.astype(_F32)[:, None] * inv[None, :]
    return (jnp.cos(a)[:, None, :], (jnp.sin(a) * sign)[:, None, :], jnp.cos(b), jnp.sin(b) * sign)


def _feature_major(cache):
    n, t, heads, d = cache.shape
    return jnp.transpose(cache, (0, 2, 3, 1)).reshape(n, heads * d, t)


def _token_major(rows, heads):
    t = rows.shape[1]
    return jnp.transpose(rows.reshape(heads, HEAD_DIM, t), (2, 0, 1))[None, None]


def _bias_profile(table):
    n_clip = BIAS_COLS - 1 - MAX_REL
    head = jnp.broadcast_to(table[:, 2 * MAX_REL:], (A_HEADS, n_clip))
    return jnp.concatenate([head, table[:, ::-1][:, :BIAS_VAR - n_clip]], axis=1)


def kernel(x_prompt, x_sample, cache_a_k, cache_a_v, cache_b_k, cache_b_v, state_conv, norm_attn, w_in, q_norm_a,
           k_norm_a, rel_bias_a, q_norm_b, k_norm_b, sinks_b, w_out, norm_ffn, w_gate_up, conv_w, conv_b, w_down):
    assert norm_attn.shape[0] == 1
    bsz, seq, _ = x_prompt.shape
    dec_b, dec_s, _ = x_sample.shape
    assert bsz == 1 and dec_s == CHUNK and (dec_b * dec_s) % TOKEN_TILE == 0
    assert seq % TOKEN_TILE == 0 and seq % PROMPT_PROJ_TILE == 0 and seq % ATTN_TILE == 0
    assert cache_a_k.shape[2] == A_CTX and cache_b_k.shape[2] == B_CTX

    w_in_b = w_in[0].astype(_BF16)
    gn_a = norm_attn[0][None, :]
    gn_f = norm_ffn[0][None, :]
    gqa = jnp.tile(q_norm_a[0], A_HEADS)[None, :]
    gka = jnp.tile(k_norm_a[0], A_HEADS)[None, :]
    gqb = jnp.tile(q_norm_b[0], B_HEADS)[None, :]
    gkb = jnp.tile(k_norm_b[0], B_KV_HEADS)[None, :]
    u = _bias_profile(rel_bias_a[0])
    sink = jnp.repeat(sinks_b[0], CHUNK).reshape(B_KV_HEADS, B_GROUP * CHUNK, 1)
    cw = conv_w[0]
    cb = conv_b[0][None, :]

    rope_p = _rope_tables(jnp.arange(0, seq, PROMPT_PROJ_TILE), jnp.arange(PROMPT_PROJ_TILE))
    rope_s = _rope_tables(jnp.full((1,), PAST_LEN), jnp.arange(TOKEN_TILE) % dec_s)

    xp = x_prompt.reshape(seq, D_MODEL)
    qa, ka, va, qb, kb, vb, kaf, vaf, kbf, vbf, w_out_b, w_gu_b, w_down_b = _project(
        xp, gn_a, w_in_b, gqa, gka, gqb, gkb, rope_p, keep_all=False, name="proj_prompt",
        cast=(w_out[0], w_gate_up[0], w_down[0]))
    o_p = _attend_prompt(qa, ka, va, qb, kb, vb, u, sink)
    zero_state = jnp.zeros((1, CONV_W - 1, D_FF), _F32)
    y_p, cn_p = _out_ffn(o_p, xp, w_out_b, gn_f, w_gu_b, cw, cb, w_down_b, zero_state,
                         carry_rows=True, name="ffn_prompt")

    xs = x_sample.reshape(dec_b * dec_s, D_MODEL)
    qa, ka, va, qb, kb, vb, kaf_s, vaf_s, kbf_s, vbf_s = _project(
        xs, gn_a, w_in_b, gqa, gka, gqb, gkb, rope_s, keep_all=True, name="proj_sample")
    o_s = _attend_sample(qa, ka, va, qb, kb, vb, _feature_major(cache_a_k[0]), _feature_major(cache_a_v[0]),
                         _feature_major(cache_b_k[0]), _feature_major(cache_b_v[0]), u, sink)
    y_s, cn_s = _out_ffn(o_s, xs, w_out_b, gn_f, w_gu_b, cw, cb, w_down_b, state_conv[0],
                         carry_rows=False, name="ffn_sample")

    return (
        y_p.reshape(1, seq, D_MODEL),
        y_s.reshape(dec_b, dec_s, D_MODEL),
        _token_major(kaf, A_HEADS),
        _token_major(vaf, A_HEADS),
        _token_major(kbf, B_KV_HEADS),
        _token_major(vbf, B_KV_HEADS),
        cn_p.reshape(1, 1, CONV_W - 1, D_FF),
        kaf_s.reshape(1, dec_b, dec_s, A_HEADS, HEAD_DIM),
        vaf_s.reshape(1, dec_b, dec_s, A_HEADS, HEAD_DIM),
        kbf_s.reshape(1, dec_b, dec_s, B_KV_HEADS, HEAD_DIM),
        vbf_s.reshape(1, dec_b, dec_s, B_KV_HEADS, HEAD_DIM),
        cn_s.reshape(1, dec_b, CONV_W - 1, D_FF),
    )
```

```python
import functools

import jax
import jax.numpy as jnp
from jax import lax
from jax.experimental import pallas as pl
from jax.experimental.pallas import tpu as pltpu

D_MODEL = 1024
CHUNK = 64
HEAD_DIM = 64
EPS = 1e-6
SCALE = HEAD_DIM ** -0.5
LOG2E = 1.4426950408889634
Q_SCALE = SCALE * LOG2E
NEG = -1e30
PAST_LEN = 1024
A_HEADS = 8
A_LEFT_CHUNKS = 8
A_CTX = A_LEFT_CHUNKS * CHUNK
A_WIN = A_CTX + CHUNK
MAX_REL = 128
A_WIDTH = A_HEADS * HEAD_DIM
B_HEADS = 8
B_KV_HEADS = 2
B_GROUP = B_HEADS // B_KV_HEADS
B_CTX = 128
B_WIN = B_CTX + CHUNK
B_WIDTH = B_HEADS * HEAD_DIM
B_KV_WIDTH = B_KV_HEADS * HEAD_DIM
ROPE_THETA = 10000.0
IN_COLS = 3 * A_WIDTH + B_WIDTH + 2 * B_KV_WIDTH
D_FF = 2816
CONV_W = 3

LANES = 128
TOKEN_TILE = 512
ATTN_TILE = 1024
STAGE_EVERY = 2
SAMPLE_STREAMS = 4
FF_CHUNK = 256
BIAS_COLS = MAX_REL + CHUNK
BIAS_VAR = 256
VMEM_LIMIT = 56 * 1024 * 1024
FUSED_VMEM_LIMIT = 62 * 1024 * 1024

_BF16 = jnp.bfloat16
_F32 = jnp.float32


def _const_spec(shape):
    nd = len(shape)
    return pl.BlockSpec(shape, lambda i: (0,) * nd, pipeline_mode=pl.Buffered(1))


def _head_sumsq(t, bd):
    sq = (t * t).astype(_BF16)
    w = t.shape[1]
    if w <= 2 * LANES:
        return jnp.dot(sq, bd[:w, :w], preferred_element_type=_F32)
    parts = [jnp.dot(sq[:, c:c + 2 * LANES], bd, preferred_element_type=_F32) for c in range(0, w, 2 * LANES)]
    return jnp.concatenate(parts, axis=1)


def _head_norm(t, bd, gain):
    ss = _head_sumsq(t, bd)
    return t * lax.rsqrt(ss * (1.0 / HEAD_DIM) + EPS) * gain


def _rope(t, cos, sin_signed):
    outs = []
    for c in range(0, t.shape[1], LANES):
        tc = t[:, c:c + LANES]
        lane = lax.broadcasted_iota(jnp.int32, tc.shape, 1)
        first_half = (lane & (HEAD_DIM - 1)) < HEAD_DIM // 2
        rot = jnp.where(first_half, pltpu.roll(tc, LANES - HEAD_DIM // 2, 1), pltpu.roll(tc, HEAD_DIM // 2, 1))
        outs.append(tc * cos + rot * sin_signed)
    return outs[0] if len(outs) == 1 else jnp.concatenate(outs, axis=1)


def _rep_pair(t):
    rolled = pltpu.roll(t, HEAD_DIM, 1)
    lo = lax.broadcasted_iota(jnp.int32, t.shape, 1) < HEAD_DIM
    return jnp.concatenate([jnp.where(lo, t, rolled), jnp.where(lo, rolled, t)], axis=1)


def _store_heads(ref, t):
    n, w = t.shape
    heads = w // HEAD_DIM
    for hd in range(heads):
        ref[pl.ds(hd, n, stride=heads), :] = t[:, hd * HEAD_DIM:(hd + 1) * HEAD_DIM]


def _proj_stages(x_ref, gn_ref, w_ref, gqa_ref, gka_ref, gqb_ref, gkb_ref, ct_ref, st_ref, cr_ref, sr_ref,
                 qa_ref, ka_ref, va_ref, qb_ref, kb_ref, vb_ref, keep):
    x = x_ref[...]
    ms = jnp.mean(x * x, axis=-1, keepdims=True)
    h = (x * lax.rsqrt(ms + EPS) * gn_ref[...]).astype(_BF16)
    r = lax.broadcasted_iota(jnp.int32, (2 * LANES, 2 * LANES), 0) // HEAD_DIM
    c = lax.broadcasted_iota(jnp.int32, (2 * LANES, 2 * LANES), 1) // HEAD_DIM
    bd = jnp.where(r == c, 1.0, 0.0).astype(_BF16)
    ct, st, cr, sr = ct_ref[0], st_ref[0], cr_ref[...], sr_ref[...]
    cos = ct * cr - st * sr
    sin = st * cr + ct * sr

    def proj(lo, width):
        return jnp.dot(h, w_ref[:, lo:lo + width], preferred_element_type=_F32)

    o_ka, o_va, o_qb, o_kvb = A_WIDTH, 2 * A_WIDTH, 3 * A_WIDTH, 3 * A_WIDTH + B_WIDTH
    yield
    p_qa = proj(0, A_WIDTH)
    yield
    p_ka = proj(o_ka, A_WIDTH)
    qa_ref[...] = (_head_norm(p_qa, bd, gqa_ref[...]) * Q_SCALE).astype(_BF16)
    yield
    va = proj(o_va, A_WIDTH)
    ka = _head_norm(p_ka, bd, gka_ref[...])
    ka_ref[...] = ka.astype(_BF16)
    yield
    p_qb = proj(o_qb, B_WIDTH)
    va_ref[...] = va.astype(_BF16)
    yield
    p_kvb = proj(o_kvb, 2 * B_KV_WIDTH)
    qb_ref[...] = (_rope(_head_norm(p_qb, bd, gqb_ref[...]), cos, sin) * Q_SCALE).astype(_BF16)
    yield
    kb = _rope(_head_norm(p_kvb[:, :B_KV_WIDTH], bd, gkb_ref[...]), cos, sin)
    vb = p_kvb[:, B_KV_WIDTH:]
    kb_ref[...] = _rep_pair(kb).astype(_BF16)
    vb_ref[...] = _rep_pair(vb).astype(_BF16)
    keep.update(ka=ka, va=va, kb=kb, vb=vb)


def _proj_kernel(*refs):
    kaf_ref, vaf_ref, kbf_ref, vbf_ref = refs[17:21]
    keep = {}
    for _ in _proj_stages(*refs[:17], keep):
        pass
    _store_heads(kaf_ref, keep["ka"])
    _store_heads(vaf_ref, keep["va"])
    kbf_ref[...] = keep["kb"]
    vbf_ref[...] = keep["vb"]


def _cast_blocks(w, n_steps):
    rows = w.shape[0]
    n_blocks = max(d for d in range(1, n_steps + 1) if rows % d == 0 and (rows // d) % 16 == 0)
    return pl.BlockSpec((rows // n_blocks, w.shape[1]), lambda i: (jnp.minimum(i, n_blocks - 1), 0))


def _proj_in_specs(tm, tile_map, trig_map):
    return [
        pl.BlockSpec((tm, D_MODEL), tile_map), _const_spec((1, D_MODEL)), _const_spec((D_MODEL, IN_COLS)),
        _const_spec((1, A_WIDTH)), _const_spec((1, A_WIDTH)), _const_spec((1, B_WIDTH)), _const_spec((1, B_KV_WIDTH)),
        pl.BlockSpec((1, 1, LANES), trig_map), pl.BlockSpec((1, 1, LANES), trig_map),
        _const_spec((tm, LANES)), _const_spec((tm, LANES)),
    ]


def _project_streams(x, proj_args, rope, name):
    t = x.shape[0]
    tm = TOKEN_TILE
    tile = lambda w: pl.BlockSpec((tm, w), lambda i: (i, 0))
    widths = [A_WIDTH, A_WIDTH, A_WIDTH, B_WIDTH, 2 * B_KV_WIDTH, 2 * B_KV_WIDTH]
    out_shape = [jax.ShapeDtypeStruct((t, w), _BF16) for w in widths] + [
        jax.ShapeDtypeStruct((t * A_HEADS, HEAD_DIM), _F32), jax.ShapeDtypeStruct((t * A_HEADS, HEAD_DIM), _F32),
        jax.ShapeDtypeStruct((t, B_KV_WIDTH), _F32), jax.ShapeDtypeStruct((t, B_KV_WIDTH), _F32)]
    out_specs = [tile(w) for w in widths] + [
        pl.BlockSpec((tm * A_HEADS, HEAD_DIM), lambda i: (i, 0))] * 2 + [tile(B_KV_WIDTH)] * 2
    return pl.pallas_call(
        _proj_kernel, grid=(t // tm,),
        in_specs=_proj_in_specs(tm, lambda i: (i, 0), lambda i: (0, 0, 0)),
        out_specs=out_specs, out_shape=out_shape, name=name,
        compiler_params=pltpu.CompilerParams(dimension_semantics=("arbitrary",), vmem_limit_bytes=VMEM_LIMIT),
    )(x, *proj_args, *rope)


def _build_bias(u_ref, bias_ref):
    qrow = lax.broadcasted_iota(jnp.int32, (CHUNK, BIAS_VAR), 0)
    for h in range(A_HEADS):
        prof = u_ref[h:h + 1, :]
        y = jnp.broadcast_to(prof - prof[:, 0:1], (CHUNK, BIAS_VAR))
        y = pltpu.roll(y, BIAS_VAR - (CHUNK - 1), 1)
        for b in range(6):
            y = jnp.where(((qrow >> b) & 1) == 1, pltpu.roll(y, 1 << b, 1), y)
        bias_ref[h // 4, (h % 4) * CHUNK:(h % 4 + 1) * CHUNK, :] = y[:, :BIAS_COLS] * LOG2E


_NT = (((1,), (1,)), ((), ()))
_NN = (((1,), (0,)), ((), ()))


def _scores(qa, qb, keys_a, keys_b, bias_ref, lim_a, lim_b):
    out = []
    head_of_lane = lax.broadcasted_iota(jnp.int32, (CHUNK, 2 * LANES), 1) // HEAD_DIM
    for g in range(2):
        qg = qa[:, 2 * LANES * g:2 * LANES * (g + 1)]
        qbd = jnp.concatenate([jnp.where(head_of_lane == r, qg, jnp.zeros_like(qg)) for r in range(4)], axis=0)
        pieces, off = [], 0
        for keys, on_rows in keys_a(g):
            s = lax.dot_general(qbd, keys, _NT if on_rows else _NN, preferred_element_type=_F32)
            n = s.shape[1]
            if off + n > A_WIN - BIAS_COLS:
                lo_col = max(A_WIN - BIAS_COLS - off, 0)
                biased = s[:, lo_col:] + bias_ref[g, :, off + lo_col - (A_WIN - BIAS_COLS):off + n - (A_WIN - BIAS_COLS)]
                s = biased if lo_col == 0 else jnp.concatenate([s[:, :lo_col], biased], axis=1)
            if lim_a:
                s = jnp.where(lax.broadcasted_iota(jnp.int32, s.shape, 1) >= lim_a, s, NEG)
            pieces.append(s)
            off += n
        out.append(pieces)
    lo = lax.broadcasted_iota(jnp.int32, (CHUNK, LANES), 1) < HEAD_DIM
    for g in range(B_KV_HEADS):
        rows = []
        for r in range(B_GROUP):
            hd = B_GROUP * g + r
            qp = qb[:, LANES * (hd // 2):LANES * (hd // 2 + 1)]
            keep = lo if hd % 2 == 0 else jnp.logical_not(lo)
            rows.append(jnp.where(keep, qp, jnp.zeros_like(qp)))
        qs = jnp.concatenate(rows, axis=0)
        pieces = []
        for keys, on_rows in keys_b(g):
            s = lax.dot_general(qs, keys, _NT if on_rows else _NN, preferred_element_type=_F32)
            if lim_b:
                s = jnp.where(lax.broadcasted_iota(jnp.int32, s.shape, 1) >= lim_b, s, NEG)
            pieces.append(s)
        out.append(pieces)
    return out


def _row_reduce(pieces, combine, reduce, fill):
    acc = None
    for s in pieces:
        for c in range(0, s.shape[1], LANES):
            tile = s[:, c:c + LANES]
            if tile.shape[1] < LANES:
                pad = jnp.full((tile.shape[0], LANES - tile.shape[1]), fill, tile.dtype)
                tile = jnp.concatenate([tile, pad], axis=1)
            acc = tile if acc is None else combine(acc, tile)
    return reduce(acc, axis=1, keepdims=True)


def _softmax_values(pieces, vals, sink):
    m = _row_reduce(pieces, jnp.maximum, jnp.max, -jnp.inf)
    if sink is not None:
        m = jnp.maximum(m, sink)
    es = [jnp.exp2(s - m) for s in pieces]
    l = _row_reduce(es, jnp.add, jnp.sum, 0.0)
    if sink is not None:
        l = l + jnp.exp2(sink - m)
    acc = None
    for e, (v, on_rows) in zip(es, vals):
        part = lax.dot_general(e.astype(_BF16), v, _NN if on_rows else _NT, preferred_element_type=_F32)
        acc = part if acc is None else acc + part
    return acc / l


def _softmax_pv(scores, vals_a, vals_b, sink_ref):
    outs = []
    head_of_lane = lax.broadcasted_iota(jnp.int32, (CHUNK, 2 * LANES), 1) // HEAD_DIM
    for g in range(2):
        of = _softmax_values(scores[g], vals_a(g), None)
        og = of[3 * CHUNK:4 * CHUNK]
        for r in (2, 1, 0):
            og = jnp.where(head_of_lane == r, of[r * CHUNK:(r + 1) * CHUNK], og)
        outs.append(og)
    lo = lax.broadcasted_iota(jnp.int32, (CHUNK, LANES), 1) < HEAD_DIM
    for g in range(B_KV_HEADS):
        of = _softmax_values(scores[2 + g], vals_b(g), sink_ref[g] * LOG2E)
        outs.append(jnp.where(lo, of[0:CHUNK], of[CHUNK:2 * CHUNK]))
        outs.append(jnp.where(lo, of[2 * CHUNK:3 * CHUNK], of[3 * CHUNK:4 * CHUNK]))
    return jnp.concatenate(outs, axis=1)


def _prompt_kernel(*refs, n_cast):
    proj_in, rest = list(refs[:11]), refs[11:]
    x0_ref, ct0_ref, st0_ref, u_ref, sink_ref = rest[:5]
    cast_in, rest = rest[5:5 + n_cast], rest[5 + n_cast:]
    o_ref, kaf_ref, vaf_ref, kbf_ref, vbf_ref = rest[:5]
    cast_out, rest = rest[5:5 + n_cast], rest[5 + n_cast:]
    qa_s, qb_s, ka_st, va_st, kb_st, vb_st, kwa, vwa, kwb, vwb, bias = rest
    s = pl.program_id(0)
    n_tiles = pl.num_programs(0)
    tm = o_ref.shape[0]
    n_chunks = tm // CHUNK
    windows = ((kwa, ka_st, A_CTX), (vwa, va_st, A_CTX), (kwb, kb_st, B_CTX), (vwb, vb_st, B_CTX))

    for src, dst in zip(cast_in, cast_out):
        dst[...] = src[...].astype(_BF16)

    @pl.when(s == 0)
    def _():
        _build_bias(u_ref, bias)
        first = [x0_ref] + proj_in[1:7] + [ct0_ref, st0_ref] + proj_in[9:]
        for _ in _proj_stages(*first, qa_s.at[0], ka_st, va_st, qb_s.at[0], kb_st, vb_st, {}):
            pass
        for win, stage, ctx in windows:
            win[0:ctx, :] = jnp.zeros((ctx, win.shape[1]), win.dtype)
            win[ctx:ctx + tm, :] = stage[...]

    @pl.when(s > 0)
    def _():
        for win, stage, ctx in windows:
            win[0:ctx, :] = win[tm:tm + ctx, :]
            win[ctx:ctx + tm, :] = stage[...]

    cur, nxt_slot = s % 2, (s + 1) % 2

    def window(ref, r0, rows, width):
        return lambda g: [(ref[r0:r0 + rows, width * g:width * (g + 1)], True)]

    def run(masked):
        keep = {}
        stages = _proj_stages(*proj_in, qa_s.at[nxt_slot], ka_st, va_st, qb_s.at[nxt_slot], kb_st, vb_st, keep)

        def scores(c):
            r0 = c * CHUNK
            lim_a = max(A_LEFT_CHUNKS - c, 0) * CHUNK if masked else 0
            lim_b = max(B_CTX // CHUNK - c, 0) * CHUNK if masked else 0
            return _scores(qa_s[cur, r0:r0 + CHUNK, :], qb_s[cur, r0:r0 + CHUNK, :],
                           window(kwa, r0, A_WIN, 2 * LANES), window(kwb, r0, B_WIN, LANES), bias, lim_a, lim_b)

        nxt = scores(0)
        for c in range(n_chunks):
            cur_scores = nxt
            if c + 1 < n_chunks:
                nxt = scores(c + 1)
            if c % STAGE_EVERY == 0:
                next(stages, None)
            r0 = c * CHUNK
            o = _softmax_pv(cur_scores, window(vwa, r0, A_WIN, 2 * LANES), window(vwb, r0, B_WIN, LANES), sink_ref)
            o_ref[r0:r0 + CHUNK, :] = o.astype(_BF16)
        for _ in stages:
            pass

        @pl.when(s == n_tiles - 1)
        def _():
            kaf_ref[...] = keep["ka"][tm - A_CTX:, :].T
            vaf_ref[...] = keep["va"][tm - A_CTX:, :].T
            kbf_ref[...] = keep["kb"][tm - B_CTX:, :].T
            vbf_ref[...] = keep["vb"][tm - B_CTX:, :].T

    @pl.when(s == 0)
    def _():
        run(True)

    @pl.when(s > 0)
    def _():
        run(False)


def _attn_sample_kernel(qa_ref, ka_ref, va_ref, qb_ref, kb_ref, vb_ref, cak_ref, cav_ref, cbk_ref, cbv_ref,
                        u_ref, sink_ref, o_ref, bias):
    n_streams = cak_ref.shape[0]

    @pl.when(pl.program_id(0) == 0)
    def _():
        _build_bias(u_ref, bias)

    def pieces_a(cache_ref, new_ref, b):
        rows = slice(b * CHUNK, (b + 1) * CHUNK)

        def get(g):
            cols = slice(2 * LANES * g, 2 * LANES * (g + 1))
            return [(cache_ref[b, cols, :].astype(_BF16), False), (new_ref[rows, cols], True)]
        return get

    def pieces_b(cache_ref, new_ref, b):
        rows = slice(b * CHUNK, (b + 1) * CHUNK)

        def get(g):
            past = cache_ref[b, HEAD_DIM * g:HEAD_DIM * (g + 1), :].astype(_BF16)
            return [(jnp.concatenate([past, past], axis=0), False), (new_ref[rows, LANES * g:LANES * (g + 1)], True)]
        return get

    def scores(b):
        rows = slice(b * CHUNK, (b + 1) * CHUNK)
        return _scores(qa_ref[rows, :], qb_ref[rows, :], pieces_a(cak_ref, ka_ref, b), pieces_b(cbk_ref, kb_ref, b),
                       bias, 0, 0)

    nxt = scores(0)
    for b in range(n_streams):
        cur = nxt
        if b + 1 < n_streams:
            nxt = scores(b + 1)
        o = _softmax_pv(cur, pieces_a(cav_ref, va_ref, b), pieces_b(cbv_ref, vb_ref, b), sink_ref)
        o_ref[b * CHUNK:(b + 1) * CHUNK, :] = o.astype(_BF16)


_BIAS_SCRATCH = pltpu.VMEM((2, 4 * CHUNK, BIAS_COLS), _F32)


def _project_attend_prompt(x, proj_args, rope, u, sink, cast):
    t = x.shape[0]
    tm = ATTN_TILE
    n = t // tm
    ct, st, cr, sr = rope
    nxt = lambda s: jnp.minimum(s + 1, n - 1)
    cast_specs = [_cast_blocks(w, n) for w in cast]
    ctx_shapes = [(A_WIDTH, A_CTX), (A_WIDTH, A_CTX), (B_KV_WIDTH, B_CTX), (B_KV_WIDTH, B_CTX)]
    out_shape = ([jax.ShapeDtypeStruct((t, A_WIDTH + B_WIDTH), _BF16)]
                 + [jax.ShapeDtypeStruct(sh, _F32) for sh in ctx_shapes]
                 + [jax.ShapeDtypeStruct(w.shape, _BF16) for w in cast])
    out_specs = ([pl.BlockSpec((tm, A_WIDTH + B_WIDTH), lambda s: (s, 0))]
                 + [pl.BlockSpec(sh, lambda s: (0, 0)) for sh in ctx_shapes] + cast_specs)
    in_specs = (_proj_in_specs(tm, lambda s: (nxt(s), 0), lambda s: (nxt(s), 0, 0))
                + [_const_spec((tm, D_MODEL)), _const_spec((1, 1, LANES)), _const_spec((1, 1, LANES)),
                   _const_spec(u.shape), _const_spec(sink.shape)] + cast_specs)
    scratch = [
        pltpu.VMEM((2, tm, A_WIDTH), _BF16), pltpu.VMEM((2, tm, B_WIDTH), _BF16),
        pltpu.VMEM((tm, A_WIDTH), _BF16), pltpu.VMEM((tm, A_WIDTH), _BF16),
        pltpu.VMEM((tm, 2 * B_KV_WIDTH), _BF16), pltpu.VMEM((tm, 2 * B_KV_WIDTH), _BF16),
        pltpu.VMEM((A_CTX + tm, A_WIDTH), _BF16), pltpu.VMEM((A_CTX + tm, A_WIDTH), _BF16),
        pltpu.VMEM((B_CTX + tm, 2 * B_KV_WIDTH), _BF16), pltpu.VMEM((B_CTX + tm, 2 * B_KV_WIDTH), _BF16),
        _BIAS_SCRATCH,
    ]
    return pl.pallas_call(
        functools.partial(_prompt_kernel, n_cast=len(cast)),
        grid=(n,), in_specs=in_specs, out_specs=out_specs, out_shape=out_shape, scratch_shapes=scratch,
        name="proj_attn_prompt",
        compiler_params=pltpu.CompilerParams(dimension_semantics=("arbitrary",), vmem_limit_bytes=FUSED_VMEM_LIMIT),
    )(x, *proj_args, ct, st, cr, sr, x, ct, st, u, sink, *cast)


def _attend_sample(qa, ka, va, qb, kb, vb, cak, cav, cbk, cbv, u, sink):
    t = qa.shape[0]
    sb = SAMPLE_STREAMS
    tile = lambda w: pl.BlockSpec((sb * CHUNK, w), lambda i: (i, 0))
    cache = lambda arr: pl.BlockSpec((sb,) + arr.shape[1:], lambda i: (i, 0, 0))
    return pl.pallas_call(
        _attn_sample_kernel,
        grid=(t // (sb * CHUNK),),
        in_specs=[tile(A_WIDTH), tile(A_WIDTH), tile(A_WIDTH), tile(B_WIDTH), tile(2 * B_KV_WIDTH),
                  tile(2 * B_KV_WIDTH), cache(cak), cache(cav), cache(cbk), cache(cbv),
                  _const_spec(u.shape), _const_spec(sink.shape)],
        out_specs=tile(A_WIDTH + B_WIDTH),
        out_shape=jax.ShapeDtypeStruct((t, A_WIDTH + B_WIDTH), _BF16),
        scratch_shapes=[_BIAS_SCRATCH],
        name="attn_sample",
        compiler_params=pltpu.CompilerParams(dimension_semantics=("arbitrary",), vmem_limit_bytes=VMEM_LIMIT),
    )(qa, ka, va, qb, kb, vb, cak, cav, cbk, cbv, u, sink)


def _ffn_kernel(o_ref, x_ref, wo_ref, gn_ref, wgu_ref, cw_ref, cb_ref, wd_ref, st_ref,
                y_ref, cn_ref, carry, act_buf, *, seg_len, carry_rows):
    i = pl.program_id(0)
    tm = x_ref.shape[0]
    n_seg = tm // seg_len

    if carry_rows:
        @pl.when(i == 0)
        def _():
            carry[...] = st_ref[0]

    y1 = x_ref[...] + jnp.dot(o_ref[...], wo_ref[...], preferred_element_type=_F32)
    ms = jnp.mean(y1 * y1, axis=-1, keepdims=True)
    h = (y1 * lax.rsqrt(ms + EPS) * gn_ref[...]).astype(_BF16)

    row8 = lax.broadcasted_iota(jnp.int32, (8, FF_CHUNK), 0)

    def gate_up(c0):
        g = jnp.dot(h, wgu_ref[:, c0:c0 + FF_CHUNK], preferred_element_type=_F32)
        up = jnp.dot(h, wgu_ref[:, D_FF + c0:D_FF + c0 + FF_CHUNK], preferred_element_type=_F32)
        return g, up

    nxt = gate_up(0)
    for c0 in range(0, D_FF, FF_CHUNK):
        cols = slice(c0, c0 + FF_CHUNK)
        g, up = nxt
        if c0 + FF_CHUNK < D_FF:
            nxt = gate_up(c0 + FF_CHUNK)
        g1 = pltpu.roll(g, 1, 0)
        g2 = pltpu.roll(g, 2, 0)
        p1, p2 = [], []
        for s in range(n_seg):
            a = s * seg_len
            prev = carry[:, cols] if carry_rows else st_ref[s, :, cols]
            prev0 = jnp.broadcast_to(prev[0:1, :], (8, FF_CHUNK))
            prev1 = jnp.broadcast_to(prev[1:2, :], (8, FF_CHUNK))
            p1.append(jnp.where(row8 == 0, prev1, g1[a:a + 8]))
            p1.append(g1[a + 8:a + seg_len])
            p2.append(jnp.where(row8 == 0, prev0, jnp.where(row8 == 1, prev1, g2[a:a + 8])))
            p2.append(g2[a + 8:a + seg_len])
            tail = g[a + seg_len - 8:a + seg_len][8 - (CONV_W - 1):, :]
            if carry_rows:
                carry[:, cols] = tail
            else:
                cn_ref[s, :, cols] = tail
        g1 = jnp.concatenate(p1, axis=0)
        g2 = jnp.concatenate(p2, axis=0)
        conv = cb_ref[:, cols] + cw_ref[0:1, cols] * g2 + cw_ref[1:2, cols] * g1 + cw_ref[2:3, cols] * g
        act_buf[:, cols] = (conv * jax.nn.sigmoid(conv) * up).astype(_BF16)
    y_ref[...] = y1 + jnp.dot(act_buf[...], wd_ref[...], preferred_element_type=_F32)
    if carry_rows:
        cn_ref[0] = carry[...]


def _out_ffn(o, x, w_out, gn, w_gu, conv_w, conv_b, w_down, state, *, carry_rows, name):
    t = x.shape[0]
    tm = TOKEN_TILE
    n = t // tm
    seg_len = tm if carry_rows else CHUNK
    n_seg = tm // seg_len
    tile = lambda w: pl.BlockSpec((tm, w), lambda i: (i, 0))
    if carry_rows:
        st_spec = _const_spec((1, CONV_W - 1, D_FF))
        cn_spec = pl.BlockSpec((1, CONV_W - 1, D_FF), lambda i: (0, 0, 0))
        cn_shape = (1, CONV_W - 1, D_FF)
    else:
        st_spec = pl.BlockSpec((n_seg, CONV_W - 1, D_FF), lambda i: (i, 0, 0))
        cn_spec = pl.BlockSpec((n_seg, CONV_W - 1, D_FF), lambda i: (i, 0, 0))
        cn_shape = (t // seg_len, CONV_W - 1, D_FF)
    return pl.pallas_call(
        functools.partial(_ffn_kernel, seg_len=seg_len, carry_rows=carry_rows),
        grid=(n,),
        in_specs=[tile(A_WIDTH + B_WIDTH), tile(D_MODEL), _const_spec((D_MODEL, D_MODEL)),
                  _const_spec((1, D_MODEL)), _const_spec((D_MODEL, 2 * D_FF)), _const_spec((CONV_W, D_FF)),
                  _const_spec((1, D_FF)), _const_spec((D_FF, D_MODEL)), st_spec],
        out_specs=[tile(D_MODEL), cn_spec],
        out_shape=[jax.ShapeDtypeStruct((t, D_MODEL), _F32), jax.ShapeDtypeStruct(cn_shape, _F32)],
        scratch_shapes=[pltpu.VMEM((CONV_W - 1, D_FF), _F32), pltpu.VMEM((tm, D_FF), _BF16)],
        name=name,
        compiler_params=pltpu.CompilerParams(dimension_semantics=("arbitrary",), vmem_limit_bytes=VMEM_LIMIT),
    )(o, x, w_out, gn, w_gu, conv_w, conv_b, w_down, state)


def _rope_tables(tile_pos, row_pos):
    half = HEAD_DIM // 2
    lane = jnp.arange(LANES)
    inv = jnp.power(ROPE_THETA, -(lane % half).astype(_F32) / half)
    sign = jnp.where(lane % HEAD_DIM < half, -1.0, 1.0).astype(_F32)
    a = tile_pos.astype(_F32)[:, None] * inv[None, :]
    b = row_pos.astype(_F32)[:, None] * inv[None, :]
    return (jnp.cos(a)[:, None, :], (jnp.sin(a) * sign)[:, None, :], jnp.cos(b), jnp.sin(b) * sign)


def _feature_major(cache):
    n, t, heads, d = cache.shape
    return jnp.transpose(cache, (0, 2, 3, 1)).reshape(n, heads * d, t)


def _token_major(rows, heads):
    t = rows.shape[1]
    return jnp.transpose(rows.reshape(heads, HEAD_DIM, t), (2, 0, 1))[None, None]


def _bias_profile(table):
    n_clip = BIAS_COLS - 1 - MAX_REL
    head = jnp.broadcast_to(table[:, 2 * MAX_REL:], (A_HEADS, n_clip))
    return jnp.concatenate([head, table[:, ::-1][:, :BIAS_VAR - n_clip]], axis=1)


def kernel(x_prompt, x_sample, cache_a_k, cache_a_v, cache_b_k, cache_b_v, state_conv, norm_attn, w_in, q_norm_a,
           k_norm_a, rel_bias_a, q_norm_b, k_norm_b, sinks_b, w_out, norm_ffn, w_gate_up, conv_w, conv_b, w_down):
    assert norm_attn.shape[0] == 1
    bsz, seq, _ = x_prompt.shape
    dec_b, dec_s, _ = x_sample.shape
    assert bsz == 1 and dec_s == CHUNK and (dec_b * dec_s) % TOKEN_TILE == 0
    assert seq % TOKEN_TILE == 0 and seq % ATTN_TILE == 0
    assert cache_a_k.shape[2] == A_CTX and cache_b_k.shape[2] == B_CTX

    w_in_b = w_in[0].astype(_BF16)
    gn_a = norm_attn[0][None, :]
    gn_f = norm_ffn[0][None, :]
    gqa = jnp.tile(q_norm_a[0], A_HEADS)[None, :]
    gka = jnp.tile(k_norm_a[0], A_HEADS)[None, :]
    gqb = jnp.tile(q_norm_b[0], B_HEADS)[None, :]
    gkb = jnp.tile(k_norm_b[0], B_KV_HEADS)[None, :]
    u = _bias_profile(rel_bias_a[0])
    sink = jnp.repeat(sinks_b[0], CHUNK).reshape(B_KV_HEADS, B_GROUP * CHUNK, 1)
    cw = conv_w[0]
    cb = conv_b[0][None, :]

    rope_p = _rope_tables(jnp.arange(0, seq, ATTN_TILE), jnp.arange(ATTN_TILE))
    rope_s = _rope_tables(jnp.full((1,), PAST_LEN), jnp.arange(TOKEN_TILE) % dec_s)

    xp = x_prompt.reshape(seq, D_MODEL)
    proj_args = (gn_a, w_in_b, gqa, gka, gqb, gkb)
    o_p, kaf, vaf, kbf, vbf, w_out_b, w_gu_b, w_down_b = _project_attend_prompt(
        xp, proj_args, rope_p, u, sink, (w_out[0], w_gate_up[0], w_down[0]))
    zero_state = jnp.zeros((1, CONV_W - 1, D_FF), _F32)
    y_p, cn_p = _out_ffn(o_p, xp, w_out_b, gn_f, w_gu_b, cw, cb, w_down_b, zero_state,
                         carry_rows=True, name="ffn_prompt")

    xs = x_sample.reshape(dec_b * dec_s, D_MODEL)
    qa, ka, va, qb, kb, vb, kaf_s, vaf_s, kbf_s, vbf_s = _project_streams(xs, proj_args, rope_s, "proj_sample")
    o_s = _attend_sample(qa, ka, va, qb, kb, vb, _feature_major(cache_a_k[0]), _feature_major(cache_a_v[0]),
                         _feature_major(cache_b_k[0]), _feature_major(cache_b_v[0]), u, sink)
    y_s, cn_s = _out_ffn(o_s, xs, w_out_b, gn_f, w_gu_b, cw, cb, w_down_b, state_conv[0],
                         carry_rows=False, name="ffn_sample")

    return (
        y_p.reshape(1, seq, D_MODEL),
        y_s.reshape(dec_b, dec_s, D_MODEL),
        _token_major(kaf, A_HEADS),
        _token_major(vaf, A_HEADS),
        _token_major(kbf, B_KV_HEADS),
        _token_major(vbf, B_KV_HEADS),
        cn_p.reshape(1, 1, CONV_W - 1, D_FF),
        kaf_s.reshape(1, dec_b, dec_s, A_HEADS, HEAD_DIM),
        vaf_s.reshape(1, dec_b, dec_s, A_HEADS, HEAD_DIM),
        kbf_s.reshape(1, dec_b, dec_s, B_KV_HEADS, HEAD_DIM),
        vbf_s.reshape(1, dec_b, dec_s, B_KV_HEADS, HEAD_DIM),
        cn_s.reshape(1, dec_b, CONV_W - 1, D_FF),
    )
```

```python
import functools

import jax
import jax.numpy as jnp
from jax import lax
from jax.experimental import pallas as pl
from jax.experimental.pallas import tpu as pltpu

D_MODEL = 1024
CHUNK = 64
HEAD_DIM = 64
EPS = 1e-6
SCALE = HEAD_DIM ** -0.5
LOG2E = 1.4426950408889634
Q_SCALE = SCALE * LOG2E
NEG = -1e30
PAST_LEN = 1024
A_HEADS = 8
A_LEFT_CHUNKS = 8
A_CTX = A_LEFT_CHUNKS * CHUNK
A_WIN = A_CTX + CHUNK
MAX_REL = 128
A_WIDTH = A_HEADS * HEAD_DIM
B_HEADS = 8
B_KV_HEADS = 2
B_GROUP = B_HEADS // B_KV_HEADS
B_CTX = 128
B_WIN = B_CTX + CHUNK
B_WIDTH = B_HEADS * HEAD_DIM
B_KV_WIDTH = B_KV_HEADS * HEAD_DIM
ROPE_THETA = 10000.0
IN_COLS = 3 * A_WIDTH + B_WIDTH + 2 * B_KV_WIDTH
D_FF = 2816
CONV_W = 3

LANES = 128
TOKEN_TILE = 512
PROMPT_PROJ_TILE = 1024
ATTN_TILE = 1024
SAMPLE_STREAMS = 4
FF_CHUNK = 256
BIAS_COLS = MAX_REL + CHUNK
BIAS_VAR = 256
BIAS_PAD = BIAS_COLS + CHUNK
VMEM_LIMIT = 56 * 1024 * 1024

_BF16 = jnp.bfloat16
_F32 = jnp.float32


def _const_spec(shape):
    nd = len(shape)
    return pl.BlockSpec(shape, lambda i: (0,) * nd, pipeline_mode=pl.Buffered(1))


def _head_sumsq(t, bd):
    sq = (t * t).astype(_BF16)
    w = t.shape[1]
    if w <= 2 * LANES:
        return jnp.dot(sq, bd[:w, :w], preferred_element_type=_F32)
    parts = [jnp.dot(sq[:, c:c + 2 * LANES], bd, preferred_element_type=_F32) for c in range(0, w, 2 * LANES)]
    return jnp.concatenate(parts, axis=1)


def _head_norm(t, bd, gain):
    ss = _head_sumsq(t, bd)
    return t * lax.rsqrt(ss * (1.0 / HEAD_DIM) + EPS) * gain


def _rope(t, cos, sin_signed):
    outs = []
    for c in range(0, t.shape[1], LANES):
        tc = t[:, c:c + LANES]
        lane = lax.broadcasted_iota(jnp.int32, tc.shape, 1)
        first_half = (lane & (HEAD_DIM - 1)) < HEAD_DIM // 2
        rot = jnp.where(first_half, pltpu.roll(tc, LANES - HEAD_DIM // 2, 1), pltpu.roll(tc, HEAD_DIM // 2, 1))
        outs.append(tc * cos + rot * sin_signed)
    return outs[0] if len(outs) == 1 else jnp.concatenate(outs, axis=1)


def _rep_pair(t):
    rolled = pltpu.roll(t, HEAD_DIM, 1)
    lo = lax.broadcasted_iota(jnp.int32, t.shape, 1) < HEAD_DIM
    return jnp.concatenate([jnp.where(lo, t, rolled), jnp.where(lo, rolled, t)], axis=1)


def _store_heads(ref, t):
    n, w = t.shape
    heads = w // HEAD_DIM
    for hd in range(heads):
        ref[pl.ds(hd, n, stride=heads), :] = t[:, hd * HEAD_DIM:(hd + 1) * HEAD_DIM]


def _proj_kernel(x_ref, gn_ref, w_ref, gqa_ref, gka_ref, gqb_ref, gkb_ref, ct_ref, st_ref, cr_ref, sr_ref, *rest,
                 last_only, n_cast):
    cast_in, rest = rest[:n_cast], rest[n_cast:]
    qa_ref, ka_ref, va_ref, qb_ref, kb_ref, vb_ref, kaf_ref, vaf_ref, kbf_ref, vbf_ref = rest[:10]
    for src, dst in zip(cast_in, rest[10:]):
        dst[...] = src[...].astype(_BF16)
    tm = x_ref.shape[0]
    x = x_ref[...]
    ms = jnp.mean(x * x, axis=-1, keepdims=True)
    h = (x * lax.rsqrt(ms + EPS) * gn_ref[...]).astype(_BF16)

    r = lax.broadcasted_iota(jnp.int32, (2 * LANES, 2 * LANES), 0) // HEAD_DIM
    c = lax.broadcasted_iota(jnp.int32, (2 * LANES, 2 * LANES), 1) // HEAD_DIM
    bd = jnp.where(r == c, 1.0, 0.0).astype(_BF16)
    ct, st, cr, sr = ct_ref[0], st_ref[0], cr_ref[...], sr_ref[...]
    cos = ct * cr - st * sr
    sin = st * cr + ct * sr

    def proj(lo, width):
        return jnp.dot(h, w_ref[:, lo:lo + width], preferred_element_type=_F32)

    o_ka, o_va, o_qb, o_kvb = A_WIDTH, 2 * A_WIDTH, 3 * A_WIDTH, 3 * A_WIDTH + B_WIDTH
    p_qa = proj(0, A_WIDTH)
    p_ka = proj(o_ka, A_WIDTH)
    qa_ref[...] = (_head_norm(p_qa, bd, gqa_ref[...]) * Q_SCALE).astype(_BF16)
    va = proj(o_va, A_WIDTH)
    ka = _head_norm(p_ka, bd, gka_ref[...])
    ka_ref[...] = ka.astype(_BF16)
    p_qb = proj(o_qb, B_WIDTH)
    va_ref[...] = va.astype(_BF16)
    p_kvb = proj(o_kvb, 2 * B_KV_WIDTH)
    qb_ref[...] = (_rope(_head_norm(p_qb, bd, gqb_ref[...]), cos, sin) * Q_SCALE).astype(_BF16)
    kb = _rope(_head_norm(p_kvb[:, :B_KV_WIDTH], bd, gkb_ref[...]), cos, sin)
    vb = p_kvb[:, B_KV_WIDTH:]
    kb_ref[...] = _rep_pair(kb).astype(_BF16)
    vb_ref[...] = _rep_pair(vb).astype(_BF16)

    if last_only:
        @pl.when(pl.program_id(0) == pl.num_programs(0) - 1)
        def _():
            kaf_ref[...] = ka[tm - A_CTX:, :].T
            vaf_ref[...] = va[tm - A_CTX:, :].T
            kbf_ref[...] = kb[tm - B_CTX:, :].T
            vbf_ref[...] = vb[tm - B_CTX:, :].T
    else:
        _store_heads(kaf_ref, ka)
        _store_heads(vaf_ref, va)
        kbf_ref[...] = kb
        vbf_ref[...] = vb


def _cast_blocks(w, n_steps):
    rows = w.shape[0]
    n_blocks = max(d for d in range(1, n_steps + 1) if rows % d == 0 and (rows // d) % 16 == 0)
    return pl.BlockSpec((rows // n_blocks, w.shape[1]), lambda i: (jnp.minimum(i, n_blocks - 1), 0))


def _project(x, gn, w_in, gqa, gka, gqb, gkb, rope, *, keep_all, name, cast=()):
    t = x.shape[0]
    tm = rope[2].shape[0]
    n = t // tm
    cast_specs = [_cast_blocks(w, n) for w in cast]
    tile = lambda w: pl.BlockSpec((tm, w), lambda i: (i, 0))
    if keep_all:
        f32_shapes = [(t * A_HEADS, HEAD_DIM), (t * A_HEADS, HEAD_DIM), (t, B_KV_WIDTH), (t, B_KV_WIDTH)]
        f32_specs = [pl.BlockSpec((tm * A_HEADS, HEAD_DIM), lambda i: (i, 0))] * 2 + [tile(B_KV_WIDTH)] * 2
    else:
        f32_shapes = [(A_WIDTH, A_CTX), (A_WIDTH, A_CTX), (B_KV_WIDTH, B_CTX), (B_KV_WIDTH, B_CTX)]
        f32_specs = [pl.BlockSpec(sh, lambda i: (0, 0)) for sh in f32_shapes]
    ct, st, cr, sr = rope
    if ct.shape[0] == n:
        tile_trig = pl.BlockSpec((1, 1, LANES), lambda i: (i, 0, 0))
    else:
        tile_trig = _const_spec((1, 1, LANES))
    out_shape = [
        jax.ShapeDtypeStruct((t, A_WIDTH), _BF16), jax.ShapeDtypeStruct((t, A_WIDTH), _BF16),
        jax.ShapeDtypeStruct((t, A_WIDTH), _BF16), jax.ShapeDtypeStruct((t, B_WIDTH), _BF16),
        jax.ShapeDtypeStruct((t, 2 * B_KV_WIDTH), _BF16), jax.ShapeDtypeStruct((t, 2 * B_KV_WIDTH), _BF16),
    ] + [jax.ShapeDtypeStruct(sh, _F32) for sh in f32_shapes] + [jax.ShapeDtypeStruct(w.shape, _BF16) for w in cast]
    out_specs = [
        tile(A_WIDTH), tile(A_WIDTH), tile(A_WIDTH), tile(B_WIDTH), tile(2 * B_KV_WIDTH), tile(2 * B_KV_WIDTH),
    ] + f32_specs + cast_specs
    in_specs = [
        tile(D_MODEL), _const_spec((1, D_MODEL)), _const_spec((D_MODEL, IN_COLS)),
        _const_spec((1, A_WIDTH)), _const_spec((1, A_WIDTH)), _const_spec((1, B_WIDTH)),
        _const_spec((1, B_KV_WIDTH)), tile_trig, tile_trig, _const_spec((tm, LANES)), _const_spec((tm, LANES)),
    ] + cast_specs
    return pl.pallas_call(
        functools.partial(_proj_kernel, last_only=not keep_all, n_cast=len(cast)),
        grid=(n,), in_specs=in_specs, out_specs=out_specs, out_shape=out_shape, name=name,
        compiler_params=pltpu.CompilerParams(dimension_semantics=("arbitrary",), vmem_limit_bytes=VMEM_LIMIT),
    )(x, gn, w_in, gqa, gka, gqb, gkb, ct, st, cr, sr, *cast)


def _build_bias(u_ref, bias_ref):
    qrow = lax.broadcasted_iota(jnp.int32, (CHUNK, BIAS_VAR), 0)
    for h in range(A_HEADS):
        prof = u_ref[h:h + 1, :]
        y = jnp.broadcast_to(prof - prof[:, 0:1], (CHUNK, BIAS_VAR))
        y = pltpu.roll(y, BIAS_VAR - (CHUNK - 1), 1)
        for b in range(6):
            y = jnp.where(((qrow >> b) & 1) == 1, pltpu.roll(y, 1 << b, 1), y)
        rows = slice((h % 4) * CHUNK, (h % 4 + 1) * CHUNK)
        bias_ref[h // 4, rows, 0:BIAS_COLS] = y[:, :BIAS_COLS] * LOG2E
        bias_ref[h // 4, rows, BIAS_COLS:] = jnp.full((CHUNK, BIAS_PAD - BIAS_COLS), NEG, _F32)


_NT = (((1,), (1,)), ((), ()))
_NN = (((1,), (0,)), ((), ()))


def _scores(qa, qb, keys_a, keys_b, bias_ref, lim_a, lim_b):
    out = []
    head_of_lane = lax.broadcasted_iota(jnp.int32, (CHUNK, 2 * LANES), 1) // HEAD_DIM
    for g in range(2):
        qg = qa[:, 2 * LANES * g:2 * LANES * (g + 1)]
        qbd = jnp.concatenate([jnp.where(head_of_lane == r, qg, jnp.zeros_like(qg)) for r in range(4)], axis=0)
        pieces, off = [], 0
        for keys, on_rows in keys_a(g):
            s = lax.dot_general(qbd, keys, _NT if on_rows else _NN, preferred_element_type=_F32)
            n = s.shape[1]
            if off + n > A_WIN - BIAS_COLS:
                lo_col = max(A_WIN - BIAS_COLS - off, 0)
                biased = s[:, lo_col:] + bias_ref[g, :, off + lo_col - (A_WIN - BIAS_COLS):off + n - (A_WIN - BIAS_COLS)]
                s = biased if lo_col == 0 else jnp.concatenate([s[:, :lo_col], biased], axis=1)
            if lim_a:
                s = jnp.where(lax.broadcasted_iota(jnp.int32, s.shape, 1) >= lim_a, s, NEG)
            pieces.append(s)
            off += n
        out.append(pieces)
    lo = lax.broadcasted_iota(jnp.int32, (CHUNK, LANES), 1) < HEAD_DIM
    for g in range(B_KV_HEADS):
        rows = []
        for r in range(B_GROUP):
            hd = B_GROUP * g + r
            qp = qb[:, LANES * (hd // 2):LANES * (hd // 2 + 1)]
            keep = lo if hd % 2 == 0 else jnp.logical_not(lo)
            rows.append(jnp.where(keep, qp, jnp.zeros_like(qp)))
        qs = jnp.concatenate(rows, axis=0)
        pieces = []
        for keys, on_rows in keys_b(g):
            s = lax.dot_general(qs, keys, _NT if on_rows else _NN, preferred_element_type=_F32)
            if s.shape[1] > B_WIN:
                lane = lax.broadcasted_iota(jnp.int32, (1, LANES), 1)
                pad_row = jnp.where(lane < B_WIN - LANES, 0.0, NEG)
                s = jnp.concatenate([s[:, :LANES], s[:, LANES:] + pad_row], axis=1)
            if lim_b:
                s = jnp.where(lax.broadcasted_iota(jnp.int32, s.shape, 1) >= lim_b, s, NEG)
            pieces.append(s)
        out.append(pieces)
    return out


def _row_reduce(pieces, combine, reduce, fill):
    acc = None
    for s in pieces:
        for c in range(0, s.shape[1], LANES):
            tile = s[:, c:c + LANES]
            if tile.shape[1] < LANES:
                pad = jnp.full((tile.shape[0], LANES - tile.shape[1]), fill, tile.dtype)
                tile = jnp.concatenate([tile, pad], axis=1)
            acc = tile if acc is None else combine(acc, tile)
    return reduce(acc, axis=1, keepdims=True)


def _softmax_values(pieces, vals, sink):
    m = _row_reduce(pieces, jnp.maximum, jnp.max, -jnp.inf)
    if sink is not None:
        m = jnp.maximum(m, sink)
    es = [jnp.exp2(s - m) for s in pieces]
    l = _row_reduce(es, jnp.add, jnp.sum, 0.0)
    if sink is not None:
        l = l + jnp.exp2(sink - m)
    acc = None
    for e, (v, on_rows) in zip(es, vals):
        part = lax.dot_general(e.astype(_BF16), v, _NN if on_rows else _NT, preferred_element_type=_F32)
        acc = part if acc is None else acc + part
    return acc / l


def _softmax_pv(scores, vals_a, vals_b, sink_ref):
    outs = []
    head_of_lane = lax.broadcasted_iota(jnp.int32, (CHUNK, 2 * LANES), 1) // HEAD_DIM
    for g in range(2):
        of = _softmax_values(scores[g], vals_a(g), None)
        og = of[3 * CHUNK:4 * CHUNK]
        for r in (2, 1, 0):
            og = jnp.where(head_of_lane == r, of[r * CHUNK:(r + 1) * CHUNK], og)
        outs.append(og)
    lo = lax.broadcasted_iota(jnp.int32, (CHUNK, LANES), 1) < HEAD_DIM
    for g in range(B_KV_HEADS):
        of = _softmax_values(scores[2 + g], vals_b(g), sink_ref[g] * LOG2E)
        outs.append(jnp.where(lo, of[0:CHUNK], of[CHUNK:2 * CHUNK]))
        outs.append(jnp.where(lo, of[2 * CHUNK:3 * CHUNK], of[3 * CHUNK:4 * CHUNK]))
    return jnp.concatenate(outs, axis=1)


def _attn_prompt_kernel(qa_ref, ka_ref, va_ref, qb_ref, kb_ref, vb_ref, u_ref, sink_ref, o_ref,
                        kwa, vwa, kwb, vwb, bias):
    i = pl.program_id(0)
    tm = qa_ref.shape[0]
    n_chunks = tm // CHUNK

    @pl.when(i == 0)
    def _():
        kwa[0:A_CTX, :] = jnp.zeros((A_CTX, A_WIDTH), _BF16)
        vwa[0:A_CTX, :] = jnp.zeros((A_CTX, A_WIDTH), _BF16)
        kwa[A_CTX + tm:, :] = jnp.zeros((CHUNK, A_WIDTH), _BF16)
        vwa[A_CTX + tm:, :] = jnp.zeros((CHUNK, A_WIDTH), _BF16)
        kwb[0:B_CTX, :] = jnp.zeros((B_CTX, 2 * B_KV_WIDTH), _BF16)
        vwb[0:B_CTX, :] = jnp.zeros((B_CTX, 2 * B_KV_WIDTH), _BF16)
        kwb[B_CTX + tm:, :] = jnp.zeros((CHUNK, 2 * B_KV_WIDTH), _BF16)
        vwb[B_CTX + tm:, :] = jnp.zeros((CHUNK, 2 * B_KV_WIDTH), _BF16)
        _build_bias(u_ref, bias)

    @pl.when(i > 0)
    def _():
        kwa[0:A_CTX, :] = kwa[tm:tm + A_CTX, :]
        vwa[0:A_CTX, :] = vwa[tm:tm + A_CTX, :]
        kwb[0:B_CTX, :] = kwb[tm:tm + B_CTX, :]
        vwb[0:B_CTX, :] = vwb[tm:tm + B_CTX, :]

    kwa[A_CTX:A_CTX + tm, :] = ka_ref[...]
    vwa[A_CTX:A_CTX + tm, :] = va_ref[...]
    kwb[B_CTX:B_CTX + tm, :] = kb_ref[...]
    vwb[B_CTX:B_CTX + tm, :] = vb_ref[...]

    def window(ref, r0, rows, width):
        return lambda g: [(ref[r0:r0 + rows, width * g:width * (g + 1)], True)]

    def run(masked):
        def scores(c):
            r0 = c * CHUNK
            lim_a = max(A_LEFT_CHUNKS - c, 0) * CHUNK if masked else 0
            lim_b = max(B_CTX // CHUNK - c, 0) * CHUNK if masked else 0
            return _scores(qa_ref[r0:r0 + CHUNK, :], qb_ref[r0:r0 + CHUNK, :],
                           window(kwa, r0, A_WIN + CHUNK, 2 * LANES), window(kwb, r0, B_WIN + CHUNK, LANES),
                           bias, lim_a, lim_b)

        nxt = scores(0)
        for c in range(n_chunks):
            cur = nxt
            if c + 1 < n_chunks:
                nxt = scores(c + 1)
            r0 = c * CHUNK
            o = _softmax_pv(cur, window(vwa, r0, A_WIN + CHUNK, 2 * LANES), window(vwb, r0, B_WIN + CHUNK, LANES),
                            sink_ref)
            o_ref[r0:r0 + CHUNK, :] = o.astype(_BF16)

    @pl.when(i == 0)
    def _():
        run(True)

    @pl.when(i > 0)
    def _():
        run(False)


def _attn_sample_kernel(qa_ref, ka_ref, va_ref, qb_ref, kb_ref, vb_ref, cak_ref, cav_ref, cbk_ref, cbv_ref,
                        u_ref, sink_ref, o_ref, bias):
    n_streams = cak_ref.shape[0]

    @pl.when(pl.program_id(0) == 0)
    def _():
        _build_bias(u_ref, bias)

    def pieces_a(cache_ref, new_ref, b):
        rows = slice(b * CHUNK, (b + 1) * CHUNK)

        def get(g):
            cols = slice(2 * LANES * g, 2 * LANES * (g + 1))
            return [(cache_ref[b, cols, :].astype(_BF16), False), (new_ref[rows, cols], True)]
        return get

    def pieces_b(cache_ref, new_ref, b):
        rows = slice(b * CHUNK, (b + 1) * CHUNK)

        def get(g):
            past = cache_ref[b, HEAD_DIM * g:HEAD_DIM * (g + 1), :].astype(_BF16)
            return [(jnp.concatenate([past, past], axis=0), False), (new_ref[rows, LANES * g:LANES * (g + 1)], True)]
        return get

    def scores(b):
        rows = slice(b * CHUNK, (b + 1) * CHUNK)
        return _scores(qa_ref[rows, :], qb_ref[rows, :], pieces_a(cak_ref, ka_ref, b), pieces_b(cbk_ref, kb_ref, b),
                       bias, 0, 0)

    nxt = scores(0)
    for b in range(n_streams):
        cur = nxt
        if b + 1 < n_streams:
            nxt = scores(b + 1)
        o = _softmax_pv(cur, pieces_a(cav_ref, va_ref, b), pieces_b(cbv_ref, vb_ref, b), sink_ref)
        o_ref[b * CHUNK:(b + 1) * CHUNK, :] = o.astype(_BF16)


_BIAS_SCRATCH = pltpu.VMEM((2, 4 * CHUNK, BIAS_PAD), _F32)


def _attend_prompt(qa, ka, va, qb, kb, vb, u, sink):
    t = qa.shape[0]
    tm = ATTN_TILE
    tile = lambda w: pl.BlockSpec((tm, w), lambda i: (i, 0))
    return pl.pallas_call(
        _attn_prompt_kernel,
        grid=(t // tm,),
        in_specs=[tile(A_WIDTH), tile(A_WIDTH), tile(A_WIDTH), tile(B_WIDTH), tile(2 * B_KV_WIDTH),
                  tile(2 * B_KV_WIDTH), _const_spec(u.shape), _const_spec(sink.shape)],
        out_specs=tile(A_WIDTH + B_WIDTH),
        out_shape=jax.ShapeDtypeStruct((t, A_WIDTH + B_WIDTH), _BF16),
        scratch_shapes=[pltpu.VMEM((A_CTX + tm + CHUNK, A_WIDTH), _BF16),
                        pltpu.VMEM((A_CTX + tm + CHUNK, A_WIDTH), _BF16),
                        pltpu.VMEM((B_CTX + tm + CHUNK, 2 * B_KV_WIDTH), _BF16),
                        pltpu.VMEM((B_CTX + tm + CHUNK, 2 * B_KV_WIDTH), _BF16), _BIAS_SCRATCH],
        name="attn_prompt",
        compiler_params=pltpu.CompilerParams(dimension_semantics=("arbitrary",), vmem_limit_bytes=VMEM_LIMIT),
    )(qa, ka, va, qb, kb, vb, u, sink)


def _attend_sample(qa, ka, va, qb, kb, vb, cak, cav, cbk, cbv, u, sink):
    t = qa.shape[0]
    sb = SAMPLE_STREAMS
    tile = lambda w: pl.BlockSpec((sb * CHUNK, w), lambda i: (i, 0))
    cache = lambda arr: pl.BlockSpec((sb,) + arr.shape[1:], lambda i: (i, 0, 0))
    return pl.pallas_call(
        _attn_sample_kernel,
        grid=(t // (sb * CHUNK),),
        in_specs=[tile(A_WIDTH), tile(A_WIDTH), tile(A_WIDTH), tile(B_WIDTH), tile(2 * B_KV_WIDTH),
                  tile(2 * B_KV_WIDTH), cache(cak), cache(cav), cache(cbk), cache(cbv),
                  _const_spec(u.shape), _const_spec(sink.shape)],
        out_specs=tile(A_WIDTH + B_WIDTH),
        out_shape=jax.ShapeDtypeStruct((t, A_WIDTH + B_WIDTH), _BF16),
        scratch_shapes=[_BIAS_SCRATCH],
        name="attn_sample",
        compiler_params=pltpu.CompilerParams(dimension_semantics=("arbitrary",), vmem_limit_bytes=VMEM_LIMIT),
    )(qa, ka, va, qb, kb, vb, cak, cav, cbk, cbv, u, sink)


def _ffn_kernel(o_ref, x_ref, wo_ref, gn_ref, wgu_ref, cw_ref, cb_ref, wd_ref, st_ref,
                y_ref, cn_ref, carry, act_buf, *, seg_len, carry_rows):
    i = pl.program_id(0)
    tm = x_ref.shape[0]
    n_seg = tm // seg_len

    if carry_rows:
        @pl.when(i == 0)
        def _():
            carry[...] = st_ref[0]

    y1 = x_ref[...] + jnp.dot(o_ref[...], wo_ref[...], preferred_element_type=_F32)
    ms = jnp.mean(y1 * y1, axis=-1, keepdims=True)
    h = (y1 * lax.rsqrt(ms + EPS) * gn_ref[...]).astype(_BF16)

    row8 = lax.broadcasted_iota(jnp.int32, (8, FF_CHUNK), 0)

    def gate_up(c0):
        g = jnp.dot(h, wgu_ref[:, c0:c0 + FF_CHUNK], preferred_element_type=_F32)
        up = jnp.dot(h, wgu_ref[:, D_FF + c0:D_FF + c0 + FF_CHUNK], preferred_element_type=_F32)
        return g, up

    nxt = gate_up(0)
    for c0 in range(0, D_FF, FF_CHUNK):
        cols = slice(c0, c0 + FF_CHUNK)
        g, up = nxt
        if c0 + FF_CHUNK < D_FF:
            nxt = gate_up(c0 + FF_CHUNK)
        g1 = pltpu.roll(g, 1, 0)
        g2 = pltpu.roll(g, 2, 0)
        p1, p2 = [], []
        for s in range(n_seg):
            a = s * seg_len
            prev = carry[:, cols] if carry_rows else st_ref[s, :, cols]
            prev0 = jnp.broadcast_to(prev[0:1, :], (8, FF_CHUNK))
            prev1 = jnp.broadcast_to(prev[1:2, :], (8, FF_CHUNK))
            p1.append(jnp.where(row8 == 0, prev1, g1[a:a + 8]))
            p1.append(g1[a + 8:a + seg_len])
            p2.append(jnp.where(row8 == 0, prev0, jnp.where(row8 == 1, prev1, g2[a:a + 8])))
            p2.append(g2[a + 8:a + seg_len])
            tail = g[a + seg_len - 8:a + seg_len][8 - (CONV_W - 1):, :]
            if carry_rows:
                carry[:, cols] = tail
            else:
                cn_ref[s, :, cols] = tail
        g1 = jnp.concatenate(p1, axis=0)
        g2 = jnp.concatenate(p2, axis=0)
        conv = cb_ref[:, cols] + cw_ref[0:1, cols] * g2 + cw_ref[1:2, cols] * g1 + cw_ref[2:3, cols] * g
        act_buf[:, cols] = (conv * jax.nn.sigmoid(conv) * up).astype(_BF16)
    y_ref[...] = y1 + jnp.dot(act_buf[...], wd_ref[...], preferred_element_type=_F32)
    if carry_rows:
        cn_ref[0] = carry[...]


def _out_ffn(o, x, w_out, gn, w_gu, conv_w, conv_b, w_down, state, *, carry_rows, name):
    t = x.shape[0]
    tm = TOKEN_TILE
    n = t // tm
    seg_len = tm if carry_rows else CHUNK
    n_seg = tm // seg_len
    tile = lambda w: pl.BlockSpec((tm, w), lambda i: (i, 0))
    if carry_rows:
        st_spec = _const_spec((1, CONV_W - 1, D_FF))
        cn_spec = pl.BlockSpec((1, CONV_W - 1, D_FF), lambda i: (0, 0, 0))
        cn_shape = (1, CONV_W - 1, D_FF)
    else:
        st_spec = pl.BlockSpec((n_seg, CONV_W - 1, D_FF), lambda i: (i, 0, 0))
        cn_spec = pl.BlockSpec((n_seg, CONV_W - 1, D_FF), lambda i: (i, 0, 0))
        cn_shape = (t // seg_len, CONV_W - 1, D_FF)
    return pl.pallas_call(
        functools.partial(_ffn_kernel, seg_len=seg_len, carry_rows=carry_rows),
        grid=(n,),
        in_specs=[tile(A_WIDTH + B_WIDTH), tile(D_MODEL), _const_spec((D_MODEL, D_MODEL)),
                  _const_spec((1, D_MODEL)), _const_spec((D_MODEL, 2 * D_FF)), _const_spec((CONV_W, D_FF)),
                  _const_spec((1, D_FF)), _const_spec((D_FF, D_MODEL)), st_spec],
        out_specs=[tile(D_MODEL), cn_spec],
        out_shape=[jax.ShapeDtypeStruct((t, D_MODEL), _F32), jax.ShapeDtypeStruct(cn_shape, _F32)],
        scratch_shapes=[pltpu.VMEM((CONV_W - 1, D_FF), _F32), pltpu.VMEM((tm, D_FF), _BF16)],
        name=name,
        compiler_params=pltpu.CompilerParams(dimension_semantics=("arbitrary",), vmem_limit_bytes=VMEM_LIMIT),
    )(o, x, w_out, gn, w_gu, conv_w, conv_b, w_down, state)


def _rope_tables(tile_pos, row_pos):
    half = HEAD_DIM // 2
    lane = jnp.arange(LANES)
    inv = jnp.power(ROPE_THETA, -(lane % half).astype(_F32) / half)
    sign = jnp.where(lane % HEAD_DIM < half, -1.0, 1.0).astype(_F32)
    a = tile_pos.astype(_F32)[:, None] * inv[None, :]
    b = row_pos.astype(_F32)[:, None] * inv[None, :]
    return (jnp.cos(a)[:, None, :], (jnp.sin(a) * sign)[:, None, :], jnp.cos(b), jnp.sin(b) * sign)


def _feature_major(cache):
    n, t, heads, d = cache.shape
    return jnp.transpose(cache, (0, 2, 3, 1)).reshape(n, heads * d, t)


def _token_major(rows, heads):
    t = rows.shape[1]
    return jnp.transpose(rows.reshape(heads, HEAD_DIM, t), (2, 0, 1))[None, None]


def _bias_profile(table):
    n_clip = BIAS_COLS - 1 - MAX_REL
    head = jnp.broadcast_to(table[:, 2 * MAX_REL:], (A_HEADS, n_clip))
    return jnp.concatenate([head, table[:, ::-1][:, :BIAS_VAR - n_clip]], axis=1)


def kernel(x_prompt, x_sample, cache_a_k, cache_a_v, cache_b_k, cache_b_v, state_conv, norm_attn, w_in, q_norm_a,
           k_norm_a, rel_bias_a, q_norm_b, k_norm_b, sinks_b, w_out, norm_ffn, w_gate_up, conv_w, conv_b, w_down):
    assert norm_attn.shape[0] == 1
    bsz, seq, _ = x_prompt.shape
    dec_b, dec_s, _ = x_sample.shape
    assert bsz == 1 and dec_s == CHUNK and (dec_b * dec_s) % TOKEN_TILE == 0
    assert seq % TOKEN_TILE == 0 and seq % PROMPT_PROJ_TILE == 0 and seq % ATTN_TILE == 0
    assert cache_a_k.shape[2] == A_CTX and cache_b_k.shape[2] == B_CTX

    w_in_b = w_in[0].astype(_BF16)
    gn_a = norm_attn[0][None, :]
    gn_f = norm_ffn[0][None, :]
    gqa = jnp.tile(q_norm_a[0], A_HEADS)[None, :]
    gka = jnp.tile(k_norm_a[0], A_HEADS)[None, :]
    gqb = jnp.tile(q_norm_b[0], B_HEADS)[None, :]
    gkb = jnp.tile(k_norm_b[0], B_KV_HEADS)[None, :]
    u = _bias_profile(rel_bias_a[0])
    sink = jnp.repeat(sinks_b[0], CHUNK).reshape(B_KV_HEADS, B_GROUP * CHUNK, 1)
    cw = conv_w[0]
    cb = conv_b[0][None, :]

    rope_p = _rope_tables(jnp.arange(0, seq, PROMPT_PROJ_TILE), jnp.arange(PROMPT_PROJ_TILE))
    rope_s = _rope_tables(jnp.full((1,), PAST_LEN), jnp.arange(TOKEN_TILE) % dec_s)

    xp = x_prompt.reshape(seq, D_MODEL)
    qa, ka, va, qb, kb, vb, kaf, vaf, kbf, vbf, w_out_b, w_gu_b, w_down_b = _project(
        xp, gn_a, w_in_b, gqa, gka, gqb, gkb, rope_p, keep_all=False, name="proj_prompt",
        cast=(w_out[0], w_gate_up[0], w_down[0]))
    o_p = _attend_prompt(qa, ka, va, qb, kb, vb, u, sink)
    zero_state = jnp.zeros((1, CONV_W - 1, D_FF), _F32)
    y_p, cn_p = _out_ffn(o_p, xp, w_out_b, gn_f, w_gu_b, cw, cb, w_down_b, zero_state,
                         carry_rows=True, name="ffn_prompt")

    xs = x_sample.reshape(dec_b * dec_s, D_MODEL)
    qa, ka, va, qb, kb, vb, kaf_s, vaf_s, kbf_s, vbf_s = _project(
        xs, gn_a, w_in_b, gqa, gka, gqb, gkb, rope_s, keep_all=True, name="proj_sample")
    o_s = _attend_sample(qa, ka, va, qb, kb, vb, _feature_major(cache_a_k[0]), _feature_major(cache_a_v[0]),
                         _feature_major(cache_b_k[0]), _feature_major(cache_b_v[0]), u, sink)
    y_s, cn_s = _out_ffn(o_s, xs, w_out_b, gn_f, w_gu_b, cw, cb, w_down_b, state_conv[0],
                         carry_rows=False, name="ffn_sample")

    return (
        y_p.reshape(1, seq, D_MODEL),
        y_s.reshape(dec_b, dec_s, D_MODEL),
        _token_major(kaf, A_HEADS),
        _token_major(vaf, A_HEADS),
        _token_major(kbf, B_KV_HEADS),
        _token_major(vbf, B_KV_HEADS),
        cn_p.reshape(1, 1, CONV_W - 1, D_FF),
        kaf_s.reshape(1, dec_b, dec_s, A_HEADS, HEAD_DIM),
        vaf_s.reshape(1, dec_b, dec_s, A_HEADS, HEAD_DIM),
        kbf_s.reshape(1, dec_b, dec_s, B_KV_HEADS, HEAD_DIM),
        vbf_s.reshape(1, dec_b, dec_s, B_KV_HEADS, HEAD_DIM),
        cn_s.reshape(1, dec_b, CONV_W - 1, D_FF),
    )
```

```python
import functools

import jax
import jax.numpy as jnp
from jax import lax
from jax.experimental import pallas as pl
from jax.experimental.pallas import tpu as pltpu

D_MODEL = 1024
CHUNK = 64
HEAD_DIM = 64
EPS = 1e-6
SCALE = HEAD_DIM ** -0.5
LOG2E = 1.4426950408889634
Q_SCALE = SCALE * LOG2E
NEG = -1e30
PAST_LEN = 1024
A_HEADS = 8
A_LEFT_CHUNKS = 8
A_CTX = A_LEFT_CHUNKS * CHUNK
A_WIN = A_CTX + CHUNK
MAX_REL = 128
A_WIDTH = A_HEADS * HEAD_DIM
B_HEADS = 8
B_KV_HEADS = 2
B_GROUP = B_HEADS // B_KV_HEADS
B_CTX = 128
B_WIN = B_CTX + CHUNK
B_WIDTH = B_HEADS * HEAD_DIM
B_KV_WIDTH = B_KV_HEADS * HEAD_DIM
ROPE_THETA = 10000.0
IN_COLS = 3 * A_WIDTH + B_WIDTH + 2 * B_KV_WIDTH
D_FF = 2816
CONV_W = 3

LANES = 128
TOKEN_TILE = 512
PROMPT_PROJ_TILE = 1024
ATTN_TILE = 2048
SAMPLE_STREAMS = 4
FF_CHUNK = 256
BIAS_COLS = MAX_REL + CHUNK
BIAS_VAR = 256
VMEM_LIMIT = 56 * 1024 * 1024

_BF16 = jnp.bfloat16
_F32 = jnp.float32


def _const_spec(shape):
    nd = len(shape)
    return pl.BlockSpec(shape, lambda i: (0,) * nd, pipeline_mode=pl.Buffered(1))


def _head_sumsq(t, bd):
    sq = (t * t).astype(_BF16)
    w = t.shape[1]
    if w <= 2 * LANES:
        return jnp.dot(sq, bd[:w, :w], preferred_element_type=_F32)
    parts = [jnp.dot(sq[:, c:c + 2 * LANES], bd, preferred_element_type=_F32) for c in range(0, w, 2 * LANES)]
    return jnp.concatenate(parts, axis=1)


def _head_norm(t, bd, gain):
    ss = _head_sumsq(t, bd)
    return t * lax.rsqrt(ss * (1.0 / HEAD_DIM) + EPS) * gain


def _rope(t, cos, sin_signed):
    outs = []
    for c in range(0, t.shape[1], LANES):
        tc = t[:, c:c + LANES]
        lane = lax.broadcasted_iota(jnp.int32, tc.shape, 1)
        first_half = (lane & (HEAD_DIM - 1)) < HEAD_DIM // 2
        rot = jnp.where(first_half, pltpu.roll(tc, LANES - HEAD_DIM // 2, 1), pltpu.roll(tc, HEAD_DIM // 2, 1))
        outs.append(tc * cos + rot * sin_signed)
    return outs[0] if len(outs) == 1 else jnp.concatenate(outs, axis=1)


def _rep_pair(t):
    rolled = pltpu.roll(t, HEAD_DIM, 1)
    lo = lax.broadcasted_iota(jnp.int32, t.shape, 1) < HEAD_DIM
    return jnp.concatenate([jnp.where(lo, t, rolled), jnp.where(lo, rolled, t)], axis=1)


def _store_heads(ref, t):
    n, w = t.shape
    heads = w // HEAD_DIM
    for hd in range(heads):
        ref[pl.ds(hd, n, stride=heads), :] = t[:, hd * HEAD_DIM:(hd + 1) * HEAD_DIM]


def _proj_kernel(x_ref, gn_ref, w_ref, gqa_ref, gka_ref, gqb_ref, gkb_ref, ct_ref, st_ref, cr_ref, sr_ref, *rest,
                 last_only, n_cast):
    cast_in, rest = rest[:n_cast], rest[n_cast:]
    qa_ref, ka_ref, va_ref, qb_ref, kb_ref, vb_ref, kaf_ref, vaf_ref, kbf_ref, vbf_ref = rest[:10]
    for src, dst in zip(cast_in, rest[10:]):
        dst[...] = src[...].astype(_BF16)
    tm = x_ref.shape[0]
    x = x_ref[...]
    ms = jnp.mean(x * x, axis=-1, keepdims=True)
    h = (x * lax.rsqrt(ms + EPS) * gn_ref[...]).astype(_BF16)

    r = lax.broadcasted_iota(jnp.int32, (2 * LANES, 2 * LANES), 0) // HEAD_DIM
    c = lax.broadcasted_iota(jnp.int32, (2 * LANES, 2 * LANES), 1) // HEAD_DIM
    bd = jnp.where(r == c, 1.0, 0.0).astype(_BF16)
    ct, st, cr, sr = ct_ref[0], st_ref[0], cr_ref[...], sr_ref[...]
    cos = ct * cr - st * sr
    sin = st * cr + ct * sr

    def proj(lo, width):
        return jnp.dot(h, w_ref[:, lo:lo + width], preferred_element_type=_F32)

    o_ka, o_va, o_qb, o_kvb = A_WIDTH, 2 * A_WIDTH, 3 * A_WIDTH, 3 * A_WIDTH + B_WIDTH
    p_qa = proj(0, A_WIDTH)
    p_ka = proj(o_ka, A_WIDTH)
    qa_ref[...] = (_head_norm(p_qa, bd, gqa_ref[...]) * Q_SCALE).astype(_BF16)
    va = proj(o_va, A_WIDTH)
    ka = _head_norm(p_ka, bd, gka_ref[...])
    ka_ref[...] = ka.astype(_BF16)
    p_qb = proj(o_qb, B_WIDTH)
    va_ref[...] = va.astype(_BF16)
    p_kvb = proj(o_kvb, 2 * B_KV_WIDTH)
    qb_ref[...] = (_rope(_head_norm(p_qb, bd, gqb_ref[...]), cos, sin) * Q_SCALE).astype(_BF16)
    kb = _rope(_head_norm(p_kvb[:, :B_KV_WIDTH], bd, gkb_ref[...]), cos, sin)
    vb = p_kvb[:, B_KV_WIDTH:]
    kb_ref[...] = _rep_pair(kb).astype(_BF16)
    vb_ref[...] = _rep_pair(vb).astype(_BF16)

    if last_only:
        @pl.when(pl.program_id(0) == pl.num_programs(0) - 1)
        def _():
            kaf_ref[...] = ka[tm - A_CTX:, :].T
            vaf_ref[...] = va[tm - A_CTX:, :].T
            kbf_ref[...] = kb[tm - B_CTX:, :].T
            vbf_ref[...] = vb[tm - B_CTX:, :].T
    else:
        _store_heads(kaf_ref, ka)
        _store_heads(vaf_ref, va)
        _store_heads(kbf_ref, kb)
        _store_heads(vbf_ref, vb)


def _cast_blocks(w, n_steps):
    rows = w.shape[0]
    n_blocks = max(d for d in range(1, n_steps + 1) if rows % d == 0 and (rows // d) % 16 == 0)
    return pl.BlockSpec((rows // n_blocks, w.shape[1]), lambda i: (jnp.minimum(i, n_blocks - 1), 0))


def _project(x, gn, w_in, gqa, gka, gqb, gkb, rope, *, keep_all, name, cast=()):
    t = x.shape[0]
    tm = rope[2].shape[0]
    n = t // tm
    cast_specs = [_cast_blocks(w, n) for w in cast]
    tile = lambda w: pl.BlockSpec((tm, w), lambda i: (i, 0))
    if keep_all:
        f32_shapes = [(t * A_HEADS, HEAD_DIM)] * 2 + [(t * B_KV_HEADS, HEAD_DIM)] * 2
        f32_specs = ([pl.BlockSpec((tm * A_HEADS, HEAD_DIM), lambda i: (i, 0))] * 2
                     + [pl.BlockSpec((tm * B_KV_HEADS, HEAD_DIM), lambda i: (i, 0))] * 2)
    else:
        f32_shapes = [(A_WIDTH, A_CTX), (A_WIDTH, A_CTX), (B_KV_WIDTH, B_CTX), (B_KV_WIDTH, B_CTX)]
        f32_specs = [pl.BlockSpec(sh, lambda i: (0, 0)) for sh in f32_shapes]
    ct, st, cr, sr = rope
    if ct.shape[0] == n:
        tile_trig = pl.BlockSpec((1, 1, LANES), lambda i: (i, 0, 0))
    else:
        tile_trig = _const_spec((1, 1, LANES))
    out_shape = [
        jax.ShapeDtypeStruct((t, A_WIDTH), _BF16), jax.ShapeDtypeStruct((t, A_WIDTH), _BF16),
        jax.ShapeDtypeStruct((t, A_WIDTH), _BF16), jax.ShapeDtypeStruct((t, B_WIDTH), _BF16),
        jax.ShapeDtypeStruct((t, 2 * B_KV_WIDTH), _BF16), jax.ShapeDtypeStruct((t, 2 * B_KV_WIDTH), _BF16),
    ] + [jax.ShapeDtypeStruct(sh, _F32) for sh in f32_shapes] + [jax.ShapeDtypeStruct(w.shape, _BF16) for w in cast]
    out_specs = [
        tile(A_WIDTH), tile(A_WIDTH), tile(A_WIDTH), tile(B_WIDTH), tile(2 * B_KV_WIDTH), tile(2 * B_KV_WIDTH),
    ] + f32_specs + cast_specs
    in_specs = [
        tile(D_MODEL), _const_spec((1, D_MODEL)), _const_spec((D_MODEL, IN_COLS)),
        _const_spec((1, A_WIDTH)), _const_spec((1, A_WIDTH)), _const_spec((1, B_WIDTH)),
        _const_spec((1, B_KV_WIDTH)), tile_trig, tile_trig, _const_spec((tm, LANES)), _const_spec((tm, LANES)),
    ] + cast_specs
    return pl.pallas_call(
        functools.partial(_proj_kernel, last_only=not keep_all, n_cast=len(cast)),
        grid=(n,), in_specs=in_specs, out_specs=out_specs, out_shape=out_shape, name=name,
        compiler_params=pltpu.CompilerParams(dimension_semantics=("arbitrary",), vmem_limit_bytes=VMEM_LIMIT),
    )(x, gn, w_in, gqa, gka, gqb, gkb, ct, st, cr, sr, *cast)


def _build_bias(u_ref, bias_ref):
    qrow = lax.broadcasted_iota(jnp.int32, (CHUNK, BIAS_VAR), 0)
    for h in range(A_HEADS):
        prof = u_ref[h:h + 1, :]
        y = jnp.broadcast_to(prof - prof[:, 0:1], (CHUNK, BIAS_VAR))
        y = pltpu.roll(y, BIAS_VAR - (CHUNK - 1), 1)
        for b in range(6):
            y = jnp.where(((qrow >> b) & 1) == 1, pltpu.roll(y, 1 << b, 1), y)
        bias_ref[h // 4, (h % 4) * CHUNK:(h % 4 + 1) * CHUNK, :] = y[:, :BIAS_COLS] * LOG2E


_NT = (((1,), (1,)), ((), ()))
_NN = (((1,), (0,)), ((), ()))


def _scores(qa, qb, keys_a, keys_b, bias_ref, lim_a, lim_b):
    out = []
    head_of_lane = lax.broadcasted_iota(jnp.int32, (CHUNK, 2 * LANES), 1) // HEAD_DIM
    for g in range(2):
        qg = qa[:, 2 * LANES * g:2 * LANES * (g + 1)]
        qbd = jnp.concatenate([jnp.where(head_of_lane == r, qg, jnp.zeros_like(qg)) for r in range(4)], axis=0)
        pieces, off = [], 0
        for keys, on_rows in keys_a(g):
            s = lax.dot_general(qbd, keys, _NT if on_rows else _NN, preferred_element_type=_F32)
            n = s.shape[1]
            if off + n > A_WIN - BIAS_COLS:
                lo_col = max(A_WIN - BIAS_COLS - off, 0)
                biased = s[:, lo_col:] + bias_ref[g, :, off + lo_col - (A_WIN - BIAS_COLS):off + n - (A_WIN - BIAS_COLS)]
                s = biased if lo_col == 0 else jnp.concatenate([s[:, :lo_col], biased], axis=1)
            if lim_a:
                s = jnp.where(lax.broadcasted_iota(jnp.int32, s.shape, 1) >= lim_a, s, NEG)
            pieces.append(s)
            off += n
        out.append(pieces)
    lo = lax.broadcasted_iota(jnp.int32, (CHUNK, LANES), 1) < HEAD_DIM
    for g in range(B_KV_HEADS):
        rows = []
        for r in range(B_GROUP):
            hd = B_GROUP * g + r
            qp = qb[:, LANES * (hd // 2):LANES * (hd // 2 + 1)]
            keep = lo if hd % 2 == 0 else jnp.logical_not(lo)
            rows.append(jnp.where(keep, qp, jnp.zeros_like(qp)))
        qs = jnp.concatenate(rows, axis=0)
        pieces = []
        for keys, on_rows in keys_b(g):
            s = lax.dot_general(qs, keys, _NT if on_rows else _NN, preferred_element_type=_F32)
            if lim_b:
                s = jnp.where(lax.broadcasted_iota(jnp.int32, s.shape, 1) >= lim_b, s, NEG)
            pieces.append(s)
        out.append(pieces)
    return out


def _row_reduce(pieces, combine, reduce, fill):
    acc = None
    for s in pieces:
        for c in range(0, s.shape[1], LANES):
            tile = s[:, c:c + LANES]
            if tile.shape[1] < LANES:
                pad = jnp.full((tile.shape[0], LANES - tile.shape[1]), fill, tile.dtype)
                tile = jnp.concatenate([tile, pad], axis=1)
            acc = tile if acc is None else combine(acc, tile)
    return reduce(acc, axis=1, keepdims=True)


def _softmax_values(pieces, vals, sink):
    m = _row_reduce(pieces, jnp.maximum, jnp.max, -jnp.inf)
    if sink is not None:
        m = jnp.maximum(m, sink)
    es = [jnp.exp2(s - m) for s in pieces]
    l = _row_reduce(es, jnp.add, jnp.sum, 0.0)
    if sink is not None:
        l = l + jnp.exp2(sink - m)
    acc = None
    for e, (v, on_rows) in zip(es, vals):
        part = lax.dot_general(e.astype(_BF16), v, _NN if on_rows else _NT, preferred_element_type=_F32)
        acc = part if acc is None else acc + part
    return acc / l


def _softmax_pv(scores, vals_a, vals_b, sink_ref):
    outs = []
    head_of_lane = lax.broadcasted_iota(jnp.int32, (CHUNK, 2 * LANES), 1) // HEAD_DIM
    for g in range(2):
        of = _softmax_values(scores[g], vals_a(g), None)
        og = of[3 * CHUNK:4 * CHUNK]
        for r in (2, 1, 0):
            og = jnp.where(head_of_lane == r, of[r * CHUNK:(r + 1) * CHUNK], og)
        outs.append(og)
    lo = lax.broadcasted_iota(jnp.int32, (CHUNK, LANES), 1) < HEAD_DIM
    for g in range(B_KV_HEADS):
        of = _softmax_values(scores[2 + g], vals_b(g), sink_ref[g] * LOG2E)
        outs.append(jnp.where(lo, of[0:CHUNK], of[CHUNK:2 * CHUNK]))
        outs.append(jnp.where(lo, of[2 * CHUNK:3 * CHUNK], of[3 * CHUNK:4 * CHUNK]))
    return jnp.concatenate(outs, axis=1)


def _attn_prompt_kernel(qa_ref, ka_ref, va_ref, qb_ref, kb_ref, vb_ref, u_ref, sink_ref, o_ref,
                        kwa, vwa, kwb, vwb, bias):
    i = pl.program_id(0)
    tm = qa_ref.shape[0]
    n_chunks = tm // CHUNK

    @pl.when(i == 0)
    def _():
        kwa[0:A_CTX, :] = jnp.zeros((A_CTX, A_WIDTH), _BF16)
        vwa[0:A_CTX, :] = jnp.zeros((A_CTX, A_WIDTH), _BF16)
        kwb[0:B_CTX, :] = jnp.zeros((B_CTX, 2 * B_KV_WIDTH), _BF16)
        vwb[0:B_CTX, :] = jnp.zeros((B_CTX, 2 * B_KV_WIDTH), _BF16)
        _build_bias(u_ref, bias)

    @pl.when(i > 0)
    def _():
        kwa[0:A_CTX, :] = kwa[tm:tm + A_CTX, :]
        vwa[0:A_CTX, :] = vwa[tm:tm + A_CTX, :]
        kwb[0:B_CTX, :] = kwb[tm:tm + B_CTX, :]
        vwb[0:B_CTX, :] = vwb[tm:tm + B_CTX, :]

    kwa[A_CTX:A_CTX + tm, :] = ka_ref[...]
    vwa[A_CTX:A_CTX + tm, :] = va_ref[...]
    kwb[B_CTX:B_CTX + tm, :] = kb_ref[...]
    vwb[B_CTX:B_CTX + tm, :] = vb_ref[...]

    def window(ref, r0, rows, width):
        return lambda g: [(ref[r0:r0 + rows, width * g:width * (g + 1)], True)]

    def run(masked):
        def scores(c):
            r0 = c * CHUNK
            lim_a = max(A_LEFT_CHUNKS - c, 0) * CHUNK if masked else 0
            lim_b = max(B_CTX // CHUNK - c, 0) * CHUNK if masked else 0
            return _scores(qa_ref[r0:r0 + CHUNK, :], qb_ref[r0:r0 + CHUNK, :],
                           window(kwa, r0, A_WIN, 2 * LANES), window(kwb, r0, B_WIN, LANES), bias, lim_a, lim_b)

        nxt = scores(0)
        for c in range(n_chunks):
            cur = nxt
            if c + 1 < n_chunks:
                nxt = scores(c + 1)
            r0 = c * CHUNK
            o = _softmax_pv(cur, window(vwa, r0, A_WIN, 2 * LANES), window(vwb, r0, B_WIN, LANES), sink_ref)
            o_ref[r0:r0 + CHUNK, :] = o.astype(_BF16)

    @pl.when(i == 0)
    def _():
        run(True)

    @pl.when(i > 0)
    def _():
        run(False)


def _attn_sample_kernel(qa_ref, ka_ref, va_ref, qb_ref, kb_ref, vb_ref, cak_ref, cav_ref, cbk_ref, cbv_ref,
                        u_ref, sink_ref, o_ref, bias):
    n_streams = cak_ref.shape[0]

    @pl.when(pl.program_id(0) == 0)
    def _():
        _build_bias(u_ref, bias)

    def pieces_a(cache_ref, new_ref, b):
        rows = slice(b * CHUNK, (b + 1) * CHUNK)

        def get(g):
            cols = slice(2 * LANES * g, 2 * LANES * (g + 1))
            return [(cache_ref[b, cols, :].astype(_BF16), False), (new_ref[rows, cols], True)]
        return get

    def pieces_b(cache_ref, new_ref, b):
        rows = slice(b * CHUNK, (b + 1) * CHUNK)

        def get(g):
            past = cache_ref[b, HEAD_DIM * g:HEAD_DIM * (g + 1), :].astype(_BF16)
            return [(jnp.concatenate([past, past], axis=0), False), (new_ref[rows, LANES * g:LANES * (g + 1)], True)]
        return get

    def scores(b):
        rows = slice(b * CHUNK, (b + 1) * CHUNK)
        return _scores(qa_ref[rows, :], qb_ref[rows, :], pieces_a(cak_ref, ka_ref, b), pieces_b(cbk_ref, kb_ref, b),
                       bias, 0, 0)

    nxt = scores(0)
    for b in range(n_streams):
        cur = nxt
        if b + 1 < n_streams:
            nxt = scores(b + 1)
        o = _softmax_pv(cur, pieces_a(cav_ref, va_ref, b), pieces_b(cbv_ref, vb_ref, b), sink_ref)
        o_ref[b * CHUNK:(b + 1) * CHUNK, :] = o.astype(_BF16)


_BIAS_SCRATCH = pltpu.VMEM((2, 4 * CHUNK, BIAS_COLS), _F32)


def _attend_prompt(qa, ka, va, qb, kb, vb, u, sink):
    t = qa.shape[0]
    tm = ATTN_TILE
    tile = lambda w: pl.BlockSpec((tm, w), lambda i: (i, 0))
    return pl.pallas_call(
        _attn_prompt_kernel,
        grid=(t // tm,),
        in_specs=[tile(A_WIDTH), tile(A_WIDTH), tile(A_WIDTH), tile(B_WIDTH), tile(2 * B_KV_WIDTH),
                  tile(2 * B_KV_WIDTH), _const_spec(u.shape), _const_spec(sink.shape)],
        out_specs=tile(A_WIDTH + B_WIDTH),
        out_shape=jax.ShapeDtypeStruct((t, A_WIDTH + B_WIDTH), _BF16),
        scratch_shapes=[pltpu.VMEM((A_CTX + tm, A_WIDTH), _BF16), pltpu.VMEM((A_CTX + tm, A_WIDTH), _BF16),
                        pltpu.VMEM((B_CTX + tm, 2 * B_KV_WIDTH), _BF16),
                        pltpu.VMEM((B_CTX + tm, 2 * B_KV_WIDTH), _BF16), _BIAS_SCRATCH],
        name="attn_prompt",
        compiler_params=pltpu.CompilerParams(dimension_semantics=("arbitrary",), vmem_limit_bytes=VMEM_LIMIT),
    )(qa, ka, va, qb, kb, vb, u, sink)


def _attend_sample(qa, ka, va, qb, kb, vb, cak, cav, cbk, cbv, u, sink):
    t = qa.shape[0]
    sb = SAMPLE_STREAMS
    tile = lambda w: pl.BlockSpec((sb * CHUNK, w), lambda i: (i, 0))
    cache = lambda arr: pl.BlockSpec((sb,) + arr.shape[1:], lambda i: (i, 0, 0))
    return pl.pallas_call(
        _attn_sample_kernel,
        grid=(t // (sb * CHUNK),),
        in_specs=[tile(A_WIDTH), tile(A_WIDTH), tile(A_WIDTH), tile(B_WIDTH), tile(2 * B_KV_WIDTH),
                  tile(2 * B_KV_WIDTH), cache(cak), cache(cav), cache(cbk), cache(cbv),
                  _const_spec(u.shape), _const_spec(sink.shape)],
        out_specs=tile(A_WIDTH + B_WIDTH),
        out_shape=jax.ShapeDtypeStruct((t, A_WIDTH + B_WIDTH), _BF16),
        scratch_shapes=[_BIAS_SCRATCH],
        name="attn_sample",
        compiler_params=pltpu.CompilerParams(dimension_semantics=("arbitrary",), vmem_limit_bytes=VMEM_LIMIT),
    )(qa, ka, va, qb, kb, vb, cak, cav, cbk, cbv, u, sink)


def _ffn_kernel(o_ref, x_ref, wo_ref, gn_ref, wgu_ref, cw_ref, cb_ref, wd_ref, st_ref,
                y_ref, cn_ref, carry, act_buf, *, seg_len, carry_rows):
    i = pl.program_id(0)
    tm = x_ref.shape[0]
    n_seg = tm // seg_len

    if carry_rows:
        @pl.when(i == 0)
        def _():
            carry[...] = st_ref[0]

    y1 = x_ref[...] + jnp.dot(o_ref[...], wo_ref[...], preferred_element_type=_F32)
    ms = jnp.mean(y1 * y1, axis=-1, keepdims=True)
    h = (y1 * lax.rsqrt(ms + EPS) * gn_ref[...]).astype(_BF16)

    row8 = lax.broadcasted_iota(jnp.int32, (8, FF_CHUNK), 0)

    def gate_up(c0):
        g = jnp.dot(h, wgu_ref[:, c0:c0 + FF_CHUNK], preferred_element_type=_F32)
        up = jnp.dot(h, wgu_ref[:, D_FF + c0:D_FF + c0 + FF_CHUNK], preferred_element_type=_F32)
        return g, up

    nxt = gate_up(0)
    for c0 in range(0, D_FF, FF_CHUNK):
        cols = slice(c0, c0 + FF_CHUNK)
        g, up = nxt
        if c0 + FF_CHUNK < D_FF:
            nxt = gate_up(c0 + FF_CHUNK)
        g1 = pltpu.roll(g, 1, 0)
        g2 = pltpu.roll(g, 2, 0)
        p1, p2 = [], []
        for s in range(n_seg):
            a = s * seg_len
            prev = carry[:, cols] if carry_rows else st_ref[s, :, cols]
            prev0 = jnp.broadcast_to(prev[0:1, :], (8, FF_CHUNK))
            prev1 = jnp.broadcast_to(prev[1:2, :], (8, FF_CHUNK))
            p1.append(jnp.where(row8 == 0, prev1, g1[a:a + 8]))
            p1.append(g1[a + 8:a + seg_len])
            p2.append(jnp.where(row8 == 0, prev0, jnp.where(row8 == 1, prev1, g2[a:a + 8])))
            p2.append(g2[a + 8:a + seg_len])
            tail = g[a + seg_len - 8:a + seg_len][8 - (CONV_W - 1):, :]
            if carry_rows:
                carry[:, cols] = tail
            else:
                cn_ref[s, :, cols] = tail
        g1 = jnp.concatenate(p1, axis=0)
        g2 = jnp.concatenate(p2, axis=0)
        conv = cb_ref[:, cols] + cw_ref[0:1, cols] * g2 + cw_ref[1:2, cols] * g1 + cw_ref[2:3, cols] * g
        act_buf[:, cols] = (conv * jax.nn.sigmoid(conv) * up).astype(_BF16)
    y_ref[...] = y1 + jnp.dot(act_buf[...], wd_ref[...], preferred_element_type=_F32)
    if carry_rows:
        cn_ref[0] = carry[...]


def _out_ffn(o, x, w_out, gn, w_gu, conv_w, conv_b, w_down, state, *, carry_rows, name):
    t = x.shape[0]
    tm = TOKEN_TILE
    n = t // tm
    seg_len = tm if carry_rows else CHUNK
    n_seg = tm // seg_len
    tile = lambda w: pl.BlockSpec((tm, w), lambda i: (i, 0))
    if carry_rows:
        st_spec = _const_spec((1, CONV_W - 1, D_FF))
        cn_spec = pl.BlockSpec((1, CONV_W - 1, D_FF), lambda i: (0, 0, 0))
        cn_shape = (1, CONV_W - 1, D_FF)
    else:
        st_spec = pl.BlockSpec((n_seg, CONV_W - 1, D_FF), lambda i: (i, 0, 0))
        cn_spec = pl.BlockSpec((n_seg, CONV_W - 1, D_FF), lambda i: (i, 0, 0))
        cn_shape = (t // seg_len, CONV_W - 1, D_FF)
    return pl.pallas_call(
        functools.partial(_ffn_kernel, seg_len=seg_len, carry_rows=carry_rows),
        grid=(n,),
        in_specs=[tile(A_WIDTH + B_WIDTH), tile(D_MODEL), _const_spec((D_MODEL, D_MODEL)),
                  _const_spec((1, D_MODEL)), _const_spec((D_MODEL, 2 * D_FF)), _const_spec((CONV_W, D_FF)),
                  _const_spec((1, D_FF)), _const_spec((D_FF, D_MODEL)), st_spec],
        out_specs=[tile(D_MODEL), cn_spec],
        out_shape=[jax.ShapeDtypeStruct((t, D_MODEL), _F32), jax.ShapeDtypeStruct(cn_shape, _F32)],
        scratch_shapes=[pltpu.VMEM((CONV_W - 1, D_FF), _F32), pltpu.VMEM((tm, D_FF), _BF16)],
        name=name,
        compiler_params=pltpu.CompilerParams(dimension_semantics=("arbitrary",), vmem_limit_bytes=VMEM_LIMIT),
    )(o, x, w_out, gn, w_gu, conv_w, conv_b, w_down, state)


def _rope_tables(tile_pos, row_pos):
    half = HEAD_DIM // 2
    lane = jnp.arange(LANES)
    inv = jnp.power(ROPE_THETA, -(lane % half).astype(_F32) / half)
    sign = jnp.where(lane % HEAD_DIM < half, -1.0, 1.0).astype(_F32)
    a = tile_pos.astype(_F32)[:, None] * inv[None, :]
    b = row_pos.astype(_F32)[:, None] * inv[None, :]
    return (jnp.cos(a)[:, None, :], (jnp.sin(a) * sign)[:, None, :], jnp.cos(b), jnp.sin(b) * sign)


def _feature_major(cache):
    n, t, heads, d = cache.shape
    return jnp.transpose(cache, (0, 2, 3, 1)).reshape(n, heads * d, t)


def _token_major(rows, heads):
    t = rows.shape[1]
    return jnp.transpose(rows.reshape(heads, HEAD_DIM, t), (2, 0, 1))[None, None]


def _bias_profile(table):
    n_clip = BIAS_COLS - 1 - MAX_REL
    head = jnp.broadcast_to(table[:, 2 * MAX_REL:], (A_HEADS, n_clip))
    return jnp.concatenate([head, table[:, ::-1][:, :BIAS_VAR - n_clip]], axis=1)


def kernel(x_prompt, x_sample, cache_a_k, cache_a_v, cache_b_k, cache_b_v, state_conv, norm_attn, w_in, q_norm_a,
           k_norm_a, rel_bias_a, q_norm_b, k_norm_b, sinks_b, w_out, norm_ffn, w_gate_up, conv_w, conv_b, w_down):
    assert norm_attn.shape[0] == 1
    bsz, seq, _ = x_prompt.shape
    dec_b, dec_s, _ = x_sample.shape
    assert bsz == 1 and dec_s == CHUNK and (dec_b * dec_s) % TOKEN_TILE == 0
    assert seq % TOKEN_TILE == 0 and seq % PROMPT_PROJ_TILE == 0 and seq % ATTN_TILE == 0
    assert cache_a_k.shape[2] == A_CTX and cache_b_k.shape[2] == B_CTX

    w_in_b = w_in[0].astype(_BF16)
    gn_a = norm_attn[0][None, :]
    gn_f = norm_ffn[0][None, :]
    gqa = jnp.tile(q_norm_a[0], A_HEADS)[None, :]
    gka = jnp.tile(k_norm_a[0], A_HEADS)[None, :]
    gqb = jnp.tile(q_norm_b[0], B_HEADS)[None, :]
    gkb = jnp.tile(k_norm_b[0], B_KV_HEADS)[None, :]
    u = _bias_profile(rel_bias_a[0])
    sink = jnp.repeat(sinks_b[0], CHUNK).reshape(B_KV_HEADS, B_GROUP * CHUNK, 1)
    cw = conv_w[0]
    cb = conv_b[0][None, :]

    rope_p = _rope_tables(jnp.arange(0, seq, PROMPT_PROJ_TILE), jnp.arange(PROMPT_PROJ_TILE))
    rope_s = _rope_tables(jnp.full((1,), PAST_LEN), jnp.arange(TOKEN_TILE) % dec_s)

    xp = x_prompt.reshape(seq, D_MODEL)
    qa, ka, va, qb, kb, vb, kaf, vaf, kbf, vbf, w_out_b, w_gu_b, w_down_b = _project(
        xp, gn_a, w_in_b, gqa, gka, gqb, gkb, rope_p, keep_all=False, name="proj_prompt",
        cast=(w_out[0], w_gate_up[0], w_down[0]))
    o_p = _attend_prompt(qa, ka, va, qb, kb, vb, u, sink)
    zero_state = jnp.zeros((1, CONV_W - 1, D_FF), _F32)
    y_p, cn_p = _out_ffn(o_p, xp, w_out_b, gn_f, w_gu_b, cw, cb, w_down_b, zero_state,
                         carry_rows=True, name="ffn_prompt")

    xs = x_sample.reshape(dec_b * dec_s, D_MODEL)
    qa, ka, va, qb, kb, vb, kaf_s, vaf_s, kbf_s, vbf_s = _project(
        xs, gn_a, w_in_b, gqa, gka, gqb, gkb, rope_s, keep_all=True, name="proj_sample")
    o_s = _attend_sample(qa, ka, va, qb, kb, vb, _feature_major(cache_a_k[0]), _feature_major(cache_a_v[0]),
                         _feature_major(cache_b_k[0]), _feature_major(cache_b_v[0]), u, sink)
    y_s, cn_s = _out_ffn(o_s, xs, w_out_b, gn_f, w_gu_b, cw, cb, w_down_b, state_conv[0],
                         carry_rows=False, name="ffn_sample")

    return (
        y_p.reshape(1, seq, D_MODEL),
        y_s.reshape(dec_b, dec_s, D_MODEL),
        _token_major(kaf, A_HEADS),
        _token_major(vaf, A_HEADS),
        _token_major(kbf, B_KV_HEADS),
        _token_major(vbf, B_KV_HEADS),
        cn_p.reshape(1, 1, CONV_W - 1, D_FF),
        kaf_s.reshape(1, dec_b, dec_s, A_HEADS, HEAD_DIM),
        vaf_s.reshape(1, dec_b, dec_s, A_HEADS, HEAD_DIM),
        kbf_s.reshape(1, dec_b, dec_s, B_KV_HEADS, HEAD_DIM),
        vbf_s.reshape(1, dec_b, dec_s, B_KV_HEADS, HEAD_DIM),
        cn_s.reshape(1, dec_b, CONV_W - 1, D_FF),
    )
```

```python
import functools

import jax
import jax.numpy as jnp
from jax import lax
from jax.experimental import pallas as pl
from jax.experimental.pallas import tpu as pltpu

D_MODEL = 1024
CHUNK = 64
HEAD_DIM = 64
EPS = 1e-6
SCALE = HEAD_DIM ** -0.5
LOG2E = 1.4426950408889634
Q_SCALE = SCALE * LOG2E
NEG = -1e30
PAST_LEN = 1024
A_HEADS = 8
A_LEFT_CHUNKS = 8
A_CTX = A_LEFT_CHUNKS * CHUNK
A_WIN = A_CTX + CHUNK
MAX_REL = 128
A_WIDTH = A_HEADS * HEAD_DIM
B_HEADS = 8
B_KV_HEADS = 2
B_GROUP = B_HEADS // B_KV_HEADS
B_CTX = 128
B_WIN = B_CTX + CHUNK
B_WIDTH = B_HEADS * HEAD_DIM
B_KV_WIDTH = B_KV_HEADS * HEAD_DIM
ROPE_THETA = 10000.0
IN_COLS = 3 * A_WIDTH + B_WIDTH + 2 * B_KV_WIDTH
D_FF = 2816
CONV_W = 3

LANES = 128
TOKEN_TILE = 512
PROMPT_PROJ_TILE = 1024
ATTN_TILE = 1024
SAMPLE_STREAMS = 4
FF_CHUNK = 256
BIAS_COLS = MAX_REL + CHUNK
BIAS_VAR = 256
VMEM_LIMIT = 56 * 1024 * 1024

_BF16 = jnp.bfloat16
_F32 = jnp.float32


def _const_spec(shape):
    nd = len(shape)
    return pl.BlockSpec(shape, lambda i: (0,) * nd, pipeline_mode=pl.Buffered(1))


def _head_sumsq(t, bd):
    sq = (t * t).astype(_BF16)
    w = t.shape[1]
    if w <= 2 * LANES:
        return jnp.dot(sq, bd[:w, :w], preferred_element_type=_F32)
    parts = [jnp.dot(sq[:, c:c + 2 * LANES], bd, preferred_element_type=_F32) for c in range(0, w, 2 * LANES)]
    return jnp.concatenate(parts, axis=1)


def _head_norm(t, bd, gain):
    ss = _head_sumsq(t, bd)
    return t * lax.rsqrt(ss * (1.0 / HEAD_DIM) + EPS) * gain


def _rope(t, cos, sin_signed):
    outs = []
    for c in range(0, t.shape[1], LANES):
        tc = t[:, c:c + LANES]
        lane = lax.broadcasted_iota(jnp.int32, tc.shape, 1)
        first_half = (lane & (HEAD_DIM - 1)) < HEAD_DIM // 2
        rot = jnp.where(first_half, pltpu.roll(tc, LANES - HEAD_DIM // 2, 1), pltpu.roll(tc, HEAD_DIM // 2, 1))
        outs.append(tc * cos + rot * sin_signed)
    return outs[0] if len(outs) == 1 else jnp.concatenate(outs, axis=1)


def _rep_pair(t):
    rolled = pltpu.roll(t, HEAD_DIM, 1)
    lo = lax.broadcasted_iota(jnp.int32, t.shape, 1) < HEAD_DIM
    return jnp.concatenate([jnp.where(lo, t, rolled), jnp.where(lo, rolled, t)], axis=1)


def _store_heads(ref, t):
    n, w = t.shape
    heads = w // HEAD_DIM
    for hd in range(heads):
        ref[pl.ds(hd, n, stride=heads), :] = t[:, hd * HEAD_DIM:(hd + 1) * HEAD_DIM]


def _proj_kernel(x_ref, gn_ref, w_ref, gqa_ref, gka_ref, gqb_ref, gkb_ref, ct_ref, st_ref, cr_ref, sr_ref, *rest,
                 last_only, n_cast):
    cast_in, rest = rest[:n_cast], rest[n_cast:]
    qa_ref, ka_ref, va_ref, qb_ref, kb_ref, vb_ref, kaf_ref, vaf_ref, kbf_ref, vbf_ref = rest[:10]
    for src, dst in zip(cast_in, rest[10:]):
        dst[...] = src[...].astype(_BF16)
    tm = x_ref.shape[0]
    x = x_ref[...]
    ms = jnp.mean(x * x, axis=-1, keepdims=True)
    h = (x * lax.rsqrt(ms + EPS) * gn_ref[...]).astype(_BF16)

    r = lax.broadcasted_iota(jnp.int32, (2 * LANES, 2 * LANES), 0) // HEAD_DIM
    c = lax.broadcasted_iota(jnp.int32, (2 * LANES, 2 * LANES), 1) // HEAD_DIM
    bd = jnp.where(r == c, 1.0, 0.0).astype(_BF16)
    ct, st, cr, sr = ct_ref[0], st_ref[0], cr_ref[...], sr_ref[...]
    cos = ct * cr - st * sr
    sin = st * cr + ct * sr

    def proj(lo, width):
        return jnp.dot(h, w_ref[:, lo:lo + width], preferred_element_type=_F32)

    o_ka, o_va, o_qb, o_kvb = A_WIDTH, 2 * A_WIDTH, 3 * A_WIDTH, 3 * A_WIDTH + B_WIDTH
    p_qa = proj(0, A_WIDTH)
    p_ka = proj(o_ka, A_WIDTH)
    qa_ref[...] = (_head_norm(p_qa, bd, gqa_ref[...]) * Q_SCALE).astype(_BF16)
    va = proj(o_va, A_WIDTH)
    ka = _head_norm(p_ka, bd, gka_ref[...])
    ka_ref[...] = ka.astype(_BF16)
    p_qb = proj(o_qb, B_WIDTH)
    va_ref[...] = va.astype(_BF16)
    p_kvb = proj(o_kvb, 2 * B_KV_WIDTH)
    qb_ref[...] = (_rope(_head_norm(p_qb, bd, gqb_ref[...]), cos, sin) * Q_SCALE).astype(_BF16)
    kb = _rope(_head_norm(p_kvb[:, :B_KV_WIDTH], bd, gkb_ref[...]), cos, sin)
    vb = p_kvb[:, B_KV_WIDTH:]
    kb_ref[...] = _rep_pair(kb).astype(_BF16)
    vb_ref[...] = _rep_pair(vb).astype(_BF16)

    if last_only:
        @pl.when(pl.program_id(0) == pl.num_programs(0) - 1)
        def _():
            kaf_ref[...] = ka[tm - A_CTX:, :].T
            vaf_ref[...] = va[tm - A_CTX:, :].T
            kbf_ref[...] = kb[tm - B_CTX:, :].T
            vbf_ref[...] = vb[tm - B_CTX:, :].T
    else:
        _store_heads(kaf_ref, ka)
        _store_heads(vaf_ref, va)
        _store_heads(kbf_ref, kb)
        _store_heads(vbf_ref, vb)


def _cast_blocks(w, n_steps):
    rows = w.shape[0]
    n_blocks = max(d for d in range(1, n_steps + 1) if rows % d == 0 and (rows // d) % 16 == 0)
    return pl.BlockSpec((rows // n_blocks, w.shape[1]), lambda i: (jnp.minimum(i, n_blocks - 1), 0))


def _project(x, gn, w_in, gqa, gka, gqb, gkb, rope, *, keep_all, name, cast=()):
    t = x.shape[0]
    tm = rope[2].shape[0]
    n = t // tm
    cast_specs = [_cast_blocks(w, n) for w in cast]
    tile = lambda w: pl.BlockSpec((tm, w), lambda i: (i, 0))
    if keep_all:
        f32_shapes = [(t * A_HEADS, HEAD_DIM)] * 2 + [(t * B_KV_HEADS, HEAD_DIM)] * 2
        f32_specs = ([pl.BlockSpec((tm * A_HEADS, HEAD_DIM), lambda i: (i, 0))] * 2
                     + [pl.BlockSpec((tm * B_KV_HEADS, HEAD_DIM), lambda i: (i, 0))] * 2)
    else:
        f32_shapes = [(A_WIDTH, A_CTX), (A_WIDTH, A_CTX), (B_KV_WIDTH, B_CTX), (B_KV_WIDTH, B_CTX)]
        f32_specs = [pl.BlockSpec(sh, lambda i: (0, 0)) for sh in f32_shapes]
    ct, st, cr, sr = rope
    if ct.shape[0] == n:
        tile_trig = pl.BlockSpec((1, 1, LANES), lambda i: (i, 0, 0))
    else:
        tile_trig = _const_spec((1, 1, LANES))
    out_shape = [
        jax.ShapeDtypeStruct((t, A_WIDTH), _BF16), jax.ShapeDtypeStruct((t, A_WIDTH), _BF16),
        jax.ShapeDtypeStruct((t, A_WIDTH), _BF16), jax.ShapeDtypeStruct((t, B_WIDTH), _BF16),
        jax.ShapeDtypeStruct((t, 2 * B_KV_WIDTH), _BF16), jax.ShapeDtypeStruct((t, 2 * B_KV_WIDTH), _BF16),
    ] + [jax.ShapeDtypeStruct(sh, _F32) for sh in f32_shapes] + [jax.ShapeDtypeStruct(w.shape, _BF16) for w in cast]
    out_specs = [
        tile(A_WIDTH), tile(A_WIDTH), tile(A_WIDTH), tile(B_WIDTH), tile(2 * B_KV_WIDTH), tile(2 * B_KV_WIDTH),
    ] + f32_specs + cast_specs
    in_specs = [
        tile(D_MODEL), _const_spec((1, D_MODEL)), _const_spec((D_MODEL, IN_COLS)),
        _const_spec((1, A_WIDTH)), _const_spec((1, A_WIDTH)), _const_spec((1, B_WIDTH)),
        _const_spec((1, B_KV_WIDTH)), tile_trig, tile_trig, _const_spec((tm, LANES)), _const_spec((tm, LANES)),
    ] + cast_specs
    return pl.pallas_call(
        functools.partial(_proj_kernel, last_only=not keep_all, n_cast=len(cast)),
        grid=(n,), in_specs=in_specs, out_specs=out_specs, out_shape=out_shape, name=name,
        compiler_params=pltpu.CompilerParams(dimension_semantics=("arbitrary",), vmem_limit_bytes=VMEM_LIMIT),
    )(x, gn, w_in, gqa, gka, gqb, gkb, ct, st, cr, sr, *cast)


def _build_bias(u_ref, bias_ref):
    qrow = lax.broadcasted_iota(jnp.int32, (CHUNK, BIAS_VAR), 0)
    for h in range(A_HEADS):
        prof = u_ref[h:h + 1, :]
        y = jnp.broadcast_to(prof - prof[:, 0:1], (CHUNK, BIAS_VAR))
        y = pltpu.roll(y, BIAS_VAR - (CHUNK - 1), 1)
        for b in range(6):
            y = jnp.where(((qrow >> b) & 1) == 1, pltpu.roll(y, 1 << b, 1), y)
        bias_ref[h // 4, (h % 4) * CHUNK:(h % 4 + 1) * CHUNK, :] = y[:, :BIAS_COLS] * LOG2E


_NT = (((1,), (1,)), ((), ()))
_NN = (((1,), (0,)), ((), ()))


def _scores(qa, qb, keys_a, keys_b, bias_ref, lim_a, lim_b):
    out = []
    head_of_lane = lax.broadcasted_iota(jnp.int32, (CHUNK, 2 * LANES), 1) // HEAD_DIM
    for g in range(2):
        qg = qa[:, 2 * LANES * g:2 * LANES * (g + 1)]
        qbd = jnp.concatenate([jnp.where(head_of_lane == r, qg, jnp.zeros_like(qg)) for r in range(4)], axis=0)
        pieces, off = [], 0
        for keys, on_rows in keys_a(g):
            s = lax.dot_general(qbd, keys, _NT if on_rows else _NN, preferred_element_type=_F32)
            n = s.shape[1]
            if off + n > A_WIN - BIAS_COLS:
                lo_col = max(A_WIN - BIAS_COLS - off, 0)
                biased = s[:, lo_col:] + bias_ref[g, :, off + lo_col - (A_WIN - BIAS_COLS):off + n - (A_WIN - BIAS_COLS)]
                s = biased if lo_col == 0 else jnp.concatenate([s[:, :lo_col], biased], axis=1)
            if lim_a:
                s = jnp.where(lax.broadcasted_iota(jnp.int32, s.shape, 1) >= lim_a, s, NEG)
            pieces.append(s)
            off += n
        out.append(pieces)
    lo = lax.broadcasted_iota(jnp.int32, (CHUNK, LANES), 1) < HEAD_DIM
    for g in range(B_KV_HEADS):
        rows = []
        for r in range(B_GROUP):
            hd = B_GROUP * g + r
            qp = qb[:, LANES * (hd // 2):LANES * (hd // 2 + 1)]
            keep = lo if hd % 2 == 0 else jnp.logical_not(lo)
            rows.append(jnp.where(keep, qp, jnp.zeros_like(qp)))
        qs = jnp.concatenate(rows, axis=0)
        pieces = []
        for keys, on_rows in keys_b(g):
            s = lax.dot_general(qs, keys, _NT if on_rows else _NN, preferred_element_type=_F32)
            if lim_b:
                s = jnp.where(lax.broadcasted_iota(jnp.int32, s.shape, 1) >= lim_b, s, NEG)
            pieces.append(s)
        out.append(pieces)
    return out


def _row_reduce(pieces, combine, reduce, fill):
    acc = None
    for s in pieces:
        for c in range(0, s.shape[1], LANES):
            tile = s[:, c:c + LANES]
            if tile.shape[1] < LANES:
                pad = jnp.full((tile.shape[0], LANES - tile.shape[1]), fill, tile.dtype)
                tile = jnp.concatenate([tile, pad], axis=1)
            acc = tile if acc is None else combine(acc, tile)
    return reduce(acc, axis=1, keepdims=True)


def _softmax_values(pieces, vals, sink):
    m = _row_reduce(pieces, jnp.maximum, jnp.max, -jnp.inf)
    if sink is not None:
        m = jnp.maximum(m, sink)
    es = [jnp.exp2(s - m) for s in pieces]
    l = _row_reduce(es, jnp.add, jnp.sum, 0.0)
    if sink is not None:
        l = l + jnp.exp2(sink - m)
    acc = None
    for e, (v, on_rows) in zip(es, vals):
        part = lax.dot_general(e.astype(_BF16), v, _NN if on_rows else _NT, preferred_element_type=_F32)
        acc = part if acc is None else acc + part
    return acc / l


def _softmax_pv(scores, vals_a, vals_b, sink_ref):
    outs = []
    head_of_lane = lax.broadcasted_iota(jnp.int32, (CHUNK, 2 * LANES), 1) // HEAD_DIM
    for g in range(2):
        of = _softmax_values(scores[g], vals_a(g), None)
        og = of[3 * CHUNK:4 * CHUNK]
        for r in (2, 1, 0):
            og = jnp.where(head_of_lane == r, of[r * CHUNK:(r + 1) * CHUNK], og)
        outs.append(og)
    lo = lax.broadcasted_iota(jnp.int32, (CHUNK, LANES), 1) < HEAD_DIM
    for g in range(B_KV_HEADS):
        of = _softmax_values(scores[2 + g], vals_b(g), sink_ref[g] * LOG2E)
        outs.append(jnp.where(lo, of[0:CHUNK], of[CHUNK:2 * CHUNK]))
        outs.append(jnp.where(lo, of[2 * CHUNK:3 * CHUNK], of[3 * CHUNK:4 * CHUNK]))
    return jnp.concatenate(outs, axis=1)


def _attn_prompt_kernel(qa_ref, ka_ref, va_ref, qb_ref, kb_ref, vb_ref, u_ref, sink_ref, o_ref,
                        kwa, vwa, kwb, vwb, bias):
    i = pl.program_id(0)
    tm = qa_ref.shape[0]
    n_chunks = tm // CHUNK

    @pl.when(i == 0)
    def _():
        kwa[0:A_CTX, :] = jnp.zeros((A_CTX, A_WIDTH), _BF16)
        vwa[0:A_CTX, :] = jnp.zeros((A_CTX, A_WIDTH), _BF16)
        kwb[0:B_CTX, :] = jnp.zeros((B_CTX, 2 * B_KV_WIDTH), _BF16)
        vwb[0:B_CTX, :] = jnp.zeros((B_CTX, 2 * B_KV_WIDTH), _BF16)
        _build_bias(u_ref, bias)

    @pl.when(i > 0)
    def _():
        kwa[0:A_CTX, :] = kwa[tm:tm + A_CTX, :]
        vwa[0:A_CTX, :] = vwa[tm:tm + A_CTX, :]
        kwb[0:B_CTX, :] = kwb[tm:tm + B_CTX, :]
        vwb[0:B_CTX, :] = vwb[tm:tm + B_CTX, :]

    kwa[A_CTX:A_CTX + tm, :] = ka_ref[...]
    vwa[A_CTX:A_CTX + tm, :] = va_ref[...]
    kwb[B_CTX:B_CTX + tm, :] = kb_ref[...]
    vwb[B_CTX:B_CTX + tm, :] = vb_ref[...]

    def window(ref, r0, rows, width):
        return lambda g: [(ref[r0:r0 + rows, width * g:width * (g + 1)], True)]

    def run(masked):
        def scores(c):
            r0 = c * CHUNK
            lim_a = max(A_LEFT_CHUNKS - c, 0) * CHUNK if masked else 0
            lim_b = max(B_CTX // CHUNK - c, 0) * CHUNK if masked else 0
            return _scores(qa_ref[r0:r0 + CHUNK, :], qb_ref[r0:r0 + CHUNK, :],
                           window(kwa, r0, A_WIN, 2 * LANES), window(kwb, r0, B_WIN, LANES), bias, lim_a, lim_b)

        nxt = scores(0)
        for c in range(n_chunks):
            cur = nxt
            if c + 1 < n_chunks:
                nxt = scores(c + 1)
            r0 = c * CHUNK
            o = _softmax_pv(cur, window(vwa, r0, A_WIN, 2 * LANES), window(vwb, r0, B_WIN, LANES), sink_ref)
            o_ref[r0:r0 + CHUNK, :] = o.astype(_BF16)

    @pl.when(i == 0)
    def _():
        run(True)

    @pl.when(i > 0)
    def _():
        run(False)


def _attn_sample_kernel(qa_ref, ka_ref, va_ref, qb_ref, kb_ref, vb_ref, cak_ref, cav_ref, cbk_ref, cbv_ref,
                        u_ref, sink_ref, o_ref, bias):
    n_streams = cak_ref.shape[0]

    @pl.when(pl.program_id(0) == 0)
    def _():
        _build_bias(u_ref, bias)

    def pieces_a(cache_ref, new_ref, b):
        rows = slice(b * CHUNK, (b + 1) * CHUNK)

        def get(g):
            cols = slice(2 * LANES * g, 2 * LANES * (g + 1))
            return [(cache_ref[b, cols, :].astype(_BF16), False), (new_ref[rows, cols], True)]
        return get

    def pieces_b(cache_ref, new_ref, b):
        rows = slice(b * CHUNK, (b + 1) * CHUNK)

        def get(g):
            past = cache_ref[b, HEAD_DIM * g:HEAD_DIM * (g + 1), :].astype(_BF16)
            return [(jnp.concatenate([past, past], axis=0), False), (new_ref[rows, LANES * g:LANES * (g + 1)], True)]
        return get

    def scores(b):
        rows = slice(b * CHUNK, (b + 1) * CHUNK)
        return _scores(qa_ref[rows, :], qb_ref[rows, :], pieces_a(cak_ref, ka_ref, b), pieces_b(cbk_ref, kb_ref, b),
                       bias, 0, 0)

    nxt = scores(0)
    for b in range(n_streams):
        cur = nxt
        if b + 1 < n_streams:
            nxt = scores(b + 1)
        o = _softmax_pv(cur, pieces_a(cav_ref, va_ref, b), pieces_b(cbv_ref, vb_ref, b), sink_ref)
        o_ref[b * CHUNK:(b + 1) * CHUNK, :] = o.astype(_BF16)


_BIAS_SCRATCH = pltpu.VMEM((2, 4 * CHUNK, BIAS_COLS), _F32)


def _attend_prompt(qa, ka, va, qb, kb, vb, u, sink):
    t = qa.shape[0]
    tm = ATTN_TILE
    tile = lambda w: pl.BlockSpec((tm, w), lambda i: (i, 0))
    return pl.pallas_call(
        _attn_prompt_kernel,
        grid=(t // tm,),
        in_specs=[tile(A_WIDTH), tile(A_WIDTH), tile(A_WIDTH), tile(B_WIDTH), tile(2 * B_KV_WIDTH),
                  tile(2 * B_KV_WIDTH), _const_spec(u.shape), _const_spec(sink.shape)],
        out_specs=tile(A_WIDTH + B_WIDTH),
        out_shape=jax.ShapeDtypeStruct((t, A_WIDTH + B_WIDTH), _BF16),
        scratch_shapes=[pltpu.VMEM((A_CTX + tm, A_WIDTH), _BF16), pltpu.VMEM((A_CTX + tm, A_WIDTH), _BF16),
                        pltpu.VMEM((B_CTX + tm, 2 * B_KV_WIDTH), _BF16),
                        pltpu.VMEM((B_CTX + tm, 2 * B_KV_WIDTH), _BF16), _BIAS_SCRATCH],
        name="attn_prompt",
        compiler_params=pltpu.CompilerParams(dimension_semantics=("arbitrary",), vmem_limit_bytes=VMEM_LIMIT),
    )(qa, ka, va, qb, kb, vb, u, sink)


def _attend_sample(qa, ka, va, qb, kb, vb, cak, cav, cbk, cbv, u, sink):
    t = qa.shape[0]
    sb = SAMPLE_STREAMS
    tile = lambda w: pl.BlockSpec((sb * CHUNK, w), lambda i: (i, 0))
    cache = lambda arr: pl.BlockSpec((sb,) + arr.shape[1:], lambda i: (i, 0, 0))
    return pl.pallas_call(
        _attn_sample_kernel,
        grid=(t // (sb * CHUNK),),
        in_specs=[tile(A_WIDTH), tile(A_WIDTH), tile(A_WIDTH), tile(B_WIDTH), tile(2 * B_KV_WIDTH),
                  tile(2 * B_KV_WIDTH), cache(cak), cache(cav), cache(cbk), cache(cbv),
                  _const_spec(u.shape), _const_spec(sink.shape)],
        out_specs=tile(A_WIDTH + B_WIDTH),
        out_shape=jax.ShapeDtypeStruct((t, A_WIDTH + B_WIDTH), _BF16),
        scratch_shapes=[_BIAS_SCRATCH],
        name="attn_sample",
        compiler_params=pltpu.CompilerParams(dimension_semantics=("arbitrary",), vmem_limit_bytes=VMEM_LIMIT),
    )(qa, ka, va, qb, kb, vb, cak, cav, cbk, cbv, u, sink)


def _ffn_kernel(o_ref, x_ref, wo_ref, gn_ref, wgu_ref, cw_ref, cb_ref, wd_ref, st_ref,
                y_ref, cn_ref, carry, act_buf, *, seg_len, carry_rows):
    i = pl.program_id(0)
    tm = x_ref.shape[0]
    n_seg = tm // seg_len

    if carry_rows:
        @pl.when(i == 0)
        def _():
            carry[...] = st_ref[0]

    y1 = x_ref[...] + jnp.dot(o_ref[...], wo_ref[...], preferred_element_type=_F32)
    ms = jnp.mean(y1 * y1, axis=-1, keepdims=True)
    h = (y1 * lax.rsqrt(ms + EPS) * gn_ref[...]).astype(_BF16)

    row8 = lax.broadcasted_iota(jnp.int32, (8, FF_CHUNK), 0)

    def gate_up(c0):
        g = jnp.dot(h, wgu_ref[:, c0:c0 + FF_CHUNK], preferred_element_type=_F32)
        up = jnp.dot(h, wgu_ref[:, D_FF + c0:D_FF + c0 + FF_CHUNK], preferred_element_type=_F32)
        return g, up

    nxt = gate_up(0)
    for c0 in range(0, D_FF, FF_CHUNK):
        cols = slice(c0, c0 + FF_CHUNK)
        g, up = nxt
        if c0 + FF_CHUNK < D_FF:
            nxt = gate_up(c0 + FF_CHUNK)
        g1 = pltpu.roll(g, 1, 0)
        g2 = pltpu.roll(g, 2, 0)
        p1, p2 = [], []
        for s in range(n_seg):
            a = s * seg_len
            prev = carry[:, cols] if carry_rows else st_ref[s, :, cols]
            prev0 = jnp.broadcast_to(prev[0:1, :], (8, FF_CHUNK))
            prev1 = jnp.broadcast_to(prev[1:2, :], (8, FF_CHUNK))
            p1.append(jnp.where(row8 == 0, prev1, g1[a:a + 8]))
            p1.append(g1[a + 8:a + seg_len])
            p2.append(jnp.where(row8 == 0, prev0, jnp.where(row8 == 1, prev1, g2[a:a + 8])))
            p2.append(g2[a + 8:a + seg_len])
            tail = g[a + seg_len - 8:a + seg_len][8 - (CONV_W - 1):, :]
            if carry_rows:
                carry[:, cols] = tail
            else:
                cn_ref[s, :, cols] = tail
        g1 = jnp.concatenate(p1, axis=0)
        g2 = jnp.concatenate(p2, axis=0)
        conv = cb_ref[:, cols] + cw_ref[0:1, cols] * g2 + cw_ref[1:2, cols] * g1 + cw_ref[2:3, cols] * g
        act_buf[:, cols] = (conv * jax.nn.sigmoid(conv) * up).astype(_BF16)
    y_ref[...] = y1 + jnp.dot(act_buf[...], wd_ref[...], preferred_element_type=_F32)
    if carry_rows:
        cn_ref[0] = carry[...]


def _out_ffn(o, x, w_out, gn, w_gu, conv_w, conv_b, w_down, state, *, carry_rows, name):
    t = x.shape[0]
    tm = TOKEN_TILE
    n = t // tm
    seg_len = tm if carry_rows else CHUNK
    n_seg = tm // seg_len
    tile = lambda w: pl.BlockSpec((tm, w), lambda i: (i, 0))
    if carry_rows:
        st_spec = _const_spec((1, CONV_W - 1, D_FF))
        cn_spec = pl.BlockSpec((1, CONV_W - 1, D_FF), lambda i: (0, 0, 0))
        cn_shape = (1, CONV_W - 1, D_FF)
    else:
        st_spec = pl.BlockSpec((n_seg, CONV_W - 1, D_FF), lambda i: (i, 0, 0))
        cn_spec = pl.BlockSpec((n_seg, CONV_W - 1, D_FF), lambda i: (i, 0, 0))
        cn_shape = (t // seg_len, CONV_W - 1, D_FF)
    return pl.pallas_call(
        functools.partial(_ffn_kernel, seg_len=seg_len, carry_rows=carry_rows),
        grid=(n,),
        in_specs=[tile(A_WIDTH + B_WIDTH), tile(D_MODEL), _const_spec((D_MODEL, D_MODEL)),
                  _const_spec((1, D_MODEL)), _const_spec((D_MODEL, 2 * D_FF)), _const_spec((CONV_W, D_FF)),
                  _const_spec((1, D_FF)), _const_spec((D_FF, D_MODEL)), st_spec],
        out_specs=[tile(D_MODEL), cn_spec],
        out_shape=[jax.ShapeDtypeStruct((t, D_MODEL), _F32), jax.ShapeDtypeStruct(cn_shape, _F32)],
        scratch_shapes=[pltpu.VMEM((CONV_W - 1, D_FF), _F32), pltpu.VMEM((tm, D_FF), _BF16)],
        name=name,
        compiler_params=pltpu.CompilerParams(dimension_semantics=("arbitrary",), vmem_limit_bytes=VMEM_LIMIT),
    )(o, x, w_out, gn, w_gu, conv_w, conv_b, w_down, state)


def _rope_tables(tile_pos, row_pos):
    half = HEAD_DIM // 2
    lane = jnp.arange(LANES)
    inv = jnp.power(ROPE_THETA, -(lane % half).astype(_F32) / half)
    sign = jnp.where(lane % HEAD_DIM < half, -1.0, 1.0).astype(_F32)
    a = tile_pos.astype(_F32)[:, None] * inv[None, :]
    b = row_pos.astype(_F32)[:, None] * inv[None, :]
    return (jnp.cos(a)[:, None, :], (jnp.sin(a) * sign)[:, None, :], jnp.cos(b), jnp.sin(b) * sign)


def _feature_major(cache):
    n, t, heads, d = cache.shape
    return jnp.transpose(cache, (0, 2, 3, 1)).reshape(n, heads * d, t)


def _token_major(rows, heads):
    t = rows.shape[1]
    return jnp.transpose(rows.reshape(heads, HEAD_DIM, t), (2, 0, 1))[None, None]


def _bias_profile(table):
    n_clip = BIAS_COLS - 1 - MAX_REL
    head = jnp.broadcast_to(table[:, 2 * MAX_REL:], (A_HEADS, n_clip))
    return jnp.concatenate([head, table[:, ::-1][:, :BIAS_VAR - n_clip]], axis=1)


def kernel(x_prompt, x_sample, cache_a_k, cache_a_v, cache_b_k, cache_b_v, state_conv, norm_attn, w_in, q_norm_a,
           k_norm_a, rel_bias_a, q_norm_b, k_norm_b, sinks_b, w_out, norm_ffn, w_gate_up, conv_w, conv_b, w_down):
    assert norm_attn.shape[0] == 1
    bsz, seq, _ = x_prompt.shape
    dec_b, dec_s, _ = x_sample.shape
    assert bsz == 1 and dec_s == CHUNK and (dec_b * dec_s) % TOKEN_TILE == 0
    assert seq % TOKEN_TILE == 0 and seq % PROMPT_PROJ_TILE == 0 and seq % ATTN_TILE == 0
    assert cache_a_k.shape[2] == A_CTX and cache_b_k.shape[2] == B_CTX

    w_in_b = w_in[0].astype(_BF16)
    gn_a = norm_attn[0][None, :]
    gn_f = norm_ffn[0][None, :]
    gqa = jnp.tile(q_norm_a[0], A_HEADS)[None, :]
    gka = jnp.tile(k_norm_a[0], A_HEADS)[None, :]
    gqb = jnp.tile(q_norm_b[0], B_HEADS)[None, :]
    gkb = jnp.tile(k_norm_b[0], B_KV_HEADS)[None, :]
    u = _bias_profile(rel_bias_a[0])
    sink = jnp.repeat(sinks_b[0], CHUNK).reshape(B_KV_HEADS, B_GROUP * CHUNK, 1)
    cw = conv_w[0]
    cb = conv_b[0][None, :]

    rope_p = _rope_tables(jnp.arange(0, seq, PROMPT_PROJ_TILE), jnp.arange(PROMPT_PROJ_TILE))
    rope_s = _rope_tables(jnp.full((1,), PAST_LEN), jnp.arange(TOKEN_TILE) % dec_s)

    xp = x_prompt.reshape(seq, D_MODEL)
    qa, ka, va, qb, kb, vb, kaf, vaf, kbf, vbf, w_out_b, w_gu_b, w_down_b = _project(
        xp, gn_a, w_in_b, gqa, gka, gqb, gkb, rope_p, keep_all=False, name="proj_prompt",
        cast=(w_out[0], w_gate_up[0], w_down[0]))
    o_p = _attend_prompt(qa, ka, va, qb, kb, vb, u, sink)
    zero_state = jnp.zeros((1, CONV_W - 1, D_FF), _F32)
    y_p, cn_p = _out_ffn(o_p, xp, w_out_b, gn_f, w_gu_b, cw, cb, w_down_b, zero_state,
                         carry_rows=True, name="ffn_prompt")

    xs = x_sample.reshape(dec_b * dec_s, D_MODEL)
    qa, ka, va, qb, kb, vb, kaf_s, vaf_s, kbf_s, vbf_s = _project(
        xs, gn_a, w_in_b, gqa, gka, gqb, gkb, rope_s, keep_all=True, name="proj_sample")
    o_s = _attend_sample(qa, ka, va, qb, kb, vb, _feature_major(cache_a_k[0]), _feature_major(cache_a_v[0]),
                         _feature_major(cache_b_k[0]), _feature_major(cache_b_v[0]), u, sink)
    y_s, cn_s = _out_ffn(o_s, xs, w_out_b, gn_f, w_gu_b, cw, cb, w_down_b, state_conv[0],
                         carry_rows=False, name="ffn_sample")

    return (
        y_p.reshape(1, seq, D_MODEL),
        y_s.reshape(dec_b, dec_s, D_MODEL),
        _token_major(kaf, A_HEADS),
        _token_major(vaf, A_HEADS),
        _token_major(kbf, B_KV_HEADS),
        _token_major(vbf, B_KV_HEADS),
        cn_p.reshape(1, 1, CONV_W - 1, D_FF),
        kaf_s.reshape(1, dec_b, dec_s, A_HEADS, HEAD_DIM),
        vaf_s.reshape(1, dec_b, dec_s, A_HEADS, HEAD_DIM),
        kbf_s.reshape(1, dec_b, dec_s, B_KV_HEADS, HEAD_DIM),
        vbf_s.reshape(1, dec_b, dec_s, B_KV_HEADS, HEAD_DIM),
        cn_s.reshape(1, dec_b, CONV_W - 1, D_FF),
    )
```

```python
import functools

import jax
import jax.numpy as jnp
from jax import lax
from jax.experimental import pallas as pl
from jax.experimental.pallas import tpu as pltpu

D_MODEL = 1024
CHUNK = 64
HEAD_DIM = 64
EPS = 1e-6
SCALE = HEAD_DIM ** -0.5
LOG2E = 1.4426950408889634
Q_SCALE = SCALE * LOG2E
NEG = -1e30
PAST_LEN = 1024
A_HEADS = 8
A_LEFT_CHUNKS = 8
A_CTX = A_LEFT_CHUNKS * CHUNK
A_WIN = A_CTX + CHUNK
MAX_REL = 128
A_WIDTH = A_HEADS * HEAD_DIM
B_HEADS = 8
B_KV_HEADS = 2
B_GROUP = B_HEADS // B_KV_HEADS
B_CTX = 128
B_WIN = B_CTX + CHUNK
B_WIDTH = B_HEADS * HEAD_DIM
B_KV_WIDTH = B_KV_HEADS * HEAD_DIM
ROPE_THETA = 10000.0
IN_COLS = 3 * A_WIDTH + B_WIDTH + 2 * B_KV_WIDTH
D_FF = 2816
CONV_W = 3

LANES = 128
TOKEN_TILE = 512
PROMPT_PROJ_TILE = 1024
ATTN_TILE = 1024
SAMPLE_STREAMS = 4
FF_CHUNK = 256
BIAS_COLS = MAX_REL + CHUNK
BIAS_VAR = 256
VMEM_LIMIT = 56 * 1024 * 1024

_BF16 = jnp.bfloat16
_F32 = jnp.float32


def _const_spec(shape):
    nd = len(shape)
    return pl.BlockSpec(shape, lambda i: (0,) * nd, pipeline_mode=pl.Buffered(1))


def _head_sumsq(t, bd):
    sq = (t * t).astype(_BF16)
    w = t.shape[1]
    if w <= 2 * LANES:
        return jnp.dot(sq, bd[:w, :w], preferred_element_type=_F32)
    parts = [jnp.dot(sq[:, c:c + 2 * LANES], bd, preferred_element_type=_F32) for c in range(0, w, 2 * LANES)]
    return jnp.concatenate(parts, axis=1)


def _head_norm(t, bd, gain):
    ss = _head_sumsq(t, bd)
    return t * lax.rsqrt(ss * (1.0 / HEAD_DIM) + EPS) * gain


def _rope(t, cos, sin_signed):
    outs = []
    for c in range(0, t.shape[1], LANES):
        tc = t[:, c:c + LANES]
        lane = lax.broadcasted_iota(jnp.int32, tc.shape, 1)
        first_half = (lane & (HEAD_DIM - 1)) < HEAD_DIM // 2
        rot = jnp.where(first_half, pltpu.roll(tc, LANES - HEAD_DIM // 2, 1), pltpu.roll(tc, HEAD_DIM // 2, 1))
        outs.append(tc * cos + rot * sin_signed)
    return outs[0] if len(outs) == 1 else jnp.concatenate(outs, axis=1)


def _rep_pair(t):
    rolled = pltpu.roll(t, HEAD_DIM, 1)
    lo = lax.broadcasted_iota(jnp.int32, t.shape, 1) < HEAD_DIM
    return jnp.concatenate([jnp.where(lo, t, rolled), jnp.where(lo, rolled, t)], axis=1)


def _store_heads(ref, t):
    n, w = t.shape
    heads = w // HEAD_DIM
    for hd in range(heads):
        ref[pl.ds(hd, n, stride=heads), :] = t[:, hd * HEAD_DIM:(hd + 1) * HEAD_DIM]


def _proj_kernel(x_ref, gn_ref, w_ref, gqa_ref, gka_ref, gqb_ref, gkb_ref, ct_ref, st_ref, cr_ref, sr_ref, *rest,
                 last_only, n_cast):
    cast_in, rest = rest[:n_cast], rest[n_cast:]
    qa_ref, ka_ref, va_ref, qb_ref, kb_ref, vb_ref, kaf_ref, vaf_ref, kbf_ref, vbf_ref = rest[:10]
    for src, dst in zip(cast_in, rest[10:]):
        dst[...] = src[...].astype(_BF16)
    tm = x_ref.shape[0]
    x = x_ref[...]
    ms = jnp.mean(x * x, axis=-1, keepdims=True)
    h = (x * lax.rsqrt(ms + EPS) * gn_ref[...]).astype(_BF16)

    r = lax.broadcasted_iota(jnp.int32, (2 * LANES, 2 * LANES), 0) // HEAD_DIM
    c = lax.broadcasted_iota(jnp.int32, (2 * LANES, 2 * LANES), 1) // HEAD_DIM
    bd = jnp.where(r == c, 1.0, 0.0).astype(_BF16)
    ct, st, cr, sr = ct_ref[0], st_ref[0], cr_ref[...], sr_ref[...]
    cos = ct * cr - st * sr
    sin = st * cr + ct * sr

    def proj(lo, width):
        return jnp.dot(h, w_ref[:, lo:lo + width], preferred_element_type=_F32)

    o_ka, o_va, o_qb, o_kvb = A_WIDTH, 2 * A_WIDTH, 3 * A_WIDTH, 3 * A_WIDTH + B_WIDTH
    p_qa = proj(0, A_WIDTH)
    p_ka = proj(o_ka, A_WIDTH)
    qa_ref[...] = (_head_norm(p_qa, bd, gqa_ref[...]) * Q_SCALE).astype(_BF16)
    va = proj(o_va, A_WIDTH)
    ka = _head_norm(p_ka, bd, gka_ref[...])
    ka_ref[...] = ka.astype(_BF16)
    p_qb = proj(o_qb, B_WIDTH)
    va_ref[...] = va.astype(_BF16)
    p_kvb = proj(o_kvb, 2 * B_KV_WIDTH)
    qb_ref[...] = (_rope(_head_norm(p_qb, bd, gqb_ref[...]), cos, sin) * Q_SCALE).astype(_BF16)
    kb = _rope(_head_norm(p_kvb[:, :B_KV_WIDTH], bd, gkb_ref[...]), cos, sin)
    vb = p_kvb[:, B_KV_WIDTH:]
    kb_ref[...] = _rep_pair(kb).astype(_BF16)
    vb_ref[...] = _rep_pair(vb).astype(_BF16)

    if last_only:
        @pl.when(pl.program_id(0) == pl.num_programs(0) - 1)
        def _():
            kaf_ref[...] = ka[tm - A_CTX:, :].T
            vaf_ref[...] = va[tm - A_CTX:, :].T
            kbf_ref[...] = kb[tm - B_CTX:, :].T
            vbf_ref[...] = vb[tm - B_CTX:, :].T
    else:
        _store_heads(kaf_ref, ka)
        _store_heads(vaf_ref, va)
        _store_heads(kbf_ref, kb)
        _store_heads(vbf_ref, vb)


def _cast_blocks(w, n_steps):
    rows = w.shape[0]
    n_blocks = max(d for d in range(1, n_steps + 1) if rows % d == 0 and (rows // d) % 16 == 0)
    return pl.BlockSpec((rows // n_blocks, w.shape[1]), lambda i: (jnp.minimum(i, n_blocks - 1), 0))


def _project(x, gn, w_in, gqa, gka, gqb, gkb, rope, *, keep_all, name, cast=()):
    t = x.shape[0]
    tm = rope[2].shape[0]
    n = t // tm
    cast_specs = [_cast_blocks(w, n) for w in cast]
    tile = lambda w: pl.BlockSpec((tm, w), lambda i: (i, 0))
    if keep_all:
        f32_shapes = [(t * A_HEADS, HEAD_DIM)] * 2 + [(t * B_KV_HEADS, HEAD_DIM)] * 2
        f32_specs = ([pl.BlockSpec((tm * A_HEADS, HEAD_DIM), lambda i: (i, 0))] * 2
                     + [pl.BlockSpec((tm * B_KV_HEADS, HEAD_DIM), lambda i: (i, 0))] * 2)
    else:
        f32_shapes = [(A_WIDTH, A_CTX), (A_WIDTH, A_CTX), (B_KV_WIDTH, B_CTX), (B_KV_WIDTH, B_CTX)]
        f32_specs = [pl.BlockSpec(sh, lambda i: (0, 0)) for sh in f32_shapes]
    ct, st, cr, sr = rope
    if ct.shape[0] == n:
        tile_trig = pl.BlockSpec((1, 1, LANES), lambda i: (i, 0, 0))
    else:
        tile_trig = _const_spec((1, 1, LANES))
    out_shape = [
        jax.ShapeDtypeStruct((t, A_WIDTH), _BF16), jax.ShapeDtypeStruct((t, A_WIDTH), _BF16),
        jax.ShapeDtypeStruct((t, A_WIDTH), _BF16), jax.ShapeDtypeStruct((t, B_WIDTH), _BF16),
        jax.ShapeDtypeStruct((t, 2 * B_KV_WIDTH), _BF16), jax.ShapeDtypeStruct((t, 2 * B_KV_WIDTH), _BF16),
    ] + [jax.ShapeDtypeStruct(sh, _F32) for sh in f32_shapes] + [jax.ShapeDtypeStruct(w.shape, _BF16) for w in cast]
    out_specs = [
        tile(A_WIDTH), tile(A_WIDTH), tile(A_WIDTH), tile(B_WIDTH), tile(2 * B_KV_WIDTH), tile(2 * B_KV_WIDTH),
    ] + f32_specs + cast_specs
    in_specs = [
        tile(D_MODEL), _const_spec((1, D_MODEL)), _const_spec((D_MODEL, IN_COLS)),
        _const_spec((1, A_WIDTH)), _const_spec((1, A_WIDTH)), _const_spec((1, B_WIDTH)),
        _const_spec((1, B_KV_WIDTH)), tile_trig, tile_trig, _const_spec((tm, LANES)), _const_spec((tm, LANES)),
    ] + cast_specs
    return pl.pallas_call(
        functools.partial(_proj_kernel, last_only=not keep_all, n_cast=len(cast)),
        grid=(n,), in_specs=in_specs, out_specs=out_specs, out_shape=out_shape, name=name,
        compiler_params=pltpu.CompilerParams(dimension_semantics=("arbitrary",), vmem_limit_bytes=VMEM_LIMIT),
    )(x, gn, w_in, gqa, gka, gqb, gkb, ct, st, cr, sr, *cast)


def _build_bias(u_ref, bias_ref):
    qrow = lax.broadcasted_iota(jnp.int32, (CHUNK, BIAS_VAR), 0)
    for h in range(A_HEADS):
        prof = u_ref[h:h + 1, :]
        y = jnp.broadcast_to(prof - prof[:, 0:1], (CHUNK, BIAS_VAR))
        y = pltpu.roll(y, BIAS_VAR - (CHUNK - 1), 1)
        for b in range(6):
            y = jnp.where(((qrow >> b) & 1) == 1, pltpu.roll(y, 1 << b, 1), y)
        bias_ref[h // 4, (h % 4) * CHUNK:(h % 4 + 1) * CHUNK, :] = y[:, :BIAS_COLS] * LOG2E


_NT = (((1,), (1,)), ((), ()))
_NN = (((1,), (0,)), ((), ()))


def _scores(qa, qb, keys_a, keys_b, bias_ref, lim_a, lim_b):
    out = []
    head_of_lane = lax.broadcasted_iota(jnp.int32, (CHUNK, 2 * LANES), 1) // HEAD_DIM
    for g in range(2):
        qg = qa[:, 2 * LANES * g:2 * LANES * (g + 1)]
        qbd = jnp.concatenate([jnp.where(head_of_lane == r, qg, jnp.zeros_like(qg)) for r in range(4)], axis=0)
        pieces, off = [], 0
        for keys, on_rows in keys_a(g):
            s = lax.dot_general(qbd, keys, _NT if on_rows else _NN, preferred_element_type=_F32)
            n = s.shape[1]
            if off + n > A_WIN - BIAS_COLS:
                lo_col = max(A_WIN - BIAS_COLS - off, 0)
                biased = s[:, lo_col:] + bias_ref[g, :, off + lo_col - (A_WIN - BIAS_COLS):off + n - (A_WIN - BIAS_COLS)]
                s = biased if lo_col == 0 else jnp.concatenate([s[:, :lo_col], biased], axis=1)
            if lim_a:
                s = jnp.where(lax.broadcasted_iota(jnp.int32, s.shape, 1) >= lim_a, s, NEG)
            pieces.append(s)
            off += n
        out.append(pieces)
    lo = lax.broadcasted_iota(jnp.int32, (CHUNK, LANES), 1) < HEAD_DIM
    for g in range(B_KV_HEADS):
        rows = []
        for r in range(B_GROUP):
            hd = B_GROUP * g + r
            qp = qb[:, LANES * (hd // 2):LANES * (hd // 2 + 1)]
            keep = lo if hd % 2 == 0 else jnp.logical_not(lo)
            rows.append(jnp.where(keep, qp, jnp.zeros_like(qp)))
        qs = jnp.concatenate(rows, axis=0)
        pieces = []
        for keys, on_rows in keys_b(g):
            s = lax.dot_general(qs, keys, _NT if on_rows else _NN, preferred_element_type=_F32)
            if lim_b:
                s = jnp.where(lax.broadcasted_iota(jnp.int32, s.shape, 1) >= lim_b, s, NEG)
            pieces.append(s)
        out.append(pieces)
    return out


def _row_reduce(pieces, combine, reduce, fill):
    acc = None
    for s in pieces:
        for c in range(0, s.shape[1], LANES):
            tile = s[:, c:c + LANES]
            if tile.shape[1] < LANES:
                pad = jnp.full((tile.shape[0], LANES - tile.shape[1]), fill, tile.dtype)
                tile = jnp.concatenate([tile, pad], axis=1)
            acc = tile if acc is None else combine(acc, tile)
    return reduce(acc, axis=1, keepdims=True)


def _softmax_values(pieces, vals, sink):
    m = _row_reduce(pieces, jnp.maximum, jnp.max, -jnp.inf)
    if sink is not None:
        m = jnp.maximum(m, sink)
    es = [jnp.exp2(s - m) for s in pieces]
    l = _row_reduce(es, jnp.add, jnp.sum, 0.0)
    if sink is not None:
        l = l + jnp.exp2(sink - m)
    acc = None
    for e, (v, on_rows) in zip(es, vals):
        part = lax.dot_general(e.astype(_BF16), v, _NN if on_rows else _NT, preferred_element_type=_F32)
        acc = part if acc is None else acc + part
    return acc / l


def _softmax_pv(scores, vals_a, vals_b, sink_ref):
    outs = []
    head_of_lane = lax.broadcasted_iota(jnp.int32, (CHUNK, 2 * LANES), 1) // HEAD_DIM
    for g in range(2):
        of = _softmax_values(scores[g], vals_a(g), None)
        og = of[3 * CHUNK:4 * CHUNK]
        for r in (2, 1, 0):
            og = jnp.where(head_of_lane == r, of[r * CHUNK:(r + 1) * CHUNK], og)
        outs.append(og)
    lo = lax.broadcasted_iota(jnp.int32, (CHUNK, LANES), 1) < HEAD_DIM
    for g in range(B_KV_HEADS):
        of = _softmax_values(scores[2 + g], vals_b(g), sink_ref[g] * LOG2E)
        outs.append(jnp.where(lo, of[0:CHUNK], of[CHUNK:2 * CHUNK]))
        outs.append(jnp.where(lo, of[2 * CHUNK:3 * CHUNK], of[3 * CHUNK:4 * CHUNK]))
    return jnp.concatenate(outs, axis=1)


def _attn_prompt_kernel(qa_ref, ka_ref, va_ref, qb_ref, kb_ref, vb_ref, u_ref, sink_ref, o_ref,
                        kwa, vwa, kwb, vwb, bias):
    i = pl.program_id(0)
    tm = qa_ref.shape[0]
    n_chunks = tm // CHUNK

    @pl.when(i == 0)
    def _():
        kwa[0:A_CTX, :] = jnp.zeros((A_CTX, A_WIDTH), _BF16)
        vwa[0:A_CTX, :] = jnp.zeros((A_CTX, A_WIDTH), _BF16)
        kwb[0:B_CTX, :] = jnp.zeros((B_CTX, 2 * B_KV_WIDTH), _BF16)
        vwb[0:B_CTX, :] = jnp.zeros((B_CTX, 2 * B_KV_WIDTH), _BF16)
        _build_bias(u_ref, bias)

    @pl.when(i > 0)
    def _():
        kwa[0:A_CTX, :] = kwa[tm:tm + A_CTX, :]
        vwa[0:A_CTX, :] = vwa[tm:tm + A_CTX, :]
        kwb[0:B_CTX, :] = kwb[tm:tm + B_CTX, :]
        vwb[0:B_CTX, :] = vwb[tm:tm + B_CTX, :]

    kwa[A_CTX:A_CTX + tm, :] = ka_ref[...]
    vwa[A_CTX:A_CTX + tm, :] = va_ref[...]
    kwb[B_CTX:B_CTX + tm, :] = kb_ref[...]
    vwb[B_CTX:B_CTX + tm, :] = vb_ref[...]

    def window(ref, r0, rows, width):
        return lambda g: [(ref[r0:r0 + rows, width * g:width * (g + 1)], True)]

    def run(masked):
        def scores(c):
            r0 = c * CHUNK
            lim_a = max(A_LEFT_CHUNKS - c, 0) * CHUNK if masked else 0
            lim_b = max(B_CTX // CHUNK - c, 0) * CHUNK if masked else 0
            return _scores(qa_ref[r0:r0 + CHUNK, :], qb_ref[r0:r0 + CHUNK, :],
                           window(kwa, r0, A_WIN, 2 * LANES), window(kwb, r0, B_WIN, LANES), bias, lim_a, lim_b)

        nxt = scores(0)
        for c in range(n_chunks):
            cur = nxt
            if c + 1 < n_chunks:
                nxt = scores(c + 1)
            r0 = c * CHUNK
            o = _softmax_pv(cur, window(vwa, r0, A_WIN, 2 * LANES), window(vwb, r0, B_WIN, LANES), sink_ref)
            o_ref[r0:r0 + CHUNK, :] = o.astype(_BF16)

    @pl.when(i == 0)
    def _():
        run(True)

    @pl.when(i > 0)
    def _():
        run(False)


def _attn_sample_kernel(qa_ref, ka_ref, va_ref, qb_ref, kb_ref, vb_ref, cak_ref, cav_ref, cbk_ref, cbv_ref,
                        u_ref, sink_ref, o_ref, bias):
    n_streams = cak_ref.shape[0]

    @pl.when(pl.program_id(0) == 0)
    def _():
        _build_bias(u_ref, bias)

    def pieces_a(cache_ref, new_ref, b):
        rows = slice(b * CHUNK, (b + 1) * CHUNK)

        def get(g):
            cols = slice(2 * LANES * g, 2 * LANES * (g + 1))
            return [(cache_ref[b, cols, :].astype(_BF16), False), (new_ref[rows, cols], True)]
        return get

    def pieces_b(cache_ref, new_ref, b):
        rows = slice(b * CHUNK, (b + 1) * CHUNK)

        def get(g):
            past = cache_ref[b, HEAD_DIM * g:HEAD_DIM * (g + 1), :].astype(_BF16)
            return [(jnp.concatenate([past, past], axis=0), False), (new_ref[rows, LANES * g:LANES * (g + 1)], True)]
        return get

    def scores(b):
        rows = slice(b * CHUNK, (b + 1) * CHUNK)
        return _scores(qa_ref[rows, :], qb_ref[rows, :], pieces_a(cak_ref, ka_ref, b), pieces_b(cbk_ref, kb_ref, b),
                       bias, 0, 0)

    nxt = scores(0)
    for b in range(n_streams):
        cur = nxt
        if b + 1 < n_streams:
            nxt = scores(b + 1)
        o = _softmax_pv(cur, pieces_a(cav_ref, va_ref, b), pieces_b(cbv_ref, vb_ref, b), sink_ref)
        o_ref[b * CHUNK:(b + 1) * CHUNK, :] = o.astype(_BF16)


_BIAS_SCRATCH = pltpu.VMEM((2, 4 * CHUNK, BIAS_COLS), _F32)


def _attend_prompt(qa, ka, va, qb, kb, vb, u, sink):
    t = qa.shape[0]
    tm = ATTN_TILE
    tile = lambda w: pl.BlockSpec((tm, w), lambda i: (i, 0))
    return pl.pallas_call(
        _attn_prompt_kernel,
        grid=(t // tm,),
        in_specs=[tile(A_WIDTH), tile(A_WIDTH), tile(A_WIDTH), tile(B_WIDTH), tile(2 * B_KV_WIDTH),
                  tile(2 * B_KV_WIDTH), _const_spec(u.shape), _const_spec(sink.shape)],
        out_specs=tile(A_WIDTH + B_WIDTH),
        out_shape=jax.ShapeDtypeStruct((t, A_WIDTH + B_WIDTH), _BF16),
        scratch_shapes=[pltpu.VMEM((A_CTX + tm, A_WIDTH), _BF16), pltpu.VMEM((A_CTX + tm, A_WIDTH), _BF16),
                        pltpu.VMEM((B_CTX + tm, 2 * B_KV_WIDTH), _BF16),
                        pltpu.VMEM((B_CTX + tm, 2 * B_KV_WIDTH), _BF16), _BIAS_SCRATCH],
        name="attn_prompt",
        compiler_params=pltpu.CompilerParams(dimension_semantics=("arbitrary",), vmem_limit_bytes=VMEM_LIMIT),
    )(qa, ka, va, qb, kb, vb, u, sink)


def _attend_sample(qa, ka, va, qb, kb, vb, cak, cav, cbk, cbv, u, sink):
    t = qa.shape[0]
    sb = SAMPLE_STREAMS
    tile = lambda w: pl.BlockSpec((sb * CHUNK, w), lambda i: (i, 0))
    cache = lambda arr: pl.BlockSpec((sb,) + arr.shape[1:], lambda i: (i, 0, 0))
    return pl.pallas_call(
        _attn_sample_kernel,
        grid=(t // (sb * CHUNK),),
        in_specs=[tile(A_WIDTH), tile(A_WIDTH), tile(A_WIDTH), tile(B_WIDTH), tile(2 * B_KV_WIDTH),
                  tile(2 * B_KV_WIDTH), cache(cak), cache(cav), cache(cbk), cache(cbv),
                  _const_spec(u.shape), _const_spec(sink.shape)],
        out_specs=tile(A_WIDTH + B_WIDTH),
        out_shape=jax.ShapeDtypeStruct((t, A_WIDTH + B_WIDTH), _BF16),
        scratch_shapes=[_BIAS_SCRATCH],
        name="attn_sample",
        compiler_params=pltpu.CompilerParams(dimension_semantics=("arbitrary",), vmem_limit_bytes=VMEM_LIMIT),
    )(qa, ka, va, qb, kb, vb, cak, cav, cbk, cbv, u, sink)


def _ffn_kernel(o_ref, x_ref, wo_ref, gn_ref, wgu_ref, cw_ref, cb_ref, wd_ref, st_ref,
                y_ref, cn_ref, carry, act_buf, *, seg_len, carry_rows):
    i = pl.program_id(0)
    tm = x_ref.shape[0]
    n_seg = tm // seg_len

    if carry_rows:
        @pl.when(i == 0)
        def _():
            carry[...] = st_ref[0]

    y1 = x_ref[...] + jnp.dot(o_ref[...], wo_ref[...], preferred_element_type=_F32)
    ms = jnp.mean(y1 * y1, axis=-1, keepdims=True)
    h = (y1 * lax.rsqrt(ms + EPS) * gn_ref[...]).astype(_BF16)

    row8 = lax.broadcasted_iota(jnp.int32, (8, FF_CHUNK), 0)

    def gate_up(c0):
        g = jnp.dot(h, wgu_ref[:, c0:c0 + FF_CHUNK], preferred_element_type=_F32)
        up = jnp.dot(h, wgu_ref[:, D_FF + c0:D_FF + c0 + FF_CHUNK], preferred_element_type=_F32)
        return g, up

    nxt = gate_up(0)
    for c0 in range(0, D_FF, FF_CHUNK):
        cols = slice(c0, c0 + FF_CHUNK)
        g, up = nxt
        if c0 + FF_CHUNK < D_FF:
            nxt = gate_up(c0 + FF_CHUNK)
        g1 = pltpu.roll(g, 1, 0)
        g2 = pltpu.roll(g, 2, 0)
        p1, p2 = [], []
        for s in range(n_seg):
            a = s * seg_len
            prev = carry[:, cols] if carry_rows else st_ref[s, :, cols]
            prev0 = jnp.broadcast_to(prev[0:1, :], (8, FF_CHUNK))
            prev1 = jnp.broadcast_to(prev[1:2, :], (8, FF_CHUNK))
            p1.append(jnp.where(row8 == 0, prev1, g1[a:a + 8]))
            p1.append(g1[a + 8:a + seg_len])
            p2.append(jnp.where(row8 == 0, prev0, jnp.where(row8 == 1, prev1, g2[a:a + 8])))
            p2.append(g2[a + 8:a + seg_len])
            tail = g[a + seg_len - 8:a + seg_len][8 - (CONV_W - 1):, :]
            if carry_rows:
                carry[:, cols] = tail
            else:
                cn_ref[s, :, cols] = tail
        g1 = jnp.concatenate(p1, axis=0)
        g2 = jnp.concatenate(p2, axis=0)
        conv = cb_ref[:, cols] + cw_ref[0:1, cols] * g2 + cw_ref[1:2, cols] * g1 + cw_ref[2:3, cols] * g
        act_buf[:, cols] = (conv * jax.nn.sigmoid(conv) * up).astype(_BF16)
    y_ref[...] = y1 + jnp.dot(act_buf[...], wd_ref[...], preferred_element_type=_F32)
    if carry_rows:
        cn_ref[0] = carry[...]


def _out_ffn(o, x, w_out, gn, w_gu, conv_w, conv_b, w_down, state, *, carry_rows, name):
    t = x.shape[0]
    tm = TOKEN_TILE
    n = t // tm
    seg_len = tm if carry_rows else CHUNK
    n_seg = tm // seg_len
    tile = lambda w: pl.BlockSpec((tm, w), lambda i: (i, 0))
    if carry_rows:
        st_spec = _const_spec((1, CONV_W - 1, D_FF))
        cn_spec = pl.BlockSpec((1, CONV_W - 1, D_FF), lambda i: (0, 0, 0))
        cn_shape = (1, CONV_W - 1, D_FF)
    else:
        st_spec = pl.BlockSpec((n_seg, CONV_W - 1, D_FF), lambda i: (i, 0, 0))
        cn_spec = pl.BlockSpec((n_seg, CONV_W - 1, D_FF), lambda i: (i, 0, 0))
        cn_shape = (t // seg_len, CONV_W - 1, D_FF)
    return pl.pallas_call(
        functools.partial(_ffn_kernel, seg_len=seg_len, carry_rows=carry_rows),
        grid=(n,),
        in_specs=[tile(A_WIDTH + B_WIDTH), tile(D_MODEL), _const_spec((D_MODEL, D_MODEL)),
                  _const_spec((1, D_MODEL)), _const_spec((D_MODEL, 2 * D_FF)), _const_spec((CONV_W, D_FF)),
                  _const_spec((1, D_FF)), _const_spec((D_FF, D_MODEL)), st_spec],
        out_specs=[tile(D_MODEL), cn_spec],
        out_shape=[jax.ShapeDtypeStruct((t, D_MODEL), _F32), jax.ShapeDtypeStruct(cn_shape, _F32)],
        scratch_shapes=[pltpu.VMEM((CONV_W - 1, D_FF), _F32), pltpu.VMEM((tm, D_FF), _BF16)],
        name=name,
        compiler_params=pltpu.CompilerParams(dimension_semantics=("arbitrary",), vmem_limit_bytes=VMEM_LIMIT),
    )(o, x, w_out, gn, w_gu, conv_w, conv_b, w_down, state)


def _rope_tables(tile_pos, row_pos):
    half = HEAD_DIM // 2
    inv = jnp.power(ROPE_THETA, -jnp.arange(half, dtype=_F32) / half)

    def table(pos):
        ang = pos.astype(_F32)[:, None] * inv[None, :]
        cos, sin = jnp.cos(ang), jnp.sin(ang)
        return jnp.tile(cos, (1, LANES // half)), jnp.tile(jnp.concatenate([-sin, sin], axis=1), (1, LANES // HEAD_DIM))

    ct, st = table(tile_pos)
    cr, sr = table(row_pos)
    return ct[:, None, :], st[:, None, :], cr, sr


def _feature_major(cache):
    n, t, heads, d = cache.shape
    return jnp.transpose(cache, (0, 2, 3, 1)).reshape(n, heads * d, t)


def _token_major(rows, heads):
    t = rows.shape[1]
    return jnp.transpose(rows.reshape(heads, HEAD_DIM, t), (2, 0, 1))[None, None]


def _bias_profile(table):
    n_clip = BIAS_COLS - 1 - MAX_REL
    head = jnp.broadcast_to(table[:, 2 * MAX_REL:], (A_HEADS, n_clip))
    return jnp.concatenate([head, table[:, ::-1][:, :BIAS_VAR - n_clip]], axis=1)


def kernel(x_prompt, x_sample, cache_a_k, cache_a_v, cache_b_k, cache_b_v, state_conv, norm_attn, w_in, q_norm_a,
           k_norm_a, rel_bias_a, q_norm_b, k_norm_b, sinks_b, w_out, norm_ffn, w_gate_up, conv_w, conv_b, w_down):
    assert norm_attn.shape[0] == 1
    bsz, seq, _ = x_prompt.shape
    dec_b, dec_s, _ = x_sample.shape
    assert bsz == 1 and dec_s == CHUNK and (dec_b * dec_s) % TOKEN_TILE == 0
    assert seq % TOKEN_TILE == 0 and seq % PROMPT_PROJ_TILE == 0 and seq % ATTN_TILE == 0
    assert cache_a_k.shape[2] == A_CTX and cache_b_k.shape[2] == B_CTX

    w_in_b = w_in[0].astype(_BF16)
    gn_a = norm_attn[0][None, :]
    gn_f = norm_ffn[0][None, :]
    gqa = jnp.tile(q_norm_a[0], A_HEADS)[None, :]
    gka = jnp.tile(k_norm_a[0], A_HEADS)[None, :]
    gqb = jnp.tile(q_norm_b[0], B_HEADS)[None, :]
    gkb = jnp.tile(k_norm_b[0], B_KV_HEADS)[None, :]
    u = _bias_profile(rel_bias_a[0])
    sink = jnp.repeat(sinks_b[0], CHUNK).reshape(B_KV_HEADS, B_GROUP * CHUNK, 1)
    cw = conv_w[0]
    cb = conv_b[0][None, :]

    rope_p = _rope_tables(jnp.arange(0, seq, PROMPT_PROJ_TILE), jnp.arange(PROMPT_PROJ_TILE))
    rope_s = _rope_tables(jnp.full((1,), PAST_LEN), jnp.arange(TOKEN_TILE) % dec_s)

    xp = x_prompt.reshape(seq, D_MODEL)
    qa, ka, va, qb, kb, vb, kaf, vaf, kbf, vbf, w_out_b, w_gu_b, w_down_b = _project(
        xp, gn_a, w_in_b, gqa, gka, gqb, gkb, rope_p, keep_all=False, name="proj_prompt",
        cast=(w_out[0], w_gate_up[0], w_down[0]))
    o_p = _attend_prompt(qa, ka, va, qb, kb, vb, u, sink)
    zero_state = jnp.zeros((1, CONV_W - 1, D_FF), _F32)
    y_p, cn_p = _out_ffn(o_p, xp, w_out_b, gn_f, w_gu_b, cw, cb, w_down_b, zero_state,
                         carry_rows=True, name="ffn_prompt")

    xs = x_sample.reshape(dec_b * dec_s, D_MODEL)
    qa, ka, va, qb, kb, vb, kaf_s, vaf_s, kbf_s, vbf_s = _project(
        xs, gn_a, w_in_b, gqa, gka, gqb, gkb, rope_s, keep_all=True, name="proj_sample")
    o_s = _attend_sample(qa, ka, va, qb, kb, vb, _feature_major(cache_a_k[0]), _feature_major(cache_a_v[0]),
                         _feature_major(cache_b_k[0]), _feature_major(cache_b_v[0]), u, sink)
    y_s, cn_s = _out_ffn(o_s, xs, w_out_b, gn_f, w_gu_b, cw, cb, w_down_b, state_conv[0],
                         carry_rows=False, name="ffn_sample")

    return (
        y_p.reshape(1, seq, D_MODEL),
        y_s.reshape(dec_b, dec_s, D_MODEL),
        _token_major(kaf, A_HEADS),
        _token_major(vaf, A_HEADS),
        _token_major(kbf, B_KV_HEADS),
        _token_major(vbf, B_KV_HEADS),
        cn_p.reshape(1, 1, CONV_W - 1, D_FF),
        kaf_s.reshape(1, dec_b, dec_s, A_HEADS, HEAD_DIM),
        vaf_s.reshape(1, dec_b, dec_s, A_HEADS, HEAD_DIM),
        kbf_s.reshape(1, dec_b, dec_s, B_KV_HEADS, HEAD_DIM),
        vbf_s.reshape(1, dec_b, dec_s, B_KV_HEADS, HEAD_DIM),
        cn_s.reshape(1, dec_b, CONV_W - 1, D_FF),
    )
```

```python
import functools

import jax
import jax.numpy as jnp
from jax import lax
from jax.experimental import pallas as pl
from jax.experimental.pallas import tpu as pltpu

D_MODEL = 1024
CHUNK = 64
HEAD_DIM = 64
EPS = 1e-6
SCALE = HEAD_DIM ** -0.5
LOG2E = 1.4426950408889634
Q_SCALE = SCALE * LOG2E
NEG = -1e30
PAST_LEN = 1024
A_HEADS = 8
A_LEFT_CHUNKS = 8
A_CTX = A_LEFT_CHUNKS * CHUNK
A_WIN = A_CTX + CHUNK
MAX_REL = 128
A_WIDTH = A_HEADS * HEAD_DIM
B_HEADS = 8
B_KV_HEADS = 2
B_GROUP = B_HEADS // B_KV_HEADS
B_CTX = 128
B_WIN = B_CTX + CHUNK
B_WIDTH = B_HEADS * HEAD_DIM
B_KV_WIDTH = B_KV_HEADS * HEAD_DIM
ROPE_THETA = 10000.0
IN_COLS = 3 * A_WIDTH + B_WIDTH + 2 * B_KV_WIDTH
QKV_WIDTHS = (A_WIDTH, A_WIDTH, A_WIDTH, B_WIDTH, 2 * B_KV_WIDTH, 2 * B_KV_WIDTH)
D_FF = 2816
CONV_W = 3

LANES = 128
TOKEN_TILE = 512
PROMPT_PROJ_TILE = 1024
ATTN_TILE = 1024
SAMPLE_STREAMS = 4
FF_CHUNK = 256
BIAS_COLS = MAX_REL + CHUNK
BIAS_VAR = 256
VMEM_LIMIT = 56 * 1024 * 1024

_BF16 = jnp.bfloat16
_F32 = jnp.float32


def _const_spec(shape):
    nd = len(shape)
    return pl.BlockSpec(shape, lambda i: (0,) * nd, pipeline_mode=pl.Buffered(1))


def _head_sumsq(t, bd):
    sq = (t * t).astype(_BF16)
    w = t.shape[1]
    if w <= 2 * LANES:
        return jnp.dot(sq, bd[:w, :w], preferred_element_type=_F32)
    parts = [jnp.dot(sq[:, c:c + 2 * LANES], bd, preferred_element_type=_F32) for c in range(0, w, 2 * LANES)]
    return jnp.concatenate(parts, axis=1)


def _head_norm(t, bd, gain):
    ss = _head_sumsq(t, bd)
    return t * lax.rsqrt(ss * (1.0 / HEAD_DIM) + EPS) * gain


def _rope(t, cos, sin_signed):
    outs = []
    for c in range(0, t.shape[1], LANES):
        tc = t[:, c:c + LANES]
        lane = lax.broadcasted_iota(jnp.int32, tc.shape, 1)
        first_half = (lane & (HEAD_DIM - 1)) < HEAD_DIM // 2
        rot = jnp.where(first_half, pltpu.roll(tc, LANES - HEAD_DIM // 2, 1), pltpu.roll(tc, HEAD_DIM // 2, 1))
        outs.append(tc * cos + rot * sin_signed)
    return outs[0] if len(outs) == 1 else jnp.concatenate(outs, axis=1)


def _rep_pair(t):
    rolled = pltpu.roll(t, HEAD_DIM, 1)
    lo = lax.broadcasted_iota(jnp.int32, t.shape, 1) < HEAD_DIM
    return jnp.concatenate([jnp.where(lo, t, rolled), jnp.where(lo, rolled, t)], axis=1)


def _store_heads(ref, t):
    n, w = t.shape
    heads = w // HEAD_DIM
    for hd in range(heads):
        ref[pl.ds(hd, n, stride=heads), :] = t[:, hd * HEAD_DIM:(hd + 1) * HEAD_DIM]


def _split_qkv(ref):
    views, lo = [], 0
    for w in QKV_WIDTHS:
        views.append(ref.at[:, lo:lo + w])
        lo += w
    return views


def _proj_kernel(x_ref, gn_ref, w_ref, gqa_ref, gka_ref, gqb_ref, gkb_ref, ct_ref, st_ref, cr_ref, sr_ref, *rest,
                 last_only, n_cast):
    cast_in, rest = rest[:n_cast], rest[n_cast:]
    qkv_ref, kaf_ref, vaf_ref, kbf_ref, vbf_ref = rest[:5]
    qa_ref, ka_ref, va_ref, qb_ref, kb_ref, vb_ref = _split_qkv(qkv_ref)
    for src, dst in zip(cast_in, rest[5:]):
        dst[...] = src[...].astype(_BF16)
    tm = x_ref.shape[0]
    x = x_ref[...]
    ms = jnp.mean(x * x, axis=-1, keepdims=True)
    h = (x * lax.rsqrt(ms + EPS) * gn_ref[...]).astype(_BF16)

    r = lax.broadcasted_iota(jnp.int32, (2 * LANES, 2 * LANES), 0) // HEAD_DIM
    c = lax.broadcasted_iota(jnp.int32, (2 * LANES, 2 * LANES), 1) // HEAD_DIM
    bd = jnp.where(r == c, 1.0, 0.0).astype(_BF16)
    ct, st, cr, sr = ct_ref[0], st_ref[0], cr_ref[...], sr_ref[...]
    cos = ct * cr - st * sr
    sin = st * cr + ct * sr

    def proj(lo, width):
        return jnp.dot(h, w_ref[:, lo:lo + width], preferred_element_type=_F32)

    o_ka, o_va, o_qb, o_kvb = A_WIDTH, 2 * A_WIDTH, 3 * A_WIDTH, 3 * A_WIDTH + B_WIDTH
    p_qa = proj(0, A_WIDTH)
    p_ka = proj(o_ka, A_WIDTH)
    qa_ref[...] = (_head_norm(p_qa, bd, gqa_ref[...]) * Q_SCALE).astype(_BF16)
    va = proj(o_va, A_WIDTH)
    ka = _head_norm(p_ka, bd, gka_ref[...])
    ka_ref[...] = ka.astype(_BF16)
    p_qb = proj(o_qb, B_WIDTH)
    va_ref[...] = va.astype(_BF16)
    p_kvb = proj(o_kvb, 2 * B_KV_WIDTH)
    qb_ref[...] = (_rope(_head_norm(p_qb, bd, gqb_ref[...]), cos, sin) * Q_SCALE).astype(_BF16)
    kb = _rope(_head_norm(p_kvb[:, :B_KV_WIDTH], bd, gkb_ref[...]), cos, sin)
    vb = p_kvb[:, B_KV_WIDTH:]
    kb_ref[...] = _rep_pair(kb).astype(_BF16)
    vb_ref[...] = _rep_pair(vb).astype(_BF16)

    if last_only:
        @pl.when(pl.program_id(0) == pl.num_programs(0) - 1)
        def _():
            kaf_ref[...] = ka[tm - A_CTX:, :].T
            vaf_ref[...] = va[tm - A_CTX:, :].T
            kbf_ref[...] = kb[tm - B_CTX:, :].T
            vbf_ref[...] = vb[tm - B_CTX:, :].T
    else:
        _store_heads(kaf_ref, ka)
        _store_heads(vaf_ref, va)
        _store_heads(kbf_ref, kb)
        _store_heads(vbf_ref, vb)


def _cast_blocks(w, n_steps):
    rows = w.shape[0]
    n_blocks = max(d for d in range(1, n_steps + 1) if rows % d == 0 and (rows // d) % 16 == 0)
    return pl.BlockSpec((rows // n_blocks, w.shape[1]), lambda i: (jnp.minimum(i, n_blocks - 1), 0))


def _project(x, gn, w_in, gqa, gka, gqb, gkb, rope, *, keep_all, name, cast=()):
    t = x.shape[0]
    tm = rope[2].shape[0]
    n = t // tm
    cast_specs = [_cast_blocks(w, n) for w in cast]
    tile = lambda w: pl.BlockSpec((tm, w), lambda i: (i, 0))
    if keep_all:
        f32_shapes = [(t * A_HEADS, HEAD_DIM)] * 2 + [(t * B_KV_HEADS, HEAD_DIM)] * 2
        f32_specs = ([pl.BlockSpec((tm * A_HEADS, HEAD_DIM), lambda i: (i, 0))] * 2
                     + [pl.BlockSpec((tm * B_KV_HEADS, HEAD_DIM), lambda i: (i, 0))] * 2)
    else:
        f32_shapes = [(A_WIDTH, A_CTX), (A_WIDTH, A_CTX), (B_KV_WIDTH, B_CTX), (B_KV_WIDTH, B_CTX)]
        f32_specs = [pl.BlockSpec(sh, lambda i: (0, 0)) for sh in f32_shapes]
    ct, st, cr, sr = rope
    if ct.shape[0] == n:
        tile_trig = pl.BlockSpec((1, 1, LANES), lambda i: (i, 0, 0))
    else:
        tile_trig = _const_spec((1, 1, LANES))
    out_shape = ([jax.ShapeDtypeStruct((t, sum(QKV_WIDTHS)), _BF16)]
                 + [jax.ShapeDtypeStruct(sh, _F32) for sh in f32_shapes]
                 + [jax.ShapeDtypeStruct(w.shape, _BF16) for w in cast])
    out_specs = [tile(sum(QKV_WIDTHS))] + f32_specs + cast_specs
    in_specs = [
        tile(D_MODEL), _const_spec((1, D_MODEL)), _const_spec((D_MODEL, IN_COLS)),
        _const_spec((1, A_WIDTH)), _const_spec((1, A_WIDTH)), _const_spec((1, B_WIDTH)),
        _const_spec((1, B_KV_WIDTH)), tile_trig, tile_trig, _const_spec((tm, LANES)), _const_spec((tm, LANES)),
    ] + cast_specs
    return pl.pallas_call(
        functools.partial(_proj_kernel, last_only=not keep_all, n_cast=len(cast)),
        grid=(n,), in_specs=in_specs, out_specs=out_specs, out_shape=out_shape, name=name,
        compiler_params=pltpu.CompilerParams(dimension_semantics=("arbitrary",), vmem_limit_bytes=VMEM_LIMIT),
    )(x, gn, w_in, gqa, gka, gqb, gkb, ct, st, cr, sr, *cast)


def _build_bias(u_ref, bias_ref):
    qrow = lax.broadcasted_iota(jnp.int32, (CHUNK, BIAS_VAR), 0)
    for h in range(A_HEADS):
        prof = u_ref[h:h + 1, :]
        y = jnp.broadcast_to(prof - prof[:, 0:1], (CHUNK, BIAS_VAR))
        y = pltpu.roll(y, BIAS_VAR - (CHUNK - 1), 1)
        for b in range(6):
            y = jnp.where(((qrow >> b) & 1) == 1, pltpu.roll(y, 1 << b, 1), y)
        bias_ref[h // 4, (h % 4) * CHUNK:(h % 4 + 1) * CHUNK, :] = y[:, :BIAS_COLS] * LOG2E


_NT = (((1,), (1,)), ((), ()))
_NN = (((1,), (0,)), ((), ()))


def _scores(qa, qb, keys_a, keys_b, bias_ref, lim_a, lim_b):
    out = []
    head_of_lane = lax.broadcasted_iota(jnp.int32, (CHUNK, 2 * LANES), 1) // HEAD_DIM
    for g in range(2):
        qg = qa[:, 2 * LANES * g:2 * LANES * (g + 1)]
        qbd = jnp.concatenate([jnp.where(head_of_lane == r, qg, jnp.zeros_like(qg)) for r in range(4)], axis=0)
        pieces, off = [], 0
        for keys, on_rows in keys_a(g):
            s = lax.dot_general(qbd, keys, _NT if on_rows else _NN, preferred_element_type=_F32)
            n = s.shape[1]
            if off + n > A_WIN - BIAS_COLS:
                lo_col = max(A_WIN - BIAS_COLS - off, 0)
                biased = s[:, lo_col:] + bias_ref[g, :, off + lo_col - (A_WIN - BIAS_COLS):off + n - (A_WIN - BIAS_COLS)]
                s = biased if lo_col == 0 else jnp.concatenate([s[:, :lo_col], biased], axis=1)
            if lim_a:
                s = jnp.where(lax.broadcasted_iota(jnp.int32, s.shape, 1) >= lim_a, s, NEG)
            pieces.append(s)
            off += n
        out.append(pieces)
    lo = lax.broadcasted_iota(jnp.int32, (CHUNK, LANES), 1) < HEAD_DIM
    for g in range(B_KV_HEADS):
        rows = []
        for r in range(B_GROUP):
            hd = B_GROUP * g + r
            qp = qb[:, LANES * (hd // 2):LANES * (hd // 2 + 1)]
            keep = lo if hd % 2 == 0 else jnp.logical_not(lo)
            rows.append(jnp.where(keep, qp, jnp.zeros_like(qp)))
        qs = jnp.concatenate(rows, axis=0)
        pieces = []
        for keys, on_rows in keys_b(g):
            s = lax.dot_general(qs, keys, _NT if on_rows else _NN, preferred_element_type=_F32)
            if lim_b:
                s = jnp.where(lax.broadcasted_iota(jnp.int32, s.shape, 1) >= lim_b, s, NEG)
            pieces.append(s)
        out.append(pieces)
    return out


def _row_reduce(pieces, combine, reduce, fill):
    acc = None
    for s in pieces:
        for c in range(0, s.shape[1], LANES):
            tile = s[:, c:c + LANES]
            if tile.shape[1] < LANES:
                pad = jnp.full((tile.shape[0], LANES - tile.shape[1]), fill, tile.dtype)
                tile = jnp.concatenate([tile, pad], axis=1)
            acc = tile if acc is None else combine(acc, tile)
    return reduce(acc, axis=1, keepdims=True)


def _softmax_values(pieces, vals, sink):
    m = _row_reduce(pieces, jnp.maximum, jnp.max, -jnp.inf)
    if sink is not None:
        m = jnp.maximum(m, sink)
    es = [jnp.exp2(s - m) for s in pieces]
    l = _row_reduce(es, jnp.add, jnp.sum, 0.0)
    if sink is not None:
        l = l + jnp.exp2(sink - m)
    acc = None
    for e, (v, on_rows) in zip(es, vals):
        part = lax.dot_general(e.astype(_BF16), v, _NN if on_rows else _NT, preferred_element_type=_F32)
        acc = part if acc is None else acc + part
    return acc / l


def _softmax_pv(scores, vals_a, vals_b, sink_ref):
    outs = []
    head_of_lane = lax.broadcasted_iota(jnp.int32, (CHUNK, 2 * LANES), 1) // HEAD_DIM
    for g in range(2):
        of = _softmax_values(scores[g], vals_a(g), None)
        og = of[3 * CHUNK:4 * CHUNK]
        for r in (2, 1, 0):
            og = jnp.where(head_of_lane == r, of[r * CHUNK:(r + 1) * CHUNK], og)
        outs.append(og)
    lo = lax.broadcasted_iota(jnp.int32, (CHUNK, LANES), 1) < HEAD_DIM
    for g in range(B_KV_HEADS):
        of = _softmax_values(scores[2 + g], vals_b(g), sink_ref[g] * LOG2E)
        outs.append(jnp.where(lo, of[0:CHUNK], of[CHUNK:2 * CHUNK]))
        outs.append(jnp.where(lo, of[2 * CHUNK:3 * CHUNK], of[3 * CHUNK:4 * CHUNK]))
    return jnp.concatenate(outs, axis=1)


def _attn_prompt_kernel(qkv_ref, u_ref, sink_ref, o_ref, kwa, vwa, kwb, vwb, bias):
    qa_ref, ka_ref, va_ref, qb_ref, kb_ref, vb_ref = _split_qkv(qkv_ref)
    i = pl.program_id(0)
    tm = qa_ref.shape[0]
    n_chunks = tm // CHUNK

    @pl.when(i == 0)
    def _():
        kwa[0:A_CTX, :] = jnp.zeros((A_CTX, A_WIDTH), _BF16)
        vwa[0:A_CTX, :] = jnp.zeros((A_CTX, A_WIDTH), _BF16)
        kwb[0:B_CTX, :] = jnp.zeros((B_CTX, 2 * B_KV_WIDTH), _BF16)
        vwb[0:B_CTX, :] = jnp.zeros((B_CTX, 2 * B_KV_WIDTH), _BF16)
        _build_bias(u_ref, bias)

    @pl.when(i > 0)
    def _():
        kwa[0:A_CTX, :] = kwa[tm:tm + A_CTX, :]
        vwa[0:A_CTX, :] = vwa[tm:tm + A_CTX, :]
        kwb[0:B_CTX, :] = kwb[tm:tm + B_CTX, :]
        vwb[0:B_CTX, :] = vwb[tm:tm + B_CTX, :]

    kwa[A_CTX:A_CTX + tm, :] = ka_ref[...]
    vwa[A_CTX:A_CTX + tm, :] = va_ref[...]
    kwb[B_CTX:B_CTX + tm, :] = kb_ref[...]
    vwb[B_CTX:B_CTX + tm, :] = vb_ref[...]

    def window(ref, r0, rows, width):
        return lambda g: [(ref[r0:r0 + rows, width * g:width * (g + 1)], True)]

    def run(masked):
        def scores(c):
            r0 = c * CHUNK
            lim_a = max(A_LEFT_CHUNKS - c, 0) * CHUNK if masked else 0
            lim_b = max(B_CTX // CHUNK - c, 0) * CHUNK if masked else 0
            return _scores(qa_ref[r0:r0 + CHUNK, :], qb_ref[r0:r0 + CHUNK, :],
                           window(kwa, r0, A_WIN, 2 * LANES), window(kwb, r0, B_WIN, LANES), bias, lim_a, lim_b)

        nxt = scores(0)
        for c in range(n_chunks):
            cur = nxt
            if c + 1 < n_chunks:
                nxt = scores(c + 1)
            r0 = c * CHUNK
            o = _softmax_pv(cur, window(vwa, r0, A_WIN, 2 * LANES), window(vwb, r0, B_WIN, LANES), sink_ref)
            o_ref[r0:r0 + CHUNK, :] = o.astype(_BF16)

    @pl.when(i == 0)
    def _():
        run(True)

    @pl.when(i > 0)
    def _():
        run(False)


def _attn_sample_kernel(qkv_ref, cak_ref, cav_ref, cbk_ref, cbv_ref, u_ref, sink_ref, o_ref, bias):
    qa_ref, ka_ref, va_ref, qb_ref, kb_ref, vb_ref = _split_qkv(qkv_ref)
    n_streams = cak_ref.shape[0]

    @pl.when(pl.program_id(0) == 0)
    def _():
        _build_bias(u_ref, bias)

    def pieces_a(cache_ref, new_ref, b):
        rows = slice(b * CHUNK, (b + 1) * CHUNK)

        def get(g):
            cols = slice(2 * LANES * g, 2 * LANES * (g + 1))
            return [(cache_ref[b, cols, :].astype(_BF16), False), (new_ref[rows, cols], True)]
        return get

    def pieces_b(cache_ref, new_ref, b):
        rows = slice(b * CHUNK, (b + 1) * CHUNK)

        def get(g):
            past = cache_ref[b, HEAD_DIM * g:HEAD_DIM * (g + 1), :].astype(_BF16)
            return [(jnp.concatenate([past, past], axis=0), False), (new_ref[rows, LANES * g:LANES * (g + 1)], True)]
        return get

    def scores(b):
        rows = slice(b * CHUNK, (b + 1) * CHUNK)
        return _scores(qa_ref[rows, :], qb_ref[rows, :], pieces_a(cak_ref, ka_ref, b), pieces_b(cbk_ref, kb_ref, b),
                       bias, 0, 0)

    nxt = scores(0)
    for b in range(n_streams):
        cur = nxt
        if b + 1 < n_streams:
            nxt = scores(b + 1)
        o = _softmax_pv(cur, pieces_a(cav_ref, va_ref, b), pieces_b(cbv_ref, vb_ref, b), sink_ref)
        o_ref[b * CHUNK:(b + 1) * CHUNK, :] = o.astype(_BF16)


_BIAS_SCRATCH = pltpu.VMEM((2, 4 * CHUNK, BIAS_COLS), _F32)


def _attend_prompt(qkv, u, sink):
    t = qkv.shape[0]
    tm = ATTN_TILE
    tile = lambda w: pl.BlockSpec((tm, w), lambda i: (i, 0))
    return pl.pallas_call(
        _attn_prompt_kernel,
        grid=(t // tm,),
        in_specs=[tile(sum(QKV_WIDTHS)), _const_spec(u.shape), _const_spec(sink.shape)],
        out_specs=tile(A_WIDTH + B_WIDTH),
        out_shape=jax.ShapeDtypeStruct((t, A_WIDTH + B_WIDTH), _BF16),
        scratch_shapes=[pltpu.VMEM((A_CTX + tm, A_WIDTH), _BF16), pltpu.VMEM((A_CTX + tm, A_WIDTH), _BF16),
                        pltpu.VMEM((B_CTX + tm, 2 * B_KV_WIDTH), _BF16),
                        pltpu.VMEM((B_CTX + tm, 2 * B_KV_WIDTH), _BF16), _BIAS_SCRATCH],
        name="attn_prompt",
        compiler_params=pltpu.CompilerParams(dimension_semantics=("arbitrary",), vmem_limit_bytes=VMEM_LIMIT),
    )(qkv, u, sink)


def _attend_sample(qkv, cak, cav, cbk, cbv, u, sink):
    t = qkv.shape[0]
    sb = SAMPLE_STREAMS
    tile = lambda w: pl.BlockSpec((sb * CHUNK, w), lambda i: (i, 0))
    cache = lambda arr: pl.BlockSpec((sb,) + arr.shape[1:], lambda i: (i, 0, 0))
    return pl.pallas_call(
        _attn_sample_kernel,
        grid=(t // (sb * CHUNK),),
        in_specs=[tile(sum(QKV_WIDTHS)), cache(cak), cache(cav), cache(cbk), cache(cbv),
                  _const_spec(u.shape), _const_spec(sink.shape)],
        out_specs=tile(A_WIDTH + B_WIDTH),
        out_shape=jax.ShapeDtypeStruct((t, A_WIDTH + B_WIDTH), _BF16),
        scratch_shapes=[_BIAS_SCRATCH],
        name="attn_sample",
        compiler_params=pltpu.CompilerParams(dimension_semantics=("arbitrary",), vmem_limit_bytes=VMEM_LIMIT),
    )(qkv, cak, cav, cbk, cbv, u, sink)


def _ffn_kernel(o_ref, x_ref, wo_ref, gn_ref, wgu_ref, cw_ref, cb_ref, wd_ref, st_ref,
                y_ref, cn_ref, carry, act_buf, *, seg_len, carry_rows):
    i = pl.program_id(0)
    tm = x_ref.shape[0]
    n_seg = tm // seg_len

    if carry_rows:
        @pl.when(i == 0)
        def _():
            carry[...] = st_ref[0]

    y1 = x_ref[...] + jnp.dot(o_ref[...], wo_ref[...], preferred_element_type=_F32)
    ms = jnp.mean(y1 * y1, axis=-1, keepdims=True)
    h = (y1 * lax.rsqrt(ms + EPS) * gn_ref[...]).astype(_BF16)

    row8 = lax.broadcasted_iota(jnp.int32, (8, FF_CHUNK), 0)

    def gate_up(c0):
        g = jnp.dot(h, wgu_ref[:, c0:c0 + FF_CHUNK], preferred_element_type=_F32)
        up = jnp.dot(h, wgu_ref[:, D_FF + c0:D_FF + c0 + FF_CHUNK], preferred_element_type=_F32)
        return g, up

    nxt = gate_up(0)
    for c0 in range(0, D_FF, FF_CHUNK):
        cols = slice(c0, c0 + FF_CHUNK)
        g, up = nxt
        if c0 + FF_CHUNK < D_FF:
            nxt = gate_up(c0 + FF_CHUNK)
        g1 = pltpu.roll(g, 1, 0)
        g2 = pltpu.roll(g, 2, 0)
        p1, p2 = [], []
        for s in range(n_seg):
            a = s * seg_len
            prev = carry[:, cols] if carry_rows else st_ref[s, :, cols]
            prev0 = jnp.broadcast_to(prev[0:1, :], (8, FF_CHUNK))
            prev1 = jnp.broadcast_to(prev[1:2, :], (8, FF_CHUNK))
            p1.append(jnp.where(row8 == 0, prev1, g1[a:a + 8]))
            p1.append(g1[a + 8:a + seg_len])
            p2.append(jnp.where(row8 == 0, prev0, jnp.where(row8 == 1, prev1, g2[a:a + 8])))
            p2.append(g2[a + 8:a + seg_len])
            tail = g[a + seg_len - 8:a + seg_len][8 - (CONV_W - 1):, :]
            if carry_rows:
                carry[:, cols] = tail
            else:
                cn_ref[s, :, cols] = tail
        g1 = jnp.concatenate(p1, axis=0)
        g2 = jnp.concatenate(p2, axis=0)
        conv = cb_ref[:, cols] + cw_ref[0:1, cols] * g2 + cw_ref[1:2, cols] * g1 + cw_ref[2:3, cols] * g
        act_buf[:, cols] = (conv * jax.nn.sigmoid(conv) * up).astype(_BF16)
    y_ref[...] = y1 + jnp.dot(act_buf[...], wd_ref[...], preferred_element_type=_F32)
    if carry_rows:
        cn_ref[0] = carry[...]


def _out_ffn(o, x, w_out, gn, w_gu, conv_w, conv_b, w_down, state, *, carry_rows, name):
    t = x.shape[0]
    tm = TOKEN_TILE
    n = t // tm
    seg_len = tm if carry_rows else CHUNK
    n_seg = tm // seg_len
    tile = lambda w: pl.BlockSpec((tm, w), lambda i: (i, 0))
    if carry_rows:
        st_spec = _const_spec((1, CONV_W - 1, D_FF))
        cn_spec = pl.BlockSpec((1, CONV_W - 1, D_FF), lambda i: (0, 0, 0))
        cn_shape = (1, CONV_W - 1, D_FF)
    else:
        st_spec = pl.BlockSpec((n_seg, CONV_W - 1, D_FF), lambda i: (i, 0, 0))
        cn_spec = pl.BlockSpec((n_seg, CONV_W - 1, D_FF), lambda i: (i, 0, 0))
        cn_shape = (t // seg_len, CONV_W - 1, D_FF)
    return pl.pallas_call(
        functools.partial(_ffn_kernel, seg_len=seg_len, carry_rows=carry_rows),
        grid=(n,),
        in_specs=[tile(A_WIDTH + B_WIDTH), tile(D_MODEL), _const_spec((D_MODEL, D_MODEL)),
                  _const_spec((1, D_MODEL)), _const_spec((D_MODEL, 2 * D_FF)), _const_spec((CONV_W, D_FF)),
                  _const_spec((1, D_FF)), _const_spec((D_FF, D_MODEL)), st_spec],
        out_specs=[tile(D_MODEL), cn_spec],
        out_shape=[jax.ShapeDtypeStruct((t, D_MODEL), _F32), jax.ShapeDtypeStruct(cn_shape, _F32)],
        scratch_shapes=[pltpu.VMEM((CONV_W - 1, D_FF), _F32), pltpu.VMEM((tm, D_FF), _BF16)],
        name=name,
        compiler_params=pltpu.CompilerParams(dimension_semantics=("arbitrary",), vmem_limit_bytes=VMEM_LIMIT),
    )(o, x, w_out, gn, w_gu, conv_w, conv_b, w_down, state)


def _rope_tables(tile_pos, row_pos):
    half = HEAD_DIM // 2
    inv = jnp.power(ROPE_THETA, -jnp.arange(half, dtype=_F32) / half)

    def table(pos):
        ang = pos.astype(_F32)[:, None] * inv[None, :]
        cos, sin = jnp.cos(ang), jnp.sin(ang)
        return jnp.tile(cos, (1, LANES // half)), jnp.tile(jnp.concatenate([-sin, sin], axis=1), (1, LANES // HEAD_DIM))

    ct, st = table(tile_pos)
    cr, sr = table(row_pos)
    return ct[:, None, :], st[:, None, :], cr, sr


def _feature_major(cache):
    n, t, heads, d = cache.shape
    return jnp.transpose(cache, (0, 2, 3, 1)).reshape(n, heads * d, t)


def _token_major(rows, heads):
    t = rows.shape[1]
    return jnp.transpose(rows.reshape(heads, HEAD_DIM, t), (2, 0, 1))[None, None]


def _bias_profile(table):
    n_clip = BIAS_COLS - 1 - MAX_REL
    head = jnp.broadcast_to(table[:, 2 * MAX_REL:], (A_HEADS, n_clip))
    return jnp.concatenate([head, table[:, ::-1][:, :BIAS_VAR - n_clip]], axis=1)


def kernel(x_prompt, x_sample, cache_a_k, cache_a_v, cache_b_k, cache_b_v, state_conv, norm_attn, w_in, q_norm_a,
           k_norm_a, rel_bias_a, q_norm_b, k_norm_b, sinks_b, w_out, norm_ffn, w_gate_up, conv_w, conv_b, w_down):
    assert norm_attn.shape[0] == 1
    bsz, seq, _ = x_prompt.shape
    dec_b, dec_s, _ = x_sample.shape
    assert bsz == 1 and dec_s == CHUNK and (dec_b * dec_s) % TOKEN_TILE == 0
    assert seq % TOKEN_TILE == 0 and seq % PROMPT_PROJ_TILE == 0 and seq % ATTN_TILE == 0
    assert cache_a_k.shape[2] == A_CTX and cache_b_k.shape[2] == B_CTX

    w_in_b = w_in[0].astype(_BF16)
    gn_a = norm_attn[0][None, :]
    gn_f = norm_ffn[0][None, :]
    gqa = jnp.tile(q_norm_a[0], A_HEADS)[None, :]
    gka = jnp.tile(k_norm_a[0], A_HEADS)[None, :]
    gqb = jnp.tile(q_norm_b[0], B_HEADS)[None, :]
    gkb = jnp.tile(k_norm_b[0], B_KV_HEADS)[None, :]
    u = _bias_profile(rel_bias_a[0])
    sink = jnp.repeat(sinks_b[0], CHUNK).reshape(B_KV_HEADS, B_GROUP * CHUNK, 1)
    cw = conv_w[0]
    cb = conv_b[0][None, :]

    rope_p = _rope_tables(jnp.arange(0, seq, PROMPT_PROJ_TILE), jnp.arange(PROMPT_PROJ_TILE))
    rope_s = _rope_tables(jnp.full((1,), PAST_LEN), jnp.arange(TOKEN_TILE) % dec_s)

    xp = x_prompt.reshape(seq, D_MODEL)
    qkv, kaf, vaf, kbf, vbf, w_out_b, w_gu_b, w_down_b = _project(
        xp, gn_a, w_in_b, gqa, gka, gqb, gkb, rope_p, keep_all=False, name="proj_prompt",
        cast=(w_out[0], w_gate_up[0], w_down[0]))
    o_p = _attend_prompt(qkv, u, sink)
    zero_state = jnp.zeros((1, CONV_W - 1, D_FF), _F32)
    y_p, cn_p = _out_ffn(o_p, xp, w_out_b, gn_f, w_gu_b, cw, cb, w_down_b, zero_state,
                         carry_rows=True, name="ffn_prompt")

    xs = x_sample.reshape(dec_b * dec_s, D_MODEL)
    qkv, kaf_s, vaf_s, kbf_s, vbf_s = _project(
        xs, gn_a, w_in_b, gqa, gka, gqb, gkb, rope_s, keep_all=True, name="proj_sample")
    o_s = _attend_sample(qkv, _feature_major(cache_a_k[0]), _feature_major(cache_a_v[0]),
                         _feature_major(cache_b_k[0]), _feature_major(cache_b_v[0]), u, sink)
    y_s, cn_s = _out_ffn(o_s, xs, w_out_b, gn_f, w_gu_b, cw, cb, w_down_b, state_conv[0],
                         carry_rows=False, name="ffn_sample")

    return (
        y_p.reshape(1, seq, D_MODEL),
        y_s.reshape(dec_b, dec_s, D_MODEL),
        _token_major(kaf, A_HEADS),
        _token_major(vaf, A_HEADS),
        _token_major(kbf, B_KV_HEADS),
        _token_major(vbf, B_KV_HEADS),
        cn_p.reshape(1, 1, CONV_W - 1, D_FF),
        kaf_s.reshape(1, dec_b, dec_s, A_HEADS, HEAD_DIM),
        vaf_s.reshape(1, dec_b, dec_s, A_HEADS, HEAD_DIM),
        kbf_s.reshape(1, dec_b, dec_s, B_KV_HEADS, HEAD_DIM),
        vbf_s.reshape(1, dec_b, dec_s, B_KV_HEADS, HEAD_DIM),
        cn_s.reshape(1, dec_b, CONV_W - 1, D_FF),
    )
```

```python
import functools

import jax
import jax.numpy as jnp
from jax import lax
from jax.experimental import pallas as pl
from jax.experimental.pallas import tpu as pltpu

D_MODEL = 1024
CHUNK = 64
HEAD_DIM = 64
EPS = 1e-6
SCALE = HEAD_DIM ** -0.5
LOG2E = 1.4426950408889634
Q_SCALE = SCALE * LOG2E
NEG = -1e30
PAST_LEN = 1024
A_HEADS = 8
A_LEFT_CHUNKS = 8
A_CTX = A_LEFT_CHUNKS * CHUNK
A_WIN = A_CTX + CHUNK
MAX_REL = 128
A_WIDTH = A_HEADS * HEAD_DIM
B_HEADS = 8
B_KV_HEADS = 2
B_GROUP = B_HEADS // B_KV_HEADS
B_CTX = 128
B_WIN = B_CTX + CHUNK
B_WIDTH = B_HEADS * HEAD_DIM
B_KV_WIDTH = B_KV_HEADS * HEAD_DIM
ROPE_THETA = 10000.0
IN_COLS = 3 * A_WIDTH + B_WIDTH + 2 * B_KV_WIDTH
QKV_WIDTHS = (A_WIDTH, A_WIDTH, A_WIDTH, B_WIDTH, 2 * B_KV_WIDTH, 2 * B_KV_WIDTH)
D_FF = 2816
CONV_W = 3

LANES = 128
TOKEN_TILE = 512
PROMPT_PROJ_TILE = 1024
ATTN_TILE = 1024
SAMPLE_STREAMS = 4
FF_CHUNK = 256
BIAS_COLS = MAX_REL + CHUNK
BIAS_VAR = 256
VMEM_LIMIT = 56 * 1024 * 1024

_BF16 = jnp.bfloat16
_F32 = jnp.float32


def _const_spec(shape):
    nd = len(shape)
    return pl.BlockSpec(shape, lambda i: (0,) * nd, pipeline_mode=pl.Buffered(1))


def _head_sumsq(t, bd):
    sq = (t * t).astype(_BF16)
    w = t.shape[1]
    if w <= 2 * LANES:
        return jnp.dot(sq, bd[:w, :w], preferred_element_type=_F32)
    parts = [jnp.dot(sq[:, c:c + 2 * LANES], bd, preferred_element_type=_F32) for c in range(0, w, 2 * LANES)]
    return jnp.concatenate(parts, axis=1)


def _head_norm(t, bd, gain):
    ss = _head_sumsq(t, bd)
    return t * lax.rsqrt(ss * (1.0 / HEAD_DIM) + EPS) * gain


def _rope(t, cos, sin_signed):
    outs = []
    for c in range(0, t.shape[1], LANES):
        tc = t[:, c:c + LANES]
        lane = lax.broadcasted_iota(jnp.int32, tc.shape, 1)
        first_half = (lane & (HEAD_DIM - 1)) < HEAD_DIM // 2
        rot = jnp.where(first_half, pltpu.roll(tc, LANES - HEAD_DIM // 2, 1), pltpu.roll(tc, HEAD_DIM // 2, 1))
        outs.append(tc * cos + rot * sin_signed)
    return outs[0] if len(outs) == 1 else jnp.concatenate(outs, axis=1)


def _rep_pair(t):
    rolled = pltpu.roll(t, HEAD_DIM, 1)
    lo = lax.broadcasted_iota(jnp.int32, t.shape, 1) < HEAD_DIM
    return jnp.concatenate([jnp.where(lo, t, rolled), jnp.where(lo, rolled, t)], axis=1)


def _store_heads(ref, t):
    n, w = t.shape
    heads = w // HEAD_DIM
    for hd in range(heads):
        ref[pl.ds(hd, n, stride=heads), :] = t[:, hd * HEAD_DIM:(hd + 1) * HEAD_DIM]


def _split_qkv(ref):
    views, lo = [], 0
    for w in QKV_WIDTHS:
        views.append(ref.at[:, lo:lo + w])
        lo += w
    return views


def _proj_kernel(x_ref, gn_ref, w_ref, gqa_ref, gka_ref, gqb_ref, gkb_ref, ct_ref, st_ref, cr_ref, sr_ref, *rest,
                 last_only, n_cast):
    cast_in, rest = rest[:n_cast], rest[n_cast:]
    qkv_ref, kaf_ref, vaf_ref, kbf_ref, vbf_ref = rest[:5]
    qa_ref, ka_ref, va_ref, qb_ref, kb_ref, vb_ref = _split_qkv(qkv_ref)
    for src, dst in zip(cast_in, rest[5:]):
        dst[...] = src[...].astype(_BF16)
    tm = x_ref.shape[0]
    x = x_ref[...]
    ms = jnp.mean(x * x, axis=-1, keepdims=True)
    h = (x * lax.rsqrt(ms + EPS) * gn_ref[...]).astype(_BF16)

    r = lax.broadcasted_iota(jnp.int32, (2 * LANES, 2 * LANES), 0) // HEAD_DIM
    c = lax.broadcasted_iota(jnp.int32, (2 * LANES, 2 * LANES), 1) // HEAD_DIM
    bd = jnp.where(r == c, 1.0, 0.0).astype(_BF16)
    ct, st, cr, sr = ct_ref[0], st_ref[0], cr_ref[...], sr_ref[...]
    cos = ct * cr - st * sr
    sin = st * cr + ct * sr

    def proj(lo, width):
        return jnp.dot(h, w_ref[:, lo:lo + width], preferred_element_type=_F32)

    o_ka, o_va, o_qb, o_kvb = A_WIDTH, 2 * A_WIDTH, 3 * A_WIDTH, 3 * A_WIDTH + B_WIDTH
    p_qa = proj(0, A_WIDTH)
    p_ka = proj(o_ka, A_WIDTH)
    qa_ref[...] = (_head_norm(p_qa, bd, gqa_ref[...]) * Q_SCALE).astype(_BF16)
    va = proj(o_va, A_WIDTH)
    ka = _head_norm(p_ka, bd, gka_ref[...])
    ka_ref[...] = ka.astype(_BF16)
    p_qb = proj(o_qb, B_WIDTH)
    va_ref[...] = va.astype(_BF16)
    p_kvb = proj(o_kvb, 2 * B_KV_WIDTH)
    qb_ref[...] = (_rope(_head_norm(p_qb, bd, gqb_ref[...]), cos, sin) * Q_SCALE).astype(_BF16)
    kb = _rope(_head_norm(p_kvb[:, :B_KV_WIDTH], bd, gkb_ref[...]), cos, sin)
    vb = p_kvb[:, B_KV_WIDTH:]
    kb_ref[...] = _rep_pair(kb).astype(_BF16)
    vb_ref[...] = _rep_pair(vb).astype(_BF16)

    if last_only:
        @pl.when(pl.program_id(0) == pl.num_programs(0) - 1)
        def _():
            kaf_ref[...] = ka[tm - A_CTX:, :].T
            vaf_ref[...] = va[tm - A_CTX:, :].T
            kbf_ref[...] = kb[tm - B_CTX:, :].T
            vbf_ref[...] = vb[tm - B_CTX:, :].T
    else:
        _store_heads(kaf_ref, ka)
        _store_heads(vaf_ref, va)
        kbf_ref[...] = kb
        vbf_ref[...] = vb


def _cast_blocks(w, n_steps):
    rows = w.shape[0]
    n_blocks = max(d for d in range(1, n_steps + 1) if rows % d == 0 and (rows // d) % 16 == 0)
    return pl.BlockSpec((rows // n_blocks, w.shape[1]), lambda i: (jnp.minimum(i, n_blocks - 1), 0))


def _project(x, gn, w_in, gqa, gka, gqb, gkb, rope, *, keep_all, name, cast=()):
    t = x.shape[0]
    tm = rope[2].shape[0]
    n = t // tm
    cast_specs = [_cast_blocks(w, n) for w in cast]
    tile = lambda w: pl.BlockSpec((tm, w), lambda i: (i, 0))
    if keep_all:
        f32_shapes = [(t * A_HEADS, HEAD_DIM), (t * A_HEADS, HEAD_DIM), (t, B_KV_WIDTH), (t, B_KV_WIDTH)]
        f32_specs = [pl.BlockSpec((tm * A_HEADS, HEAD_DIM), lambda i: (i, 0))] * 2 + [tile(B_KV_WIDTH)] * 2
    else:
        f32_shapes = [(A_WIDTH, A_CTX), (A_WIDTH, A_CTX), (B_KV_WIDTH, B_CTX), (B_KV_WIDTH, B_CTX)]
        f32_specs = [pl.BlockSpec(sh, lambda i: (0, 0)) for sh in f32_shapes]
    ct, st, cr, sr = rope
    if ct.shape[0] == n:
        tile_trig = pl.BlockSpec((1, 1, LANES), lambda i: (i, 0, 0))
    else:
        tile_trig = _const_spec((1, 1, LANES))
    out_shape = ([jax.ShapeDtypeStruct((t, sum(QKV_WIDTHS)), _BF16)]
                 + [jax.ShapeDtypeStruct(sh, _F32) for sh in f32_shapes]
                 + [jax.ShapeDtypeStruct(w.shape, _BF16) for w in cast])
    out_specs = [tile(sum(QKV_WIDTHS))] + f32_specs + cast_specs
    in_specs = [
        tile(D_MODEL), _const_spec((1, D_MODEL)), _const_spec((D_MODEL, IN_COLS)),
        _const_spec((1, A_WIDTH)), _const_spec((1, A_WIDTH)), _const_spec((1, B_WIDTH)),
        _const_spec((1, B_KV_WIDTH)), tile_trig, tile_trig, _const_spec((tm, LANES)), _const_spec((tm, LANES)),
    ] + cast_specs
    return pl.pallas_call(
        functools.partial(_proj_kernel, last_only=not keep_all, n_cast=len(cast)),
        grid=(n,), in_specs=in_specs, out_specs=out_specs, out_shape=out_shape, name=name,
        compiler_params=pltpu.CompilerParams(dimension_semantics=("arbitrary",), vmem_limit_bytes=VMEM_LIMIT),
    )(x, gn, w_in, gqa, gka, gqb, gkb, ct, st, cr, sr, *cast)


def _build_bias(u_ref, bias_ref):
    qrow = lax.broadcasted_iota(jnp.int32, (CHUNK, BIAS_VAR), 0)
    for h in range(A_HEADS):
        prof = u_ref[h:h + 1, :]
        y = jnp.broadcast_to(prof - prof[:, 0:1], (CHUNK, BIAS_VAR))
        y = pltpu.roll(y, BIAS_VAR - (CHUNK - 1), 1)
        for b in range(6):
            y = jnp.where(((qrow >> b) & 1) == 1, pltpu.roll(y, 1 << b, 1), y)
        bias_ref[h // 4, (h % 4) * CHUNK:(h % 4 + 1) * CHUNK, :] = y[:, :BIAS_COLS] * LOG2E


_NT = (((1,), (1,)), ((), ()))
_NN = (((1,), (0,)), ((), ()))


def _scores(qa, qb, keys_a, keys_b, bias_ref, lim_a, lim_b):
    out = []
    head_of_lane = lax.broadcasted_iota(jnp.int32, (CHUNK, 2 * LANES), 1) // HEAD_DIM
    for g in range(2):
        qg = qa[:, 2 * LANES * g:2 * LANES * (g + 1)]
        qbd = jnp.concatenate([jnp.where(head_of_lane == r, qg, jnp.zeros_like(qg)) for r in range(4)], axis=0)
        pieces, off = [], 0
        for keys, on_rows in keys_a(g):
            s = lax.dot_general(qbd, keys, _NT if on_rows else _NN, preferred_element_type=_F32)
            n = s.shape[1]
            if off + n > A_WIN - BIAS_COLS:
                lo_col = max(A_WIN - BIAS_COLS - off, 0)
                biased = s[:, lo_col:] + bias_ref[g, :, off + lo_col - (A_WIN - BIAS_COLS):off + n - (A_WIN - BIAS_COLS)]
                s = biased if lo_col == 0 else jnp.concatenate([s[:, :lo_col], biased], axis=1)
            if lim_a:
                s = jnp.where(lax.broadcasted_iota(jnp.int32, s.shape, 1) >= lim_a, s, NEG)
            pieces.append(s)
            off += n
        out.append(pieces)
    lo = lax.broadcasted_iota(jnp.int32, (CHUNK, LANES), 1) < HEAD_DIM
    for g in range(B_KV_HEADS):
        rows = []
        for r in range(B_GROUP):
            hd = B_GROUP * g + r
            qp = qb[:, LANES * (hd // 2):LANES * (hd // 2 + 1)]
            keep = lo if hd % 2 == 0 else jnp.logical_not(lo)
            rows.append(jnp.where(keep, qp, jnp.zeros_like(qp)))
        qs = jnp.concatenate(rows, axis=0)
        pieces = []
        for keys, on_rows in keys_b(g):
            s = lax.dot_general(qs, keys, _NT if on_rows else _NN, preferred_element_type=_F32)
            if lim_b:
                s = jnp.where(lax.broadcasted_iota(jnp.int32, s.shape, 1) >= lim_b, s, NEG)
            pieces.append(s)
        out.append(pieces)
    return out


def _row_reduce(pieces, combine, reduce, fill):
    acc = None
    for s in pieces:
        for c in range(0, s.shape[1], LANES):
            tile = s[:, c:c + LANES]
            if tile.shape[1] < LANES:
                pad = jnp.full((tile.shape[0], LANES - tile.shape[1]), fill, tile.dtype)
                tile = jnp.concatenate([tile, pad], axis=1)
            acc = tile if acc is None else combine(acc, tile)
    return reduce(acc, axis=1, keepdims=True)


def _softmax_values(pieces, vals, sink):
    m = _row_reduce(pieces, jnp.maximum, jnp.max, -jnp.inf)
    if sink is not None:
        m = jnp.maximum(m, sink)
    es = [jnp.exp2(s - m) for s in pieces]
    l = _row_reduce(es, jnp.add, jnp.sum, 0.0)
    if sink is not None:
        l = l + jnp.exp2(sink - m)
    acc = None
    for e, (v, on_rows) in zip(es, vals):
        part = lax.dot_general(e.astype(_BF16), v, _NN if on_rows else _NT, preferred_element_type=_F32)
        acc = part if acc is None else acc + part
    return acc / l


def _softmax_pv(scores, vals_a, vals_b, sink_ref):
    outs = []
    head_of_lane = lax.broadcasted_iota(jnp.int32, (CHUNK, 2 * LANES), 1) // HEAD_DIM
    for g in range(2):
        of = _softmax_values(scores[g], vals_a(g), None)
        og = of[3 * CHUNK:4 * CHUNK]
        for r in (2, 1, 0):
            og = jnp.where(head_of_lane == r, of[r * CHUNK:(r + 1) * CHUNK], og)
        outs.append(og)
    lo = lax.broadcasted_iota(jnp.int32, (CHUNK, LANES), 1) < HEAD_DIM
    for g in range(B_KV_HEADS):
        of = _softmax_values(scores[2 + g], vals_b(g), sink_ref[g] * LOG2E)
        outs.append(jnp.where(lo, of[0:CHUNK], of[CHUNK:2 * CHUNK]))
        outs.append(jnp.where(lo, of[2 * CHUNK:3 * CHUNK], of[3 * CHUNK:4 * CHUNK]))
    return jnp.concatenate(outs, axis=1)


def _attn_prompt_kernel(qkv_ref, u_ref, sink_ref, o_ref, kwa, vwa, kwb, vwb, bias):
    qa_ref, ka_ref, va_ref, qb_ref, kb_ref, vb_ref = _split_qkv(qkv_ref)
    i = pl.program_id(0)
    tm = qa_ref.shape[0]
    n_chunks = tm // CHUNK

    @pl.when(i == 0)
    def _():
        kwa[0:A_CTX, :] = jnp.zeros((A_CTX, A_WIDTH), _BF16)
        vwa[0:A_CTX, :] = jnp.zeros((A_CTX, A_WIDTH), _BF16)
        kwb[0:B_CTX, :] = jnp.zeros((B_CTX, 2 * B_KV_WIDTH), _BF16)
        vwb[0:B_CTX, :] = jnp.zeros((B_CTX, 2 * B_KV_WIDTH), _BF16)
        _build_bias(u_ref, bias)

    @pl.when(i > 0)
    def _():
        kwa[0:A_CTX, :] = kwa[tm:tm + A_CTX, :]
        vwa[0:A_CTX, :] = vwa[tm:tm + A_CTX, :]
        kwb[0:B_CTX, :] = kwb[tm:tm + B_CTX, :]
        vwb[0:B_CTX, :] = vwb[tm:tm + B_CTX, :]

    kwa[A_CTX:A_CTX + tm, :] = ka_ref[...]
    vwa[A_CTX:A_CTX + tm, :] = va_ref[...]
    kwb[B_CTX:B_CTX + tm, :] = kb_ref[...]
    vwb[B_CTX:B_CTX + tm, :] = vb_ref[...]

    def window(ref, r0, rows, width):
        return lambda g: [(ref[r0:r0 + rows, width * g:width * (g + 1)], True)]

    def run(masked):
        def scores(c):
            r0 = c * CHUNK
            lim_a = max(A_LEFT_CHUNKS - c, 0) * CHUNK if masked else 0
            lim_b = max(B_CTX // CHUNK - c, 0) * CHUNK if masked else 0
            return _scores(qa_ref[r0:r0 + CHUNK, :], qb_ref[r0:r0 + CHUNK, :],
                           window(kwa, r0, A_WIN, 2 * LANES), window(kwb, r0, B_WIN, LANES), bias, lim_a, lim_b)

        nxt = scores(0)
        for c in range(n_chunks):
            cur = nxt
            if c + 1 < n_chunks:
                nxt = scores(c + 1)
            r0 = c * CHUNK
            o = _softmax_pv(cur, window(vwa, r0, A_WIN, 2 * LANES), window(vwb, r0, B_WIN, LANES), sink_ref)
            o_ref[r0:r0 + CHUNK, :] = o.astype(_BF16)

    @pl.when(i == 0)
    def _():
        run(True)

    @pl.when(i > 0)
    def _():
        run(False)


def _attn_sample_kernel(qkv_ref, cak_ref, cav_ref, cbk_ref, cbv_ref, u_ref, sink_ref, o_ref, bias):
    qa_ref, ka_ref, va_ref, qb_ref, kb_ref, vb_ref = _split_qkv(qkv_ref)
    n_streams = cak_ref.shape[0]

    @pl.when(pl.program_id(0) == 0)
    def _():
        _build_bias(u_ref, bias)

    def pieces_a(cache_ref, new_ref, b):
        rows = slice(b * CHUNK, (b + 1) * CHUNK)

        def get(g):
            cols = slice(2 * LANES * g, 2 * LANES * (g + 1))
            return [(cache_ref[b, cols, :].astype(_BF16), False), (new_ref[rows, cols], True)]
        return get

    def pieces_b(cache_ref, new_ref, b):
        rows = slice(b * CHUNK, (b + 1) * CHUNK)

        def get(g):
            past = cache_ref[b, HEAD_DIM * g:HEAD_DIM * (g + 1), :].astype(_BF16)
            return [(jnp.concatenate([past, past], axis=0), False), (new_ref[rows, LANES * g:LANES * (g + 1)], True)]
        return get

    def scores(b):
        rows = slice(b * CHUNK, (b + 1) * CHUNK)
        return _scores(qa_ref[rows, :], qb_ref[rows, :], pieces_a(cak_ref, ka_ref, b), pieces_b(cbk_ref, kb_ref, b),
                       bias, 0, 0)

    nxt = scores(0)
    for b in range(n_streams):
        cur = nxt
        if b + 1 < n_streams:
            nxt = scores(b + 1)
        o = _softmax_pv(cur, pieces_a(cav_ref, va_ref, b), pieces_b(cbv_ref, vb_ref, b), sink_ref)
        o_ref[b * CHUNK:(b + 1) * CHUNK, :] = o.astype(_BF16)


_BIAS_SCRATCH = pltpu.VMEM((2, 4 * CHUNK, BIAS_COLS), _F32)


def _attend_prompt(qkv, u, sink):
    t = qkv.shape[0]
    tm = ATTN_TILE
    tile = lambda w: pl.BlockSpec((tm, w), lambda i: (i, 0))
    return pl.pallas_call(
        _attn_prompt_kernel,
        grid=(t // tm,),
        in_specs=[tile(sum(QKV_WIDTHS)), _const_spec(u.shape), _const_spec(sink.shape)],
        out_specs=tile(A_WIDTH + B_WIDTH),
        out_shape=jax.ShapeDtypeStruct((t, A_WIDTH + B_WIDTH), _BF16),
        scratch_shapes=[pltpu.VMEM((A_CTX + tm, A_WIDTH), _BF16), pltpu.VMEM((A_CTX + tm, A_WIDTH), _BF16),
                        pltpu.VMEM((B_CTX + tm, 2 * B_KV_WIDTH), _BF16),
                        pltpu.VMEM((B_CTX + tm, 2 * B_KV_WIDTH), _BF16), _BIAS_SCRATCH],
        name="attn_prompt",
        compiler_params=pltpu.CompilerParams(dimension_semantics=("arbitrary",), vmem_limit_bytes=VMEM_LIMIT),
    )(qkv, u, sink)


def _attend_sample(qkv, cak, cav, cbk, cbv, u, sink):
    t = qkv.shape[0]
    sb = SAMPLE_STREAMS
    tile = lambda w: pl.BlockSpec((sb * CHUNK, w), lambda i: (i, 0))
    cache = lambda arr: pl.BlockSpec((sb,) + arr.shape[1:], lambda i: (i, 0, 0))
    return pl.pallas_call(
        _attn_sample_kernel,
        grid=(t // (sb * CHUNK),),
        in_specs=[tile(sum(QKV_WIDTHS)), cache(cak), cache(cav), cache(cbk), cache(cbv),
                  _const_spec(u.shape), _const_spec(sink.shape)],
        out_specs=tile(A_WIDTH + B_WIDTH),
        out_shape=jax.ShapeDtypeStruct((t, A_WIDTH + B_WIDTH), _BF16),
        scratch_shapes=[_BIAS_SCRATCH],
        name="attn_sample",
        compiler_params=pltpu.CompilerParams(dimension_semantics=("arbitrary",), vmem_limit_bytes=VMEM_LIMIT),
    )(qkv, cak, cav, cbk, cbv, u, sink)


def _ffn_kernel(o_ref, x_ref, wo_ref, gn_ref, wgu_ref, cw_ref, cb_ref, wd_ref, st_ref,
                y_ref, cn_ref, carry, act_buf, *, seg_len, carry_rows):
    i = pl.program_id(0)
    tm = x_ref.shape[0]
    n_seg = tm // seg_len

    if carry_rows:
        @pl.when(i == 0)
        def _():
            carry[...] = st_ref[0]

    y1 = x_ref[...] + jnp.dot(o_ref[...], wo_ref[...], preferred_element_type=_F32)
    ms = jnp.mean(y1 * y1, axis=-1, keepdims=True)
    h = (y1 * lax.rsqrt(ms + EPS) * gn_ref[...]).astype(_BF16)

    row8 = lax.broadcasted_iota(jnp.int32, (8, FF_CHUNK), 0)

    def gate_up(c0):
        g = jnp.dot(h, wgu_ref[:, c0:c0 + FF_CHUNK], preferred_element_type=_F32)
        up = jnp.dot(h, wgu_ref[:, D_FF + c0:D_FF + c0 + FF_CHUNK], preferred_element_type=_F32)
        return g, up

    nxt = gate_up(0)
    for c0 in range(0, D_FF, FF_CHUNK):
        cols = slice(c0, c0 + FF_CHUNK)
        g, up = nxt
        if c0 + FF_CHUNK < D_FF:
            nxt = gate_up(c0 + FF_CHUNK)
        g1 = pltpu.roll(g, 1, 0)
        g2 = pltpu.roll(g, 2, 0)
        p1, p2 = [], []
        for s in range(n_seg):
            a = s * seg_len
            prev = carry[:, cols] if carry_rows else st_ref[s, :, cols]
            prev0 = jnp.broadcast_to(prev[0:1, :], (8, FF_CHUNK))
            prev1 = jnp.broadcast_to(prev[1:2, :], (8, FF_CHUNK))
            p1.append(jnp.where(row8 == 0, prev1, g1[a:a + 8]))
            p1.append(g1[a + 8:a + seg_len])
            p2.append(jnp.where(row8 == 0, prev0, jnp.where(row8 == 1, prev1, g2[a:a + 8])))
            p2.append(g2[a + 8:a + seg_len])
            tail = g[a + seg_len - 8:a + seg_len][8 - (CONV_W - 1):, :]
            if carry_rows:
                carry[:, cols] = tail
            else:
                cn_ref[s, :, cols] = tail
        g1 = jnp.concatenate(p1, axis=0)
        g2 = jnp.concatenate(p2, axis=0)
        conv = cb_ref[:, cols] + cw_ref[0:1, cols] * g2 + cw_ref[1:2, cols] * g1 + cw_ref[2:3, cols] * g
        act_buf[:, cols] = (conv * jax.nn.sigmoid(conv) * up).astype(_BF16)
    y_ref[...] = y1 + jnp.dot(act_buf[...], wd_ref[...], preferred_element_type=_F32)
    if carry_rows:
        cn_ref[0] = carry[...]


def _out_ffn(o, x, w_out, gn, w_gu, conv_w, conv_b, w_down, state, *, carry_rows, name):
    t = x.shape[0]
    tm = TOKEN_TILE
    n = t // tm
    seg_len = tm if carry_rows else CHUNK
    n_seg = tm // seg_len
    tile = lambda w: pl.BlockSpec((tm, w), lambda i: (i, 0))
    if carry_rows:
        st_spec = _const_spec((1, CONV_W - 1, D_FF))
        cn_spec = pl.BlockSpec((1, CONV_W - 1, D_FF), lambda i: (0, 0, 0))
        cn_shape = (1, CONV_W - 1, D_FF)
    else:
        st_spec = pl.BlockSpec((n_seg, CONV_W - 1, D_FF), lambda i: (i, 0, 0))
        cn_spec = pl.BlockSpec((n_seg, CONV_W - 1, D_FF), lambda i: (i, 0, 0))
        cn_shape = (t // seg_len, CONV_W - 1, D_FF)
    return pl.pallas_call(
        functools.partial(_ffn_kernel, seg_len=seg_len, carry_rows=carry_rows),
        grid=(n,),
        in_specs=[tile(A_WIDTH + B_WIDTH), tile(D_MODEL), _const_spec((D_MODEL, D_MODEL)),
                  _const_spec((1, D_MODEL)), _const_spec((D_MODEL, 2 * D_FF)), _const_spec((CONV_W, D_FF)),
                  _const_spec((1, D_FF)), _const_spec((D_FF, D_MODEL)), st_spec],
        out_specs=[tile(D_MODEL), cn_spec],
        out_shape=[jax.ShapeDtypeStruct((t, D_MODEL), _F32), jax.ShapeDtypeStruct(cn_shape, _F32)],
        scratch_shapes=[pltpu.VMEM((CONV_W - 1, D_FF), _F32), pltpu.VMEM((tm, D_FF), _BF16)],
        name=name,
        compiler_params=pltpu.CompilerParams(dimension_semantics=("arbitrary",), vmem_limit_bytes=VMEM_LIMIT),
    )(o, x, w_out, gn, w_gu, conv_w, conv_b, w_down, state)


def _rope_tables(tile_pos, row_pos):
    half = HEAD_DIM // 2
    inv = jnp.power(ROPE_THETA, -jnp.arange(half, dtype=_F32) / half)

    def table(pos):
        ang = pos.astype(_F32)[:, None] * inv[None, :]
        cos, sin = jnp.cos(ang), jnp.sin(ang)
        return jnp.tile(cos, (1, LANES // half)), jnp.tile(jnp.concatenate([-sin, sin], axis=1), (1, LANES // HEAD_DIM))

    ct, st = table(tile_pos)
    cr, sr = table(row_pos)
    return ct[:, None, :], st[:, None, :], cr, sr


def _feature_major(cache):
    n, t, heads, d = cache.shape
    return jnp.transpose(cache, (0, 2, 3, 1)).reshape(n, heads * d, t)


def _token_major(rows, heads):
    t = rows.shape[1]
    return jnp.transpose(rows.reshape(heads, HEAD_DIM, t), (2, 0, 1))[None, None]


def _bias_profile(table):
    n_clip = BIAS_COLS - 1 - MAX_REL
    head = jnp.broadcast_to(table[:, 2 * MAX_REL:], (A_HEADS, n_clip))
    return jnp.concatenate([head, table[:, ::-1][:, :BIAS_VAR - n_clip]], axis=1)


def kernel(x_prompt, x_sample, cache_a_k, cache_a_v, cache_b_k, cache_b_v, state_conv, norm_attn, w_in, q_norm_a,
           k_norm_a, rel_bias_a, q_norm_b, k_norm_b, sinks_b, w_out, norm_ffn, w_gate_up, conv_w, conv_b, w_down):
    assert norm_attn.shape[0] == 1
    bsz, seq, _ = x_prompt.shape
    dec_b, dec_s, _ = x_sample.shape
    assert bsz == 1 and dec_s == CHUNK and (dec_b * dec_s) % TOKEN_TILE == 0
    assert seq % TOKEN_TILE == 0 and seq % PROMPT_PROJ_TILE == 0 and seq % ATTN_TILE == 0
    assert cache_a_k.shape[2] == A_CTX and cache_b_k.shape[2] == B_CTX

    w_in_b = w_in[0].astype(_BF16)
    gn_a = norm_attn[0][None, :]
    gn_f = norm_ffn[0][None, :]
    gqa = jnp.tile(q_norm_a[0], A_HEADS)[None, :]
    gka = jnp.tile(k_norm_a[0], A_HEADS)[None, :]
    gqb = jnp.tile(q_norm_b[0], B_HEADS)[None, :]
    gkb = jnp.tile(k_norm_b[0], B_KV_HEADS)[None, :]
    u = _bias_profile(rel_bias_a[0])
    sink = jnp.repeat(sinks_b[0], CHUNK).reshape(B_KV_HEADS, B_GROUP * CHUNK, 1)
    cw = conv_w[0]
    cb = conv_b[0][None, :]

    rope_p = _rope_tables(jnp.arange(0, seq, PROMPT_PROJ_TILE), jnp.arange(PROMPT_PROJ_TILE))
    rope_s = _rope_tables(jnp.full((1,), PAST_LEN), jnp.arange(TOKEN_TILE) % dec_s)

    xp = x_prompt.reshape(seq, D_MODEL)
    qkv, kaf, vaf, kbf, vbf, w_out_b, w_gu_b, w_down_b = _project(
        xp, gn_a, w_in_b, gqa, gka, gqb, gkb, rope_p, keep_all=False, name="proj_prompt",
        cast=(w_out[0], w_gate_up[0], w_down[0]))
    o_p = _attend_prompt(qkv, u, sink)
    zero_state = jnp.zeros((1, CONV_W - 1, D_FF), _F32)
    y_p, cn_p = _out_ffn(o_p, xp, w_out_b, gn_f, w_gu_b, cw, cb, w_down_b, zero_state,
                         carry_rows=True, name="ffn_prompt")

    xs = x_sample.reshape(dec_b * dec_s, D_MODEL)
    qkv, kaf_s, vaf_s, kbf_s, vbf_s = _project(
        xs, gn_a, w_in_b, gqa, gka, gqb, gkb, rope_s, keep_all=True, name="proj_sample")
    o_s = _attend_sample(qkv, _feature_major(cache_a_k[0]), _feature_major(cache_a_v[0]),
                         _feature_major(cache_b_k[0]), _feature_major(cache_b_v[0]), u, sink)
    y_s, cn_s = _out_ffn(o_s, xs, w_out_b, gn_f, w_gu_b, cw, cb, w_down_b, state_conv[0],
                         carry_rows=False, name="ffn_sample")

    return (
        y_p.reshape(1, seq, D_MODEL),
        y_s.reshape(dec_b, dec_s, D_MODEL),
        _token_major(kaf, A_HEADS),
        _token_major(vaf, A_HEADS),
        _token_major(kbf, B_KV_HEADS),
        _token_major(vbf, B_KV_HEADS),
        cn_p.reshape(1, 1, CONV_W - 1, D_FF),
        kaf_s.reshape(1, dec_b, dec_s, A_HEADS, HEAD_DIM),
        vaf_s.reshape(1, dec_b, dec_s, A_HEADS, HEAD_DIM),
        kbf_s.reshape(1, dec_b, dec_s, B_KV_HEADS, HEAD_DIM),
        vbf_s.reshape(1, dec_b, dec_s, B_KV_HEADS, HEAD_DIM),
        cn_s.reshape(1, dec_b, CONV_W - 1, D_FF),
    )
```

```python
import functools

import jax
import jax.numpy as jnp
from jax import lax
from jax.experimental import pallas as pl
from jax.experimental.pallas import tpu as pltpu

D_MODEL = 1024
CHUNK = 64
HEAD_DIM = 64
EPS = 1e-6
SCALE = HEAD_DIM ** -0.5
LOG2E = 1.4426950408889634
Q_SCALE = SCALE * LOG2E
NEG = -1e30
PAST_LEN = 1024
A_HEADS = 8
A_LEFT_CHUNKS = 8
A_CTX = A_LEFT_CHUNKS * CHUNK
A_WIN = A_CTX + CHUNK
MAX_REL = 128
A_WIDTH = A_HEADS * HEAD_DIM
B_HEADS = 8
B_KV_HEADS = 2
B_GROUP = B_HEADS // B_KV_HEADS
B_CTX = 128
B_WIN = B_CTX + CHUNK
B_WIDTH = B_HEADS * HEAD_DIM
B_KV_WIDTH = B_KV_HEADS * HEAD_DIM
ROPE_THETA = 10000.0
IN_COLS = 3 * A_WIDTH + B_WIDTH + 2 * B_KV_WIDTH
D_FF = 2816
CONV_W = 3

LANES = 128
TOKEN_TILE = 512
PROMPT_PROJ_TILE = 1024
ATTN_TILE = 1024
SAMPLE_STREAMS = 8
FF_CHUNK = 256
BIAS_COLS = MAX_REL + CHUNK
BIAS_VAR = 256
VMEM_LIMIT = 56 * 1024 * 1024

_BF16 = jnp.bfloat16
_F32 = jnp.float32


def _const_spec(shape):
    nd = len(shape)
    return pl.BlockSpec(shape, lambda i: (0,) * nd, pipeline_mode=pl.Buffered(1))


def _head_sumsq(t, bd):
    sq = (t * t).astype(_BF16)
    w = t.shape[1]
    if w <= 2 * LANES:
        return jnp.dot(sq, bd[:w, :w], preferred_element_type=_F32)
    parts = [jnp.dot(sq[:, c:c + 2 * LANES], bd, preferred_element_type=_F32) for c in range(0, w, 2 * LANES)]
    return jnp.concatenate(parts, axis=1)


def _head_norm(t, bd, gain):
    ss = _head_sumsq(t, bd)
    return t * lax.rsqrt(ss * (1.0 / HEAD_DIM) + EPS) * gain


def _rope(t, cos, sin_signed):
    outs = []
    for c in range(0, t.shape[1], LANES):
        tc = t[:, c:c + LANES]
        lane = lax.broadcasted_iota(jnp.int32, tc.shape, 1)
        first_half = (lane & (HEAD_DIM - 1)) < HEAD_DIM // 2
        rot = jnp.where(first_half, pltpu.roll(tc, LANES - HEAD_DIM // 2, 1), pltpu.roll(tc, HEAD_DIM // 2, 1))
        outs.append(tc * cos + rot * sin_signed)
    return outs[0] if len(outs) == 1 else jnp.concatenate(outs, axis=1)


def _rep_pair(t):
    rolled = pltpu.roll(t, HEAD_DIM, 1)
    lo = lax.broadcasted_iota(jnp.int32, t.shape, 1) < HEAD_DIM
    return jnp.concatenate([jnp.where(lo, t, rolled), jnp.where(lo, rolled, t)], axis=1)


def _store_heads(ref, t):
    n, w = t.shape
    heads = w // HEAD_DIM
    for hd in range(heads):
        ref[pl.ds(hd, n, stride=heads), :] = t[:, hd * HEAD_DIM:(hd + 1) * HEAD_DIM]


def _proj_kernel(x_ref, gn_ref, w_ref, gqa_ref, gka_ref, gqb_ref, gkb_ref, ct_ref, st_ref, cr_ref, sr_ref, *rest,
                 last_only, n_cast):
    cast_in, rest = rest[:n_cast], rest[n_cast:]
    qa_ref, ka_ref, va_ref, qb_ref, kb_ref, vb_ref, kaf_ref, vaf_ref, kbf_ref, vbf_ref = rest[:10]
    for src, dst in zip(cast_in, rest[10:]):
        dst[...] = src[...].astype(_BF16)
    tm = x_ref.shape[0]
    x = x_ref[...]
    ms = jnp.mean(x * x, axis=-1, keepdims=True)
    h = (x * lax.rsqrt(ms + EPS) * gn_ref[...]).astype(_BF16)

    r = lax.broadcasted_iota(jnp.int32, (2 * LANES, 2 * LANES), 0) // HEAD_DIM
    c = lax.broadcasted_iota(jnp.int32, (2 * LANES, 2 * LANES), 1) // HEAD_DIM
    bd = jnp.where(r == c, 1.0, 0.0).astype(_BF16)
    ct, st, cr, sr = ct_ref[0], st_ref[0], cr_ref[...], sr_ref[...]
    cos = ct * cr - st * sr
    sin = st * cr + ct * sr

    def proj(lo, width):
        return jnp.dot(h, w_ref[:, lo:lo + width], preferred_element_type=_F32)

    o_ka, o_va, o_qb, o_kvb = A_WIDTH, 2 * A_WIDTH, 3 * A_WIDTH, 3 * A_WIDTH + B_WIDTH
    p_qa = proj(0, A_WIDTH)
    p_ka = proj(o_ka, A_WIDTH)
    qa_ref[...] = (_head_norm(p_qa, bd, gqa_ref[...]) * Q_SCALE).astype(_BF16)
    va = proj(o_va, A_WIDTH)
    ka = _head_norm(p_ka, bd, gka_ref[...])
    ka_ref[...] = ka.astype(_BF16)
    p_qb = proj(o_qb, B_WIDTH)
    va_ref[...] = va.astype(_BF16)
    p_kvb = proj(o_kvb, 2 * B_KV_WIDTH)
    qb_ref[...] = (_rope(_head_norm(p_qb, bd, gqb_ref[...]), cos, sin) * Q_SCALE).astype(_BF16)
    kb = _rope(_head_norm(p_kvb[:, :B_KV_WIDTH], bd, gkb_ref[...]), cos, sin)
    vb = p_kvb[:, B_KV_WIDTH:]
    kb_ref[...] = _rep_pair(kb).astype(_BF16)
    vb_ref[...] = _rep_pair(vb).astype(_BF16)

    if last_only:
        @pl.when(pl.program_id(0) == pl.num_programs(0) - 1)
        def _():
            kaf_ref[...] = ka[tm - A_CTX:, :].T
            vaf_ref[...] = va[tm - A_CTX:, :].T
            kbf_ref[...] = kb[tm - B_CTX:, :].T
            vbf_ref[...] = vb[tm - B_CTX:, :].T
    else:
        _store_heads(kaf_ref, ka)
        _store_heads(vaf_ref, va)
        kbf_ref[...] = kb
        vbf_ref[...] = vb


def _cast_blocks(w, n_steps):
    rows = w.shape[0]
    n_blocks = max(d for d in range(1, n_steps + 1) if rows % d == 0 and (rows // d) % 16 == 0)
    return pl.BlockSpec((rows // n_blocks, w.shape[1]), lambda i: (jnp.minimum(i, n_blocks - 1), 0))


def _project(x, gn, w_in, gqa, gka, gqb, gkb, rope, *, keep_all, name, cast=()):
    t = x.shape[0]
    tm = rope[2].shape[0]
    n = t // tm
    cast_specs = [_cast_blocks(w, n) for w in cast]
    tile = lambda w: pl.BlockSpec((tm, w), lambda i: (i, 0))
    if keep_all:
        f32_shapes = [(t * A_HEADS, HEAD_DIM), (t * A_HEADS, HEAD_DIM), (t, B_KV_WIDTH), (t, B_KV_WIDTH)]
        f32_specs = [pl.BlockSpec((tm * A_HEADS, HEAD_DIM), lambda i: (i, 0))] * 2 + [tile(B_KV_WIDTH)] * 2
    else:
        f32_shapes = [(A_WIDTH, A_CTX), (A_WIDTH, A_CTX), (B_KV_WIDTH, B_CTX), (B_KV_WIDTH, B_CTX)]
        f32_specs = [pl.BlockSpec(sh, lambda i: (0, 0)) for sh in f32_shapes]
    ct, st, cr, sr = rope
    if ct.shape[0] == n:
        tile_trig = pl.BlockSpec((1, 1, LANES), lambda i: (i, 0, 0))
    else:
        tile_trig = _const_spec((1, 1, LANES))
    out_shape = [
        jax.ShapeDtypeStruct((t, A_WIDTH), _BF16), jax.ShapeDtypeStruct((t, A_WIDTH), _BF16),
        jax.ShapeDtypeStruct((t, A_WIDTH), _BF16), jax.ShapeDtypeStruct((t, B_WIDTH), _BF16),
        jax.ShapeDtypeStruct((t, 2 * B_KV_WIDTH), _BF16), jax.ShapeDtypeStruct((t, 2 * B_KV_WIDTH), _BF16),
    ] + [jax.ShapeDtypeStruct(sh, _F32) for sh in f32_shapes] + [jax.ShapeDtypeStruct(w.shape, _BF16) for w in cast]
    out_specs = [
        tile(A_WIDTH), tile(A_WIDTH), tile(A_WIDTH), tile(B_WIDTH), tile(2 * B_KV_WIDTH), tile(2 * B_KV_WIDTH),
    ] + f32_specs + cast_specs
    in_specs = [
        tile(D_MODEL), _const_spec((1, D_MODEL)), _const_spec((D_MODEL, IN_COLS)),
        _const_spec((1, A_WIDTH)), _const_spec((1, A_WIDTH)), _const_spec((1, B_WIDTH)),
        _const_spec((1, B_KV_WIDTH)), tile_trig, tile_trig, _const_spec((tm, LANES)), _const_spec((tm, LANES)),
    ] + cast_specs
    return pl.pallas_call(
        functools.partial(_proj_kernel, last_only=not keep_all, n_cast=len(cast)),
        grid=(n,), in_specs=in_specs, out_specs=out_specs, out_shape=out_shape, name=name,
        compiler_params=pltpu.CompilerParams(dimension_semantics=("arbitrary",), vmem_limit_bytes=VMEM_LIMIT),
    )(x, gn, w_in, gqa, gka, gqb, gkb, ct, st, cr, sr, *cast)


def _build_bias(u_ref, bias_ref):
    qrow = lax.broadcasted_iota(jnp.int32, (CHUNK, BIAS_VAR), 0)
    for h in range(A_HEADS):
        prof = u_ref[h:h + 1, :]
        y = jnp.broadcast_to(prof - prof[:, 0:1], (CHUNK, BIAS_VAR))
        y = pltpu.roll(y, BIAS_VAR - (CHUNK - 1), 1)
        for b in range(6):
            y = jnp.where(((qrow >> b) & 1) == 1, pltpu.roll(y, 1 << b, 1), y)
        bias_ref[h // 4, (h % 4) * CHUNK:(h % 4 + 1) * CHUNK, :] = y[:, :BIAS_COLS] * LOG2E


_NT = (((1,), (1,)), ((), ()))
_NN = (((1,), (0,)), ((), ()))


def _scores(qa, qb, keys_a, keys_b, bias_ref, lim_a, lim_b):
    out = []
    head_of_lane = lax.broadcasted_iota(jnp.int32, (CHUNK, 2 * LANES), 1) // HEAD_DIM
    for g in range(2):
        qg = qa[:, 2 * LANES * g:2 * LANES * (g + 1)]
        qbd = jnp.concatenate([jnp.where(head_of_lane == r, qg, jnp.zeros_like(qg)) for r in range(4)], axis=0)
        pieces, off = [], 0
        for keys, on_rows in keys_a(g):
            s = lax.dot_general(qbd, keys, _NT if on_rows else _NN, preferred_element_type=_F32)
            n = s.shape[1]
            if off + n > A_WIN - BIAS_COLS:
                lo_col = max(A_WIN - BIAS_COLS - off, 0)
                biased = s[:, lo_col:] + bias_ref[g, :, off + lo_col - (A_WIN - BIAS_COLS):off + n - (A_WIN - BIAS_COLS)]
                s = biased if lo_col == 0 else jnp.concatenate([s[:, :lo_col], biased], axis=1)
            if lim_a:
                s = jnp.where(lax.broadcasted_iota(jnp.int32, s.shape, 1) >= lim_a, s, NEG)
            pieces.append(s)
            off += n
        out.append(pieces)
    lo = lax.broadcasted_iota(jnp.int32, (CHUNK, LANES), 1) < HEAD_DIM
    for g in range(B_KV_HEADS):
        rows = []
        for r in range(B_GROUP):
            hd = B_GROUP * g + r
            qp = qb[:, LANES * (hd // 2):LANES * (hd // 2 + 1)]
            keep = lo if hd % 2 == 0 else jnp.logical_not(lo)
            rows.append(jnp.where(keep, qp, jnp.zeros_like(qp)))
        qs = jnp.concatenate(rows, axis=0)
        pieces = []
        for keys, on_rows in keys_b(g):
            s = lax.dot_general(qs, keys, _NT if on_rows else _NN, preferred_element_type=_F32)
            if lim_b:
                s = jnp.where(lax.broadcasted_iota(jnp.int32, s.shape, 1) >= lim_b, s, NEG)
            pieces.append(s)
        out.append(pieces)
    return out


def _row_reduce(pieces, combine, reduce, fill):
    acc = None
    for s in pieces:
        for c in range(0, s.shape[1], LANES):
            tile = s[:, c:c + LANES]
            if tile.shape[1] < LANES:
                pad = jnp.full((tile.shape[0], LANES - tile.shape[1]), fill, tile.dtype)
                tile = jnp.concatenate([tile, pad], axis=1)
            acc = tile if acc is None else combine(acc, tile)
    return reduce(acc, axis=1, keepdims=True)


def _softmax_values(pieces, vals, sink):
    m = _row_reduce(pieces, jnp.maximum, jnp.max, -jnp.inf)
    if sink is not None:
        m = jnp.maximum(m, sink)
    es = [jnp.exp2(s - m) for s in pieces]
    l = _row_reduce(es, jnp.add, jnp.sum, 0.0)
    if sink is not None:
        l = l + jnp.exp2(sink - m)
    acc = None
    for e, (v, on_rows) in zip(es, vals):
        part = lax.dot_general(e.astype(_BF16), v, _NN if on_rows else _NT, preferred_element_type=_F32)
        acc = part if acc is None else acc + part
    return acc / l


def _softmax_pv(scores, vals_a, vals_b, sink_ref):
    outs = []
    head_of_lane = lax.broadcasted_iota(jnp.int32, (CHUNK, 2 * LANES), 1) // HEAD_DIM
    for g in range(2):
        of = _softmax_values(scores[g], vals_a(g), None)
        og = of[3 * CHUNK:4 * CHUNK]
        for r in (2, 1, 0):
            og = jnp.where(head_of_lane == r, of[r * CHUNK:(r + 1) * CHUNK], og)
        outs.append(og)
    lo = lax.broadcasted_iota(jnp.int32, (CHUNK, LANES), 1) < HEAD_DIM
    for g in range(B_KV_HEADS):
        of = _softmax_values(scores[2 + g], vals_b(g), sink_ref[g] * LOG2E)
        outs.append(jnp.where(lo, of[0:CHUNK], of[CHUNK:2 * CHUNK]))
        outs.append(jnp.where(lo, of[2 * CHUNK:3 * CHUNK], of[3 * CHUNK:4 * CHUNK]))
    return jnp.concatenate(outs, axis=1)


def _attn_prompt_kernel(qa_ref, ka_ref, va_ref, qb_ref, kb_ref, vb_ref, u_ref, sink_ref, o_ref,
                        kwa, vwa, kwb, vwb, bias):
    i = pl.program_id(0)
    tm = qa_ref.shape[0]
    n_chunks = tm // CHUNK

    @pl.when(i == 0)
    def _():
        kwa[0:A_CTX, :] = jnp.zeros((A_CTX, A_WIDTH), _BF16)
        vwa[0:A_CTX, :] = jnp.zeros((A_CTX, A_WIDTH), _BF16)
        kwb[0:B_CTX, :] = jnp.zeros((B_CTX, 2 * B_KV_WIDTH), _BF16)
        vwb[0:B_CTX, :] = jnp.zeros((B_CTX, 2 * B_KV_WIDTH), _BF16)
        _build_bias(u_ref, bias)

    @pl.when(i > 0)
    def _():
        kwa[0:A_CTX, :] = kwa[tm:tm + A_CTX, :]
        vwa[0:A_CTX, :] = vwa[tm:tm + A_CTX, :]
        kwb[0:B_CTX, :] = kwb[tm:tm + B_CTX, :]
        vwb[0:B_CTX, :] = vwb[tm:tm + B_CTX, :]

    kwa[A_CTX:A_CTX + tm, :] = ka_ref[...]
    vwa[A_CTX:A_CTX + tm, :] = va_ref[...]
    kwb[B_CTX:B_CTX + tm, :] = kb_ref[...]
    vwb[B_CTX:B_CTX + tm, :] = vb_ref[...]

    def window(ref, r0, rows, width):
        return lambda g: [(ref[r0:r0 + rows, width * g:width * (g + 1)], True)]

    def run(masked):
        def scores(c):
            r0 = c * CHUNK
            lim_a = max(A_LEFT_CHUNKS - c, 0) * CHUNK if masked else 0
            lim_b = max(B_CTX // CHUNK - c, 0) * CHUNK if masked else 0
            return _scores(qa_ref[r0:r0 + CHUNK, :], qb_ref[r0:r0 + CHUNK, :],
                           window(kwa, r0, A_WIN, 2 * LANES), window(kwb, r0, B_WIN, LANES), bias, lim_a, lim_b)

        nxt = scores(0)
        for c in range(n_chunks):
            cur = nxt
            if c + 1 < n_chunks:
                nxt = scores(c + 1)
            r0 = c * CHUNK
            o = _softmax_pv(cur, window(vwa, r0, A_WIN, 2 * LANES), window(vwb, r0, B_WIN, LANES), sink_ref)
            o_ref[r0:r0 + CHUNK, :] = o.astype(_BF16)

    @pl.when(i == 0)
    def _():
        run(True)

    @pl.when(i > 0)
    def _():
        run(False)


def _attn_sample_kernel(qa_ref, ka_ref, va_ref, qb_ref, kb_ref, vb_ref, cak_ref, cav_ref, cbk_ref, cbv_ref,
                        u_ref, sink_ref, o_ref, bias):
    n_streams = cak_ref.shape[0]

    @pl.when(pl.program_id(0) == 0)
    def _():
        _build_bias(u_ref, bias)

    def pieces_a(cache_ref, new_ref, b):
        rows = slice(b * CHUNK, (b + 1) * CHUNK)

        def get(g):
            cols = slice(2 * LANES * g, 2 * LANES * (g + 1))
            return [(cache_ref[b, cols, :].astype(_BF16), False), (new_ref[rows, cols], True)]
        return get

    def pieces_b(cache_ref, new_ref, b):
        rows = slice(b * CHUNK, (b + 1) * CHUNK)

        def get(g):
            past = cache_ref[b, HEAD_DIM * g:HEAD_DIM * (g + 1), :].astype(_BF16)
            return [(jnp.concatenate([past, past], axis=0), False), (new_ref[rows, LANES * g:LANES * (g + 1)], True)]
        return get

    def scores(b):
        rows = slice(b * CHUNK, (b + 1) * CHUNK)
        return _scores(qa_ref[rows, :], qb_ref[rows, :], pieces_a(cak_ref, ka_ref, b), pieces_b(cbk_ref, kb_ref, b),
                       bias, 0, 0)

    nxt = scores(0)
    for b in range(n_streams):
        cur = nxt
        if b + 1 < n_streams:
            nxt = scores(b + 1)
        o = _softmax_pv(cur, pieces_a(cav_ref, va_ref, b), pieces_b(cbv_ref, vb_ref, b), sink_ref)
        o_ref[b * CHUNK:(b + 1) * CHUNK, :] = o.astype(_BF16)


_BIAS_SCRATCH = pltpu.VMEM((2, 4 * CHUNK, BIAS_COLS), _F32)


def _attend_prompt(qa, ka, va, qb, kb, vb, u, sink):
    t = qa.shape[0]
    tm = ATTN_TILE
    tile = lambda w: pl.BlockSpec((tm, w), lambda i: (i, 0))
    return pl.pallas_call(
        _attn_prompt_kernel,
        grid=(t // tm,),
        in_specs=[tile(A_WIDTH), tile(A_WIDTH), tile(A_WIDTH), tile(B_WIDTH), tile(2 * B_KV_WIDTH),
                  tile(2 * B_KV_WIDTH), _const_spec(u.shape), _const_spec(sink.shape)],
        out_specs=tile(A_WIDTH + B_WIDTH),
        out_shape=jax.ShapeDtypeStruct((t, A_WIDTH + B_WIDTH), _BF16),
        scratch_shapes=[pltpu.VMEM((A_CTX + tm, A_WIDTH), _BF16), pltpu.VMEM((A_CTX + tm, A_WIDTH), _BF16),
                        pltpu.VMEM((B_CTX + tm, 2 * B_KV_WIDTH), _BF16),
                        pltpu.VMEM((B_CTX + tm, 2 * B_KV_WIDTH), _BF16), _BIAS_SCRATCH],
        name="attn_prompt",
        compiler_params=pltpu.CompilerParams(dimension_semantics=("arbitrary",), vmem_limit_bytes=VMEM_LIMIT),
    )(qa, ka, va, qb, kb, vb, u, sink)


def _attend_sample(qa, ka, va, qb, kb, vb, cak, cav, cbk, cbv, u, sink):
    t = qa.shape[0]
    sb = SAMPLE_STREAMS
    tile = lambda w: pl.BlockSpec((sb * CHUNK, w), lambda i: (i, 0))
    cache = lambda arr: pl.BlockSpec((sb,) + arr.shape[1:], lambda i: (i, 0, 0))
    return pl.pallas_call(
        _attn_sample_kernel,
        grid=(t // (sb * CHUNK),),
        in_specs=[tile(A_WIDTH), tile(A_WIDTH), tile(A_WIDTH), tile(B_WIDTH), tile(2 * B_KV_WIDTH),
                  tile(2 * B_KV_WIDTH), cache(cak), cache(cav), cache(cbk), cache(cbv),
                  _const_spec(u.shape), _const_spec(sink.shape)],
        out_specs=tile(A_WIDTH + B_WIDTH),
        out_shape=jax.ShapeDtypeStruct((t, A_WIDTH + B_WIDTH), _BF16),
        scratch_shapes=[_BIAS_SCRATCH],
        name="attn_sample",
        compiler_params=pltpu.CompilerParams(dimension_semantics=("arbitrary",), vmem_limit_bytes=VMEM_LIMIT),
    )(qa, ka, va, qb, kb, vb, cak, cav, cbk, cbv, u, sink)


def _ffn_kernel(o_ref, x_ref, wo_ref, gn_ref, wgu_ref, cw_ref, cb_ref, wd_ref, st_ref,
                y_ref, cn_ref, carry, act_buf, *, seg_len, carry_rows):
    i = pl.program_id(0)
    tm = x_ref.shape[0]
    n_seg = tm // seg_len

    if carry_rows:
        @pl.when(i == 0)
        def _():
            carry[...] = st_ref[0]

    y1 = x_ref[...] + jnp.dot(o_ref[...], wo_ref[...], preferred_element_type=_F32)
    ms = jnp.mean(y1 * y1, axis=-1, keepdims=True)
    h = (y1 * lax.rsqrt(ms + EPS) * gn_ref[...]).astype(_BF16)

    row8 = lax.broadcasted_iota(jnp.int32, (8, FF_CHUNK), 0)

    def gate_up(c0):
        g = jnp.dot(h, wgu_ref[:, c0:c0 + FF_CHUNK], preferred_element_type=_F32)
        up = jnp.dot(h, wgu_ref[:, D_FF + c0:D_FF + c0 + FF_CHUNK], preferred_element_type=_F32)
        return g, up

    nxt = gate_up(0)
    for c0 in range(0, D_FF, FF_CHUNK):
        cols = slice(c0, c0 + FF_CHUNK)
        g, up = nxt
        if c0 + FF_CHUNK < D_FF:
            nxt = gate_up(c0 + FF_CHUNK)
        g1 = pltpu.roll(g, 1, 0)
        g2 = pltpu.roll(g, 2, 0)
        p1, p2 = [], []
        for s in range(n_seg):
            a = s * seg_len
            prev = carry[:, cols] if carry_rows else st_ref[s, :, cols]
            prev0 = jnp.broadcast_to(prev[0:1, :], (8, FF_CHUNK))
            prev1 = jnp.broadcast_to(prev[1:2, :], (8, FF_CHUNK))
            p1.append(jnp.where(row8 == 0, prev1, g1[a:a + 8]))
            p1.append(g1[a + 8:a + seg_len])
            p2.append(jnp.where(row8 == 0, prev0, jnp.where(row8 == 1, prev1, g2[a:a + 8])))
            p2.append(g2[a + 8:a + seg_len])
            tail = g[a + seg_len - 8:a + seg_len][8 - (CONV_W - 1):, :]
            if carry_rows:
                carry[:, cols] = tail
            else:
                cn_ref[s, :, cols] = tail
        g1 = jnp.concatenate(p1, axis=0)
        g2 = jnp.concatenate(p2, axis=0)
        conv = cb_ref[:, cols] + cw_ref[0:1, cols] * g2 + cw_ref[1:2, cols] * g1 + cw_ref[2:3, cols] * g
        act_buf[:, cols] = (conv * jax.nn.sigmoid(conv) * up).astype(_BF16)
    y_ref[...] = y1 + jnp.dot(act_buf[...], wd_ref[...], preferred_element_type=_F32)
    if carry_rows:
        cn_ref[0] = carry[...]


def _out_ffn(o, x, w_out, gn, w_gu, conv_w, conv_b, w_down, state, *, carry_rows, name):
    t = x.shape[0]
    tm = TOKEN_TILE
    n = t // tm
    seg_len = tm if carry_rows else CHUNK
    n_seg = tm // seg_len
    tile = lambda w: pl.BlockSpec((tm, w), lambda i: (i, 0))
    if carry_rows:
        st_spec = _const_spec((1, CONV_W - 1, D_FF))
        cn_spec = pl.BlockSpec((1, CONV_W - 1, D_FF), lambda i: (0, 0, 0))
        cn_shape = (1, CONV_W - 1, D_FF)
    else:
        st_spec = pl.BlockSpec((n_seg, CONV_W - 1, D_FF), lambda i: (i, 0, 0))
        cn_spec = pl.BlockSpec((n_seg, CONV_W - 1, D_FF), lambda i: (i, 0, 0))
        cn_shape = (t // seg_len, CONV_W - 1, D_FF)
    return pl.pallas_call(
        functools.partial(_ffn_kernel, seg_len=seg_len, carry_rows=carry_rows),
        grid=(n,),
        in_specs=[tile(A_WIDTH + B_WIDTH), tile(D_MODEL), _const_spec((D_MODEL, D_MODEL)),
                  _const_spec((1, D_MODEL)), _const_spec((D_MODEL, 2 * D_FF)), _const_spec((CONV_W, D_FF)),
                  _const_spec((1, D_FF)), _const_spec((D_FF, D_MODEL)), st_spec],
        out_specs=[tile(D_MODEL), cn_spec],
        out_shape=[jax.ShapeDtypeStruct((t, D_MODEL), _F32), jax.ShapeDtypeStruct(cn_shape, _F32)],
        scratch_shapes=[pltpu.VMEM((CONV_W - 1, D_FF), _F32), pltpu.VMEM((tm, D_FF), _BF16)],
        name=name,
        compiler_params=pltpu.CompilerParams(dimension_semantics=("arbitrary",), vmem_limit_bytes=VMEM_LIMIT),
    )(o, x, w_out, gn, w_gu, conv_w, conv_b, w_down, state)


def _rope_tables(tile_pos, row_pos):
    half = HEAD_DIM // 2
    lane = jnp.arange(LANES)
    inv = jnp.power(ROPE_THETA, -(lane % half).astype(_F32) / half)
    sign = jnp.where(lane % HEAD_DIM < half, -1.0, 1.0).astype(_F32)
    a = tile_pos.astype(_F32)[:, None] * inv[None, :]
    b = row_pos.astype(_F32)[:, None] * inv[None, :]
    return (jnp.cos(a)[:, None, :], (jnp.sin(a) * sign)[:, None, :], jnp.cos(b), jnp.sin(b) * sign)


def _feature_major(cache):
    n, t, heads, d = cache.shape
    return jnp.transpose(cache, (0, 2, 3, 1)).reshape(n, heads * d, t)


def _token_major(rows, heads):
    t = rows.shape[1]
    return jnp.transpose(rows.reshape(heads, HEAD_DIM, t), (2, 0, 1))[None, None]


def _bias_profile(table):
    n_clip = BIAS_COLS - 1 - MAX_REL
    head = jnp.broadcast_to(table[:, 2 * MAX_REL:], (A_HEADS, n_clip))
    return jnp.concatenate([head, table[:, ::-1][:, :BIAS_VAR - n_clip]], axis=1)


def kernel(x_prompt, x_sample, cache_a_k, cache_a_v, cache_b_k, cache_b_v, state_conv, norm_attn, w_in, q_norm_a,
           k_norm_a, rel_bias_a, q_norm_b, k_norm_b, sinks_b, w_out, norm_ffn, w_gate_up, conv_w, conv_b, w_down):
    assert norm_attn.shape[0] == 1
    bsz, seq, _ = x_prompt.shape
    dec_b, dec_s, _ = x_sample.shape
    assert bsz == 1 and dec_s == CHUNK and (dec_b * dec_s) % TOKEN_TILE == 0
    assert seq % TOKEN_TILE == 0 and seq % PROMPT_PROJ_TILE == 0 and seq % ATTN_TILE == 0
    assert cache_a_k.shape[2] == A_CTX and cache_b_k.shape[2] == B_CTX

    w_in_b = w_in[0].astype(_BF16)
    gn_a = norm_attn[0][None, :]
    gn_f = norm_ffn[0][None, :]
    gqa = jnp.tile(q_norm_a[0], A_HEADS)[None, :]
    gka = jnp.tile(k_norm_a[0], A_HEADS)[None, :]
    gqb = jnp.tile(q_norm_b[0], B_HEADS)[None, :]
    gkb = jnp.tile(k_norm_b[0], B_KV_HEADS)[None, :]
    u = _bias_profile(rel_bias_a[0])
    sink = jnp.repeat(sinks_b[0], CHUNK).reshape(B_KV_HEADS, B_GROUP * CHUNK, 1)
    cw = conv_w[0]
    cb = conv_b[0][None, :]

    rope_p = _rope_tables(jnp.arange(0, seq, PROMPT_PROJ_TILE), jnp.arange(PROMPT_PROJ_TILE))
    rope_s = _rope_tables(jnp.full((1,), PAST_LEN), jnp.arange(TOKEN_TILE) % dec_s)

    xp = x_prompt.reshape(seq, D_MODEL)
    qa, ka, va, qb, kb, vb, kaf, vaf, kbf, vbf, w_out_b, w_gu_b, w_down_b = _project(
        xp, gn_a, w_in_b, gqa, gka, gqb, gkb, rope_p, keep_all=False, name="proj_prompt",
        cast=(w_out[0], w_gate_up[0], w_down[0]))
    o_p = _attend_prompt(qa, ka, va, qb, kb, vb, u, sink)
    zero_state = jnp.zeros((1, CONV_W - 1, D_FF), _F32)
    y_p, cn_p = _out_ffn(o_p, xp, w_out_b, gn_f, w_gu_b, cw, cb, w_down_b, zero_state,
                         carry_rows=True, name="ffn_prompt")

    xs = x_sample.reshape(dec_b * dec_s, D_MODEL)
    qa, ka, va, qb, kb, vb, kaf_s, vaf_s, kbf_s, vbf_s = _project(
        xs, gn_a, w_in_b, gqa, gka, gqb, gkb, rope_s, keep_all=True, name="proj_sample")
    o_s = _attend_sample(qa, ka, va, qb, kb, vb, _feature_major(cache_a_k[0]), _feature_major(cache_a_v[0]),
                         _feature_major(cache_b_k[0]), _feature_major(cache_b_v[0]), u, sink)
    y_s, cn_s = _out_ffn(o_s, xs, w_out_b, gn_f, w_gu_b, cw, cb, w_down_b, state_conv[0],
                         carry_rows=False, name="ffn_sample")

    return (
        y_p.reshape(1, seq, D_MODEL),
        y_s.reshape(dec_b, dec_s, D_MODEL),
        _token_major(kaf, A_HEADS),
        _token_major(vaf, A_HEADS),
        _token_major(kbf, B_KV_HEADS),
        _token_major(vbf, B_KV_HEADS),
        cn_p.reshape(1, 1, CONV_W - 1, D_FF),
        kaf_s.reshape(1, dec_b, dec_s, A_HEADS, HEAD_DIM),
        vaf_s.reshape(1, dec_b, dec_s, A_HEADS, HEAD_DIM),
        kbf_s.reshape(1, dec_b, dec_s, B_KV_HEADS, HEAD_DIM),
        vbf_s.reshape(1, dec_b, dec_s, B_KV_HEADS, HEAD_DIM),
        cn_s.reshape(1, dec_b, CONV_W - 1, D_FF),
    )
```

```python
import functools

import jax
import jax.numpy as jnp
from jax import lax
from jax.experimental import pallas as pl
from jax.experimental.pallas import tpu as pltpu

D_MODEL = 1024
CHUNK = 64
HEAD_DIM = 64
EPS = 1e-6
SCALE = HEAD_DIM ** -0.5
LOG2E = 1.4426950408889634
Q_SCALE = SCALE * LOG2E
NEG = -1e30
PAST_LEN = 1024
A_HEADS = 8
A_LEFT_CHUNKS = 8
A_CTX = A_LEFT_CHUNKS * CHUNK
A_WIN = A_CTX + CHUNK
MAX_REL = 128
A_WIDTH = A_HEADS * HEAD_DIM
B_HEADS = 8
B_KV_HEADS = 2
B_GROUP = B_HEADS // B_KV_HEADS
B_CTX = 128
B_WIN = B_CTX + CHUNK
B_WIDTH = B_HEADS * HEAD_DIM
B_KV_WIDTH = B_KV_HEADS * HEAD_DIM
ROPE_THETA = 10000.0
IN_COLS = 3 * A_WIDTH + B_WIDTH + 2 * B_KV_WIDTH
D_FF = 2816
CONV_W = 3

LANES = 128
TOKEN_TILE = 512
PROMPT_PROJ_TILE = 1024
ATTN_TILE = 1024
SAMPLE_STREAMS = 4
FF_CHUNK = 256
BIAS_COLS = MAX_REL + CHUNK
BIAS_VAR = 256
VMEM_LIMIT = 48 * 1024 * 1024

_BF16 = jnp.bfloat16
_F32 = jnp.float32


def _const_spec(shape):
    nd = len(shape)
    return pl.BlockSpec(shape, lambda i: (0,) * nd, pipeline_mode=pl.Buffered(1))


def _head_sumsq(t, bd):
    sq = (t * t).astype(_BF16)
    w = t.shape[1]
    if w <= 2 * LANES:
        return jnp.dot(sq, bd[:w, :w], preferred_element_type=_F32)
    parts = [jnp.dot(sq[:, c:c + 2 * LANES], bd, preferred_element_type=_F32) for c in range(0, w, 2 * LANES)]
    return jnp.concatenate(parts, axis=1)


def _head_norm(t, bd, gain):
    ss = _head_sumsq(t, bd)
    return t * lax.rsqrt(ss * (1.0 / HEAD_DIM) + EPS) * gain


def _rope(t, cos, sin_signed):
    outs = []
    for c in range(0, t.shape[1], LANES):
        tc = t[:, c:c + LANES]
        lane = lax.broadcasted_iota(jnp.int32, tc.shape, 1)
        first_half = (lane & (HEAD_DIM - 1)) < HEAD_DIM // 2
        rot = jnp.where(first_half, pltpu.roll(tc, LANES - HEAD_DIM // 2, 1), pltpu.roll(tc, HEAD_DIM // 2, 1))
        outs.append(tc * cos + rot * sin_signed)
    return outs[0] if len(outs) == 1 else jnp.concatenate(outs, axis=1)


def _rep_pair(t):
    rolled = pltpu.roll(t, HEAD_DIM, 1)
    lo = lax.broadcasted_iota(jnp.int32, t.shape, 1) < HEAD_DIM
    return jnp.concatenate([jnp.where(lo, t, rolled), jnp.where(lo, rolled, t)], axis=1)


def _store_heads(ref, t):
    n, w = t.shape
    heads = w // HEAD_DIM
    for hd in range(heads):
        ref[pl.ds(hd, n, stride=heads), :] = t[:, hd * HEAD_DIM:(hd + 1) * HEAD_DIM]


def _proj_kernel(x_ref, gn_ref, w_ref, gqa_ref, gka_ref, gqb_ref, gkb_ref, ct_ref, st_ref, cr_ref, sr_ref, *rest,
                 last_only, n_cast):
    cast_in, rest = rest[:n_cast], rest[n_cast:]
    qa_ref, ka_ref, va_ref, qb_ref, kb_ref, vb_ref, kaf_ref, vaf_ref, kbf_ref, vbf_ref = rest[:10]
    for src, dst in zip(cast_in, rest[10:]):
        dst[...] = src[...].astype(_BF16)
    tm = x_ref.shape[0]
    x = x_ref[...]
    ms = jnp.mean(x * x, axis=-1, keepdims=True)
    h = (x * lax.rsqrt(ms + EPS) * gn_ref[...]).astype(_BF16)

    r = lax.broadcasted_iota(jnp.int32, (2 * LANES, 2 * LANES), 0) // HEAD_DIM
    c = lax.broadcasted_iota(jnp.int32, (2 * LANES, 2 * LANES), 1) // HEAD_DIM
    bd = jnp.where(r == c, 1.0, 0.0).astype(_BF16)
    ct, st, cr, sr = ct_ref[0], st_ref[0], cr_ref[...], sr_ref[...]
    cos = ct * cr - st * sr
    sin = st * cr + ct * sr

    def proj(lo, width):
        return jnp.dot(h, w_ref[:, lo:lo + width], preferred_element_type=_F32)

    o_ka, o_va, o_qb, o_kvb = A_WIDTH, 2 * A_WIDTH, 3 * A_WIDTH, 3 * A_WIDTH + B_WIDTH
    p_qa = proj(0, A_WIDTH)
    p_ka = proj(o_ka, A_WIDTH)
    qa_ref[...] = (_head_norm(p_qa, bd, gqa_ref[...]) * Q_SCALE).astype(_BF16)
    va = proj(o_va, A_WIDTH)
    ka = _head_norm(p_ka, bd, gka_ref[...])
    ka_ref[...] = ka.astype(_BF16)
    p_qb = proj(o_qb, B_WIDTH)
    va_ref[...] = va.astype(_BF16)
    p_kvb = proj(o_kvb, 2 * B_KV_WIDTH)
    qb_ref[...] = (_rope(_head_norm(p_qb, bd, gqb_ref[...]), cos, sin) * Q_SCALE).astype(_BF16)
    kb = _rope(_head_norm(p_kvb[:, :B_KV_WIDTH], bd, gkb_ref[...]), cos, sin)
    vb = p_kvb[:, B_KV_WIDTH:]
    kb_ref[...] = _rep_pair(kb).astype(_BF16)
    vb_ref[...] = _rep_pair(vb).astype(_BF16)

    if last_only:
        @pl.when(pl.program_id(0) == pl.num_programs(0) - 1)
        def _():
            kaf_ref[...] = ka[tm - A_CTX:, :].T
            vaf_ref[...] = va[tm - A_CTX:, :].T
            kbf_ref[...] = kb[tm - B_CTX:, :].T
            vbf_ref[...] = vb[tm - B_CTX:, :].T
    else:
        _store_heads(kaf_ref, ka)
        _store_heads(vaf_ref, va)
        kbf_ref[...] = kb
        vbf_ref[...] = vb


def _cast_blocks(w, n_steps):
    rows = w.shape[0]
    n_blocks = max(d for d in range(1, n_steps + 1) if rows % d == 0 and (rows // d) % 16 == 0)
    return pl.BlockSpec((rows // n_blocks, w.shape[1]), lambda i: (jnp.minimum(i, n_blocks - 1), 0))


def _project(x, gn, w_in, gqa, gka, gqb, gkb, rope, *, keep_all, name, cast=()):
    t = x.shape[0]
    tm = rope[2].shape[0]
    n = t // tm
    cast_specs = [_cast_blocks(w, n) for w in cast]
    tile = lambda w: pl.BlockSpec((tm, w), lambda i: (i, 0))
    if keep_all:
        f32_shapes = [(t * A_HEADS, HEAD_DIM), (t * A_HEADS, HEAD_DIM), (t, B_KV_WIDTH), (t, B_KV_WIDTH)]
        f32_specs = [pl.BlockSpec((tm * A_HEADS, HEAD_DIM), lambda i: (i, 0))] * 2 + [tile(B_KV_WIDTH)] * 2
    else:
        f32_shapes = [(A_WIDTH, A_CTX), (A_WIDTH, A_CTX), (B_KV_WIDTH, B_CTX), (B_KV_WIDTH, B_CTX)]
        f32_specs = [pl.BlockSpec(sh, lambda i: (0, 0)) for sh in f32_shapes]
    ct, st, cr, sr = rope
    if ct.shape[0] == n:
        tile_trig = pl.BlockSpec((1, 1, LANES), lambda i: (i, 0, 0))
    else:
        tile_trig = _const_spec((1, 1, LANES))
    out_shape = [
        jax.ShapeDtypeStruct((t, A_WIDTH), _BF16), jax.ShapeDtypeStruct((t, A_WIDTH), _BF16),
        jax.ShapeDtypeStruct((t, A_WIDTH), _BF16), jax.ShapeDtypeStruct((t, B_WIDTH), _BF16),
        jax.ShapeDtypeStruct((t, 2 * B_KV_WIDTH), _BF16), jax.ShapeDtypeStruct((t, 2 * B_KV_WIDTH), _BF16),
    ] + [jax.ShapeDtypeStruct(sh, _F32) for sh in f32_shapes] + [jax.ShapeDtypeStruct(w.shape, _BF16) for w in cast]
    out_specs = [
        tile(A_WIDTH), tile(A_WIDTH), tile(A_WIDTH), tile(B_WIDTH), tile(2 * B_KV_WIDTH), tile(2 * B_KV_WIDTH),
    ] + f32_specs + cast_specs
    in_specs = [
        tile(D_MODEL), _const_spec((1, D_MODEL)), _const_spec((D_MODEL, IN_COLS)),
        _const_spec((1, A_WIDTH)), _const_spec((1, A_WIDTH)), _const_spec((1, B_WIDTH)),
        _const_spec((1, B_KV_WIDTH)), tile_trig, tile_trig, _const_spec((tm, LANES)), _const_spec((tm, LANES)),
    ] + cast_specs
    return pl.pallas_call(
        functools.partial(_proj_kernel, last_only=not keep_all, n_cast=len(cast)),
        grid=(n,), in_specs=in_specs, out_specs=out_specs, out_shape=out_shape, name=name,
        compiler_params=pltpu.CompilerParams(dimension_semantics=("arbitrary",), vmem_limit_bytes=VMEM_LIMIT),
    )(x, gn, w_in, gqa, gka, gqb, gkb, ct, st, cr, sr, *cast)


def _build_bias(u_ref, bias_ref):
    qrow = lax.broadcasted_iota(jnp.int32, (CHUNK, BIAS_VAR), 0)
    for h in range(A_HEADS):
        prof = u_ref[h:h + 1, :]
        y = jnp.broadcast_to(prof - prof[:, 0:1], (CHUNK, BIAS_VAR))
        y = pltpu.roll(y, BIAS_VAR - (CHUNK - 1), 1)
        for b in range(6):
            y = jnp.where(((qrow >> b) & 1) == 1, pltpu.roll(y, 1 << b, 1), y)
        bias_ref[h // 4, (h % 4) * CHUNK:(h % 4 + 1) * CHUNK, :] = y[:, :BIAS_COLS] * LOG2E


_NT = (((1,), (1,)), ((), ()))
_NN = (((1,), (0,)), ((), ()))


def _scores(qa, qb, keys_a, keys_b, bias_ref, lim_a, lim_b):
    out = []
    head_of_lane = lax.broadcasted_iota(jnp.int32, (CHUNK, 2 * LANES), 1) // HEAD_DIM
    for g in range(2):
        qg = qa[:, 2 * LANES * g:2 * LANES * (g + 1)]
        qbd = jnp.concatenate([jnp.where(head_of_lane == r, qg, jnp.zeros_like(qg)) for r in range(4)], axis=0)
        pieces, off = [], 0
        for keys, on_rows in keys_a(g):
            s = lax.dot_general(qbd, keys, _NT if on_rows else _NN, preferred_element_type=_F32)
            n = s.shape[1]
            if off + n > A_WIN - BIAS_COLS:
                lo_col = max(A_WIN - BIAS_COLS - off, 0)
                biased = s[:, lo_col:] + bias_ref[g, :, off + lo_col - (A_WIN - BIAS_COLS):off + n - (A_WIN - BIAS_COLS)]
                s = biased if lo_col == 0 else jnp.concatenate([s[:, :lo_col], biased], axis=1)
            if lim_a:
                s = jnp.where(lax.broadcasted_iota(jnp.int32, s.shape, 1) >= lim_a, s, NEG)
            pieces.append(s)
            off += n
        out.append(pieces)
    lo = lax.broadcasted_iota(jnp.int32, (CHUNK, LANES), 1) < HEAD_DIM
    for g in range(B_KV_HEADS):
        rows = []
        for r in range(B_GROUP):
            hd = B_GROUP * g + r
            qp = qb[:, LANES * (hd // 2):LANES * (hd // 2 + 1)]
            keep = lo if hd % 2 == 0 else jnp.logical_not(lo)
            rows.append(jnp.where(keep, qp, jnp.zeros_like(qp)))
        qs = jnp.concatenate(rows, axis=0)
        pieces = []
        for keys, on_rows in keys_b(g):
            s = lax.dot_general(qs, keys, _NT if on_rows else _NN, preferred_element_type=_F32)
            if lim_b:
                s = jnp.where(lax.broadcasted_iota(jnp.int32, s.shape, 1) >= lim_b, s, NEG)
            pieces.append(s)
        out.append(pieces)
    return out


def _row_reduce(pieces, combine, reduce, fill):
    acc = None
    for s in pieces:
        for c in range(0, s.shape[1], LANES):
            tile = s[:, c:c + LANES]
            if tile.shape[1] < LANES:
                pad = jnp.full((tile.shape[0], LANES - tile.shape[1]), fill, tile.dtype)
                tile = jnp.concatenate([tile, pad], axis=1)
            acc = tile if acc is None else combine(acc, tile)
    return reduce(acc, axis=1, keepdims=True)


def _softmax_values(pieces, vals, sink):
    m = _row_reduce(pieces, jnp.maximum, jnp.max, -jnp.inf)
    if sink is not None:
        m = jnp.maximum(m, sink)
    es = [jnp.exp2(s - m) for s in pieces]
    l = _row_reduce(es, jnp.add, jnp.sum, 0.0)
    if sink is not None:
        l = l + jnp.exp2(sink - m)
    acc = None
    for e, (v, on_rows) in zip(es, vals):
        part = lax.dot_general(e.astype(_BF16), v, _NN if on_rows else _NT, preferred_element_type=_F32)
        acc = part if acc is None else acc + part
    return acc / l


def _softmax_pv(scores, vals_a, vals_b, sink_ref):
    outs = []
    head_of_lane = lax.broadcasted_iota(jnp.int32, (CHUNK, 2 * LANES), 1) // HEAD_DIM
    for g in range(2):
        of = _softmax_values(scores[g], vals_a(g), None)
        og = of[3 * CHUNK:4 * CHUNK]
        for r in (2, 1, 0):
            og = jnp.where(head_of_lane == r, of[r * CHUNK:(r + 1) * CHUNK], og)
        outs.append(og)
    lo = lax.broadcasted_iota(jnp.int32, (CHUNK, LANES), 1) < HEAD_DIM
    for g in range(B_KV_HEADS):
        of = _softmax_values(scores[2 + g], vals_b(g), sink_ref[g] * LOG2E)
        outs.append(jnp.where(lo, of[0:CHUNK], of[CHUNK:2 * CHUNK]))
        outs.append(jnp.where(lo, of[2 * CHUNK:3 * CHUNK], of[3 * CHUNK:4 * CHUNK]))
    return jnp.concatenate(outs, axis=1)


def _attn_prompt_kernel(qa_ref, ka_ref, va_ref, qb_ref, kb_ref, vb_ref, u_ref, sink_ref, o_ref,
                        kwa, vwa, kwb, vwb, bias):
    i = pl.program_id(0)
    tm = qa_ref.shape[0]
    n_chunks = tm // CHUNK

    @pl.when(i == 0)
    def _():
        kwa[0:A_CTX, :] = jnp.zeros((A_CTX, A_WIDTH), _BF16)
        vwa[0:A_CTX, :] = jnp.zeros((A_CTX, A_WIDTH), _BF16)
        kwb[0:B_CTX, :] = jnp.zeros((B_CTX, 2 * B_KV_WIDTH), _BF16)
        vwb[0:B_CTX, :] = jnp.zeros((B_CTX, 2 * B_KV_WIDTH), _BF16)
        _build_bias(u_ref, bias)

    @pl.when(i > 0)
    def _():
        kwa[0:A_CTX, :] = kwa[tm:tm + A_CTX, :]
        vwa[0:A_CTX, :] = vwa[tm:tm + A_CTX, :]
        kwb[0:B_CTX, :] = kwb[tm:tm + B_CTX, :]
        vwb[0:B_CTX, :] = vwb[tm:tm + B_CTX, :]

    kwa[A_CTX:A_CTX + tm, :] = ka_ref[...]
    vwa[A_CTX:A_CTX + tm, :] = va_ref[...]
    kwb[B_CTX:B_CTX + tm, :] = kb_ref[...]
    vwb[B_CTX:B_CTX + tm, :] = vb_ref[...]

    def window(ref, r0, rows, width):
        return lambda g: [(ref[r0:r0 + rows, width * g:width * (g + 1)], True)]

    def run(masked):
        def scores(c):
            r0 = c * CHUNK
            lim_a = max(A_LEFT_CHUNKS - c, 0) * CHUNK if masked else 0
            lim_b = max(B_CTX // CHUNK - c, 0) * CHUNK if masked else 0
            return _scores(qa_ref[r0:r0 + CHUNK, :], qb_ref[r0:r0 + CHUNK, :],
                           window(kwa, r0, A_WIN, 2 * LANES), window(kwb, r0, B_WIN, LANES), bias, lim_a, lim_b)

        nxt = scores(0)
        for c in range(n_chunks):
            cur = nxt
            if c + 1 < n_chunks:
                nxt = scores(c + 1)
            r0 = c * CHUNK
            o = _softmax_pv(cur, window(vwa, r0, A_WIN, 2 * LANES), window(vwb, r0, B_WIN, LANES), sink_ref)
            o_ref[r0:r0 + CHUNK, :] = o.astype(_BF16)

    @pl.when(i == 0)
    def _():
        run(True)

    @pl.when(i > 0)
    def _():
        run(False)


def _attn_sample_kernel(qa_ref, ka_ref, va_ref, qb_ref, kb_ref, vb_ref, cak_ref, cav_ref, cbk_ref, cbv_ref,
                        u_ref, sink_ref, o_ref, bias):
    n_streams = cak_ref.shape[0]

    @pl.when(pl.program_id(0) == 0)
    def _():
        _build_bias(u_ref, bias)

    def pieces_a(cache_ref, new_ref, b):
        rows = slice(b * CHUNK, (b + 1) * CHUNK)

        def get(g):
            cols = slice(2 * LANES * g, 2 * LANES * (g + 1))
            return [(cache_ref[b, cols, :].astype(_BF16), False), (new_ref[rows, cols], True)]
        return get

    def pieces_b(cache_ref, new_ref, b):
        rows = slice(b * CHUNK, (b + 1) * CHUNK)

        def get(g):
            past = cache_ref[b, HEAD_DIM * g:HEAD_DIM * (g + 1), :].astype(_BF16)
            return [(jnp.concatenate([past, past], axis=0), False), (new_ref[rows, LANES * g:LANES * (g + 1)], True)]
        return get

    def scores(b):
        rows = slice(b * CHUNK, (b + 1) * CHUNK)
        return _scores(qa_ref[rows, :], qb_ref[rows, :], pieces_a(cak_ref, ka_ref, b), pieces_b(cbk_ref, kb_ref, b),
                       bias, 0, 0)

    nxt = scores(0)
    for b in range(n_streams):
        cur = nxt
        if b + 1 < n_streams:
            nxt = scores(b + 1)
        o = _softmax_pv(cur, pieces_a(cav_ref, va_ref, b), pieces_b(cbv_ref, vb_ref, b), sink_ref)
        o_ref[b * CHUNK:(b + 1) * CHUNK, :] = o.astype(_BF16)


_BIAS_SCRATCH = pltpu.VMEM((2, 4 * CHUNK, BIAS_COLS), _F32)


def _attend_prompt(qa, ka, va, qb, kb, vb, u, sink):
    t = qa.shape[0]
    tm = ATTN_TILE
    tile = lambda w: pl.BlockSpec((tm, w), lambda i: (i, 0))
    return pl.pallas_call(
        _attn_prompt_kernel,
        grid=(t // tm,),
        in_specs=[tile(A_WIDTH), tile(A_WIDTH), tile(A_WIDTH), tile(B_WIDTH), tile(2 * B_KV_WIDTH),
                  tile(2 * B_KV_WIDTH), _const_spec(u.shape), _const_spec(sink.shape)],
        out_specs=tile(A_WIDTH + B_WIDTH),
        out_shape=jax.ShapeDtypeStruct((t, A_WIDTH + B_WIDTH), _BF16),
        scratch_shapes=[pltpu.VMEM((A_CTX + tm, A_WIDTH), _BF16), pltpu.VMEM((A_CTX + tm, A_WIDTH), _BF16),
                        pltpu.VMEM((B_CTX + tm, 2 * B_KV_WIDTH), _BF16),
                        pltpu.VMEM((B_CTX + tm, 2 * B_KV_WIDTH), _BF16), _BIAS_SCRATCH],
        name="attn_prompt",
        compiler_params=pltpu.CompilerParams(dimension_semantics=("arbitrary",), vmem_limit_bytes=VMEM_LIMIT),
    )(qa, ka, va, qb, kb, vb, u, sink)


def _attend_sample(qa, ka, va, qb, kb, vb, cak, cav, cbk, cbv, u, sink):
    t = qa.shape[0]
    sb = SAMPLE_STREAMS
    tile = lambda w: pl.BlockSpec((sb * CHUNK, w), lambda i: (i, 0))
    cache = lambda arr: pl.BlockSpec((sb,) + arr.shape[1:], lambda i: (i, 0, 0))
    return pl.pallas_call(
        _attn_sample_kernel,
        grid=(t // (sb * CHUNK),),
        in_specs=[tile(A_WIDTH), tile(A_WIDTH), tile(A_WIDTH), tile(B_WIDTH), tile(2 * B_KV_WIDTH),
                  tile(2 * B_KV_WIDTH), cache(cak), cache(cav), cache(cbk), cache(cbv),
                  _const_spec(u.shape), _const_spec(sink.shape)],
        out_specs=tile(A_WIDTH + B_WIDTH),
        out_shape=jax.ShapeDtypeStruct((t, A_WIDTH + B_WIDTH), _BF16),
        scratch_shapes=[_BIAS_SCRATCH],
        name="attn_sample",
        compiler_params=pltpu.CompilerParams(dimension_semantics=("arbitrary",), vmem_limit_bytes=VMEM_LIMIT),
    )(qa, ka, va, qb, kb, vb, cak, cav, cbk, cbv, u, sink)


def _ffn_kernel(o_ref, x_ref, wo_ref, gn_ref, wgu_ref, cw_ref, cb_ref, wd_ref, st_ref,
                y_ref, cn_ref, carry, act_buf, *, seg_len, carry_rows):
    i = pl.program_id(0)
    tm = x_ref.shape[0]
    n_seg = tm // seg_len

    if carry_rows:
        @pl.when(i == 0)
        def _():
            carry[...] = st_ref[0]

    y1 = x_ref[...] + jnp.dot(o_ref[...], wo_ref[...], preferred_element_type=_F32)
    ms = jnp.mean(y1 * y1, axis=-1, keepdims=True)
    h = (y1 * lax.rsqrt(ms + EPS) * gn_ref[...]).astype(_BF16)

    row8 = lax.broadcasted_iota(jnp.int32, (8, FF_CHUNK), 0)

    def gate_up(c0):
        g = jnp.dot(h, wgu_ref[:, c0:c0 + FF_CHUNK], preferred_element_type=_F32)
        up = jnp.dot(h, wgu_ref[:, D_FF + c0:D_FF + c0 + FF_CHUNK], preferred_element_type=_F32)
        return g, up

    nxt = gate_up(0)
    for c0 in range(0, D_FF, FF_CHUNK):
        cols = slice(c0, c0 + FF_CHUNK)
        g, up = nxt
        if c0 + FF_CHUNK < D_FF:
            nxt = gate_up(c0 + FF_CHUNK)
        g1 = pltpu.roll(g, 1, 0)
        g2 = pltpu.roll(g, 2, 0)
        p1, p2 = [], []
        for s in range(n_seg):
            a = s * seg_len
            prev = carry[:, cols] if carry_rows else st_ref[s, :, cols]
            prev0 = jnp.broadcast_to(prev[0:1, :], (8, FF_CHUNK))
            prev1 = jnp.broadcast_to(prev[1:2, :], (8, FF_CHUNK))
            p1.append(jnp.where(row8 == 0, prev1, g1[a:a + 8]))
            p1.append(g1[a + 8:a + seg_len])
            p2.append(jnp.where(row8 == 0, prev0, jnp.where(row8 == 1, prev1, g2[a:a + 8])))
            p2.append(g2[a + 8:a + seg_len])
            tail = g[a + seg_len - 8:a + seg_len][8 - (CONV_W - 1):, :]
            if carry_rows:
                carry[:, cols] = tail
            else:
                cn_ref[s, :, cols] = tail
        g1 = jnp.concatenate(p1, axis=0)
        g2 = jnp.concatenate(p2, axis=0)
        conv = cb_ref[:, cols] + cw_ref[0:1, cols] * g2 + cw_ref[1:2, cols] * g1 + cw_ref[2:3, cols] * g
        act_buf[:, cols] = (conv * jax.nn.sigmoid(conv) * up).astype(_BF16)
    y_ref[...] = y1 + jnp.dot(act_buf[...], wd_ref[...], preferred_element_type=_F32)
    if carry_rows:
        cn_ref[0] = carry[...]


def _out_ffn(o, x, w_out, gn, w_gu, conv_w, conv_b, w_down, state, *, carry_rows, name):
    t = x.shape[0]
    tm = TOKEN_TILE
    n = t // tm
    seg_len = tm if carry_rows else CHUNK
    n_seg = tm // seg_len
    tile = lambda w: pl.BlockSpec((tm, w), lambda i: (i, 0))
    if carry_rows:
        st_spec = _const_spec((1, CONV_W - 1, D_FF))
        cn_spec = pl.BlockSpec((1, CONV_W - 1, D_FF), lambda i: (0, 0, 0))
        cn_shape = (1, CONV_W - 1, D_FF)
    else:
        st_spec = pl.BlockSpec((n_seg, CONV_W - 1, D_FF), lambda i: (i, 0, 0))
        cn_spec = pl.BlockSpec((n_seg, CONV_W - 1, D_FF), lambda i: (i, 0, 0))
        cn_shape = (t // seg_len, CONV_W - 1, D_FF)
    return pl.pallas_call(
        functools.partial(_ffn_kernel, seg_len=seg_len, carry_rows=carry_rows),
        grid=(n,),
        in_specs=[tile(A_WIDTH + B_WIDTH), tile(D_MODEL), _const_spec((D_MODEL, D_MODEL)),
                  _const_spec((1, D_MODEL)), _const_spec((D_MODEL, 2 * D_FF)), _const_spec((CONV_W, D_FF)),
                  _const_spec((1, D_FF)), _const_spec((D_FF, D_MODEL)), st_spec],
        out_specs=[tile(D_MODEL), cn_spec],
        out_shape=[jax.ShapeDtypeStruct((t, D_MODEL), _F32), jax.ShapeDtypeStruct(cn_shape, _F32)],
        scratch_shapes=[pltpu.VMEM((CONV_W - 1, D_FF), _F32), pltpu.VMEM((tm, D_FF), _BF16)],
        name=name,
        compiler_params=pltpu.CompilerParams(dimension_semantics=("arbitrary",), vmem_limit_bytes=VMEM_LIMIT),
    )(o, x, w_out, gn, w_gu, conv_w, conv_b, w_down, state)


def _rope_tables(tile_pos, row_pos):
    half = HEAD_DIM // 2
    lane = jnp.arange(LANES)
    inv = jnp.power(ROPE_THETA, -(lane % half).astype(_F32) / half)
    sign = jnp.where(lane % HEAD_DIM < half, -1.0, 1.0).astype(_F32)
    a = tile_pos.astype(_F32)[:, None] * inv[None, :]
    b = row_pos.astype(_F32)[:, None] * inv[None, :]
    return (jnp.cos(a)[:, None, :], (jnp.sin(a) * sign)[:, None, :], jnp.cos(b), jnp.sin(b) * sign)


def _feature_major(cache):
    n, t, heads, d = cache.shape
    return jnp.transpose(cache, (0, 2, 3, 1)).reshape(n, heads * d, t)


def _token_major(rows, heads):
    t = rows.shape[1]
    return jnp.transpose(rows.reshape(heads, HEAD_DIM, t), (2, 0, 1))[None, None]


def _bias_profile(table):
    n_clip = BIAS_COLS - 1 - MAX_REL
    head = jnp.broadcast_to(table[:, 2 * MAX_REL:], (A_HEADS, n_clip))
    return jnp.concatenate([head, table[:, ::-1][:, :BIAS_VAR - n_clip]], axis=1)


def kernel(x_prompt, x_sample, cache_a_k, cache_a_v, cache_b_k, cache_b_v, state_conv, norm_attn, w_in, q_norm_a,
           k_norm_a, rel_bias_a, q_norm_b, k_norm_b, sinks_b, w_out, norm_ffn, w_gate_up, conv_w, conv_b, w_down):
    assert norm_attn.shape[0] == 1
    bsz, seq, _ = x_prompt.shape
    dec_b, dec_s, _ = x_sample.shape
    assert bsz == 1 and dec_s == CHUNK and (dec_b * dec_s) % TOKEN_TILE == 0
    assert seq % TOKEN_TILE == 0 and seq % PROMPT_PROJ_TILE == 0 and seq % ATTN_TILE == 0
    assert cache_a_k.shape[2] == A_CTX and cache_b_k.shape[2] == B_CTX

    w_in_b = w_in[0].astype(_BF16)
    gn_a = norm_attn[0][None, :]
    gn_f = norm_ffn[0][None, :]
    gqa = jnp.tile(q_norm_a[0], A_HEADS)[None, :]
    gka = jnp.tile(k_norm_a[0], A_HEADS)[None, :]
    gqb = jnp.tile(q_norm_b[0], B_HEADS)[None, :]
    gkb = jnp.tile(k_norm_b[0], B_KV_HEADS)[None, :]
    u = _bias_profile(rel_bias_a[0])
    sink = jnp.repeat(sinks_b[0], CHUNK).reshape(B_KV_HEADS, B_GROUP * CHUNK, 1)
    cw = conv_w[0]
    cb = conv_b[0][None, :]

    rope_p = _rope_tables(jnp.arange(0, seq, PROMPT_PROJ_TILE), jnp.arange(PROMPT_PROJ_TILE))
    rope_s = _rope_tables(jnp.full((1,), PAST_LEN), jnp.arange(TOKEN_TILE) % dec_s)

    xp = x_prompt.reshape(seq, D_MODEL)
    qa, ka, va, qb, kb, vb, kaf, vaf, kbf, vbf, w_out_b, w_gu_b, w_down_b = _project(
        xp, gn_a, w_in_b, gqa, gka, gqb, gkb, rope_p, keep_all=False, name="proj_prompt",
        cast=(w_out[0], w_gate_up[0], w_down[0]))
    o_p = _attend_prompt(qa, ka, va, qb, kb, vb, u, sink)
    zero_state = jnp.zeros((1, CONV_W - 1, D_FF), _F32)
    y_p, cn_p = _out_ffn(o_p, xp, w_out_b, gn_f, w_gu_b, cw, cb, w_down_b, zero_state,
                         carry_rows=True, name="ffn_prompt")

    xs = x_sample.reshape(dec_b * dec_s, D_MODEL)
    qa, ka, va, qb, kb, vb, kaf_s, vaf_s, kbf_s, vbf_s = _project(
        xs, gn_a, w_in_b, gqa, gka, gqb, gkb, rope_s, keep_all=True, name="proj_sample")
    o_s = _attend_sample(qa, ka, va, qb, kb, vb, _feature_major(cache_a_k[0]), _feature_major(cache_a_v[0]),
                         _feature_major(cache_b_k[0]), _feature_major(cache_b_v[0]), u, sink)
    y_s, cn_s = _out_ffn(o_s, xs, w_out_b, gn_f, w_gu_b, cw, cb, w_down_b, state_conv[0],
                         carry_rows=False, name="ffn_sample")

    return (
        y_p.reshape(1, seq, D_MODEL),
        y_s.reshape(dec_b, dec_s, D_MODEL),
        _token_major(kaf, A_HEADS),
        _token_major(vaf, A_HEADS),
        _token_major(kbf, B_KV_HEADS),
        _token_major(vbf, B_KV_HEADS),
        cn_p.reshape(1, 1, CONV_W - 1, D_FF),
        kaf_s.reshape(1, dec_b, dec_s, A_HEADS, HEAD_DIM),
        vaf_s.reshape(1, dec_b, dec_s, A_HEADS, HEAD_DIM),
        kbf_s.reshape(1, dec_b, dec_s, B_KV_HEADS, HEAD_DIM),
        vbf_s.reshape(1, dec_b, dec_s, B_KV_HEADS, HEAD_DIM),
        cn_s.reshape(1, dec_b, CONV_W - 1, D_FF),
    )
```

```python
import functools

import jax
import jax.numpy as jnp
from jax import lax
from jax.experimental import pallas as pl
from jax.experimental.pallas import tpu as pltpu

D_MODEL = 1024
CHUNK = 64
HEAD_DIM = 64
EPS = 1e-6
SCALE = HEAD_DIM ** -0.5
LOG2E = 1.4426950408889634
Q_SCALE = SCALE * LOG2E
NEG = -1e30
PAST_LEN = 1024
A_HEADS = 8
A_LEFT_CHUNKS = 8
A_CTX = A_LEFT_CHUNKS * CHUNK
A_WIN = A_CTX + CHUNK
MAX_REL = 128
A_WIDTH = A_HEADS * HEAD_DIM
B_HEADS = 8
B_KV_HEADS = 2
B_GROUP = B_HEADS // B_KV_HEADS
B_CTX = 128
B_WIN = B_CTX + CHUNK
B_WIDTH = B_HEADS * HEAD_DIM
B_KV_WIDTH = B_KV_HEADS * HEAD_DIM
ROPE_THETA = 10000.0
IN_COLS = 3 * A_WIDTH + B_WIDTH + 2 * B_KV_WIDTH
D_FF = 2816
CONV_W = 3

LANES = 128
TOKEN_TILE = 512
PROMPT_PROJ_TILE = 1024
ATTN_TILE = 1024
SAMPLE_STREAMS = 4
FF_CHUNK = 256
BIAS_COLS = MAX_REL + CHUNK
BIAS_VAR = 256
_MIB = 1024 * 1024
VMEM_PROJ_PROMPT = 44 * _MIB
VMEM_PROJ_SAMPLE = 32 * _MIB
VMEM_ATTN_PROMPT = 28 * _MIB
VMEM_ATTN_SAMPLE = 30 * _MIB
VMEM_FFN = 40 * _MIB

_BF16 = jnp.bfloat16
_F32 = jnp.float32


def _const_spec(shape):
    nd = len(shape)
    return pl.BlockSpec(shape, lambda i: (0,) * nd, pipeline_mode=pl.Buffered(1))


def _head_sumsq(t, bd):
    sq = (t * t).astype(_BF16)
    w = t.shape[1]
    if w <= 2 * LANES:
        return jnp.dot(sq, bd[:w, :w], preferred_element_type=_F32)
    parts = [jnp.dot(sq[:, c:c + 2 * LANES], bd, preferred_element_type=_F32) for c in range(0, w, 2 * LANES)]
    return jnp.concatenate(parts, axis=1)


def _head_norm(t, bd, gain):
    ss = _head_sumsq(t, bd)
    return t * lax.rsqrt(ss * (1.0 / HEAD_DIM) + EPS) * gain


def _rope(t, cos, sin_signed):
    outs = []
    for c in range(0, t.shape[1], LANES):
        tc = t[:, c:c + LANES]
        lane = lax.broadcasted_iota(jnp.int32, tc.shape, 1)
        first_half = (lane & (HEAD_DIM - 1)) < HEAD_DIM // 2
        rot = jnp.where(first_half, pltpu.roll(tc, LANES - HEAD_DIM // 2, 1), pltpu.roll(tc, HEAD_DIM // 2, 1))
        outs.append(tc * cos + rot * sin_signed)
    return outs[0] if len(outs) == 1 else jnp.concatenate(outs, axis=1)


def _rep_pair(t):
    rolled = pltpu.roll(t, HEAD_DIM, 1)
    lo = lax.broadcasted_iota(jnp.int32, t.shape, 1) < HEAD_DIM
    return jnp.concatenate([jnp.where(lo, t, rolled), jnp.where(lo, rolled, t)], axis=1)


def _store_heads(ref, t):
    n, w = t.shape
    heads = w // HEAD_DIM
    for hd in range(heads):
        ref[pl.ds(hd, n, stride=heads), :] = t[:, hd * HEAD_DIM:(hd + 1) * HEAD_DIM]


def _proj_kernel(x_ref, gn_ref, w_ref, gqa_ref, gka_ref, gqb_ref, gkb_ref, ct_ref, st_ref, cr_ref, sr_ref, *rest,
                 last_only, n_cast):
    cast_in, rest = rest[:n_cast], rest[n_cast:]
    qa_ref, ka_ref, va_ref, qb_ref, kb_ref, vb_ref, kaf_ref, vaf_ref, kbf_ref, vbf_ref = rest[:10]
    for src, dst in zip(cast_in, rest[10:]):
        dst[...] = src[...].astype(_BF16)
    tm = x_ref.shape[0]
    x = x_ref[...]
    ms = jnp.mean(x * x, axis=-1, keepdims=True)
    h = (x * lax.rsqrt(ms + EPS) * gn_ref[...]).astype(_BF16)

    r = lax.broadcasted_iota(jnp.int32, (2 * LANES, 2 * LANES), 0) // HEAD_DIM
    c = lax.broadcasted_iota(jnp.int32, (2 * LANES, 2 * LANES), 1) // HEAD_DIM
    bd = jnp.where(r == c, 1.0, 0.0).astype(_BF16)
    ct, st, cr, sr = ct_ref[0], st_ref[0], cr_ref[...], sr_ref[...]
    cos = ct * cr - st * sr
    sin = st * cr + ct * sr

    def proj(lo, width):
        return jnp.dot(h, w_ref[:, lo:lo + width], preferred_element_type=_F32)

    o_ka, o_va, o_qb, o_kvb = A_WIDTH, 2 * A_WIDTH, 3 * A_WIDTH, 3 * A_WIDTH + B_WIDTH
    p_qa = proj(0, A_WIDTH)
    p_ka = proj(o_ka, A_WIDTH)
    qa_ref[...] = (_head_norm(p_qa, bd, gqa_ref[...]) * Q_SCALE).astype(_BF16)
    va = proj(o_va, A_WIDTH)
    ka = _head_norm(p_ka, bd, gka_ref[...])
    ka_ref[...] = ka.astype(_BF16)
    p_qb = proj(o_qb, B_WIDTH)
    va_ref[...] = va.astype(_BF16)
    p_kvb = proj(o_kvb, 2 * B_KV_WIDTH)
    qb_ref[...] = (_rope(_head_norm(p_qb, bd, gqb_ref[...]), cos, sin) * Q_SCALE).astype(_BF16)
    kb = _rope(_head_norm(p_kvb[:, :B_KV_WIDTH], bd, gkb_ref[...]), cos, sin)
    vb = p_kvb[:, B_KV_WIDTH:]
    kb_ref[...] = _rep_pair(kb).astype(_BF16)
    vb_ref[...] = _rep_pair(vb).astype(_BF16)

    if last_only:
        @pl.when(pl.program_id(0) == pl.num_programs(0) - 1)
        def _():
            kaf_ref[...] = ka[tm - A_CTX:, :].T
            vaf_ref[...] = va[tm - A_CTX:, :].T
            kbf_ref[...] = kb[tm - B_CTX:, :].T
            vbf_ref[...] = vb[tm - B_CTX:, :].T
    else:
        _store_heads(kaf_ref, ka)
        _store_heads(vaf_ref, va)
        kbf_ref[...] = kb
        vbf_ref[...] = vb


def _cast_blocks(w, n_steps):
    rows = w.shape[0]
    n_blocks = max(d for d in range(1, n_steps + 1) if rows % d == 0 and (rows // d) % 16 == 0)
    return pl.BlockSpec((rows // n_blocks, w.shape[1]), lambda i: (jnp.minimum(i, n_blocks - 1), 0))


def _project(x, gn, w_in, gqa, gka, gqb, gkb, rope, *, keep_all, name, cast=()):
    t = x.shape[0]
    tm = rope[2].shape[0]
    n = t // tm
    cast_specs = [_cast_blocks(w, n) for w in cast]
    tile = lambda w: pl.BlockSpec((tm, w), lambda i: (i, 0))
    if keep_all:
        f32_shapes = [(t * A_HEADS, HEAD_DIM), (t * A_HEADS, HEAD_DIM), (t, B_KV_WIDTH), (t, B_KV_WIDTH)]
        f32_specs = [pl.BlockSpec((tm * A_HEADS, HEAD_DIM), lambda i: (i, 0))] * 2 + [tile(B_KV_WIDTH)] * 2
    else:
        f32_shapes = [(A_WIDTH, A_CTX), (A_WIDTH, A_CTX), (B_KV_WIDTH, B_CTX), (B_KV_WIDTH, B_CTX)]
        f32_specs = [pl.BlockSpec(sh, lambda i: (0, 0)) for sh in f32_shapes]
    ct, st, cr, sr = rope
    if ct.shape[0] == n:
        tile_trig = pl.BlockSpec((1, 1, LANES), lambda i: (i, 0, 0))
    else:
        tile_trig = _const_spec((1, 1, LANES))
    out_shape = [
        jax.ShapeDtypeStruct((t, A_WIDTH), _BF16), jax.ShapeDtypeStruct((t, A_WIDTH), _BF16),
        jax.ShapeDtypeStruct((t, A_WIDTH), _BF16), jax.ShapeDtypeStruct((t, B_WIDTH), _BF16),
        jax.ShapeDtypeStruct((t, 2 * B_KV_WIDTH), _BF16), jax.ShapeDtypeStruct((t, 2 * B_KV_WIDTH), _BF16),
    ] + [jax.ShapeDtypeStruct(sh, _F32) for sh in f32_shapes] + [jax.ShapeDtypeStruct(w.shape, _BF16) for w in cast]
    out_specs = [
        tile(A_WIDTH), tile(A_WIDTH), tile(A_WIDTH), tile(B_WIDTH), tile(2 * B_KV_WIDTH), tile(2 * B_KV_WIDTH),
    ] + f32_specs + cast_specs
    in_specs = [
        tile(D_MODEL), _const_spec((1, D_MODEL)), _const_spec((D_MODEL, IN_COLS)),
        _const_spec((1, A_WIDTH)), _const_spec((1, A_WIDTH)), _const_spec((1, B_WIDTH)),
        _const_spec((1, B_KV_WIDTH)), tile_trig, tile_trig, _const_spec((tm, LANES)), _const_spec((tm, LANES)),
    ] + cast_specs
    return pl.pallas_call(
        functools.partial(_proj_kernel, last_only=not keep_all, n_cast=len(cast)),
        grid=(n,), in_specs=in_specs, out_specs=out_specs, out_shape=out_shape, name=name,
        compiler_params=pltpu.CompilerParams(dimension_semantics=("arbitrary",),
                                             vmem_limit_bytes=VMEM_PROJ_SAMPLE if keep_all else VMEM_PROJ_PROMPT),
    )(x, gn, w_in, gqa, gka, gqb, gkb, ct, st, cr, sr, *cast)


def _build_bias(u_ref, bias_ref):
    qrow = lax.broadcasted_iota(jnp.int32, (CHUNK, BIAS_VAR), 0)
    for h in range(A_HEADS):
        prof = u_ref[h:h + 1, :]
        y = jnp.broadcast_to(prof - prof[:, 0:1], (CHUNK, BIAS_VAR))
        y = pltpu.roll(y, BIAS_VAR - (CHUNK - 1), 1)
        for b in range(6):
            y = jnp.where(((qrow >> b) & 1) == 1, pltpu.roll(y, 1 << b, 1), y)
        bias_ref[h // 4, (h % 4) * CHUNK:(h % 4 + 1) * CHUNK, :] = y[:, :BIAS_COLS] * LOG2E


_NT = (((1,), (1,)), ((), ()))
_NN = (((1,), (0,)), ((), ()))


def _scores(qa, qb, keys_a, keys_b, bias_ref, lim_a, lim_b):
    out = []
    head_of_lane = lax.broadcasted_iota(jnp.int32, (CHUNK, 2 * LANES), 1) // HEAD_DIM
    for g in range(2):
        qg = qa[:, 2 * LANES * g:2 * LANES * (g + 1)]
        qbd = jnp.concatenate([jnp.where(head_of_lane == r, qg, jnp.zeros_like(qg)) for r in range(4)], axis=0)
        pieces, off = [], 0
        for keys, on_rows in keys_a(g):
            s = lax.dot_general(qbd, keys, _NT if on_rows else _NN, preferred_element_type=_F32)
            n = s.shape[1]
            if off + n > A_WIN - BIAS_COLS:
                lo_col = max(A_WIN - BIAS_COLS - off, 0)
                biased = s[:, lo_col:] + bias_ref[g, :, off + lo_col - (A_WIN - BIAS_COLS):off + n - (A_WIN - BIAS_COLS)]
                s = biased if lo_col == 0 else jnp.concatenate([s[:, :lo_col], biased], axis=1)
            if lim_a:
                s = jnp.where(lax.broadcasted_iota(jnp.int32, s.shape, 1) >= lim_a, s, NEG)
            pieces.append(s)
            off += n
        out.append(pieces)
    lo = lax.broadcasted_iota(jnp.int32, (CHUNK, LANES), 1) < HEAD_DIM
    for g in range(B_KV_HEADS):
        rows = []
        for r in range(B_GROUP):
            hd = B_GROUP * g + r
            qp = qb[:, LANES * (hd // 2):LANES * (hd // 2 + 1)]
            keep = lo if hd % 2 == 0 else jnp.logical_not(lo)
            rows.append(jnp.where(keep, qp, jnp.zeros_like(qp)))
        qs = jnp.concatenate(rows, axis=0)
        pieces = []
        for keys, on_rows in keys_b(g):
            s = lax.dot_general(qs, keys, _NT if on_rows else _NN, preferred_element_type=_F32)
            if lim_b:
                s = jnp.where(lax.broadcasted_iota(jnp.int32, s.shape, 1) >= lim_b, s, NEG)
            pieces.append(s)
        out.append(pieces)
    return out


def _row_reduce(pieces, combine, reduce, fill):
    acc = None
    for s in pieces:
        for c in range(0, s.shape[1], LANES):
            tile = s[:, c:c + LANES]
            if tile.shape[1] < LANES:
                pad = jnp.full((tile.shape[0], LANES - tile.shape[1]), fill, tile.dtype)
                tile = jnp.concatenate([tile, pad], axis=1)
            acc = tile if acc is None else combine(acc, tile)
    return reduce(acc, axis=1, keepdims=True)


def _softmax_values(pieces, vals, sink):
    m = _row_reduce(pieces, jnp.maximum, jnp.max, -jnp.inf)
    if sink is not None:
        m = jnp.maximum(m, sink)
    es = [jnp.exp2(s - m) for s in pieces]
    l = _row_reduce(es, jnp.add, jnp.sum, 0.0)
    if sink is not None:
        l = l + jnp.exp2(sink - m)
    acc = None
    for e, (v, on_rows) in zip(es, vals):
        part = lax.dot_general(e.astype(_BF16), v, _NN if on_rows else _NT, preferred_element_type=_F32)
        acc = part if acc is None else acc + part
    return acc / l


def _softmax_pv(scores, vals_a, vals_b, sink_ref):
    outs = []
    head_of_lane = lax.broadcasted_iota(jnp.int32, (CHUNK, 2 * LANES), 1) // HEAD_DIM
    for g in range(2):
        of = _softmax_values(scores[g], vals_a(g), None)
        og = of[3 * CHUNK:4 * CHUNK]
        for r in (2, 1, 0):
            og = jnp.where(head_of_lane == r, of[r * CHUNK:(r + 1) * CHUNK], og)
        outs.append(og)
    lo = lax.broadcasted_iota(jnp.int32, (CHUNK, LANES), 1) < HEAD_DIM
    for g in range(B_KV_HEADS):
        of = _softmax_values(scores[2 + g], vals_b(g), sink_ref[g] * LOG2E)
        outs.append(jnp.where(lo, of[0:CHUNK], of[CHUNK:2 * CHUNK]))
        outs.append(jnp.where(lo, of[2 * CHUNK:3 * CHUNK], of[3 * CHUNK:4 * CHUNK]))
    return jnp.concatenate(outs, axis=1)


def _attn_prompt_kernel(qa_ref, ka_ref, va_ref, qb_ref, kb_ref, vb_ref, u_ref, sink_ref, o_ref,
                        kwa, vwa, kwb, vwb, bias):
    i = pl.program_id(0)
    tm = qa_ref.shape[0]
    n_chunks = tm // CHUNK

    @pl.when(i == 0)
    def _():
        kwa[0:A_CTX, :] = jnp.zeros((A_CTX, A_WIDTH), _BF16)
        vwa[0:A_CTX, :] = jnp.zeros((A_CTX, A_WIDTH), _BF16)
        kwb[0:B_CTX, :] = jnp.zeros((B_CTX, 2 * B_KV_WIDTH), _BF16)
        vwb[0:B_CTX, :] = jnp.zeros((B_CTX, 2 * B_KV_WIDTH), _BF16)
        _build_bias(u_ref, bias)

    @pl.when(i > 0)
    def _():
        kwa[0:A_CTX, :] = kwa[tm:tm + A_CTX, :]
        vwa[0:A_CTX, :] = vwa[tm:tm + A_CTX, :]
        kwb[0:B_CTX, :] = kwb[tm:tm + B_CTX, :]
        vwb[0:B_CTX, :] = vwb[tm:tm + B_CTX, :]

    kwa[A_CTX:A_CTX + tm, :] = ka_ref[...]
    vwa[A_CTX:A_CTX + tm, :] = va_ref[...]
    kwb[B_CTX:B_CTX + tm, :] = kb_ref[...]
    vwb[B_CTX:B_CTX + tm, :] = vb_ref[...]

    def window(ref, r0, rows, width):
        return lambda g: [(ref[r0:r0 + rows, width * g:width * (g + 1)], True)]

    def run(masked):
        def scores(c):
            r0 = c * CHUNK
            lim_a = max(A_LEFT_CHUNKS - c, 0) * CHUNK if masked else 0
            lim_b = max(B_CTX // CHUNK - c, 0) * CHUNK if masked else 0
            return _scores(qa_ref[r0:r0 + CHUNK, :], qb_ref[r0:r0 + CHUNK, :],
                           window(kwa, r0, A_WIN, 2 * LANES), window(kwb, r0, B_WIN, LANES), bias, lim_a, lim_b)

        nxt = scores(0)
        for c in range(n_chunks):
            cur = nxt
            if c + 1 < n_chunks:
                nxt = scores(c + 1)
            r0 = c * CHUNK
            o = _softmax_pv(cur, window(vwa, r0, A_WIN, 2 * LANES), window(vwb, r0, B_WIN, LANES), sink_ref)
            o_ref[r0:r0 + CHUNK, :] = o.astype(_BF16)

    @pl.when(i == 0)
    def _():
        run(True)

    @pl.when(i > 0)
    def _():
        run(False)


def _attn_sample_kernel(qa_ref, ka_ref, va_ref, qb_ref, kb_ref, vb_ref, cak_ref, cav_ref, cbk_ref, cbv_ref,
                        u_ref, sink_ref, o_ref, bias):
    n_streams = cak_ref.shape[0]

    @pl.when(pl.program_id(0) == 0)
    def _():
        _build_bias(u_ref, bias)

    def pieces_a(cache_ref, new_ref, b):
        rows = slice(b * CHUNK, (b + 1) * CHUNK)

        def get(g):
            cols = slice(2 * LANES * g, 2 * LANES * (g + 1))
            return [(cache_ref[b, cols, :].astype(_BF16), False), (new_ref[rows, cols], True)]
        return get

    def pieces_b(cache_ref, new_ref, b):
        rows = slice(b * CHUNK, (b + 1) * CHUNK)

        def get(g):
            past = cache_ref[b, HEAD_DIM * g:HEAD_DIM * (g + 1), :].astype(_BF16)
            return [(jnp.concatenate([past, past], axis=0), False), (new_ref[rows, LANES * g:LANES * (g + 1)], True)]
        return get

    def scores(b):
        rows = slice(b * CHUNK, (b + 1) * CHUNK)
        return _scores(qa_ref[rows, :], qb_ref[rows, :], pieces_a(cak_ref, ka_ref, b), pieces_b(cbk_ref, kb_ref, b),
                       bias, 0, 0)

    nxt = scores(0)
    for b in range(n_streams):
        cur = nxt
        if b + 1 < n_streams:
            nxt = scores(b + 1)
        o = _softmax_pv(cur, pieces_a(cav_ref, va_ref, b), pieces_b(cbv_ref, vb_ref, b), sink_ref)
        o_ref[b * CHUNK:(b + 1) * CHUNK, :] = o.astype(_BF16)


_BIAS_SCRATCH = pltpu.VMEM((2, 4 * CHUNK, BIAS_COLS), _F32)


def _attend_prompt(qa, ka, va, qb, kb, vb, u, sink):
    t = qa.shape[0]
    tm = ATTN_TILE
    tile = lambda w: pl.BlockSpec((tm, w), lambda i: (i, 0))
    return pl.pallas_call(
        _attn_prompt_kernel,
        grid=(t // tm,),
        in_specs=[tile(A_WIDTH), tile(A_WIDTH), tile(A_WIDTH), tile(B_WIDTH), tile(2 * B_KV_WIDTH),
                  tile(2 * B_KV_WIDTH), _const_spec(u.shape), _const_spec(sink.shape)],
        out_specs=tile(A_WIDTH + B_WIDTH),
        out_shape=jax.ShapeDtypeStruct((t, A_WIDTH + B_WIDTH), _BF16),
        scratch_shapes=[pltpu.VMEM((A_CTX + tm, A_WIDTH), _BF16), pltpu.VMEM((A_CTX + tm, A_WIDTH), _BF16),
                        pltpu.VMEM((B_CTX + tm, 2 * B_KV_WIDTH), _BF16),
                        pltpu.VMEM((B_CTX + tm, 2 * B_KV_WIDTH), _BF16), _BIAS_SCRATCH],
        name="attn_prompt",
        compiler_params=pltpu.CompilerParams(dimension_semantics=("arbitrary",), vmem_limit_bytes=VMEM_ATTN_PROMPT),
    )(qa, ka, va, qb, kb, vb, u, sink)


def _attend_sample(qa, ka, va, qb, kb, vb, cak, cav, cbk, cbv, u, sink):
    t = qa.shape[0]
    sb = SAMPLE_STREAMS
    tile = lambda w: pl.BlockSpec((sb * CHUNK, w), lambda i: (i, 0))
    cache = lambda arr: pl.BlockSpec((sb,) + arr.shape[1:], lambda i: (i, 0, 0))
    return pl.pallas_call(
        _attn_sample_kernel,
        grid=(t // (sb * CHUNK),),
        in_specs=[tile(A_WIDTH), tile(A_WIDTH), tile(A_WIDTH), tile(B_WIDTH), tile(2 * B_KV_WIDTH),
                  tile(2 * B_KV_WIDTH), cache(cak), cache(cav), cache(cbk), cache(cbv),
                  _const_spec(u.shape), _const_spec(sink.shape)],
        out_specs=tile(A_WIDTH + B_WIDTH),
        out_shape=jax.ShapeDtypeStruct((t, A_WIDTH + B_WIDTH), _BF16),
        scratch_shapes=[_BIAS_SCRATCH],
        name="attn_sample",
        compiler_params=pltpu.CompilerParams(dimension_semantics=("arbitrary",), vmem_limit_bytes=VMEM_ATTN_SAMPLE),
    )(qa, ka, va, qb, kb, vb, cak, cav, cbk, cbv, u, sink)


def _ffn_kernel(o_ref, x_ref, wo_ref, gn_ref, wgu_ref, cw_ref, cb_ref, wd_ref, st_ref,
                y_ref, cn_ref, carry, act_buf, *, seg_len, carry_rows):
    i = pl.program_id(0)
    tm = x_ref.shape[0]
    n_seg = tm // seg_len

    if carry_rows:
        @pl.when(i == 0)
        def _():
            carry[...] = st_ref[0]

    y1 = x_ref[...] + jnp.dot(o_ref[...], wo_ref[...], preferred_element_type=_F32)
    ms = jnp.mean(y1 * y1, axis=-1, keepdims=True)
    h = (y1 * lax.rsqrt(ms + EPS) * gn_ref[...]).astype(_BF16)

    row8 = lax.broadcasted_iota(jnp.int32, (8, FF_CHUNK), 0)

    def gate_up(c0):
        g = jnp.dot(h, wgu_ref[:, c0:c0 + FF_CHUNK], preferred_element_type=_F32)
        up = jnp.dot(h, wgu_ref[:, D_FF + c0:D_FF + c0 + FF_CHUNK], preferred_element_type=_F32)
        return g, up

    nxt = gate_up(0)
    for c0 in range(0, D_FF, FF_CHUNK):
        cols = slice(c0, c0 + FF_CHUNK)
        g, up = nxt
        if c0 + FF_CHUNK < D_FF:
            nxt = gate_up(c0 + FF_CHUNK)
        g1 = pltpu.roll(g, 1, 0)
        g2 = pltpu.roll(g, 2, 0)
        p1, p2 = [], []
        for s in range(n_seg):
            a = s * seg_len
            prev = carry[:, cols] if carry_rows else st_ref[s, :, cols]
            prev0 = jnp.broadcast_to(prev[0:1, :], (8, FF_CHUNK))
            prev1 = jnp.broadcast_to(prev[1:2, :], (8, FF_CHUNK))
            p1.append(jnp.where(row8 == 0, prev1, g1[a:a + 8]))
            p1.append(g1[a + 8:a + seg_len])
            p2.append(jnp.where(row8 == 0, prev0, jnp.where(row8 == 1, prev1, g2[a:a + 8])))
            p2.append(g2[a + 8:a + seg_len])
            tail = g[a + seg_len - 8:a + seg_len][8 - (CONV_W - 1):, :]
            if carry_rows:
                carry[:, cols] = tail
            else:
                cn_ref[s, :, cols] = tail
        g1 = jnp.concatenate(p1, axis=0)
        g2 = jnp.concatenate(p2, axis=0)
        conv = cb_ref[:, cols] + cw_ref[0:1, cols] * g2 + cw_ref[1:2, cols] * g1 + cw_ref[2:3, cols] * g
        act_buf[:, cols] = (conv * jax.nn.sigmoid(conv) * up).astype(_BF16)
    y_ref[...] = y1 + jnp.dot(act_buf[...], wd_ref[...], preferred_element_type=_F32)
    if carry_rows:
        cn_ref[0] = carry[...]


def _out_ffn(o, x, w_out, gn, w_gu, conv_w, conv_b, w_down, state, *, carry_rows, name):
    t = x.shape[0]
    tm = TOKEN_TILE
    n = t // tm
    seg_len = tm if carry_rows else CHUNK
    n_seg = tm // seg_len
    tile = lambda w: pl.BlockSpec((tm, w), lambda i: (i, 0))
    if carry_rows:
        st_spec = _const_spec((1, CONV_W - 1, D_FF))
        cn_spec = pl.BlockSpec((1, CONV_W - 1, D_FF), lambda i: (0, 0, 0))
        cn_shape = (1, CONV_W - 1, D_FF)
    else:
        st_spec = pl.BlockSpec((n_seg, CONV_W - 1, D_FF), lambda i: (i, 0, 0))
        cn_spec = pl.BlockSpec((n_seg, CONV_W - 1, D_FF), lambda i: (i, 0, 0))
        cn_shape = (t // seg_len, CONV_W - 1, D_FF)
    return pl.pallas_call(
        functools.partial(_ffn_kernel, seg_len=seg_len, carry_rows=carry_rows),
        grid=(n,),
        in_specs=[tile(A_WIDTH + B_WIDTH), tile(D_MODEL), _const_spec((D_MODEL, D_MODEL)),
                  _const_spec((1, D_MODEL)), _const_spec((D_MODEL, 2 * D_FF)), _const_spec((CONV_W, D_FF)),
                  _const_spec((1, D_FF)), _const_spec((D_FF, D_MODEL)), st_spec],
        out_specs=[tile(D_MODEL), cn_spec],
        out_shape=[jax.ShapeDtypeStruct((t, D_MODEL), _F32), jax.ShapeDtypeStruct(cn_shape, _F32)],
        scratch_shapes=[pltpu.VMEM((CONV_W - 1, D_FF), _F32), pltpu.VMEM((tm, D_FF), _BF16)],
        name=name,
        compiler_params=pltpu.CompilerParams(dimension_semantics=("arbitrary",), vmem_limit_bytes=VMEM_FFN),
    )(o, x, w_out, gn, w_gu, conv_w, conv_b, w_down, state)


def _rope_tables(tile_pos, row_pos):
    half = HEAD_DIM // 2
    lane = jnp.arange(LANES)
    inv = jnp.power(ROPE_THETA, -(lane % half).astype(_F32) / half)
    sign = jnp.where(lane % HEAD_DIM < half, -1.0, 1.0).astype(_F32)
    a = tile_pos.astype(_F32)[:, None] * inv[None, :]
    b = row_pos.astype(_F32)[:, None] * inv[None, :]
    return (jnp.cos(a)[:, None, :], (jnp.sin(a) * sign)[:, None, :], jnp.cos(b), jnp.sin(b) * sign)


def _feature_major(cache):
    n, t, heads, d = cache.shape
    return jnp.transpose(cache, (0, 2, 3, 1)).reshape(n, heads * d, t)


def _token_major(rows, heads):
    t = rows.shape[1]
    return jnp.transpose(rows.reshape(heads, HEAD_DIM, t), (2, 0, 1))[None, None]


def _bias_profile(table):
    n_clip = BIAS_COLS - 1 - MAX_REL
    head = jnp.broadcast_to(table[:, 2 * MAX_REL:], (A_HEADS, n_clip))
    return jnp.concatenate([head, table[:, ::-1][:, :BIAS_VAR - n_clip]], axis=1)


def kernel(x_prompt, x_sample, cache_a_k, cache_a_v, cache_b_k, cache_b_v, state_conv, norm_attn, w_in, q_norm_a,
           k_norm_a, rel_bias_a, q_norm_b, k_norm_b, sinks_b, w_out, norm_ffn, w_gate_up, conv_w, conv_b, w_down):
    assert norm_attn.shape[0] == 1
    bsz, seq, _ = x_prompt.shape
    dec_b, dec_s, _ = x_sample.shape
    assert bsz == 1 and dec_s == CHUNK and (dec_b * dec_s) % TOKEN_TILE == 0
    assert seq % TOKEN_TILE == 0 and seq % PROMPT_PROJ_TILE == 0 and seq % ATTN_TILE == 0
    assert cache_a_k.shape[2] == A_CTX and cache_b_k.shape[2] == B_CTX

    w_in_b = w_in[0].astype(_BF16)
    gn_a = norm_attn[0][None, :]
    gn_f = norm_ffn[0][None, :]
    gqa = jnp.tile(q_norm_a[0], A_HEADS)[None, :]
    gka = jnp.tile(k_norm_a[0], A_HEADS)[None, :]
    gqb = jnp.tile(q_norm_b[0], B_HEADS)[None, :]
    gkb = jnp.tile(k_norm_b[0], B_KV_HEADS)[None, :]
    u = _bias_profile(rel_bias_a[0])
    sink = jnp.repeat(sinks_b[0], CHUNK).reshape(B_KV_HEADS, B_GROUP * CHUNK, 1)
    cw = conv_w[0]
    cb = conv_b[0][None, :]

    rope_p = _rope_tables(jnp.arange(0, seq, PROMPT_PROJ_TILE), jnp.arange(PROMPT_PROJ_TILE))
    rope_s = _rope_tables(jnp.full((1,), PAST_LEN), jnp.arange(TOKEN_TILE) % dec_s)

    xp = x_prompt.reshape(seq, D_MODEL)
    qa, ka, va, qb, kb, vb, kaf, vaf, kbf, vbf, w_out_b, w_gu_b, w_down_b = _project(
        xp, gn_a, w_in_b, gqa, gka, gqb, gkb, rope_p, keep_all=False, name="proj_prompt",
        cast=(w_out[0], w_gate_up[0], w_down[0]))
    o_p = _attend_prompt(qa, ka, va, qb, kb, vb, u, sink)
    zero_state = jnp.zeros((1, CONV_W - 1, D_FF), _F32)
    y_p, cn_p = _out_ffn(o_p, xp, w_out_b, gn_f, w_gu_b, cw, cb, w_down_b, zero_state,
                         carry_rows=True, name="ffn_prompt")

    xs = x_sample.reshape(dec_b * dec_s, D_MODEL)
    qa, ka, va, qb, kb, vb, kaf_s, vaf_s, kbf_s, vbf_s = _project(
        xs, gn_a, w_in_b, gqa, gka, gqb, gkb, rope_s, keep_all=True, name="proj_sample")
    o_s = _attend_sample(qa, ka, va, qb, kb, vb, _feature_major(cache_a_k[0]), _feature_major(cache_a_v[0]),
                         _feature_major(cache_b_k[0]), _feature_major(cache_b_v[0]), u, sink)
    y_s, cn_s = _out_ffn(o_s, xs, w_out_b, gn_f, w_gu_b, cw, cb, w_down_b, state_conv[0],
                         carry_rows=False, name="ffn_sample")

    return (
        y_p.reshape(1, seq, D_MODEL),
        y_s.reshape(dec_b, dec_s, D_MODEL),
        _token_major(kaf, A_HEADS),
        _token_major(vaf, A_HEADS),
        _token_major(kbf, B_KV_HEADS),
        _token_major(vbf, B_KV_HEADS),
        cn_p.reshape(1, 1, CONV_W - 1, D_FF),
        kaf_s.reshape(1, dec_b, dec_s, A_HEADS, HEAD_DIM),
        vaf_s.reshape(1, dec_b, dec_s, A_HEADS, HEAD_DIM),
        kbf_s.reshape(1, dec_b, dec_s, B_KV_HEADS, HEAD_DIM),
        vbf_s.reshape(1, dec_b, dec_s, B_KV_HEADS, HEAD_DIM),
        cn_s.reshape(1, dec_b, CONV_W - 1, D_FF),
    )
```

```python
import functools

import jax
import jax.numpy as jnp
from jax import lax
from jax.experimental import pallas as pl
from jax.experimental.pallas import tpu as pltpu

D_MODEL = 1024
CHUNK = 64
HEAD_DIM = 64
EPS = 1e-6
SCALE = HEAD_DIM ** -0.5
LOG2E = 1.4426950408889634
Q_SCALE = SCALE * LOG2E
NEG = -1e30
PAST_LEN = 1024
A_HEADS = 8
A_LEFT_CHUNKS = 8
A_CTX = A_LEFT_CHUNKS * CHUNK
A_WIN = A_CTX + CHUNK
MAX_REL = 128
A_WIDTH = A_HEADS * HEAD_DIM
B_HEADS = 8
B_KV_HEADS = 2
B_GROUP = B_HEADS // B_KV_HEADS
B_CTX = 128
B_WIN = B_CTX + CHUNK
B_WIDTH = B_HEADS * HEAD_DIM
B_KV_WIDTH = B_KV_HEADS * HEAD_DIM
ROPE_THETA = 10000.0
IN_COLS = 3 * A_WIDTH + B_WIDTH + 2 * B_KV_WIDTH
D_FF = 2816
CONV_W = 3

LANES = 128
TOKEN_TILE = 512
PROMPT_PROJ_TILE = 1024
ATTN_TILE = 1024
SAMPLE_STREAMS = 4
FF_CHUNK = 256
BIAS_COLS = MAX_REL + CHUNK
BIAS_VAR = 256
VMEM_LIMIT = 44 * 1024 * 1024

_BF16 = jnp.bfloat16
_F32 = jnp.float32


def _const_spec(shape):
    nd = len(shape)
    return pl.BlockSpec(shape, lambda i: (0,) * nd, pipeline_mode=pl.Buffered(1))


def _head_sumsq(t, bd):
    sq = (t * t).astype(_BF16)
    w = t.shape[1]
    if w <= 2 * LANES:
        return jnp.dot(sq, bd[:w, :w], preferred_element_type=_F32)
    parts = [jnp.dot(sq[:, c:c + 2 * LANES], bd, preferred_element_type=_F32) for c in range(0, w, 2 * LANES)]
    return jnp.concatenate(parts, axis=1)


def _head_norm(t, bd, gain):
    ss = _head_sumsq(t, bd)
    return t * lax.rsqrt(ss * (1.0 / HEAD_DIM) + EPS) * gain


def _rope(t, cos, sin_signed):
    outs = []
    for c in range(0, t.shape[1], LANES):
        tc = t[:, c:c + LANES]
        lane = lax.broadcasted_iota(jnp.int32, tc.shape, 1)
        first_half = (lane & (HEAD_DIM - 1)) < HEAD_DIM // 2
        rot = jnp.where(first_half, pltpu.roll(tc, LANES - HEAD_DIM // 2, 1), pltpu.roll(tc, HEAD_DIM // 2, 1))
        outs.append(tc * cos + rot * sin_signed)
    return outs[0] if len(outs) == 1 else jnp.concatenate(outs, axis=1)


def _rep_pair(t):
    rolled = pltpu.roll(t, HEAD_DIM, 1)
    lo = lax.broadcasted_iota(jnp.int32, t.shape, 1) < HEAD_DIM
    return jnp.concatenate([jnp.where(lo, t, rolled), jnp.where(lo, rolled, t)], axis=1)


def _store_heads(ref, t):
    n, w = t.shape
    heads = w // HEAD_DIM
    for hd in range(heads):
        ref[pl.ds(hd, n, stride=heads), :] = t[:, hd * HEAD_DIM:(hd + 1) * HEAD_DIM]


def _proj_kernel(x_ref, gn_ref, w_ref, gqa_ref, gka_ref, gqb_ref, gkb_ref, ct_ref, st_ref, cr_ref, sr_ref, *rest,
                 last_only, n_cast):
    cast_in, rest = rest[:n_cast], rest[n_cast:]
    qa_ref, ka_ref, va_ref, qb_ref, kb_ref, vb_ref, kaf_ref, vaf_ref, kbf_ref, vbf_ref = rest[:10]
    for src, dst in zip(cast_in, rest[10:]):
        dst[...] = src[...].astype(_BF16)
    tm = x_ref.shape[0]
    x = x_ref[...]
    ms = jnp.mean(x * x, axis=-1, keepdims=True)
    h = (x * lax.rsqrt(ms + EPS) * gn_ref[...]).astype(_BF16)

    r = lax.broadcasted_iota(jnp.int32, (2 * LANES, 2 * LANES), 0) // HEAD_DIM
    c = lax.broadcasted_iota(jnp.int32, (2 * LANES, 2 * LANES), 1) // HEAD_DIM
    bd = jnp.where(r == c, 1.0, 0.0).astype(_BF16)
    ct, st, cr, sr = ct_ref[0], st_ref[0], cr_ref[...], sr_ref[...]
    cos = ct * cr - st * sr
    sin = st * cr + ct * sr

    def proj(lo, width):
        return jnp.dot(h, w_ref[:, lo:lo + width], preferred_element_type=_F32)

    o_ka, o_va, o_qb, o_kvb = A_WIDTH, 2 * A_WIDTH, 3 * A_WIDTH, 3 * A_WIDTH + B_WIDTH
    p_qa = proj(0, A_WIDTH)
    p_ka = proj(o_ka, A_WIDTH)
    qa_ref[...] = (_head_norm(p_qa, bd, gqa_ref[...]) * Q_SCALE).astype(_BF16)
    va = proj(o_va, A_WIDTH)
    ka = _head_norm(p_ka, bd, gka_ref[...])
    ka_ref[...] = ka.astype(_BF16)
    p_qb = proj(o_qb, B_WIDTH)
    va_ref[...] = va.astype(_BF16)
    p_kvb = proj(o_kvb, 2 * B_KV_WIDTH)
    qb_ref[...] = (_rope(_head_norm(p_qb, bd, gqb_ref[...]), cos, sin) * Q_SCALE).astype(_BF16)
    kb = _rope(_head_norm(p_kvb[:, :B_KV_WIDTH], bd, gkb_ref[...]), cos, sin)
    vb = p_kvb[:, B_KV_WIDTH:]
    kb_ref[...] = _rep_pair(kb).astype(_BF16)
    vb_ref[...] = _rep_pair(vb).astype(_BF16)

    if last_only:
        @pl.when(pl.program_id(0) == pl.num_programs(0) - 1)
        def _():
            kaf_ref[...] = ka[tm - A_CTX:, :].T
            vaf_ref[...] = va[tm - A_CTX:, :].T
            kbf_ref[...] = kb[tm - B_CTX:, :].T
            vbf_ref[...] = vb[tm - B_CTX:, :].T
    else:
        _store_heads(kaf_ref, ka)
        _store_heads(vaf_ref, va)
        kbf_ref[...] = kb
        vbf_ref[...] = vb


def _cast_blocks(w, n_steps):
    rows = w.shape[0]
    n_blocks = max(d for d in range(1, n_steps + 1) if rows % d == 0 and (rows // d) % 16 == 0)
    return pl.BlockSpec((rows // n_blocks, w.shape[1]), lambda i: (jnp.minimum(i, n_blocks - 1), 0))


def _project(x, gn, w_in, gqa, gka, gqb, gkb, rope, *, keep_all, name, cast=()):
    t = x.shape[0]
    tm = rope[2].shape[0]
    n = t // tm
    cast_specs = [_cast_blocks(w, n) for w in cast]
    tile = lambda w: pl.BlockSpec((tm, w), lambda i: (i, 0))
    if keep_all:
        f32_shapes = [(t * A_HEADS, HEAD_DIM), (t * A_HEADS, HEAD_DIM), (t, B_KV_WIDTH), (t, B_KV_WIDTH)]
        f32_specs = [pl.BlockSpec((tm * A_HEADS, HEAD_DIM), lambda i: (i, 0))] * 2 + [tile(B_KV_WIDTH)] * 2
    else:
        f32_shapes = [(A_WIDTH, A_CTX), (A_WIDTH, A_CTX), (B_KV_WIDTH, B_CTX), (B_KV_WIDTH, B_CTX)]
        f32_specs = [pl.BlockSpec(sh, lambda i: (0, 0)) for sh in f32_shapes]
    ct, st, cr, sr = rope
    if ct.shape[0] == n:
        tile_trig = pl.BlockSpec((1, 1, LANES), lambda i: (i, 0, 0))
    else:
        tile_trig = _const_spec((1, 1, LANES))
    out_shape = [
        jax.ShapeDtypeStruct((t, A_WIDTH), _BF16), jax.ShapeDtypeStruct((t, A_WIDTH), _BF16),
        jax.ShapeDtypeStruct((t, A_WIDTH), _BF16), jax.ShapeDtypeStruct((t, B_WIDTH), _BF16),
        jax.ShapeDtypeStruct((t, 2 * B_KV_WIDTH), _BF16), jax.ShapeDtypeStruct((t, 2 * B_KV_WIDTH), _BF16),
    ] + [jax.ShapeDtypeStruct(sh, _F32) for sh in f32_shapes] + [jax.ShapeDtypeStruct(w.shape, _BF16) for w in cast]
    out_specs = [
        tile(A_WIDTH), tile(A_WIDTH), tile(A_WIDTH), tile(B_WIDTH), tile(2 * B_KV_WIDTH), tile(2 * B_KV_WIDTH),
    ] + f32_specs + cast_specs
    in_specs = [
        tile(D_MODEL), _const_spec((1, D_MODEL)), _const_spec((D_MODEL, IN_COLS)),
        _const_spec((1, A_WIDTH)), _const_spec((1, A_WIDTH)), _const_spec((1, B_WIDTH)),
        _const_spec((1, B_KV_WIDTH)), tile_trig, tile_trig, _const_spec((tm, LANES)), _const_spec((tm, LANES)),
    ] + cast_specs
    return pl.pallas_call(
        functools.partial(_proj_kernel, last_only=not keep_all, n_cast=len(cast)),
        grid=(n,), in_specs=in_specs, out_specs=out_specs, out_shape=out_shape, name=name,
        compiler_params=pltpu.CompilerParams(dimension_semantics=("arbitrary",), vmem_limit_bytes=VMEM_LIMIT),
    )(x, gn, w_in, gqa, gka, gqb, gkb, ct, st, cr, sr, *cast)


def _build_bias(u_ref, bias_ref):
    qrow = lax.broadcasted_iota(jnp.int32, (CHUNK, BIAS_VAR), 0)
    for h in range(A_HEADS):
        prof = u_ref[h:h + 1, :]
        y = jnp.broadcast_to(prof - prof[:, 0:1], (CHUNK, BIAS_VAR))
        y = pltpu.roll(y, BIAS_VAR - (CHUNK - 1), 1)
        for b in range(6):
            y = jnp.where(((qrow >> b) & 1) == 1, pltpu.roll(y, 1 << b, 1), y)
        bias_ref[h // 4, (h % 4) * CHUNK:(h % 4 + 1) * CHUNK, :] = y[:, :BIAS_COLS] * LOG2E


_NT = (((1,), (1,)), ((), ()))
_NN = (((1,), (0,)), ((), ()))


def _scores(qa, qb, keys_a, keys_b, bias_ref, lim_a, lim_b):
    out = []
    head_of_lane = lax.broadcasted_iota(jnp.int32, (CHUNK, 2 * LANES), 1) // HEAD_DIM
    for g in range(2):
        qg = qa[:, 2 * LANES * g:2 * LANES * (g + 1)]
        qbd = jnp.concatenate([jnp.where(head_of_lane == r, qg, jnp.zeros_like(qg)) for r in range(4)], axis=0)
        pieces, off = [], 0
        for keys, on_rows in keys_a(g):
            s = lax.dot_general(qbd, keys, _NT if on_rows else _NN, preferred_element_type=_F32)
            n = s.shape[1]
            if off + n > A_WIN - BIAS_COLS:
                lo_col = max(A_WIN - BIAS_COLS - off, 0)
                biased = s[:, lo_col:] + bias_ref[g, :, off + lo_col - (A_WIN - BIAS_COLS):off + n - (A_WIN - BIAS_COLS)]
                s = biased if lo_col == 0 else jnp.concatenate([s[:, :lo_col], biased], axis=1)
            if lim_a:
                s = jnp.where(lax.broadcasted_iota(jnp.int32, s.shape, 1) >= lim_a, s, NEG)
            pieces.append(s)
            off += n
        out.append(pieces)
    lo = lax.broadcasted_iota(jnp.int32, (CHUNK, LANES), 1) < HEAD_DIM
    for g in range(B_KV_HEADS):
        rows = []
        for r in range(B_GROUP):
            hd = B_GROUP * g + r
            qp = qb[:, LANES * (hd // 2):LANES * (hd // 2 + 1)]
            keep = lo if hd % 2 == 0 else jnp.logical_not(lo)
            rows.append(jnp.where(keep, qp, jnp.zeros_like(qp)))
        qs = jnp.concatenate(rows, axis=0)
        pieces = []
        for keys, on_rows in keys_b(g):
            s = lax.dot_general(qs, keys, _NT if on_rows else _NN, preferred_element_type=_F32)
            if lim_b:
                s = jnp.where(lax.broadcasted_iota(jnp.int32, s.shape, 1) >= lim_b, s, NEG)
            pieces.append(s)
        out.append(pieces)
    return out


def _row_reduce(pieces, combine, reduce, fill):
    acc = None
    for s in pieces:
        for c in range(0, s.shape[1], LANES):
            tile = s[:, c:c + LANES]
            if tile.shape[1] < LANES:
                pad = jnp.full((tile.shape[0], LANES - tile.shape[1]), fill, tile.dtype)
                tile = jnp.concatenate([tile, pad], axis=1)
            acc = tile if acc is None else combine(acc, tile)
    return reduce(acc, axis=1, keepdims=True)


def _softmax_values(pieces, vals, sink):
    m = _row_reduce(pieces, jnp.maximum, jnp.max, -jnp.inf)
    if sink is not None:
        m = jnp.maximum(m, sink)
    es = [jnp.exp2(s - m) for s in pieces]
    l = _row_reduce(es, jnp.add, jnp.sum, 0.0)
    if sink is not None:
        l = l + jnp.exp2(sink - m)
    acc = None
    for e, (v, on_rows) in zip(es, vals):
        part = lax.dot_general(e.astype(_BF16), v, _NN if on_rows else _NT, preferred_element_type=_F32)
        acc = part if acc is None else acc + part
    return acc / l


def _softmax_pv(scores, vals_a, vals_b, sink_ref):
    outs = []
    head_of_lane = lax.broadcasted_iota(jnp.int32, (CHUNK, 2 * LANES), 1) // HEAD_DIM
    for g in range(2):
        of = _softmax_values(scores[g], vals_a(g), None)
        og = of[3 * CHUNK:4 * CHUNK]
        for r in (2, 1, 0):
            og = jnp.where(head_of_lane == r, of[r * CHUNK:(r + 1) * CHUNK], og)
        outs.append(og)
    lo = lax.broadcasted_iota(jnp.int32, (CHUNK, LANES), 1) < HEAD_DIM
    for g in range(B_KV_HEADS):
        of = _softmax_values(scores[2 + g], vals_b(g), sink_ref[g] * LOG2E)
        outs.append(jnp.where(lo, of[0:CHUNK], of[CHUNK:2 * CHUNK]))
        outs.append(jnp.where(lo, of[2 * CHUNK:3 * CHUNK], of[3 * CHUNK:4 * CHUNK]))
    return jnp.concatenate(outs, axis=1)


def _attn_prompt_kernel(qa_ref, ka_ref, va_ref, qb_ref, kb_ref, vb_ref, u_ref, sink_ref, o_ref,
                        kwa, vwa, kwb, vwb, bias):
    i = pl.program_id(0)
    tm = qa_ref.shape[0]
    n_chunks = tm // CHUNK

    @pl.when(i == 0)
    def _():
        kwa[0:A_CTX, :] = jnp.zeros((A_CTX, A_WIDTH), _BF16)
        vwa[0:A_CTX, :] = jnp.zeros((A_CTX, A_WIDTH), _BF16)
        kwb[0:B_CTX, :] = jnp.zeros((B_CTX, 2 * B_KV_WIDTH), _BF16)
        vwb[0:B_CTX, :] = jnp.zeros((B_CTX, 2 * B_KV_WIDTH), _BF16)
        _build_bias(u_ref, bias)

    @pl.when(i > 0)
    def _():
        kwa[0:A_CTX, :] = kwa[tm:tm + A_CTX, :]
        vwa[0:A_CTX, :] = vwa[tm:tm + A_CTX, :]
        kwb[0:B_CTX, :] = kwb[tm:tm + B_CTX, :]
        vwb[0:B_CTX, :] = vwb[tm:tm + B_CTX, :]

    kwa[A_CTX:A_CTX + tm, :] = ka_ref[...]
    vwa[A_CTX:A_CTX + tm, :] = va_ref[...]
    kwb[B_CTX:B_CTX + tm, :] = kb_ref[...]
    vwb[B_CTX:B_CTX + tm, :] = vb_ref[...]

    def window(ref, r0, rows, width):
        return lambda g: [(ref[r0:r0 + rows, width * g:width * (g + 1)], True)]

    def run(masked):
        def scores(c):
            r0 = c * CHUNK
            lim_a = max(A_LEFT_CHUNKS - c, 0) * CHUNK if masked else 0
            lim_b = max(B_CTX // CHUNK - c, 0) * CHUNK if masked else 0
            return _scores(qa_ref[r0:r0 + CHUNK, :], qb_ref[r0:r0 + CHUNK, :],
                           window(kwa, r0, A_WIN, 2 * LANES), window(kwb, r0, B_WIN, LANES), bias, lim_a, lim_b)

        nxt = scores(0)
        for c in range(n_chunks):
            cur = nxt
            if c + 1 < n_chunks:
                nxt = scores(c + 1)
            r0 = c * CHUNK
            o = _softmax_pv(cur, window(vwa, r0, A_WIN, 2 * LANES), window(vwb, r0, B_WIN, LANES), sink_ref)
            o_ref[r0:r0 + CHUNK, :] = o.astype(_BF16)

    @pl.when(i == 0)
    def _():
        run(True)

    @pl.when(i > 0)
    def _():
        run(False)


def _attn_sample_kernel(qa_ref, ka_ref, va_ref, qb_ref, kb_ref, vb_ref, cak_ref, cav_ref, cbk_ref, cbv_ref,
                        u_ref, sink_ref, o_ref, bias):
    n_streams = cak_ref.shape[0]

    @pl.when(pl.program_id(0) == 0)
    def _():
        _build_bias(u_ref, bias)

    def pieces_a(cache_ref, new_ref, b):
        rows = slice(b * CHUNK, (b + 1) * CHUNK)

        def get(g):
            cols = slice(2 * LANES * g, 2 * LANES * (g + 1))
            return [(cache_ref[b, cols, :].astype(_BF16), False), (new_ref[rows, cols], True)]
        return get

    def pieces_b(cache_ref, new_ref, b):
        rows = slice(b * CHUNK, (b + 1) * CHUNK)

        def get(g):
            past = cache_ref[b, HEAD_DIM * g:HEAD_DIM * (g + 1), :].astype(_BF16)
            return [(jnp.concatenate([past, past], axis=0), False), (new_ref[rows, LANES * g:LANES * (g + 1)], True)]
        return get

    def scores(b):
        rows = slice(b * CHUNK, (b + 1) * CHUNK)
        return _scores(qa_ref[rows, :], qb_ref[rows, :], pieces_a(cak_ref, ka_ref, b), pieces_b(cbk_ref, kb_ref, b),
                       bias, 0, 0)

    nxt = scores(0)
    for b in range(n_streams):
        cur = nxt
        if b + 1 < n_streams:
            nxt = scores(b + 1)
        o = _softmax_pv(cur, pieces_a(cav_ref, va_ref, b), pieces_b(cbv_ref, vb_ref, b), sink_ref)
        o_ref[b * CHUNK:(b + 1) * CHUNK, :] = o.astype(_BF16)


_BIAS_SCRATCH = pltpu.VMEM((2, 4 * CHUNK, BIAS_COLS), _F32)


def _attend_prompt(qa, ka, va, qb, kb, vb, u, sink):
    t = qa.shape[0]
    tm = ATTN_TILE
    tile = lambda w: pl.BlockSpec((tm, w), lambda i: (i, 0))
    return pl.pallas_call(
        _attn_prompt_kernel,
        grid=(t // tm,),
        in_specs=[tile(A_WIDTH), tile(A_WIDTH), tile(A_WIDTH), tile(B_WIDTH), tile(2 * B_KV_WIDTH),
                  tile(2 * B_KV_WIDTH), _const_spec(u.shape), _const_spec(sink.shape)],
        out_specs=tile(A_WIDTH + B_WIDTH),
        out_shape=jax.ShapeDtypeStruct((t, A_WIDTH + B_WIDTH), _BF16),
        scratch_shapes=[pltpu.VMEM((A_CTX + tm, A_WIDTH), _BF16), pltpu.VMEM((A_CTX + tm, A_WIDTH), _BF16),
                        pltpu.VMEM((B_CTX + tm, 2 * B_KV_WIDTH), _BF16),
                        pltpu.VMEM((B_CTX + tm, 2 * B_KV_WIDTH), _BF16), _BIAS_SCRATCH],
        name="attn_prompt",
        compiler_params=pltpu.CompilerParams(dimension_semantics=("arbitrary",), vmem_limit_bytes=VMEM_LIMIT),
    )(qa, ka, va, qb, kb, vb, u, sink)


def _attend_sample(qa, ka, va, qb, kb, vb, cak, cav, cbk, cbv, u, sink):
    t = qa.shape[0]
    sb = SAMPLE_STREAMS
    tile = lambda w: pl.BlockSpec((sb * CHUNK, w), lambda i: (i, 0))
    cache = lambda arr: pl.BlockSpec((sb,) + arr.shape[1:], lambda i: (i, 0, 0))
    return pl.pallas_call(
        _attn_sample_kernel,
        grid=(t // (sb * CHUNK),),
        in_specs=[tile(A_WIDTH), tile(A_WIDTH), tile(A_WIDTH), tile(B_WIDTH), tile(2 * B_KV_WIDTH),
                  tile(2 * B_KV_WIDTH), cache(cak), cache(cav), cache(cbk), cache(cbv),
                  _const_spec(u.shape), _const_spec(sink.shape)],
        out_specs=tile(A_WIDTH + B_WIDTH),
        out_shape=jax.ShapeDtypeStruct((t, A_WIDTH + B_WIDTH), _BF16),
        scratch_shapes=[_BIAS_SCRATCH],
        name="attn_sample",
        compiler_params=pltpu.CompilerParams(dimension_semantics=("arbitrary",), vmem_limit_bytes=VMEM_LIMIT),
    )(qa, ka, va, qb, kb, vb, cak, cav, cbk, cbv, u, sink)


def _ffn_kernel(o_ref, x_ref, wo_ref, gn_ref, wgu_ref, cw_ref, cb_ref, wd_ref, st_ref,
                y_ref, cn_ref, carry, act_buf, *, seg_len, carry_rows):
    i = pl.program_id(0)
    tm = x_ref.shape[0]
    n_seg = tm // seg_len

    if carry_rows:
        @pl.when(i == 0)
        def _():
            carry[...] = st_ref[0]

    y1 = x_ref[...] + jnp.dot(o_ref[...], wo_ref[...], preferred_element_type=_F32)
    ms = jnp.mean(y1 * y1, axis=-1, keepdims=True)
    h = (y1 * lax.rsqrt(ms + EPS) * gn_ref[...]).astype(_BF16)

    row8 = lax.broadcasted_iota(jnp.int32, (8, FF_CHUNK), 0)

    def gate_up(c0):
        g = jnp.dot(h, wgu_ref[:, c0:c0 + FF_CHUNK], preferred_element_type=_F32)
        up = jnp.dot(h, wgu_ref[:, D_FF + c0:D_FF + c0 + FF_CHUNK], preferred_element_type=_F32)
        return g, up

    nxt = gate_up(0)
    for c0 in range(0, D_FF, FF_CHUNK):
        cols = slice(c0, c0 + FF_CHUNK)
        g, up = nxt
        if c0 + FF_CHUNK < D_FF:
            nxt = gate_up(c0 + FF_CHUNK)
        g1 = pltpu.roll(g, 1, 0)
        g2 = pltpu.roll(g, 2, 0)
        p1, p2 = [], []
        for s in range(n_seg):
            a = s * seg_len
            prev = carry[:, cols] if carry_rows else st_ref[s, :, cols]
            prev0 = jnp.broadcast_to(prev[0:1, :], (8, FF_CHUNK))
            prev1 = jnp.broadcast_to(prev[1:2, :], (8, FF_CHUNK))
            p1.append(jnp.where(row8 == 0, prev1, g1[a:a + 8]))
            p1.append(g1[a + 8:a + seg_len])
            p2.append(jnp.where(row8 == 0, prev0, jnp.where(row8 == 1, prev1, g2[a:a + 8])))
            p2.append(g2[a + 8:a + seg_len])
            tail = g[a + seg_len - 8:a + seg_len][8 - (CONV_W - 1):, :]
            if carry_rows:
                carry[:, cols] = tail
            else:
                cn_ref[s, :, cols] = tail
        g1 = jnp.concatenate(p1, axis=0)
        g2 = jnp.concatenate(p2, axis=0)
        conv = cb_ref[:, cols] + cw_ref[0:1, cols] * g2 + cw_ref[1:2, cols] * g1 + cw_ref[2:3, cols] * g
        act_buf[:, cols] = (conv * jax.nn.sigmoid(conv) * up).astype(_BF16)
    y_ref[...] = y1 + jnp.dot(act_buf[...], wd_ref[...], preferred_element_type=_F32)
    if carry_rows:
        cn_ref[0] = carry[...]


def _out_ffn(o, x, w_out, gn, w_gu, conv_w, conv_b, w_down, state, *, carry_rows, name):
    t = x.shape[0]
    tm = TOKEN_TILE
    n = t // tm
    seg_len = tm if carry_rows else CHUNK
    n_seg = tm // seg_len
    tile = lambda w: pl.BlockSpec((tm, w), lambda i: (i, 0))
    if carry_rows:
        st_spec = _const_spec((1, CONV_W - 1, D_FF))
        cn_spec = pl.BlockSpec((1, CONV_W - 1, D_FF), lambda i: (0, 0, 0))
        cn_shape = (1, CONV_W - 1, D_FF)
    else:
        st_spec = pl.BlockSpec((n_seg, CONV_W - 1, D_FF), lambda i: (i, 0, 0))
        cn_spec = pl.BlockSpec((n_seg, CONV_W - 1, D_FF), lambda i: (i, 0, 0))
        cn_shape = (t // seg_len, CONV_W - 1, D_FF)
    return pl.pallas_call(
        functools.partial(_ffn_kernel, seg_len=seg_len, carry_rows=carry_rows),
        grid=(n,),
        in_specs=[tile(A_WIDTH + B_WIDTH), tile(D_MODEL), _const_spec((D_MODEL, D_MODEL)),
                  _const_spec((1, D_MODEL)), _const_spec((D_MODEL, 2 * D_FF)), _const_spec((CONV_W, D_FF)),
                  _const_spec((1, D_FF)), _const_spec((D_FF, D_MODEL)), st_spec],
        out_specs=[tile(D_MODEL), cn_spec],
        out_shape=[jax.ShapeDtypeStruct((t, D_MODEL), _F32), jax.ShapeDtypeStruct(cn_shape, _F32)],
        scratch_shapes=[pltpu.VMEM((CONV_W - 1, D_FF), _F32), pltpu.VMEM((tm, D_FF), _BF16)],
        name=name,
        compiler_params=pltpu.CompilerParams(dimension_semantics=("arbitrary",), vmem_limit_bytes=VMEM_LIMIT),
    )(o, x, w_out, gn, w_gu, conv_w, conv_b, w_down, state)


def _rope_tables(tile_pos, row_pos):
    half = HEAD_DIM // 2
    lane = jnp.arange(LANES)
    inv = jnp.power(ROPE_THETA, -(lane % half).astype(_F32) / half)
    sign = jnp.where(lane % HEAD_DIM < half, -1.0, 1.0).astype(_F32)
    a = tile_pos.astype(_F32)[:, None] * inv[None, :]
    b = row_pos.astype(_F32)[:, None] * inv[None, :]
    return (jnp.cos(a)[:, None, :], (jnp.sin(a) * sign)[:, None, :], jnp.cos(b), jnp.sin(b) * sign)


def _feature_major(cache):
    n, t, heads, d = cache.shape
    return jnp.transpose(cache, (0, 2, 3, 1)).reshape(n, heads * d, t)


def _token_major(rows, heads):
    t = rows.shape[1]
    return jnp.transpose(rows.reshape(heads, HEAD_DIM, t), (2, 0, 1))[None, None]


def _bias_profile(table):
    n_clip = BIAS_COLS - 1 - MAX_REL
    head = jnp.broadcast_to(table[:, 2 * MAX_REL:], (A_HEADS, n_clip))
    return jnp.concatenate([head, table[:, ::-1][:, :BIAS_VAR - n_clip]], axis=1)


def kernel(x_prompt, x_sample, cache_a_k, cache_a_v, cache_b_k, cache_b_v, state_conv, norm_attn, w_in, q_norm_a,
           k_norm_a, rel_bias_a, q_norm_b, k_norm_b, sinks_b, w_out, norm_ffn, w_gate_up, conv_w, conv_b, w_down):
    assert norm_attn.shape[0] == 1
    bsz, seq, _ = x_prompt.shape
    dec_b, dec_s, _ = x_sample.shape
    assert bsz == 1 and dec_s == CHUNK and (dec_b * dec_s) % TOKEN_TILE == 0
    assert seq % TOKEN_TILE == 0 and seq % PROMPT_PROJ_TILE == 0 and seq % ATTN_TILE == 0
    assert cache_a_k.shape[2] == A_CTX and cache_b_k.shape[2] == B_CTX

    w_in_b = w_in[0].astype(_BF16)
    gn_a = norm_attn[0][None, :]
    gn_f = norm_ffn[0][None, :]
    gqa = jnp.tile(q_norm_a[0], A_HEADS)[None, :]
    gka = jnp.tile(k_norm_a[0], A_HEADS)[None, :]
    gqb = jnp.tile(q_norm_b[0], B_HEADS)[None, :]
    gkb = jnp.tile(k_norm_b[0], B_KV_HEADS)[None, :]
    u = _bias_profile(rel_bias_a[0])
    sink = jnp.repeat(sinks_b[0], CHUNK).reshape(B_KV_HEADS, B_GROUP * CHUNK, 1)
    cw = conv_w[0]
    cb = conv_b[0][None, :]

    rope_p = _rope_tables(jnp.arange(0, seq, PROMPT_PROJ_TILE), jnp.arange(PROMPT_PROJ_TILE))
    rope_s = _rope_tables(jnp.full((1,), PAST_LEN), jnp.arange(TOKEN_TILE) % dec_s)

    xp = x_prompt.reshape(seq, D_MODEL)
    qa, ka, va, qb, kb, vb, kaf, vaf, kbf, vbf, w_out_b, w_gu_b, w_down_b = _project(
        xp, gn_a, w_in_b, gqa, gka, gqb, gkb, rope_p, keep_all=False, name="proj_prompt",
        cast=(w_out[0], w_gate_up[0], w_down[0]))
    o_p = _attend_prompt(qa, ka, va, qb, kb, vb, u, sink)
    zero_state = jnp.zeros((1, CONV_W - 1, D_FF), _F32)
    y_p, cn_p = _out_ffn(o_p, xp, w_out_b, gn_f, w_gu_b, cw, cb, w_down_b, zero_state,
                         carry_rows=True, name="ffn_prompt")

    xs = x_sample.reshape(dec_b * dec_s, D_MODEL)
    qa, ka, va, qb, kb, vb, kaf_s, vaf_s, kbf_s, vbf_s = _project(
        xs, gn_a, w_in_b, gqa, gka, gqb, gkb, rope_s, keep_all=True, name="proj_sample")
    o_s = _attend_sample(qa, ka, va, qb, kb, vb, _feature_major(cache_a_k[0]), _feature_major(cache_a_v[0]),
                         _feature_major(cache_b_k[0]), _feature_major(cache_b_v[0]), u, sink)
    y_s, cn_s = _out_ffn(o_s, xs, w_out_b, gn_f, w_gu_b, cw, cb, w_down_b, state_conv[0],
                         carry_rows=False, name="ffn_sample")

    return (
        y_p.reshape(1, seq, D_MODEL),
        y_s.reshape(dec_b, dec_s, D_MODEL),
        _token_major(kaf, A_HEADS),
        _token_major(vaf, A_HEADS),
        _token_major(kbf, B_KV_HEADS),
        _token_major(vbf, B_KV_HEADS),
        cn_p.reshape(1, 1, CONV_W - 1, D_FF),
        kaf_s.reshape(1, dec_b, dec_s, A_HEADS, HEAD_DIM),
        vaf_s.reshape(1, dec_b, dec_s, A_HEADS, HEAD_DIM),
        kbf_s.reshape(1, dec_b, dec_s, B_KV_HEADS, HEAD_DIM),
        vbf_s.reshape(1, dec_b, dec_s, B_KV_HEADS, HEAD_DIM),
        cn_s.reshape(1, dec_b, CONV_W - 1, D_FF),
    )
```

```python
import functools

import jax
import jax.numpy as jnp
from jax import lax
from jax.experimental import pallas as pl
from jax.experimental.pallas import tpu as pltpu

D_MODEL = 1024
CHUNK = 64
HEAD_DIM = 64
EPS = 1e-6
SCALE = HEAD_DIM ** -0.5
LOG2E = 1.4426950408889634
Q_SCALE = SCALE * LOG2E
NEG = -1e30
PAST_LEN = 1024
A_HEADS = 8
A_LEFT_CHUNKS = 8
A_CTX = A_LEFT_CHUNKS * CHUNK
A_WIN = A_CTX + CHUNK
MAX_REL = 128
A_WIDTH = A_HEADS * HEAD_DIM
B_HEADS = 8
B_KV_HEADS = 2
B_GROUP = B_HEADS // B_KV_HEADS
B_CTX = 128
B_WIN = B_CTX + CHUNK
B_WIDTH = B_HEADS * HEAD_DIM
B_KV_WIDTH = B_KV_HEADS * HEAD_DIM
ROPE_THETA = 10000.0
IN_COLS = 3 * A_WIDTH + B_WIDTH + 2 * B_KV_WIDTH
D_FF = 2816
CONV_W = 3

LANES = 128
TOKEN_TILE = 512
PROMPT_PROJ_TILE = 1024
ATTN_TILE = 1024
SAMPLE_STREAMS = 8
FF_CHUNK = 256
BIAS_COLS = MAX_REL + CHUNK
BIAS_VAR = 256
VMEM_LIMIT = 48 * 1024 * 1024

_BF16 = jnp.bfloat16
_F32 = jnp.float32


def _const_spec(shape):
    nd = len(shape)
    return pl.BlockSpec(shape, lambda i: (0,) * nd, pipeline_mode=pl.Buffered(1))


def _head_sumsq(t, bd):
    sq = (t * t).astype(_BF16)
    w = t.shape[1]
    if w <= 2 * LANES:
        return jnp.dot(sq, bd[:w, :w], preferred_element_type=_F32)
    parts = [jnp.dot(sq[:, c:c + 2 * LANES], bd, preferred_element_type=_F32) for c in range(0, w, 2 * LANES)]
    return jnp.concatenate(parts, axis=1)


def _head_norm(t, bd, gain):
    ss = _head_sumsq(t, bd)
    return t * lax.rsqrt(ss * (1.0 / HEAD_DIM) + EPS) * gain


def _rope(t, cos, sin_signed):
    outs = []
    for c in range(0, t.shape[1], LANES):
        tc = t[:, c:c + LANES]
        lane = lax.broadcasted_iota(jnp.int32, tc.shape, 1)
        first_half = (lane & (HEAD_DIM - 1)) < HEAD_DIM // 2
        rot = jnp.where(first_half, pltpu.roll(tc, LANES - HEAD_DIM // 2, 1), pltpu.roll(tc, HEAD_DIM // 2, 1))
        outs.append(tc * cos + rot * sin_signed)
    return outs[0] if len(outs) == 1 else jnp.concatenate(outs, axis=1)


def _rep_pair(t):
    rolled = pltpu.roll(t, HEAD_DIM, 1)
    lo = lax.broadcasted_iota(jnp.int32, t.shape, 1) < HEAD_DIM
    return jnp.concatenate([jnp.where(lo, t, rolled), jnp.where(lo, rolled, t)], axis=1)


def _store_heads(ref, t):
    n, w = t.shape
    heads = w // HEAD_DIM
    for hd in range(heads):
        ref[pl.ds(hd, n, stride=heads), :] = t[:, hd * HEAD_DIM:(hd + 1) * HEAD_DIM]


def _proj_kernel(x_ref, gn_ref, w_ref, gqa_ref, gka_ref, gqb_ref, gkb_ref, ct_ref, st_ref, cr_ref, sr_ref, *rest,
                 last_only, n_cast):
    cast_in, rest = rest[:n_cast], rest[n_cast:]
    qa_ref, ka_ref, va_ref, qb_ref, kb_ref, vb_ref, kaf_ref, vaf_ref, kbf_ref, vbf_ref = rest[:10]
    for src, dst in zip(cast_in, rest[10:]):
        dst[...] = src[...].astype(_BF16)
    tm = x_ref.shape[0]
    x = x_ref[...]
    ms = jnp.mean(x * x, axis=-1, keepdims=True)
    h = (x * lax.rsqrt(ms + EPS) * gn_ref[...]).astype(_BF16)

    r = lax.broadcasted_iota(jnp.int32, (2 * LANES, 2 * LANES), 0) // HEAD_DIM
    c = lax.broadcasted_iota(jnp.int32, (2 * LANES, 2 * LANES), 1) // HEAD_DIM
    bd = jnp.where(r == c, 1.0, 0.0).astype(_BF16)
    ct, st, cr, sr = ct_ref[0], st_ref[0], cr_ref[...], sr_ref[...]
    cos = ct * cr - st * sr
    sin = st * cr + ct * sr

    def proj(lo, width):
        return jnp.dot(h, w_ref[:, lo:lo + width], preferred_element_type=_F32)

    o_ka, o_va, o_qb, o_kvb = A_WIDTH, 2 * A_WIDTH, 3 * A_WIDTH, 3 * A_WIDTH + B_WIDTH
    p_qa = proj(0, A_WIDTH)
    p_ka = proj(o_ka, A_WIDTH)
    qa_ref[...] = (_head_norm(p_qa, bd, gqa_ref[...]) * Q_SCALE).astype(_BF16)
    va = proj(o_va, A_WIDTH)
    ka = _head_norm(p_ka, bd, gka_ref[...])
    ka_ref[...] = ka.astype(_BF16)
    p_qb = proj(o_qb, B_WIDTH)
    va_ref[...] = va.astype(_BF16)
    p_kvb = proj(o_kvb, 2 * B_KV_WIDTH)
    qb_ref[...] = (_rope(_head_norm(p_qb, bd, gqb_ref[...]), cos, sin) * Q_SCALE).astype(_BF16)
    kb = _rope(_head_norm(p_kvb[:, :B_KV_WIDTH], bd, gkb_ref[...]), cos, sin)
    vb = p_kvb[:, B_KV_WIDTH:]
    kb_ref[...] = _rep_pair(kb).astype(_BF16)
    vb_ref[...] = _rep_pair(vb).astype(_BF16)

    if last_only:
        @pl.when(pl.program_id(0) == pl.num_programs(0) - 1)
        def _():
            kaf_ref[...] = ka[tm - A_CTX:, :].T
            vaf_ref[...] = va[tm - A_CTX:, :].T
            kbf_ref[...] = kb[tm - B_CTX:, :].T
            vbf_ref[...] = vb[tm - B_CTX:, :].T
    else:
        _store_heads(kaf_ref, ka)
        _store_heads(vaf_ref, va)
        kbf_ref[...] = kb
        vbf_ref[...] = vb


def _cast_blocks(w, n_steps):
    rows = w.shape[0]
    n_blocks = max(d for d in range(1, n_steps + 1) if rows % d == 0 and (rows // d) % 16 == 0)
    return pl.BlockSpec((rows // n_blocks, w.shape[1]), lambda i: (jnp.minimum(i, n_blocks - 1), 0))


def _project(x, gn, w_in, gqa, gka, gqb, gkb, rope, *, keep_all, name, cast=()):
    t = x.shape[0]
    tm = rope[2].shape[0]
    n = t // tm
    cast_specs = [_cast_blocks(w, n) for w in cast]
    tile = lambda w: pl.BlockSpec((tm, w), lambda i: (i, 0))
    if keep_all:
        f32_shapes = [(t * A_HEADS, HEAD_DIM), (t * A_HEADS, HEAD_DIM), (t, B_KV_WIDTH), (t, B_KV_WIDTH)]
        f32_specs = [pl.BlockSpec((tm * A_HEADS, HEAD_DIM), lambda i: (i, 0))] * 2 + [tile(B_KV_WIDTH)] * 2
    else:
        f32_shapes = [(A_WIDTH, A_CTX), (A_WIDTH, A_CTX), (B_KV_WIDTH, B_CTX), (B_KV_WIDTH, B_CTX)]
        f32_specs = [pl.BlockSpec(sh, lambda i: (0, 0)) for sh in f32_shapes]
    ct, st, cr, sr = rope
    if ct.shape[0] == n:
        tile_trig = pl.BlockSpec((1, 1, LANES), lambda i: (i, 0, 0))
    else:
        tile_trig = _const_spec((1, 1, LANES))
    out_shape = [
        jax.ShapeDtypeStruct((t, A_WIDTH), _BF16), jax.ShapeDtypeStruct((t, A_WIDTH), _BF16),
        jax.ShapeDtypeStruct((t, A_WIDTH), _BF16), jax.ShapeDtypeStruct((t, B_WIDTH), _BF16),
        jax.ShapeDtypeStruct((t, 2 * B_KV_WIDTH), _BF16), jax.ShapeDtypeStruct((t, 2 * B_KV_WIDTH), _BF16),
    ] + [jax.ShapeDtypeStruct(sh, _F32) for sh in f32_shapes] + [jax.ShapeDtypeStruct(w.shape, _BF16) for w in cast]
    out_specs = [
        tile(A_WIDTH), tile(A_WIDTH), tile(A_WIDTH), tile(B_WIDTH), tile(2 * B_KV_WIDTH), tile(2 * B_KV_WIDTH),
    ] + f32_specs + cast_specs
    in_specs = [
        tile(D_MODEL), _const_spec((1, D_MODEL)), _const_spec((D_MODEL, IN_COLS)),
        _const_spec((1, A_WIDTH)), _const_spec((1, A_WIDTH)), _const_spec((1, B_WIDTH)),
        _const_spec((1, B_KV_WIDTH)), tile_trig, tile_trig, _const_spec((tm, LANES)), _const_spec((tm, LANES)),
    ] + cast_specs
    return pl.pallas_call(
        functools.partial(_proj_kernel, last_only=not keep_all, n_cast=len(cast)),
        grid=(n,), in_specs=in_specs, out_specs=out_specs, out_shape=out_shape, name=name,
        compiler_params=pltpu.CompilerParams(dimension_semantics=("arbitrary",), vmem_limit_bytes=VMEM_LIMIT),
    )(x, gn, w_in, gqa, gka, gqb, gkb, ct, st, cr, sr, *cast)


def _build_bias(u_ref, bias_ref):
    qrow = lax.broadcasted_iota(jnp.int32, (CHUNK, BIAS_VAR), 0)
    for h in range(A_HEADS):
        prof = u_ref[h:h + 1, :]
        y = jnp.broadcast_to(prof - prof[:, 0:1], (CHUNK, BIAS_VAR))
        y = pltpu.roll(y, BIAS_VAR - (CHUNK - 1), 1)
        for b in range(6):
            y = jnp.where(((qrow >> b) & 1) == 1, pltpu.roll(y, 1 << b, 1), y)
        bias_ref[h // 4, (h % 4) * CHUNK:(h % 4 + 1) * CHUNK, :] = y[:, :BIAS_COLS] * LOG2E


_NT = (((1,), (1,)), ((), ()))
_NN = (((1,), (0,)), ((), ()))


def _scores(qa, qb, keys_a, keys_b, bias_ref, lim_a, lim_b):
    out = []
    head_of_lane = lax.broadcasted_iota(jnp.int32, (CHUNK, 2 * LANES), 1) // HEAD_DIM
    for g in range(2):
        qg = qa[:, 2 * LANES * g:2 * LANES * (g + 1)]
        qbd = jnp.concatenate([jnp.where(head_of_lane == r, qg, jnp.zeros_like(qg)) for r in range(4)], axis=0)
        pieces, off = [], 0
        for keys, on_rows in keys_a(g):
            s = lax.dot_general(qbd, keys, _NT if on_rows else _NN, preferred_element_type=_F32)
            n = s.shape[1]
            if off + n > A_WIN - BIAS_COLS:
                lo_col = max(A_WIN - BIAS_COLS - off, 0)
                biased = s[:, lo_col:] + bias_ref[g, :, off + lo_col - (A_WIN - BIAS_COLS):off + n - (A_WIN - BIAS_COLS)]
                s = biased if lo_col == 0 else jnp.concatenate([s[:, :lo_col], biased], axis=1)
            if lim_a:
                s = jnp.where(lax.broadcasted_iota(jnp.int32, s.shape, 1) >= lim_a, s, NEG)
            pieces.append(s)
            off += n
        out.append(pieces)
    lo = lax.broadcasted_iota(jnp.int32, (CHUNK, LANES), 1) < HEAD_DIM
    for g in range(B_KV_HEADS):
        rows = []
        for r in range(B_GROUP):
            hd = B_GROUP * g + r
            qp = qb[:, LANES * (hd // 2):LANES * (hd // 2 + 1)]
            keep = lo if hd % 2 == 0 else jnp.logical_not(lo)
            rows.append(jnp.where(keep, qp, jnp.zeros_like(qp)))
        qs = jnp.concatenate(rows, axis=0)
        pieces = []
        for keys, on_rows in keys_b(g):
            s = lax.dot_general(qs, keys, _NT if on_rows else _NN, preferred_element_type=_F32)
            if lim_b:
                s = jnp.where(lax.broadcasted_iota(jnp.int32, s.shape, 1) >= lim_b, s, NEG)
            pieces.append(s)
        out.append(pieces)
    return out


def _row_reduce(pieces, combine, reduce, fill):
    acc = None
    for s in pieces:
        for c in range(0, s.shape[1], LANES):
            tile = s[:, c:c + LANES]
            if tile.shape[1] < LANES:
                pad = jnp.full((tile.shape[0], LANES - tile.shape[1]), fill, tile.dtype)
                tile = jnp.concatenate([tile, pad], axis=1)
            acc = tile if acc is None else combine(acc, tile)
    return reduce(acc, axis=1, keepdims=True)


def _softmax_values(pieces, vals, sink):
    m = _row_reduce(pieces, jnp.maximum, jnp.max, -jnp.inf)
    if sink is not None:
        m = jnp.maximum(m, sink)
    es = [jnp.exp2(s - m) for s in pieces]
    l = _row_reduce(es, jnp.add, jnp.sum, 0.0)
    if sink is not None:
        l = l + jnp.exp2(sink - m)
    acc = None
    for e, (v, on_rows) in zip(es, vals):
        part = lax.dot_general(e.astype(_BF16), v, _NN if on_rows else _NT, preferred_element_type=_F32)
        acc = part if acc is None else acc + part
    return acc / l


def _softmax_pv(scores, vals_a, vals_b, sink_ref):
    outs = []
    head_of_lane = lax.broadcasted_iota(jnp.int32, (CHUNK, 2 * LANES), 1) // HEAD_DIM
    for g in range(2):
        of = _softmax_values(scores[g], vals_a(g), None)
        og = of[3 * CHUNK:4 * CHUNK]
        for r in (2, 1, 0):
            og = jnp.where(head_of_lane == r, of[r * CHUNK:(r + 1) * CHUNK], og)
        outs.append(og)
    lo = lax.broadcasted_iota(jnp.int32, (CHUNK, LANES), 1) < HEAD_DIM
    for g in range(B_KV_HEADS):
        of = _softmax_values(scores[2 + g], vals_b(g), sink_ref[g] * LOG2E)
        outs.append(jnp.where(lo, of[0:CHUNK], of[CHUNK:2 * CHUNK]))
        outs.append(jnp.where(lo, of[2 * CHUNK:3 * CHUNK], of[3 * CHUNK:4 * CHUNK]))
    return jnp.concatenate(outs, axis=1)


def _attn_prompt_kernel(qa_ref, ka_ref, va_ref, qb_ref, kb_ref, vb_ref, u_ref, sink_ref, o_ref,
                        kwa, vwa, kwb, vwb, bias):
    i = pl.program_id(0)
    tm = qa_ref.shape[0]
    n_chunks = tm // CHUNK

    @pl.when(i == 0)
    def _():
        kwa[0:A_CTX, :] = jnp.zeros((A_CTX, A_WIDTH), _BF16)
        vwa[0:A_CTX, :] = jnp.zeros((A_CTX, A_WIDTH), _BF16)
        kwb[0:B_CTX, :] = jnp.zeros((B_CTX, 2 * B_KV_WIDTH), _BF16)
        vwb[0:B_CTX, :] = jnp.zeros((B_CTX, 2 * B_KV_WIDTH), _BF16)
        _build_bias(u_ref, bias)

    @pl.when(i > 0)
    def _():
        kwa[0:A_CTX, :] = kwa[tm:tm + A_CTX, :]
        vwa[0:A_CTX, :] = vwa[tm:tm + A_CTX, :]
        kwb[0:B_CTX, :] = kwb[tm:tm + B_CTX, :]
        vwb[0:B_CTX, :] = vwb[tm:tm + B_CTX, :]

    kwa[A_CTX:A_CTX + tm, :] = ka_ref[...]
    vwa[A_CTX:A_CTX + tm, :] = va_ref[...]
    kwb[B_CTX:B_CTX + tm, :] = kb_ref[...]
    vwb[B_CTX:B_CTX + tm, :] = vb_ref[...]

    def window(ref, r0, rows, width):
        return lambda g: [(ref[r0:r0 + rows, width * g:width * (g + 1)], True)]

    def run(masked):
        def scores(c):
            r0 = c * CHUNK
            lim_a = max(A_LEFT_CHUNKS - c, 0) * CHUNK if masked else 0
            lim_b = max(B_CTX // CHUNK - c, 0) * CHUNK if masked else 0
            return _scores(qa_ref[r0:r0 + CHUNK, :], qb_ref[r0:r0 + CHUNK, :],
                           window(kwa, r0, A_WIN, 2 * LANES), window(kwb, r0, B_WIN, LANES), bias, lim_a, lim_b)

        nxt = scores(0)
        for c in range(n_chunks):
            cur = nxt
            if c + 1 < n_chunks:
                nxt = scores(c + 1)
            r0 = c * CHUNK
            o = _softmax_pv(cur, window(vwa, r0, A_WIN, 2 * LANES), window(vwb, r0, B_WIN, LANES), sink_ref)
            o_ref[r0:r0 + CHUNK, :] = o.astype(_BF16)

    @pl.when(i == 0)
    def _():
        run(True)

    @pl.when(i > 0)
    def _():
        run(False)


def _attn_sample_kernel(qa_ref, ka_ref, va_ref, qb_ref, kb_ref, vb_ref, cak_ref, cav_ref, cbk_ref, cbv_ref,
                        u_ref, sink_ref, o_ref, bias):
    n_streams = cak_ref.shape[0]

    @pl.when(pl.program_id(0) == 0)
    def _():
        _build_bias(u_ref, bias)

    def pieces_a(cache_ref, new_ref, b):
        rows = slice(b * CHUNK, (b + 1) * CHUNK)

        def get(g):
            cols = slice(2 * LANES * g, 2 * LANES * (g + 1))
            return [(cache_ref[b, cols, :].astype(_BF16), False), (new_ref[rows, cols], True)]
        return get

    def pieces_b(cache_ref, new_ref, b):
        rows = slice(b * CHUNK, (b + 1) * CHUNK)

        def get(g):
            past = cache_ref[b, HEAD_DIM * g:HEAD_DIM * (g + 1), :].astype(_BF16)
            return [(jnp.concatenate([past, past], axis=0), False), (new_ref[rows, LANES * g:LANES * (g + 1)], True)]
        return get

    def scores(b):
        rows = slice(b * CHUNK, (b + 1) * CHUNK)
        return _scores(qa_ref[rows, :], qb_ref[rows, :], pieces_a(cak_ref, ka_ref, b), pieces_b(cbk_ref, kb_ref, b),
                       bias, 0, 0)

    nxt = scores(0)
    for b in range(n_streams):
        cur = nxt
        if b + 1 < n_streams:
            nxt = scores(b + 1)
        o = _softmax_pv(cur, pieces_a(cav_ref, va_ref, b), pieces_b(cbv_ref, vb_ref, b), sink_ref)
        o_ref[b * CHUNK:(b + 1) * CHUNK, :] = o.astype(_BF16)


_BIAS_SCRATCH = pltpu.VMEM((2, 4 * CHUNK, BIAS_COLS), _F32)


def _attend_prompt(qa, ka, va, qb, kb, vb, u, sink):
    t = qa.shape[0]
    tm = ATTN_TILE
    tile = lambda w: pl.BlockSpec((tm, w), lambda i: (i, 0))
    return pl.pallas_call(
        _attn_prompt_kernel,
        grid=(t // tm,),
        in_specs=[tile(A_WIDTH), tile(A_WIDTH), tile(A_WIDTH), tile(B_WIDTH), tile(2 * B_KV_WIDTH),
                  tile(2 * B_KV_WIDTH), _const_spec(u.shape), _const_spec(sink.shape)],
        out_specs=tile(A_WIDTH + B_WIDTH),
        out_shape=jax.ShapeDtypeStruct((t, A_WIDTH + B_WIDTH), _BF16),
        scratch_shapes=[pltpu.VMEM((A_CTX + tm, A_WIDTH), _BF16), pltpu.VMEM((A_CTX + tm, A_WIDTH), _BF16),
                        pltpu.VMEM((B_CTX + tm, 2 * B_KV_WIDTH), _BF16),
                        pltpu.VMEM((B_CTX + tm, 2 * B_KV_WIDTH), _BF16), _BIAS_SCRATCH],
        name="attn_prompt",
        compiler_params=pltpu.CompilerParams(dimension_semantics=("arbitrary",), vmem_limit_bytes=VMEM_LIMIT),
    )(qa, ka, va, qb, kb, vb, u, sink)


def _attend_sample(qa, ka, va, qb, kb, vb, cak, cav, cbk, cbv, u, sink):
    t = qa.shape[0]
    sb = SAMPLE_STREAMS
    tile = lambda w: pl.BlockSpec((sb * CHUNK, w), lambda i: (i, 0))
    cache = lambda arr: pl.BlockSpec((sb,) + arr.shape[1:], lambda i: (i, 0, 0))
    return pl.pallas_call(
        _attn_sample_kernel,
        grid=(t // (sb * CHUNK),),
        in_specs=[tile(A_WIDTH), tile(A_WIDTH), tile(A_WIDTH), tile(B_WIDTH), tile(2 * B_KV_WIDTH),
                  tile(2 * B_KV_WIDTH), cache(cak), cache(cav), cache(cbk), cache(cbv),
                  _const_spec(u.shape), _const_spec(sink.shape)],
        out_specs=tile(A_WIDTH + B_WIDTH),
        out_shape=jax.ShapeDtypeStruct((t, A_WIDTH + B_WIDTH), _BF16),
        scratch_shapes=[_BIAS_SCRATCH],
        name="attn_sample",
        compiler_params=pltpu.CompilerParams(dimension_semantics=("arbitrary",), vmem_limit_bytes=VMEM_LIMIT),
    )(qa, ka, va, qb, kb, vb, cak, cav, cbk, cbv, u, sink)


def _ffn_kernel(o_ref, x_ref, wo_ref, gn_ref, wgu_ref, cw_ref, cb_ref, wd_ref, st_ref,
                y_ref, cn_ref, carry, act_buf, *, seg_len, carry_rows):
    i = pl.program_id(0)
    tm = x_ref.shape[0]
    n_seg = tm // seg_len

    if carry_rows:
        @pl.when(i == 0)
        def _():
            carry[...] = st_ref[0]

    y1 = x_ref[...] + jnp.dot(o_ref[...], wo_ref[...], preferred_element_type=_F32)
    ms = jnp.mean(y1 * y1, axis=-1, keepdims=True)
    h = (y1 * lax.rsqrt(ms + EPS) * gn_ref[...]).astype(_BF16)

    row8 = lax.broadcasted_iota(jnp.int32, (8, FF_CHUNK), 0)

    def gate_up(c0):
        g = jnp.dot(h, wgu_ref[:, c0:c0 + FF_CHUNK], preferred_element_type=_F32)
        up = jnp.dot(h, wgu_ref[:, D_FF + c0:D_FF + c0 + FF_CHUNK], preferred_element_type=_F32)
        return g, up

    nxt = gate_up(0)
    for c0 in range(0, D_FF, FF_CHUNK):
        cols = slice(c0, c0 + FF_CHUNK)
        g, up = nxt
        if c0 + FF_CHUNK < D_FF:
            nxt = gate_up(c0 + FF_CHUNK)
        g1 = pltpu.roll(g, 1, 0)
        g2 = pltpu.roll(g, 2, 0)
        p1, p2 = [], []
        for s in range(n_seg):
            a = s * seg_len
            prev = carry[:, cols] if carry_rows else st_ref[s, :, cols]
            prev0 = jnp.broadcast_to(prev[0:1, :], (8, FF_CHUNK))
            prev1 = jnp.broadcast_to(prev[1:2, :], (8, FF_CHUNK))
            p1.append(jnp.where(row8 == 0, prev1, g1[a:a + 8]))
            p1.append(g1[a + 8:a + seg_len])
            p2.append(jnp.where(row8 == 0, prev0, jnp.where(row8 == 1, prev1, g2[a:a + 8])))
            p2.append(g2[a + 8:a + seg_len])
            tail = g[a + seg_len - 8:a + seg_len][8 - (CONV_W - 1):, :]
            if carry_rows:
                carry[:, cols] = tail
            else:
                cn_ref[s, :, cols] = tail
        g1 = jnp.concatenate(p1, axis=0)
        g2 = jnp.concatenate(p2, axis=0)
        conv = cb_ref[:, cols] + cw_ref[0:1, cols] * g2 + cw_ref[1:2, cols] * g1 + cw_ref[2:3, cols] * g
        act_buf[:, cols] = (conv * jax.nn.sigmoid(conv) * up).astype(_BF16)
    y_ref[...] = y1 + jnp.dot(act_buf[...], wd_ref[...], preferred_element_type=_F32)
    if carry_rows:
        cn_ref[0] = carry[...]


def _out_ffn(o, x, w_out, gn, w_gu, conv_w, conv_b, w_down, state, *, carry_rows, name):
    t = x.shape[0]
    tm = TOKEN_TILE
    n = t // tm
    seg_len = tm if carry_rows else CHUNK
    n_seg = tm // seg_len
    tile = lambda w: pl.BlockSpec((tm, w), lambda i: (i, 0))
    if carry_rows:
        st_spec = _const_spec((1, CONV_W - 1, D_FF))
        cn_spec = pl.BlockSpec((1, CONV_W - 1, D_FF), lambda i: (0, 0, 0))
        cn_shape = (1, CONV_W - 1, D_FF)
    else:
        st_spec = pl.BlockSpec((n_seg, CONV_W - 1, D_FF), lambda i: (i, 0, 0))
        cn_spec = pl.BlockSpec((n_seg, CONV_W - 1, D_FF), lambda i: (i, 0, 0))
        cn_shape = (t // seg_len, CONV_W - 1, D_FF)
    return pl.pallas_call(
        functools.partial(_ffn_kernel, seg_len=seg_len, carry_rows=carry_rows),
        grid=(n,),
        in_specs=[tile(A_WIDTH + B_WIDTH), tile(D_MODEL), _const_spec((D_MODEL, D_MODEL)),
                  _const_spec((1, D_MODEL)), _const_spec((D_MODEL, 2 * D_FF)), _const_spec((CONV_W, D_FF)),
                  _const_spec((1, D_FF)), _const_spec((D_FF, D_MODEL)), st_spec],
        out_specs=[tile(D_MODEL), cn_spec],
        out_shape=[jax.ShapeDtypeStruct((t, D_MODEL), _F32), jax.ShapeDtypeStruct(cn_shape, _F32)],
        scratch_shapes=[pltpu.VMEM((CONV_W - 1, D_FF), _F32), pltpu.VMEM((tm, D_FF), _BF16)],
        name=name,
        compiler_params=pltpu.CompilerParams(dimension_semantics=("arbitrary",), vmem_limit_bytes=VMEM_LIMIT),
    )(o, x, w_out, gn, w_gu, conv_w, conv_b, w_down, state)


def _rope_tables(tile_pos, row_pos):
    half = HEAD_DIM // 2
    lane = jnp.arange(LANES)
    inv = jnp.power(ROPE_THETA, -(lane % half).astype(_F32) / half)
    sign = jnp.where(lane % HEAD_DIM < half, -1.0, 1.0).astype(_F32)
    a = tile_pos.astype(_F32)[:, None] * inv[None, :]
    b = row_pos.astype(_F32)[:, None] * inv[None, :]
    return (jnp.cos(a)[:, None, :], (jnp.sin(a) * sign)[:, None, :], jnp.cos(b), jnp.sin(b) * sign)


def _feature_major(cache):
    n, t, heads, d = cache.shape
    return jnp.transpose(cache, (0, 2, 3, 1)).reshape(n, heads * d, t)


def _token_major(rows, heads):
    t = rows.shape[1]
    return jnp.transpose(rows.reshape(heads, HEAD_DIM, t), (2, 0, 1))[None, None]


def _bias_profile(table):
    n_clip = BIAS_COLS - 1 - MAX_REL
    head = jnp.broadcast_to(table[:, 2 * MAX_REL:], (A_HEADS, n_clip))
    return jnp.concatenate([head, table[:, ::-1][:, :BIAS_VAR - n_clip]], axis=1)


def kernel(x_prompt, x_sample, cache_a_k, cache_a_v, cache_b_k, cache_b_v, state_conv, norm_attn, w_in, q_norm_a,
           k_norm_a, rel_bias_a, q_norm_b, k_norm_b, sinks_b, w_out, norm_ffn, w_gate_up, conv_w, conv_b, w_down):
    assert norm_attn.shape[0] == 1
    bsz, seq, _ = x_prompt.shape
    dec_b, dec_s, _ = x_sample.shape
    assert bsz == 1 and dec_s == CHUNK and (dec_b * dec_s) % TOKEN_TILE == 0
    assert seq % TOKEN_TILE == 0 and seq % PROMPT_PROJ_TILE == 0 and seq % ATTN_TILE == 0
    assert cache_a_k.shape[2] == A_CTX and cache_b_k.shape[2] == B_CTX

    w_in_b = w_in[0].astype(_BF16)
    gn_a = norm_attn[0][None, :]
    gn_f = norm_ffn[0][None, :]
    gqa = jnp.tile(q_norm_a[0], A_HEADS)[None, :]
    gka = jnp.tile(k_norm_a[0], A_HEADS)[None, :]
    gqb = jnp.tile(q_norm_b[0], B_HEADS)[None, :]
    gkb = jnp.tile(k_norm_b[0], B_KV_HEADS)[None, :]
    u = _bias_profile(rel_bias_a[0])
    sink = jnp.repeat(sinks_b[0], CHUNK).reshape(B_KV_HEADS, B_GROUP * CHUNK, 1)
    cw = conv_w[0]
    cb = conv_b[0][None, :]

    rope_p = _rope_tables(jnp.arange(0, seq, PROMPT_PROJ_TILE), jnp.arange(PROMPT_PROJ_TILE))
    rope_s = _rope_tables(jnp.full((1,), PAST_LEN), jnp.arange(TOKEN_TILE) % dec_s)

    xp = x_prompt.reshape(seq, D_MODEL)
    qa, ka, va, qb, kb, vb, kaf, vaf, kbf, vbf, w_out_b, w_gu_b, w_down_b = _project(
        xp, gn_a, w_in_b, gqa, gka, gqb, gkb, rope_p, keep_all=False, name="proj_prompt",
        cast=(w_out[0], w_gate_up[0], w_down[0]))
    o_p = _attend_prompt(qa, ka, va, qb, kb, vb, u, sink)
    zero_state = jnp.zeros((1, CONV_W - 1, D_FF), _F32)
    y_p, cn_p = _out_ffn(o_p, xp, w_out_b, gn_f, w_gu_b, cw, cb, w_down_b, zero_state,
                         carry_rows=True, name="ffn_prompt")

    xs = x_sample.reshape(dec_b * dec_s, D_MODEL)
    qa, ka, va, qb, kb, vb, kaf_s, vaf_s, kbf_s, vbf_s = _project(
        xs, gn_a, w_in_b, gqa, gka, gqb, gkb, rope_s, keep_all=True, name="proj_sample")
    o_s = _attend_sample(qa, ka, va, qb, kb, vb, _feature_major(cache_a_k[0]), _feature_major(cache_a_v[0]),
                         _feature_major(cache_b_k[0]), _feature_major(cache_b_v[0]), u, sink)
    y_s, cn_s = _out_ffn(o_s, xs, w_out_b, gn_f, w_gu_b, cw, cb, w_down_b, state_conv[0],
                         carry_rows=False, name="ffn_sample")

    return (
        y_p.reshape(1, seq, D_MODEL),
        y_s.reshape(dec_b, dec_s, D_MODEL),
        _token_major(kaf, A_HEADS),
        _token_major(vaf, A_HEADS),
        _token_major(kbf, B_KV_HEADS),
        _token_major(vbf, B_KV_HEADS),
        cn_p.reshape(1, 1, CONV_W - 1, D_FF),
        kaf_s.reshape(1, dec_b, dec_s, A_HEADS, HEAD_DIM),
        vaf_s.reshape(1, dec_b, dec_s, A_HEADS, HEAD_DIM),
        kbf_s.reshape(1, dec_b, dec_s, B_KV_HEADS, HEAD_DIM),
        vbf_s.reshape(1, dec_b, dec_s, B_KV_HEADS, HEAD_DIM),
        cn_s.reshape(1, dec_b, CONV_W - 1, D_FF),
    )
```

```python
import functools

import jax
import jax.numpy as jnp
from jax import lax
from jax.experimental import pallas as pl
from jax.experimental.pallas import tpu as pltpu

D_MODEL = 1024
CHUNK = 64
HEAD_DIM = 64
EPS = 1e-6
SCALE = HEAD_DIM ** -0.5
LOG2E = 1.4426950408889634
Q_SCALE = SCALE * LOG2E
NEG = -1e30
PAST_LEN = 1024
A_HEADS = 8
A_LEFT_CHUNKS = 8
A_CTX = A_LEFT_CHUNKS * CHUNK
A_WIN = A_CTX + CHUNK
MAX_REL = 128
A_WIDTH = A_HEADS * HEAD_DIM
B_HEADS = 8
B_KV_HEADS = 2
B_GROUP = B_HEADS // B_KV_HEADS
B_CTX = 128
B_WIN = B_CTX + CHUNK
B_WIDTH = B_HEADS * HEAD_DIM
B_KV_WIDTH = B_KV_HEADS * HEAD_DIM
ROPE_THETA = 10000.0
IN_COLS = 3 * A_WIDTH + B_WIDTH + 2 * B_KV_WIDTH
D_FF = 2816
CONV_W = 3

LANES = 128
TOKEN_TILE = 512
PROMPT_PROJ_TILE = 1024
ATTN_TILE = 1024
SAMPLE_STREAMS = 4
FF_CHUNK = 256
BIAS_COLS = MAX_REL + CHUNK
BIAS_VAR = 256
VMEM_LIMIT = 52 * 1024 * 1024

_BF16 = jnp.bfloat16
_F32 = jnp.float32


def _const_spec(shape):
    nd = len(shape)
    return pl.BlockSpec(shape, lambda i: (0,) * nd, pipeline_mode=pl.Buffered(1))


def _head_sumsq(t, bd):
    sq = (t * t).astype(_BF16)
    w = t.shape[1]
    if w <= 2 * LANES:
        return jnp.dot(sq, bd[:w, :w], preferred_element_type=_F32)
    parts = [jnp.dot(sq[:, c:c + 2 * LANES], bd, preferred_element_type=_F32) for c in range(0, w, 2 * LANES)]
    return jnp.concatenate(parts, axis=1)


def _head_norm(t, bd, gain):
    ss = _head_sumsq(t, bd)
    return t * lax.rsqrt(ss * (1.0 / HEAD_DIM) + EPS) * gain


def _rope(t, cos, sin_signed):
    outs = []
    for c in range(0, t.shape[1], LANES):
        tc = t[:, c:c + LANES]
        lane = lax.broadcasted_iota(jnp.int32, tc.shape, 1)
        first_half = (lane & (HEAD_DIM - 1)) < HEAD_DIM // 2
        rot = jnp.where(first_half, pltpu.roll(tc, LANES - HEAD_DIM // 2, 1), pltpu.roll(tc, HEAD_DIM // 2, 1))
        outs.append(tc * cos + rot * sin_signed)
    return outs[0] if len(outs) == 1 else jnp.concatenate(outs, axis=1)


def _rep_pair(t):
    rolled = pltpu.roll(t, HEAD_DIM, 1)
    lo = lax.broadcasted_iota(jnp.int32, t.shape, 1) < HEAD_DIM
    return jnp.concatenate([jnp.where(lo, t, rolled), jnp.where(lo, rolled, t)], axis=1)


def _store_heads(ref, t):
    n, w = t.shape
    heads = w // HEAD_DIM
    for hd in range(heads):
        ref[pl.ds(hd, n, stride=heads), :] = t[:, hd * HEAD_DIM:(hd + 1) * HEAD_DIM]


def _proj_kernel(x_ref, gn_ref, w_ref, gqa_ref, gka_ref, gqb_ref, gkb_ref, ct_ref, st_ref, cr_ref, sr_ref, *rest,
                 last_only, n_cast):
    cast_in, rest = rest[:n_cast], rest[n_cast:]
    qa_ref, ka_ref, va_ref, qb_ref, kb_ref, vb_ref, kaf_ref, vaf_ref, kbf_ref, vbf_ref = rest[:10]
    for src, dst in zip(cast_in, rest[10:]):
        dst[...] = src[...].astype(_BF16)
    tm = x_ref.shape[0]
    x = x_ref[...]
    ms = jnp.mean(x * x, axis=-1, keepdims=True)
    h = (x * lax.rsqrt(ms + EPS) * gn_ref[...]).astype(_BF16)

    r = lax.broadcasted_iota(jnp.int32, (2 * LANES, 2 * LANES), 0) // HEAD_DIM
    c = lax.broadcasted_iota(jnp.int32, (2 * LANES, 2 * LANES), 1) // HEAD_DIM
    bd = jnp.where(r == c, 1.0, 0.0).astype(_BF16)
    ct, st, cr, sr = ct_ref[0], st_ref[0], cr_ref[...], sr_ref[...]
    cos = ct * cr - st * sr
    sin = st * cr + ct * sr

    def proj(lo, width):
        return jnp.dot(h, w_ref[:, lo:lo + width], preferred_element_type=_F32)

    o_ka, o_va, o_qb, o_kvb = A_WIDTH, 2 * A_WIDTH, 3 * A_WIDTH, 3 * A_WIDTH + B_WIDTH
    p_qa = proj(0, A_WIDTH)
    p_ka = proj(o_ka, A_WIDTH)
    qa_ref[...] = (_head_norm(p_qa, bd, gqa_ref[...]) * Q_SCALE).astype(_BF16)
    va = proj(o_va, A_WIDTH)
    ka = _head_norm(p_ka, bd, gka_ref[...])
    ka_ref[...] = ka.astype(_BF16)
    p_qb = proj(o_qb, B_WIDTH)
    va_ref[...] = va.astype(_BF16)
    p_kvb = proj(o_kvb, 2 * B_KV_WIDTH)
    qb_ref[...] = (_rope(_head_norm(p_qb, bd, gqb_ref[...]), cos, sin) * Q_SCALE).astype(_BF16)
    kb = _rope(_head_norm(p_kvb[:, :B_KV_WIDTH], bd, gkb_ref[...]), cos, sin)
    vb = p_kvb[:, B_KV_WIDTH:]
    kb_ref[...] = _rep_pair(kb).astype(_BF16)
    vb_ref[...] = _rep_pair(vb).astype(_BF16)

    if last_only:
        @pl.when(pl.program_id(0) == pl.num_programs(0) - 1)
        def _():
            kaf_ref[...] = ka[tm - A_CTX:, :].T
            vaf_ref[...] = va[tm - A_CTX:, :].T
            kbf_ref[...] = kb[tm - B_CTX:, :].T
            vbf_ref[...] = vb[tm - B_CTX:, :].T
    else:
        _store_heads(kaf_ref, ka)
        _store_heads(vaf_ref, va)
        kbf_ref[...] = kb
        vbf_ref[...] = vb


def _cast_blocks(w, n_steps):
    rows = w.shape[0]
    n_blocks = max(d for d in range(1, n_steps + 1) if rows % d == 0 and (rows // d) % 16 == 0)
    return pl.BlockSpec((rows // n_blocks, w.shape[1]), lambda i: (jnp.minimum(i, n_blocks - 1), 0))


def _project(x, gn, w_in, gqa, gka, gqb, gkb, rope, *, keep_all, name, cast=()):
    t = x.shape[0]
    tm = rope[2].shape[0]
    n = t // tm
    cast_specs = [_cast_blocks(w, n) for w in cast]
    tile = lambda w: pl.BlockSpec((tm, w), lambda i: (i, 0))
    if keep_all:
        f32_shapes = [(t * A_HEADS, HEAD_DIM), (t * A_HEADS, HEAD_DIM), (t, B_KV_WIDTH), (t, B_KV_WIDTH)]
        f32_specs = [pl.BlockSpec((tm * A_HEADS, HEAD_DIM), lambda i: (i, 0))] * 2 + [tile(B_KV_WIDTH)] * 2
    else:
        f32_shapes = [(A_WIDTH, A_CTX), (A_WIDTH, A_CTX), (B_KV_WIDTH, B_CTX), (B_KV_WIDTH, B_CTX)]
        f32_specs = [pl.BlockSpec(sh, lambda i: (0, 0)) for sh in f32_shapes]
    ct, st, cr, sr = rope
    if ct.shape[0] == n:
        tile_trig = pl.BlockSpec((1, 1, LANES), lambda i: (i, 0, 0))
    else:
        tile_trig = _const_spec((1, 1, LANES))
    out_shape = [
        jax.ShapeDtypeStruct((t, A_WIDTH), _BF16), jax.ShapeDtypeStruct((t, A_WIDTH), _BF16),
        jax.ShapeDtypeStruct((t, A_WIDTH), _BF16), jax.ShapeDtypeStruct((t, B_WIDTH), _BF16),
        jax.ShapeDtypeStruct((t, 2 * B_KV_WIDTH), _BF16), jax.ShapeDtypeStruct((t, 2 * B_KV_WIDTH), _BF16),
    ] + [jax.ShapeDtypeStruct(sh, _F32) for sh in f32_shapes] + [jax.ShapeDtypeStruct(w.shape, _BF16) for w in cast]
    out_specs = [
        tile(A_WIDTH), tile(A_WIDTH), tile(A_WIDTH), tile(B_WIDTH), tile(2 * B_KV_WIDTH), tile(2 * B_KV_WIDTH),
    ] + f32_specs + cast_specs
    in_specs = [
        tile(D_MODEL), _const_spec((1, D_MODEL)), _const_spec((D_MODEL, IN_COLS)),
        _const_spec((1, A_WIDTH)), _const_spec((1, A_WIDTH)), _const_spec((1, B_WIDTH)),
        _const_spec((1, B_KV_WIDTH)), tile_trig, tile_trig, _const_spec((tm, LANES)), _const_spec((tm, LANES)),
    ] + cast_specs
    return pl.pallas_call(
        functools.partial(_proj_kernel, last_only=not keep_all, n_cast=len(cast)),
        grid=(n,), in_specs=in_specs, out_specs=out_specs, out_shape=out_shape, name=name,
        compiler_params=pltpu.CompilerParams(dimension_semantics=("arbitrary",), vmem_limit_bytes=VMEM_LIMIT),
    )(x, gn, w_in, gqa, gka, gqb, gkb, ct, st, cr, sr, *cast)


def _build_bias(u_ref, bias_ref):
    qrow = lax.broadcasted_iota(jnp.int32, (CHUNK, BIAS_VAR), 0)
    for h in range(A_HEADS):
        prof = u_ref[h:h + 1, :]
        y = jnp.broadcast_to(prof - prof[:, 0:1], (CHUNK, BIAS_VAR))
        y = pltpu.roll(y, BIAS_VAR - (CHUNK - 1), 1)
        for b in range(6):
            y = jnp.where(((qrow >> b) & 1) == 1, pltpu.roll(y, 1 << b, 1), y)
        bias_ref[h // 4, (h % 4) * CHUNK:(h % 4 + 1) * CHUNK, :] = y[:, :BIAS_COLS] * LOG2E


_NT = (((1,), (1,)), ((), ()))
_NN = (((1,), (0,)), ((), ()))


def _scores(qa, qb, keys_a, keys_b, bias_ref, lim_a, lim_b):
    out = []
    head_of_lane = lax.broadcasted_iota(jnp.int32, (CHUNK, 2 * LANES), 1) // HEAD_DIM
    for g in range(2):
        qg = qa[:, 2 * LANES * g:2 * LANES * (g + 1)]
        qbd = jnp.concatenate([jnp.where(head_of_lane == r, qg, jnp.zeros_like(qg)) for r in range(4)], axis=0)
        pieces, off = [], 0
        for keys, on_rows in keys_a(g):
            s = lax.dot_general(qbd, keys, _NT if on_rows else _NN, preferred_element_type=_F32)
            n = s.shape[1]
            if off + n > A_WIN - BIAS_COLS:
                lo_col = max(A_WIN - BIAS_COLS - off, 0)
                biased = s[:, lo_col:] + bias_ref[g, :, off + lo_col - (A_WIN - BIAS_COLS):off + n - (A_WIN - BIAS_COLS)]
                s = biased if lo_col == 0 else jnp.concatenate([s[:, :lo_col], biased], axis=1)
            if lim_a:
                s = jnp.where(lax.broadcasted_iota(jnp.int32, s.shape, 1) >= lim_a, s, NEG)
            pieces.append(s)
            off += n
        out.append(pieces)
    lo = lax.broadcasted_iota(jnp.int32, (CHUNK, LANES), 1) < HEAD_DIM
    for g in range(B_KV_HEADS):
        rows = []
        for r in range(B_GROUP):
            hd = B_GROUP * g + r
            qp = qb[:, LANES * (hd // 2):LANES * (hd // 2 + 1)]
            keep = lo if hd % 2 == 0 else jnp.logical_not(lo)
            rows.append(jnp.where(keep, qp, jnp.zeros_like(qp)))
        qs = jnp.concatenate(rows, axis=0)
        pieces = []
        for keys, on_rows in keys_b(g):
            s = lax.dot_general(qs, keys, _NT if on_rows else _NN, preferred_element_type=_F32)
            if lim_b:
                s = jnp.where(lax.broadcasted_iota(jnp.int32, s.shape, 1) >= lim_b, s, NEG)
            pieces.append(s)
        out.append(pieces)
    return out


def _row_reduce(pieces, combine, reduce, fill):
    acc = None
    for s in pieces:
        for c in range(0, s.shape[1], LANES):
            tile = s[:, c:c + LANES]
            if tile.shape[1] < LANES:
                pad = jnp.full((tile.shape[0], LANES - tile.shape[1]), fill, tile.dtype)
                tile = jnp.concatenate([tile, pad], axis=1)
            acc = tile if acc is None else combine(acc, tile)
    return reduce(acc, axis=1, keepdims=True)


def _softmax_values(pieces, vals, sink):
    m = _row_reduce(pieces, jnp.maximum, jnp.max, -jnp.inf)
    if sink is not None:
        m = jnp.maximum(m, sink)
    es = [jnp.exp2(s - m) for s in pieces]
    l = _row_reduce(es, jnp.add, jnp.sum, 0.0)
    if sink is not None:
        l = l + jnp.exp2(sink - m)
    acc = None
    for e, (v, on_rows) in zip(es, vals):
        part = lax.dot_general(e.astype(_BF16), v, _NN if on_rows else _NT, preferred_element_type=_F32)
        acc = part if acc is None else acc + part
    return acc / l


def _softmax_pv(scores, vals_a, vals_b, sink_ref):
    outs = []
    head_of_lane = lax.broadcasted_iota(jnp.int32, (CHUNK, 2 * LANES), 1) // HEAD_DIM
    for g in range(2):
        of = _softmax_values(scores[g], vals_a(g), None)
        og = of[3 * CHUNK:4 * CHUNK]
        for r in (2, 1, 0):
            og = jnp.where(head_of_lane == r, of[r * CHUNK:(r + 1) * CHUNK], og)
        outs.append(og)
    lo = lax.broadcasted_iota(jnp.int32, (CHUNK, LANES), 1) < HEAD_DIM
    for g in range(B_KV_HEADS):
        of = _softmax_values(scores[2 + g], vals_b(g), sink_ref[g] * LOG2E)
        outs.append(jnp.where(lo, of[0:CHUNK], of[CHUNK:2 * CHUNK]))
        outs.append(jnp.where(lo, of[2 * CHUNK:3 * CHUNK], of[3 * CHUNK:4 * CHUNK]))
    return jnp.concatenate(outs, axis=1)


def _attn_prompt_kernel(qa_ref, ka_ref, va_ref, qb_ref, kb_ref, vb_ref, u_ref, sink_ref, o_ref,
                        kwa, vwa, kwb, vwb, bias):
    i = pl.program_id(0)
    tm = qa_ref.shape[0]
    n_chunks = tm // CHUNK

    @pl.when(i == 0)
    def _():
        kwa[0:A_CTX, :] = jnp.zeros((A_CTX, A_WIDTH), _BF16)
        vwa[0:A_CTX, :] = jnp.zeros((A_CTX, A_WIDTH), _BF16)
        kwb[0:B_CTX, :] = jnp.zeros((B_CTX, 2 * B_KV_WIDTH), _BF16)
        vwb[0:B_CTX, :] = jnp.zeros((B_CTX, 2 * B_KV_WIDTH), _BF16)
        _build_bias(u_ref, bias)

    @pl.when(i > 0)
    def _():
        kwa[0:A_CTX, :] = kwa[tm:tm + A_CTX, :]
        vwa[0:A_CTX, :] = vwa[tm:tm + A_CTX, :]
        kwb[0:B_CTX, :] = kwb[tm:tm + B_CTX, :]
        vwb[0:B_CTX, :] = vwb[tm:tm + B_CTX, :]

    kwa[A_CTX:A_CTX + tm, :] = ka_ref[...]
    vwa[A_CTX:A_CTX + tm, :] = va_ref[...]
    kwb[B_CTX:B_CTX + tm, :] = kb_ref[...]
    vwb[B_CTX:B_CTX + tm, :] = vb_ref[...]

    def window(ref, r0, rows, width):
        return lambda g: [(ref[r0:r0 + rows, width * g:width * (g + 1)], True)]

    def run(masked):
        def scores(c):
            r0 = c * CHUNK
            lim_a = max(A_LEFT_CHUNKS - c, 0) * CHUNK if masked else 0
            lim_b = max(B_CTX // CHUNK - c, 0) * CHUNK if masked else 0
            return _scores(qa_ref[r0:r0 + CHUNK, :], qb_ref[r0:r0 + CHUNK, :],
                           window(kwa, r0, A_WIN, 2 * LANES), window(kwb, r0, B_WIN, LANES), bias, lim_a, lim_b)

        nxt = scores(0)
        for c in range(n_chunks):
            cur = nxt
            if c + 1 < n_chunks:
                nxt = scores(c + 1)
            r0 = c * CHUNK
            o = _softmax_pv(cur, window(vwa, r0, A_WIN, 2 * LANES), window(vwb, r0, B_WIN, LANES), sink_ref)
            o_ref[r0:r0 + CHUNK, :] = o.astype(_BF16)

    @pl.when(i == 0)
    def _():
        run(True)

    @pl.when(i > 0)
    def _():
        run(False)


def _attn_sample_kernel(qa_ref, ka_ref, va_ref, qb_ref, kb_ref, vb_ref, cak_ref, cav_ref, cbk_ref, cbv_ref,
                        u_ref, sink_ref, o_ref, bias):
    n_streams = cak_ref.shape[0]

    @pl.when(pl.program_id(0) == 0)
    def _():
        _build_bias(u_ref, bias)

    def pieces_a(cache_ref, new_ref, b):
        rows = slice(b * CHUNK, (b + 1) * CHUNK)

        def get(g):
            cols = slice(2 * LANES * g, 2 * LANES * (g + 1))
            return [(cache_ref[b, cols, :].astype(_BF16), False), (new_ref[rows, cols], True)]
        return get

    def pieces_b(cache_ref, new_ref, b):
        rows = slice(b * CHUNK, (b + 1) * CHUNK)

        def get(g):
            past = cache_ref[b, HEAD_DIM * g:HEAD_DIM * (g + 1), :].astype(_BF16)
            return [(jnp.concatenate([past, past], axis=0), False), (new_ref[rows, LANES * g:LANES * (g + 1)], True)]
        return get

    def scores(b):
        rows = slice(b * CHUNK, (b + 1) * CHUNK)
        return _scores(qa_ref[rows, :], qb_ref[rows, :], pieces_a(cak_ref, ka_ref, b), pieces_b(cbk_ref, kb_ref, b),
                       bias, 0, 0)

    nxt = scores(0)
    for b in range(n_streams):
        cur = nxt
        if b + 1 < n_streams:
            nxt = scores(b + 1)
        o = _softmax_pv(cur, pieces_a(cav_ref, va_ref, b), pieces_b(cbv_ref, vb_ref, b), sink_ref)
        o_ref[b * CHUNK:(b + 1) * CHUNK, :] = o.astype(_BF16)


_BIAS_SCRATCH = pltpu.VMEM((2, 4 * CHUNK, BIAS_COLS), _F32)


def _attend_prompt(qa, ka, va, qb, kb, vb, u, sink):
    t = qa.shape[0]
    tm = ATTN_TILE
    tile = lambda w: pl.BlockSpec((tm, w), lambda i: (i, 0))
    return pl.pallas_call(
        _attn_prompt_kernel,
        grid=(t // tm,),
        in_specs=[tile(A_WIDTH), tile(A_WIDTH), tile(A_WIDTH), tile(B_WIDTH), tile(2 * B_KV_WIDTH),
                  tile(2 * B_KV_WIDTH), _const_spec(u.shape), _const_spec(sink.shape)],
        out_specs=tile(A_WIDTH + B_WIDTH),
        out_shape=jax.ShapeDtypeStruct((t, A_WIDTH + B_WIDTH), _BF16),
        scratch_shapes=[pltpu.VMEM((A_CTX + tm, A_WIDTH), _BF16), pltpu.VMEM((A_CTX + tm, A_WIDTH), _BF16),
                        pltpu.VMEM((B_CTX + tm, 2 * B_KV_WIDTH), _BF16),
                        pltpu.VMEM((B_CTX + tm, 2 * B_KV_WIDTH), _BF16), _BIAS_SCRATCH],
        name="attn_prompt",
        compiler_params=pltpu.CompilerParams(dimension_semantics=("arbitrary",), vmem_limit_bytes=VMEM_LIMIT),
    )(qa, ka, va, qb, kb, vb, u, sink)


def _attend_sample(qa, ka, va, qb, kb, vb, cak, cav, cbk, cbv, u, sink):
    t = qa.shape[0]
    sb = SAMPLE_STREAMS
    tile = lambda w: pl.BlockSpec((sb * CHUNK, w), lambda i: (i, 0))
    cache = lambda arr: pl.BlockSpec((sb,) + arr.shape[1:], lambda i: (i, 0, 0))
    return pl.pallas_call(
        _attn_sample_kernel,
        grid=(t // (sb * CHUNK),),
        in_specs=[tile(A_WIDTH), tile(A_WIDTH), tile(A_WIDTH), tile(B_WIDTH), tile(2 * B_KV_WIDTH),
                  tile(2 * B_KV_WIDTH), cache(cak), cache(cav), cache(cbk), cache(cbv),
                  _const_spec(u.shape), _const_spec(sink.shape)],
        out_specs=tile(A_WIDTH + B_WIDTH),
        out_shape=jax.ShapeDtypeStruct((t, A_WIDTH + B_WIDTH), _BF16),
        scratch_shapes=[_BIAS_SCRATCH],
        name="attn_sample",
        compiler_params=pltpu.CompilerParams(dimension_semantics=("arbitrary",), vmem_limit_bytes=VMEM_LIMIT),
    )(qa, ka, va, qb, kb, vb, cak, cav, cbk, cbv, u, sink)


def _ffn_kernel(o_ref, x_ref, wo_ref, gn_ref, wgu_ref, cw_ref, cb_ref, wd_ref, st_ref,
                y_ref, cn_ref, carry, act_buf, *, seg_len, carry_rows):
    i = pl.program_id(0)
    tm = x_ref.shape[0]
    n_seg = tm // seg_len

    if carry_rows:
        @pl.when(i == 0)
        def _():
            carry[...] = st_ref[0]

    y1 = x_ref[...] + jnp.dot(o_ref[...], wo_ref[...], preferred_element_type=_F32)
    ms = jnp.mean(y1 * y1, axis=-1, keepdims=True)
    h = (y1 * lax.rsqrt(ms + EPS) * gn_ref[...]).astype(_BF16)

    row8 = lax.broadcasted_iota(jnp.int32, (8, FF_CHUNK), 0)

    def gate_up(c0):
        g = jnp.dot(h, wgu_ref[:, c0:c0 + FF_CHUNK], preferred_element_type=_F32)
        up = jnp.dot(h, wgu_ref[:, D_FF + c0:D_FF + c0 + FF_CHUNK], preferred_element_type=_F32)
        return g, up

    nxt = gate_up(0)
    for c0 in range(0, D_FF, FF_CHUNK):
        cols = slice(c0, c0 + FF_CHUNK)
        g, up = nxt
        if c0 + FF_CHUNK < D_FF:
            nxt = gate_up(c0 + FF_CHUNK)
        g1 = pltpu.roll(g, 1, 0)
        g2 = pltpu.roll(g, 2, 0)
        p1, p2 = [], []
        for s in range(n_seg):
            a = s * seg_len
            prev = carry[:, cols] if carry_rows else st_ref[s, :, cols]
            prev0 = jnp.broadcast_to(prev[0:1, :], (8, FF_CHUNK))
            prev1 = jnp.broadcast_to(prev[1:2, :], (8, FF_CHUNK))
            p1.append(jnp.where(row8 == 0, prev1, g1[a:a + 8]))
            p1.append(g1[a + 8:a + seg_len])
            p2.append(jnp.where(row8 == 0, prev0, jnp.where(row8 == 1, prev1, g2[a:a + 8])))
            p2.append(g2[a + 8:a + seg_len])
            tail = g[a + seg_len - 8:a + seg_len][8 - (CONV_W - 1):, :]
            if carry_rows:
                carry[:, cols] = tail
            else:
                cn_ref[s, :, cols] = tail
        g1 = jnp.concatenate(p1, axis=0)
        g2 = jnp.concatenate(p2, axis=0)
        conv = cb_ref[:, cols] + cw_ref[0:1, cols] * g2 + cw_ref[1:2, cols] * g1 + cw_ref[2:3, cols] * g
        act_buf[:, cols] = (conv * jax.nn.sigmoid(conv) * up).astype(_BF16)
    y_ref[...] = y1 + jnp.dot(act_buf[...], wd_ref[...], preferred_element_type=_F32)
    if carry_rows:
        cn_ref[0] = carry[...]


def _out_ffn(o, x, w_out, gn, w_gu, conv_w, conv_b, w_down, state, *, carry_rows, name):
    t = x.shape[0]
    tm = TOKEN_TILE
    n = t // tm
    seg_len = tm if carry_rows else CHUNK
    n_seg = tm // seg_len
    tile = lambda w: pl.BlockSpec((tm, w), lambda i: (i, 0))
    if carry_rows:
        st_spec = _const_spec((1, CONV_W - 1, D_FF))
        cn_spec = pl.BlockSpec((1, CONV_W - 1, D_FF), lambda i: (0, 0, 0))
        cn_shape = (1, CONV_W - 1, D_FF)
    else:
        st_spec = pl.BlockSpec((n_seg, CONV_W - 1, D_FF), lambda i: (i, 0, 0))
        cn_spec = pl.BlockSpec((n_seg, CONV_W - 1, D_FF), lambda i: (i, 0, 0))
        cn_shape = (t // seg_len, CONV_W - 1, D_FF)
    return pl.pallas_call(
        functools.partial(_ffn_kernel, seg_len=seg_len, carry_rows=carry_rows),
        grid=(n,),
        in_specs=[tile(A_WIDTH + B_WIDTH), tile(D_MODEL), _const_spec((D_MODEL, D_MODEL)),
                  _const_spec((1, D_MODEL)), _const_spec((D_MODEL, 2 * D_FF)), _const_spec((CONV_W, D_FF)),
                  _const_spec((1, D_FF)), _const_spec((D_FF, D_MODEL)), st_spec],
        out_specs=[tile(D_MODEL), cn_spec],
        out_shape=[jax.ShapeDtypeStruct((t, D_MODEL), _F32), jax.ShapeDtypeStruct(cn_shape, _F32)],
        scratch_shapes=[pltpu.VMEM((CONV_W - 1, D_FF), _F32), pltpu.VMEM((tm, D_FF), _BF16)],
        name=name,
        compiler_params=pltpu.CompilerParams(dimension_semantics=("arbitrary",), vmem_limit_bytes=VMEM_LIMIT),
    )(o, x, w_out, gn, w_gu, conv_w, conv_b, w_down, state)


def _rope_tables(tile_pos, row_pos):
    half = HEAD_DIM // 2
    lane = jnp.arange(LANES)
    inv = jnp.power(ROPE_THETA, -(lane % half).astype(_F32) / half)
    sign = jnp.where(lane % HEAD_DIM < half, -1.0, 1.0).astype(_F32)
    a = tile_pos.astype(_F32)[:, None] * inv[None, :]
    b = row_pos.astype(_F32)[:, None] * inv[None, :]
    return (jnp.cos(a)[:, None, :], (jnp.sin(a) * sign)[:, None, :], jnp.cos(b), jnp.sin(b) * sign)


def _feature_major(cache):
    n, t, heads, d = cache.shape
    return jnp.transpose(cache, (0, 2, 3, 1)).reshape(n, heads * d, t)


def _token_major(rows, heads):
    t = rows.shape[1]
    return jnp.transpose(rows.reshape(heads, HEAD_DIM, t), (2, 0, 1))[None, None]


def _bias_profile(table):
    n_clip = BIAS_COLS - 1 - MAX_REL
    head = jnp.broadcast_to(table[:, 2 * MAX_REL:], (A_HEADS, n_clip))
    return jnp.concatenate([head, table[:, ::-1][:, :BIAS_VAR - n_clip]], axis=1)


def kernel(x_prompt, x_sample, cache_a_k, cache_a_v, cache_b_k, cache_b_v, state_conv, norm_attn, w_in, q_norm_a,
           k_norm_a, rel_bias_a, q_norm_b, k_norm_b, sinks_b, w_out, norm_ffn, w_gate_up, conv_w, conv_b, w_down):
    assert norm_attn.shape[0] == 1
    bsz, seq, _ = x_prompt.shape
    dec_b, dec_s, _ = x_sample.shape
    assert bsz == 1 and dec_s == CHUNK and (dec_b * dec_s) % TOKEN_TILE == 0
    assert seq % TOKEN_TILE == 0 and seq % PROMPT_PROJ_TILE == 0 and seq % ATTN_TILE == 0
    assert cache_a_k.shape[2] == A_CTX and cache_b_k.shape[2] == B_CTX

    w_in_b = w_in[0].astype(_BF16)
    gn_a = norm_attn[0][None, :]
    gn_f = norm_ffn[0][None, :]
    gqa = jnp.tile(q_norm_a[0], A_HEADS)[None, :]
    gka = jnp.tile(k_norm_a[0], A_HEADS)[None, :]
    gqb = jnp.tile(q_norm_b[0], B_HEADS)[None, :]
    gkb = jnp.tile(k_norm_b[0], B_KV_HEADS)[None, :]
    u = _bias_profile(rel_bias_a[0])
    sink = jnp.repeat(sinks_b[0], CHUNK).reshape(B_KV_HEADS, B_GROUP * CHUNK, 1)
    cw = conv_w[0]
    cb = conv_b[0][None, :]

    rope_p = _rope_tables(jnp.arange(0, seq, PROMPT_PROJ_TILE), jnp.arange(PROMPT_PROJ_TILE))
    rope_s = _rope_tables(jnp.full((1,), PAST_LEN), jnp.arange(TOKEN_TILE) % dec_s)

    xp = x_prompt.reshape(seq, D_MODEL)
    qa, ka, va, qb, kb, vb, kaf, vaf, kbf, vbf, w_out_b, w_gu_b, w_down_b = _project(
        xp, gn_a, w_in_b, gqa, gka, gqb, gkb, rope_p, keep_all=False, name="proj_prompt",
        cast=(w_out[0], w_gate_up[0], w_down[0]))
    o_p = _attend_prompt(qa, ka, va, qb, kb, vb, u, sink)
    zero_state = jnp.zeros((1, CONV_W - 1, D_FF), _F32)
    y_p, cn_p = _out_ffn(o_p, xp, w_out_b, gn_f, w_gu_b, cw, cb, w_down_b, zero_state,
                         carry_rows=True, name="ffn_prompt")

    xs = x_sample.reshape(dec_b * dec_s, D_MODEL)
    qa, ka, va, qb, kb, vb, kaf_s, vaf_s, kbf_s, vbf_s = _project(
        xs, gn_a, w_in_b, gqa, gka, gqb, gkb, rope_s, keep_all=True, name="proj_sample")
    o_s = _attend_sample(qa, ka, va, qb, kb, vb, _feature_major(cache_a_k[0]), _feature_major(cache_a_v[0]),
                         _feature_major(cache_b_k[0]), _feature_major(cache_b_v[0]), u, sink)
    y_s, cn_s = _out_ffn(o_s, xs, w_out_b, gn_f, w_gu_b, cw, cb, w_down_b, state_conv[0],
                         carry_rows=False, name="ffn_sample")

    return (
        y_p.reshape(1, seq, D_MODEL),
        y_s.reshape(dec_b, dec_s, D_MODEL),
        _token_major(kaf, A_HEADS),
        _token_major(vaf, A_HEADS),
        _token_major(kbf, B_KV_HEADS),
        _token_major(vbf, B_KV_HEADS),
        cn_p.reshape(1, 1, CONV_W - 1, D_FF),
        kaf_s.reshape(1, dec_b, dec_s, A_HEADS, HEAD_DIM),
        vaf_s.reshape(1, dec_b, dec_s, A_HEADS, HEAD_DIM),
        kbf_s.reshape(1, dec_b, dec_s, B_KV_HEADS, HEAD_DIM),
        vbf_s.reshape(1, dec_b, dec_s, B_KV_HEADS, HEAD_DIM),
        cn_s.reshape(1, dec_b, CONV_W - 1, D_FF),
    )
```
